```python
import jax
import jax.numpy as jnp
from jax import lax

D_MODEL = 2048
BATCH = 32
SEQ = 256
DEPTH = 2
DEC_BATCH = 2
DEC_SEQ = 4096
PAST_LEN = 512

GRID_W = 64
N_MIXERS = 2
RET_HEADS = 8
RET_HEAD_DIM = D_MODEL // RET_HEADS
RET_CHUNK = 128
ROPE_BASE = 10000.0
RWKV_HEAD_DIM = 64
RWKV_HEADS = D_MODEL // RWKV_HEAD_DIM
DECAY_LORA = 96
AAA_LORA = 96
GATE_LORA = 256
RWKV_LN_EPS = 64e-5
D_FF = 5632
N_EXPERTS = 8
TOP_K = 2
MOE_BLOCK = 256
NORM_EPS = 1e-6

kernel_name = 'hybrid_retention_rwkv7_diffusion_step'

F32 = jnp.float32


def rmsnorm(x, w):
    xf = x.astype(F32)
    return (xf * lax.rsqrt(jnp.mean(xf * xf, -1, keepdims=True) + NORM_EPS) * w.astype(F32)).astype(x.dtype)


def grid_rope(T, dim):
    t = jnp.arange(T)
    row = (t // GRID_W).astype(F32)
    col = (t % GRID_W).astype(F32)
    half = dim // 2
    inv = ROPE_BASE ** (-(jnp.arange(0, half, 2, dtype=F32) / half))
    ang = jnp.concatenate([row[:, None] * inv, col[:, None] * inv], axis=-1)
    return jnp.cos(ang), jnp.sin(ang)


def apply_rope(x, cos, sin):
    c = cos[None, :, None, :].astype(x.dtype)
    s = sin[None, :, None, :].astype(x.dtype)
    x1, x2 = x[..., 0::2], x[..., 1::2]
    return jnp.stack([x1 * c - x2 * s, x1 * s + x2 * c], axis=-1).reshape(x.shape)


def retention_chunkwise(q, k, v, log_gamma, s0):
    B, T, H, _ = q.shape
    Dv = v.shape[-1]
    C = RET_CHUNK
    n_chunks = T // C

    def chunks(t):
        return t.astype(F32).reshape(B, n_chunks, C, H, t.shape[-1]).transpose(1, 0, 3, 2, 4)

    qc, kc, vc = chunks(q), chunks(k), chunks(v)
    pos = jnp.arange(C, dtype=F32)
    diff = pos[:, None] - pos[None, :]
    intra = jnp.where(diff >= 0, jnp.exp(jnp.maximum(diff, 0.0) * log_gamma[:, None, None]), 0.0)
    q_decay = jnp.exp((pos + 1.0) * log_gamma[:, None])[None, :, :, None]
    k_decay = jnp.exp((C - 1.0 - pos) * log_gamma[:, None])[None, :, :, None]
    chunk_decay = jnp.exp(C * log_gamma)[None, :, None, None]

    def step(s, inp):
        qi, ki, vi = inp
        scores = jnp.einsum('bhqd,bhkd->bhqk', qi, ki) * intra
        o = jnp.einsum('bhqk,bhkv->bhqv', scores, vi) + jnp.einsum('bhqd,bhdv->bhqv', qi, s) * q_decay
        s = s * chunk_decay + jnp.einsum('bhkd,bhkv->bhdv', ki * k_decay, vi)
        return s, o

    s, o = lax.scan(step, s0.astype(F32), (qc, kc, vc))
    return o.transpose(1, 0, 3, 2, 4).reshape(B, T, H, Dv), s


def retention_mixer(h, s0, rope, w_in, w_out, decay_theta):
    B, T, D = h.shape
    H, Dh = RET_HEADS, RET_HEAD_DIM
    q, k, v, g = jnp.split(h @ w_in, 4, axis=-1)
    q = q.reshape(B, T, H, Dh)
    k = k.reshape(B, T, H, Dh) * (Dh ** -0.5)
    v = v.reshape(B, T, H, Dh)
    if rope is not None:
        q = apply_rope(q, *rope)
        k = apply_rope(k, *rope)
    log_gamma = -jnp.exp(decay_theta.astype(F32))
    o_f, s_f = retention_chunkwise(q, k, v, log_gamma[0], s0[:, 0])
    flip = lambda t: jnp.flip(t, axis=1)
    o_b, s_b = retention_chunkwise(flip(q), flip(k), flip(v), log_gamma[1], s0[:, 1])
    o = o_f + flip(o_b)
    o = o * lax.rsqrt(jnp.mean(o * o, -1, keepdims=True) + NORM_EPS)
    o = o.reshape(B, T, D).astype(h.dtype) * jax.nn.silu(g)
    return o @ w_out, jnp.stack([s_f, s_b], axis=1)


def shift_seq(x):
    hd = x.shape[-1] // 2
    prev = jnp.pad(x[:, :-1, :hd], ((0, 0), (1, 0), (0, 0)))
    nxt = jnp.pad(x[:, 1:, hd:], ((0, 0), (0, 1), (0, 0)))
    return jnp.concatenate([prev, nxt], axis=-1)


def shift_grid(x):
    B, T, D = x.shape
    rows = T // GRID_W
    qd = D // 4
    g = x.reshape(B, rows, GRID_W, D)
    left = jnp.pad(g[:, :, :-1, :qd], ((0, 0), (0, 0), (1, 0), (0, 0)))
    right = jnp.pad(g[:, :, 1:, qd:2 * qd], ((0, 0), (0, 0), (0, 1), (0, 0)))
    up = jnp.pad(g[:, :-1, :, 2 * qd:3 * qd], ((0, 0), (1, 0), (0, 0), (0, 0)))
    down = jnp.pad(g[:, 1:, :, 3 * qd:], ((0, 0), (0, 1), (0, 0), (0, 0)))
    return jnp.concatenate([left, right, up, down], axis=-1).reshape(B, T, D)


def rwkv7_mixer(h, h_shift, s0, mu, w_rkv, w0, w1, w2, a0, a1, a2, g1, g2, k_k, k_a, r_k, ln_w, ln_b, w_out):
    B, T, D = h.shape
    H, K = RWKV_HEADS, RWKV_HEAD_DIM
    heads = lambda t: t.reshape(t.shape[:-1] + (H, K))
    xs = h[None] + (h_shift - h)[None] * mu[:, None, None, :]
    rkv = jnp.einsum('nbtd,nde->nbte', xs[:3], w_rkv)
    r, k, v = rkv[0], rkv[1], rkv[2]
    w_lr = jnp.einsum('nbtr,nrd->nbtd', jnp.tanh(jnp.einsum('btd,ndr->nbtr', xs[3], w1)), w2)
    w_log = -jax.nn.softplus(-(w0[:, None, None, :] + w_lr).astype(F32)) - 0.5
    decay = jnp.exp(-jnp.exp(w_log))
    a_lr = jnp.einsum('nbtr,nrd->nbtd', jnp.einsum('btd,ndr->nbtr', xs[4], a1), a2)
    a = jax.nn.sigmoid((a0[:, None, None, :] + a_lr).astype(F32))
    g = jax.nn.sigmoid(xs[5] @ g1) @ g2
    kk = heads((k * k_k).astype(F32))
    kk = kk * lax.rsqrt(jnp.sum(kk * kk, -1, keepdims=True) + 1e-12)
    k_dir = k.astype(F32)[None] * (1.0 + (a - 1.0) * k_a.astype(F32))
    r_h = heads(r.astype(F32))
    v_h = heads(v.astype(F32))

    def scan_dir(d, reverse):
        seq = (r_h, heads(decay[d]), heads(k_dir[d]), v_h, kk, heads(a[d]))
        seq = tuple(jnp.swapaxes(t, 0, 1) for t in seq)

        def step(S, inp):
            r_t, w_t, k_t, v_t, kk_t, a_t = inp
            S = (S * w_t[:, :, None, :]
                 - jnp.einsum('bhvk,bhk->bhv', S, kk_t)[..., None] * (kk_t * a_t)[:, :, None, :]
                 + v_t[..., None] * k_t[:, :, None, :])
            return S, jnp.einsum('bhvk,bhk->bhv', S, r_t)

        S, y = lax.scan(step, s0[:, d].astype(F32), seq, reverse=reverse)
        return jnp.swapaxes(y, 0, 1), S

    y_f, s_f = scan_dir(0, False)
    y_b, s_b = scan_dir(1, True)
    y = y_f + y_b
    mean = jnp.mean(y, -1, keepdims=True)
    var = jnp.mean(jnp.square(y - mean), -1, keepdims=True)
    y = ((y - mean) * lax.rsqrt(var + RWKV_LN_EPS)).reshape(B, T, D) * ln_w.astype(F32) + ln_b.astype(F32)
    bonus = jnp.sum(jnp.sum(r_h[None] * heads(k_dir) * r_k.astype(F32), axis=-1, keepdims=True), axis=0) * v_h
    y = (y + bonus.reshape(B, T, D)) * g.astype(F32)
    return y.astype(h.dtype) @ w_out, jnp.stack([s_f, s_b], axis=1)


def swiglu(x, w_gate, w_up, w_down):
    return (jax.nn.silu(x @ w_gate) * (x @ w_up)) @ w_down


def moe_swiglu(h, w_router, w_gate, w_up, w_down):
    B, T, D = h.shape
    n_tok = B * T
    xf = h.reshape(n_tok, D)
    logits = xf.astype(F32) @ w_router.astype(F32)
    top_val, top_idx = lax.top_k(logits, TOP_K)
    gates = jax.nn.softmax(top_val, axis=-1)
    n_assign = n_tok * TOP_K
    flat_e = top_idx.reshape(-1)
    flat_tok = jnp.arange(n_assign, dtype=jnp.int32) // TOP_K
    order = jnp.argsort(flat_e)
    sorted_e = flat_e[order]
    counts = jnp.bincount(flat_e, length=N_EXPERTS)
    padded = ((counts + MOE_BLOCK - 1) // MOE_BLOCK) * MOE_BLOCK
    pad_end = jnp.cumsum(padded)
    pad_start = pad_end - padded
    start = jnp.cumsum(counts) - counts
    dest = pad_start[sorted_e] + jnp.arange(n_assign, dtype=jnp.int32) - start[sorted_e]
    n_blocks = -(-n_assign // MOE_BLOCK) + N_EXPERTS
    slot_tok = jnp.full((n_blocks * MOE_BLOCK,), n_tok, jnp.int32).at[dest].set(flat_tok[order])
    block_e = jnp.minimum(jnp.searchsorted(pad_end, jnp.arange(n_blocks) * MOE_BLOCK, side='right'), N_EXPERTS - 1)
    x_pad = jnp.concatenate([xf, jnp.zeros((1, D), xf.dtype)], axis=0)
    xb = x_pad[slot_tok].reshape(n_blocks, MOE_BLOCK, D)

    def expert_block(args):
        xblk, e = args
        return swiglu(xblk, w_gate[e], w_up[e], w_down[e])

    yb = lax.map(expert_block, (xb, block_e)).reshape(n_blocks * MOE_BLOCK, D)
    y_sorted = yb[dest]
    y_assign = jnp.zeros_like(y_sorted).at[order].set(y_sorted)
    y = jnp.einsum('nkd,nk->nd', y_assign.reshape(n_tok, TOP_K, D), gates.astype(h.dtype))
    return y.reshape(B, T, D)


def trunk(x, cond, init_states, is_latent, ada_w, ada_b, norm_w, final_norm_w, mixers, ffns):
    T = x.shape[1]
    rope = grid_rope(T, RET_HEAD_DIM) if is_latent else None
    final_states = []
    for i in range(DEPTH):
        mod = jax.nn.silu(cond) @ ada_w[i] + ada_b[i]
        sh1, sc1, gt1, sh2, sc2, gt2 = jnp.split(mod[:, None, :], 6, axis=-1)
        h = rmsnorm(x, norm_w[i, 0]) * (1.0 + sc1) + sh1
        if i % N_MIXERS == 0:
            y, s = retention_mixer(h, init_states[i], rope, *mixers[i])
        else:
            h_shift = shift_grid(h) if is_latent else shift_seq(h)
            y, s = rwkv7_mixer(h, h_shift, init_states[i], *mixers[i])
        x = x + gt1 * y
        h = rmsnorm(x, norm_w[i, 1]) * (1.0 + sc2) + sh2
        y = swiglu(h, *ffns[i]) if i % 2 == 0 else moe_swiglu(h, *ffns[i])
        x = x + gt2 * y
        final_states.append(s)
    return rmsnorm(x, final_norm_w), final_states


def setup_inputs(seed: int = 0) -> dict:
    key = jax.random.key(seed)
    ks = iter(jax.random.split(key, 48))
    D = D_MODEL
    nrm = lambda shape, scale: jax.random.normal(next(ks), shape, F32) * scale
    ret_base = jnp.log(-jnp.log1p(-(2.0 ** (-5.0 - jnp.arange(RET_HEADS, dtype=F32)))))
    return {
        'x_prompt': nrm((BATCH, SEQ, D), 1.0),
        'x_sample': nrm((DEC_BATCH, DEC_SEQ, D), 1.0),
        'state_l0_ret': nrm((DEC_BATCH, 2, RET_HEADS, RET_HEAD_DIM, RET_HEAD_DIM), 1.0),
        'state_l1_rwkv': nrm((DEC_BATCH, 2, RWKV_HEADS, RWKV_HEAD_DIM, RWKV_HEAD_DIM), 1.0),
        'c': nrm((DEC_BATCH, D), 1.0),
        'c_ctx': nrm((D,), 1.0),
        'ada_w': nrm((DEPTH, D, 6 * D), D ** -0.5),
        'ada_b': nrm((DEPTH, 6 * D), 0.02),
        'norm_w': 1.0 + nrm((DEPTH, 2, D), 0.02),
        'final_norm_w': 1.0 + nrm((D,), 0.02),
        'l0_ret_w_in': nrm((D, 4 * D), D ** -0.5),
        'l0_ret_w_out': nrm((D, D), D ** -0.5),
        'l0_ret_decay': ret_base[None, :] + nrm((2, RET_HEADS), 0.05),
        'l0_ffn_w_gate': nrm((D, D_FF), D ** -0.5),
        'l0_ffn_w_up': nrm((D, D_FF), D ** -0.5),
        'l0_ffn_w_down': nrm((D_FF, D), D_FF ** -0.5),
        'l1_rwkv_mu': jax.random.uniform(next(ks), (6, D), F32),
        'l1_rwkv_w_rkv': nrm((3, D, D), D ** -0.5),
        'l1_rwkv_w0': jnp.linspace(-6.5, -1.5, D, dtype=F32)[None, :] + nrm((2, D), 0.1),
        'l1_rwkv_w1': nrm((2, D, DECAY_LORA), D ** -0.5),
        'l1_rwkv_w2': nrm((2, DECAY_LORA, D), 0.5 * DECAY_LORA ** -0.5),
        'l1_rwkv_a0': nrm((2, D), 0.1),
        'l1_rwkv_a1': nrm((2, D, AAA_LORA), D ** -0.5),
        'l1_rwkv_a2': nrm((2, AAA_LORA, D), AAA_LORA ** -0.5),
        'l1_rwkv_g1': nrm((D, GATE_LORA), D ** -0.5),
        'l1_rwkv_g2': nrm((GATE_LORA, D), GATE_LORA ** -0.5),
        'l1_rwkv_k_k': 0.85 + nrm((D,), 0.02),
        'l1_rwkv_k_a': 1.0 + nrm((D,), 0.02),
        'l1_rwkv_r_k': nrm((RWKV_HEADS, RWKV_HEAD_DIM), 0.1),
        'l1_rwkv_ln_w': 1.0 + nrm((D,), 0.02),
        'l1_rwkv_ln_b': nrm((D,), 0.02),
        'l1_rwkv_w_out': nrm((D, D), D ** -0.5),
        'l1_moe_router': nrm((D, N_EXPERTS), D ** -0.5),
        'l1_moe_w_gate': nrm((N_EXPERTS, D, D_FF), D ** -0.5),
        'l1_moe_w_up': nrm((N_EXPERTS, D, D_FF), D ** -0.5),
        'l1_moe_w_down': nrm((N_EXPERTS, D_FF, D), D_FF ** -0.5),
    }


def reference(x_prompt, x_sample, state_l0_ret, state_l1_rwkv, c, c_ctx, ada_w, ada_b, norm_w, final_norm_w,
              l0_ret_w_in, l0_ret_w_out, l0_ret_decay, l0_ffn_w_gate, l0_ffn_w_up, l0_ffn_w_down,
              l1_rwkv_mu, l1_rwkv_w_rkv, l1_rwkv_w0, l1_rwkv_w1, l1_rwkv_w2, l1_rwkv_a0, l1_rwkv_a1, l1_rwkv_a2,
              l1_rwkv_g1, l1_rwkv_g2, l1_rwkv_k_k, l1_rwkv_k_a, l1_rwkv_r_k, l1_rwkv_ln_w, l1_rwkv_ln_b,
              l1_rwkv_w_out, l1_moe_router, l1_moe_w_gate, l1_moe_w_up, l1_moe_w_down):
    mixers = (
        (l0_ret_w_in, l0_ret_w_out, l0_ret_decay),
        (l1_rwkv_mu, l1_rwkv_w_rkv, l1_rwkv_w0, l1_rwkv_w1, l1_rwkv_w2, l1_rwkv_a0, l1_rwkv_a1, l1_rwkv_a2,
         l1_rwkv_g1, l1_rwkv_g2, l1_rwkv_k_k, l1_rwkv_k_a, l1_rwkv_r_k, l1_rwkv_ln_w, l1_rwkv_ln_b, l1_rwkv_w_out),
    )
    ffns = (
        (l0_ffn_w_gate, l0_ffn_w_up, l0_ffn_w_down),
        (l1_moe_router, l1_moe_w_gate, l1_moe_w_up, l1_moe_w_down),
    )
    b = x_prompt.shape[0]
    ctx_init = (jnp.zeros((b, 2, RET_HEADS, RET_HEAD_DIM, RET_HEAD_DIM), F32),
                jnp.zeros((b, 2, RWKV_HEADS, RWKV_HEAD_DIM, RWKV_HEAD_DIM), F32))
    y_prompt, ctx_states = trunk(x_prompt, c_ctx[None, :], ctx_init, False,
                                 ada_w, ada_b, norm_w, final_norm_w, mixers, ffns)
    new_state_l0_ret = ctx_states[0]
    new_state_l1_rwkv = ctx_states[1]
    y_sample, _ = trunk(x_sample, c, (state_l0_ret, state_l1_rwkv), True,
                        ada_w, ada_b, norm_w, final_norm_w, mixers, ffns)
    return (y_prompt, y_sample, new_state_l0_ret, new_state_l1_rwkv)
```

```python
import functools
from typing import NamedTuple

import jax
import jax.numpy as jnp
from jax import lax
from jax.experimental import pallas as pl
from jax.experimental.pallas import tpu as pltpu

F32 = jnp.float32
BF16 = jnp.bfloat16

GRID_W = 64
RET_HEADS = 8
RET_CHUNK = 128
ROPE_BASE = 10000.0
RWKV_HEAD_DIM = 64
RWKV_CHUNK = 64
RWKV_LN_EPS = 64e-5
N_EXPERTS = 8
TOP_K = 2
MOE_BLOCK = 512
NORM_EPS = 1e-6

VMEM_LIMIT_BYTES = 56 * 1024 * 1024


class _Layout(NamedTuple):
    p_rows: int
    p_len: int
    s_len: int
    n_rows: int


def _cparams(n_axes):
    return pltpu.CompilerParams(dimension_semantics=("arbitrary",) * n_axes,
                                vmem_limit_bytes=VMEM_LIMIT_BYTES)


def _cond_of_tile(i, tm, lay):
    pt = lay.p_rows // tm
    st = lay.s_len // tm
    return jnp.where(i < pt, 0, 1 + (i - pt) // st)


def _bdot(a, b):
    return jnp.dot(a.astype(BF16), b.astype(BF16), preferred_element_type=F32)


def _bdot_nt(a, b):
    return lax.dot_general(a.astype(BF16), b.astype(BF16), (((1,), (1,)), ((), ())),
                           preferred_element_type=F32)


def _bdot_tn(a, b):
    return lax.dot_general(a.astype(BF16), b.astype(BF16), (((0,), (0,)), ((), ())),
                           preferred_element_type=F32)


def _silu(x):
    return x * jax.nn.sigmoid(x)


def _ada_body(c_ref, w_ref, b_ref, o_ref):
    o_ref[...] = _bdot(_silu(c_ref[...]), w_ref[...]) + b_ref[...]


def _ada_mod(cond8, ada_w, ada_b):
    n_layers, d, d6 = ada_w.shape
    tn = min(1024, d6)
    return pl.pallas_call(
        _ada_body,
        grid=(n_layers, d6 // tn),
        in_specs=[pl.BlockSpec((8, d), lambda l, j: (0, 0)),
                  pl.BlockSpec((None, d, tn), lambda l, j: (l, 0, j)),
                  pl.BlockSpec((None, 1, tn), lambda l, j: (l, 0, j))],
        out_specs=pl.BlockSpec((None, 8, tn), lambda l, j: (l, 0, j)),
        out_shape=jax.ShapeDtypeStruct((n_layers, 8, d6), F32),
        compiler_params=_cparams(2), name="ada_mod",
    )(cond8, ada_w, ada_b.reshape(n_layers, 1, d6))


def _mod_spec(layer, which, tm, lay, n_grid_axes=1):
    def imap(i, *_):
        return (layer, _cond_of_tile(i, tm, lay), which, 0, 0)
    return imap


def _rms_mod(x, nw, sc, sh):
    xn = x * lax.rsqrt(jnp.mean(x * x, axis=-1, keepdims=True) + NORM_EPS) * nw
    return xn * (1.0 + sc) + sh


def _norm_mod_body(x_ref, nw_ref, sc_ref, sh_ref, o_ref):
    o_ref[...] = _rms_mod(x_ref[...], nw_ref[...], sc_ref[...], sh_ref[...]).astype(o_ref.dtype)


def _norm_mod_router_body(x_ref, nw_ref, sc_ref, sh_ref, wr_ref, o_ref, lg_ref):
    h = _rms_mod(x_ref[...], nw_ref[...], sc_ref[...], sh_ref[...])
    o_ref[...] = h.astype(o_ref.dtype)
    lg_ref[...] = jnp.dot(h, wr_ref[...], preferred_element_type=F32, precision=lax.Precision.HIGHEST)


def _norm_mod(x, nw, mod5, layer, which_sc, which_sh, lay, router_w=None, tm=512):
    n, d = x.shape
    tm = min(tm, lay.s_len, lay.p_rows)
    d_spec = pl.BlockSpec((None, None, None, 1, d), None)
    in_specs = [pl.BlockSpec((tm, d), lambda i: (i, 0)),
                pl.BlockSpec((1, d), lambda i: (0, 0)),
                pl.BlockSpec((None, None, None, 1, d), _mod_spec(layer, which_sc, tm, lay)),
                pl.BlockSpec((None, None, None, 1, d), _mod_spec(layer, which_sh, tm, lay))]
    del d_spec
    args = [x, nw.reshape(1, d), mod5, mod5]
    if router_w is None:
        return pl.pallas_call(
            _norm_mod_body, grid=(n // tm,), in_specs=in_specs,
            out_specs=pl.BlockSpec((tm, d), lambda i: (i, 0)),
            out_shape=jax.ShapeDtypeStruct((n, d), BF16),
            compiler_params=_cparams(1), name="norm_mod")(*args)
    ne = router_w.shape[1]
    return pl.pallas_call(
        _norm_mod_router_body, grid=(n // tm,),
        in_specs=in_specs + [pl.BlockSpec((d, ne), lambda i: (0, 0))],
        out_specs=[pl.BlockSpec((tm, d), lambda i: (i, 0)), pl.BlockSpec((tm, ne), lambda i: (i, 0))],
        out_shape=[jax.ShapeDtypeStruct((n, d), BF16), jax.ShapeDtypeStruct((n, ne), F32)],
        compiler_params=_cparams(1), name="norm_mod_router")(*args, router_w)


def _linear_body(x_ref, w_ref, o_ref, *, act):
    acc = _bdot(x_ref[...], w_ref[...])
    if act == "tanh":
        acc = jnp.tanh(acc)
    elif act == "sigmoid":
        acc = jax.nn.sigmoid(acc)
    o_ref[...] = acc.astype(o_ref.dtype)


def _linear(x, w, *, x_lead=None, w_lead=None, act=None, out_dtype=F32, tm=1024, tn=512):
    m, k = x.shape[-2:]
    n = w.shape[-1]
    tm, tn = min(tm, m), min(tn, n)
    if x_lead is None:
        x_spec = pl.BlockSpec((tm, k), lambda i, j: (i, 0))
    else:
        x_spec = pl.BlockSpec((None, tm, k), lambda i, j: (x_lead, i, 0))
    if w_lead is None:
        w_spec = pl.BlockSpec((k, tn), lambda i, j: (0, j))
    else:
        w_spec = pl.BlockSpec((None, k, tn), lambda i, j: (w_lead, 0, j))
    return pl.pallas_call(
        functools.partial(_linear_body, act=act),
        grid=(m // tm, n // tn), in_specs=[x_spec, w_spec],
        out_specs=pl.BlockSpec((tm, tn), lambda i, j: (i, j)),
        out_shape=jax.ShapeDtypeStruct((m, n), out_dtype),
        compiler_params=_cparams(2), name="linear")(x, w)


def _linear_res_body(x_ref, w_ref, res_ref, gt_ref, o_ref):
    o_ref[...] = res_ref[...] + gt_ref[...] * _bdot(x_ref[...], w_ref[...])


def _linear_residual(x, w, res, mod5, layer, which_gate, lay, tm=1024, tn=512):
    m, k = x.shape
    n = w.shape[1]
    tm, tn = min(tm, lay.s_len, lay.p_rows), min(tn, n)

    def gmap(i, j):
        return (layer, _cond_of_tile(i, tm, lay), which_gate, 0, j)

    return pl.pallas_call(
        _linear_res_body, grid=(m // tm, n // tn),
        in_specs=[pl.BlockSpec((tm, k), lambda i, j: (i, 0)),
                  pl.BlockSpec((k, tn), lambda i, j: (0, j)),
                  pl.BlockSpec((tm, tn), lambda i, j: (i, j)),
                  pl.BlockSpec((None, None, None, 1, tn), gmap)],
        out_specs=pl.BlockSpec((tm, tn), lambda i, j: (i, j)),
        out_shape=jax.ShapeDtypeStruct((m, n), F32),
        compiler_params=_cparams(2), name="linear_residual")(x, w, res, mod5)


def _swiglu_up_body(x_ref, wg_ref, wu_ref, o_ref):
    x = x_ref[...]
    o_ref[...] = (_silu(_bdot(x, wg_ref[...])) * _bdot(x, wu_ref[...])).astype(o_ref.dtype)


def _swiglu_up(x, wg, wu, tm=1024, tn=512):
    m, k = x.shape
    n = wg.shape[1]
    tm, tn = min(tm, m), min(tn, n)
    return pl.pallas_call(
        _swiglu_up_body, grid=(m // tm, n // tn),
        in_specs=[pl.BlockSpec((tm, k), lambda i, j: (i, 0)),
                  pl.BlockSpec((k, tn), lambda i, j: (0, j)),
                  pl.BlockSpec((k, tn), lambda i, j: (0, j))],
        out_specs=pl.BlockSpec((tm, tn), lambda i, j: (i, j)),
        out_shape=jax.ShapeDtypeStruct((m, n), BF16),
        compiler_params=_cparams(2), name="swiglu_up")(x, wg, wu)


def _rope(x, c, s):
    w = x.shape[-1]
    lane = lax.broadcasted_iota(jnp.int32, x.shape, x.ndim - 1)
    nxt = pltpu.roll(x, w - 1, axis=x.ndim - 1)
    prv = pltpu.roll(x, 1, axis=x.ndim - 1)
    return x * c + jnp.where(lane % 2 == 0, nxt, prv) * s


def _retention_body(lg_ref, *refs, n_chunk, n_blk, has_rope, has_s0, emit_state, scale):
    refs = list(refs)
    q_ref, k_ref, v_ref = refs[:3]
    refs = refs[3:]
    if has_rope:
        cos_ref, sin_ref = refs[:2]
        refs = refs[2:]
    if has_s0:
        s0_ref = refs.pop(0)
    o_ref = refs.pop(0)
    if emit_state:
        so_ref = refs.pop(0)
    st = refs.pop(0)
    c_sz = RET_CHUNK
    dh = q_ref.shape[-1]
    h, d, cb = pl.program_id(1), pl.program_id(2), pl.program_id(3)

    @pl.when(cb == 0)
    def _():
        st[...] = s0_ref[...] if has_s0 else jnp.zeros_like(st)

    lgv = lg_ref[d, h]
    fwd = d == 0
    row = lax.broadcasted_iota(jnp.int32, (c_sz, c_sz), 0)
    col = lax.broadcasted_iota(jnp.int32, (c_sz, c_sz), 1)
    diff = jnp.where(fwd, row - col, col - row).astype(F32)
    intra = jnp.where(diff >= 0, jnp.exp(jnp.maximum(diff, 0.0) * lgv), 0.0)
    pos = lax.broadcasted_iota(jnp.int32, (c_sz, dh), 0)
    npos = jnp.where(fwd, pos, c_sz - 1 - pos).astype(F32)
    q_decay = jnp.exp((npos + 1.0) * lgv)
    k_decay = jnp.exp((c_sz - 1.0 - npos) * lgv)
    chunk_decay = jnp.exp(jnp.full((1, dh), c_sz, F32) * lgv)

    for j in range(n_chunk):
        jj = jnp.where(fwd, j, n_chunk - 1 - j)
        rows = pl.ds(pl.multiple_of(jj * c_sz, c_sz), c_sz)
        q = q_ref[rows, :]
        k = k_ref[rows, :] * scale
        v = v_ref[rows, :]
        if has_rope:
            c, s = cos_ref[rows, :], sin_ref[rows, :]
            q, k = _rope(q, c, s), _rope(k, c, s)
        state = st[...]
        scores = _bdot_nt(q, k) * intra
        o_ref[rows, :] = _bdot(scores, v) + _bdot(q, state) * q_decay
        st[...] = state * chunk_decay + _bdot_tn(k * k_decay, v)

    if emit_state:
        @pl.when(cb == n_blk - 1)
        def _():
            so_ref[...] = st[...]


def _retention(qkvg, log_gamma, s0, rope, *, row0, n_seq, seq_len, emit_state):
    d_model = qkvg.shape[1] // 4
    n_heads = RET_HEADS
    dh = d_model // n_heads
    tb = min(512, seq_len)
    n_blk = seq_len // tb
    rb0 = row0 // tb

    def blk(c, d):
        return jnp.where(d == 0, c, n_blk - 1 - c)

    def in_map(part):
        return lambda s, h, d, c, lg: (rb0 + s * n_blk + blk(c, d), part * n_heads + h)

    in_specs = [pl.BlockSpec((tb, dh), in_map(p)) for p in range(3)]
    args = [qkvg, qkvg, qkvg]
    if rope is not None:
        in_specs += [pl.BlockSpec((tb, dh), lambda s, h, d, c, lg: (blk(c, d), 0))] * 2
        args += list(rope)
    if s0 is not None:
        in_specs.append(pl.BlockSpec((None, None, None, dh, dh), lambda s, h, d, c, lg: (s, d, h, 0, 0)))
        args.append(s0)
    n_out_rows = n_seq * seq_len
    out_specs = [pl.BlockSpec((None, tb, dh), lambda s, h, d, c, lg: (d, s * n_blk + blk(c, d), h))]
    out_shape = [jax.ShapeDtypeStruct((2, n_out_rows, d_model), F32)]
    if emit_state:
        out_specs.append(pl.BlockSpec((None, None, None, dh, dh), lambda s, h, d, c, lg: (s, d, h, 0, 0)))
        out_shape.append(jax.ShapeDtypeStruct((n_seq, 2, n_heads, dh, dh), F32))
    body = functools.partial(_retention_body, n_chunk=tb // RET_CHUNK, n_blk=n_blk, has_rope=rope is not None,
                             has_s0=s0 is not None, emit_state=emit_state, scale=dh ** -0.5)
    return pl.pallas_call(
        body,
        grid_spec=pltpu.PrefetchScalarGridSpec(
            num_scalar_prefetch=1, grid=(n_seq, n_heads, 2, n_blk), in_specs=in_specs, out_specs=out_specs,
            scratch_shapes=[pltpu.VMEM((dh, dh), F32)]),
        out_shape=out_shape, compiler_params=_cparams(4), name="retention")(log_gamma, *args)


def _ret_finalize_body(o_ref, g_ref, out_ref, *, n_heads):
    o = o_ref[0] + o_ref[1]
    g = g_ref[...]
    dh = o.shape[1] // n_heads
    for h in range(n_heads):
        sl = slice(h * dh, (h + 1) * dh)
        oh = o[:, sl]
        oh = oh * lax.rsqrt(jnp.mean(oh * oh, axis=-1, keepdims=True) + NORM_EPS)
        out_ref[:, sl] = (oh * _silu(g[:, sl])).astype(out_ref.dtype)


def _ret_finalize(o2, qkvg, row0, tm=256):
    _, n, d = o2.shape
    tm = min(tm, n)
    rb0 = row0 // tm
    return pl.pallas_call(
        functools.partial(_ret_finalize_body, n_heads=RET_HEADS), grid=(n // tm,),
        in_specs=[pl.BlockSpec((2, tm, d), lambda i: (0, i, 0)),
                  pl.BlockSpec((tm, d), lambda i: (rb0 + i, 3))],
        out_specs=pl.BlockSpec((tm, d), lambda i: (i, 0)),
        out_shape=jax.ShapeDtypeStruct((n, d), BF16),
        compiler_params=_cparams(1), name="ret_finalize")(o2, qkvg)


def _rwkv_mix_body(x_ref, xp_ref, xn_ref, nw_ref, sc_ref, sh_ref, mu_ref, o_ref, hext, *, tm, halo, lay):
    i = pl.program_id(0)
    d = x_ref.shape[1]
    nw, sc, sh = nw_ref[...], sc_ref[...], sh_ref[...]
    hext[0:halo, :] = _rms_mod(xp_ref[...], nw, sc, sh)
    hext[halo:halo + tm, :] = _rms_mod(x_ref[...], nw, sc, sh)
    hext[halo + tm:halo + tm + halo, :] = _rms_mod(xn_ref[...], nw, sc, sh)
    h = hext[halo:halo + tm, :]
    g_row = i * tm + lax.broadcasted_iota(jnp.int32, (tm, 1), 0)

    def emit(h_shift):
        diff = h_shift - h
        for n in range(6):
            o_ref[n] = (h + diff * mu_ref[n:n + 1, :]).astype(o_ref.dtype)

    def shifted(off, lo, hi, keep):
        return jnp.where(keep, hext[halo + off:halo + off + tm, lo:hi], 0.0)

    @pl.when(i < lay.p_rows // tm)
    def _():
        t = g_row % lay.p_len
        hd = d // 2
        emit(jnp.concatenate([shifted(-1, 0, hd, t != 0),
                              shifted(1, hd, d, t != lay.p_len - 1)], axis=1))

    @pl.when(i >= lay.p_rows // tm)
    def _():
        t = (g_row - lay.p_rows) % lay.s_len
        colw = t % GRID_W
        qd = d // 4
        emit(jnp.concatenate([shifted(-1, 0, qd, colw != 0),
                              shifted(1, qd, 2 * qd, colw != GRID_W - 1),
                              shifted(-GRID_W, 2 * qd, 3 * qd, t >= GRID_W),
                              shifted(GRID_W, 3 * qd, d, t < lay.s_len - GRID_W)], axis=1))


def _rwkv_mix(x, nw, mod5, layer, mu, lay, tm=512):
    n, d = x.shape
    halo = GRID_W
    tm = min(tm, lay.s_len, lay.p_rows)
    r = tm // halo
    n_halo_blk = n // halo
    body = functools.partial(_rwkv_mix_body, tm=tm, halo=halo, lay=lay)
    return pl.pallas_call(
        body, grid=(n // tm,),
        in_specs=[pl.BlockSpec((tm, d), lambda i: (i, 0)),
                  pl.BlockSpec((halo, d), lambda i: (jnp.maximum(i * r - 1, 0), 0)),
                  pl.BlockSpec((halo, d), lambda i: (jnp.minimum((i + 1) * r, n_halo_blk - 1), 0)),
                  pl.BlockSpec((1, d), lambda i: (0, 0)),
                  pl.BlockSpec((None, None, None, 1, d), _mod_spec(layer, 1, tm, lay)),
                  pl.BlockSpec((None, None, None, 1, d), _mod_spec(layer, 0, tm, lay)),
                  pl.BlockSpec((6, d), lambda i: (0, 0))],
        out_specs=pl.BlockSpec((6, tm, d), lambda i: (0, i, 0)),
        out_shape=jax.ShapeDtypeStruct((6, n, d), BF16),
        scratch_shapes=[pltpu.VMEM((tm + 2 * halo, d), F32)],
        compiler_params=_cparams(1), name="rwkv_mix")(x, x, x, nw.reshape(1, d), mod5, mod5, mu)


def _softplus(x):
    return jnp.maximum(x, 0.0) + jnp.log1p(jnp.exp(-jnp.abs(x)))


def _rwkv_scan_body(*refs, n_chunk, n_blk, n_hg, has_s0, emit_state):
    refs = list(refs)
    r_ref, k_ref, v_ref, zw_ref, za_ref, w0_ref, a0_ref, kk_ref, ka_ref = refs[:9]
    refs = refs[9:]
    if has_s0:
        s0_ref = refs.pop(0)
    y_ref = refs.pop(0)
    if emit_state:
        so_ref = refs.pop(0)
    st = refs.pop(0)
    c_sz, hd = RWKV_CHUNK, RWKV_HEAD_DIM
    d, cb = pl.program_id(1), pl.program_id(3)
    fwd = d == 0

    @pl.when(cb == 0)
    def _():
        st[...] = s0_ref[...] if has_s0 else jnp.zeros_like(st)

    row = lax.broadcasted_iota(jnp.int32, (c_sz, c_sz), 0)
    col = lax.broadcasted_iota(jnp.int32, (c_sz, c_sz), 1)
    diff = jnp.where(fwd, row - col, col - row)
    strict = diff > 0
    incl = diff >= 0
    tri_incl = jnp.where(incl, 1.0, 0.0).astype(F32)
    eye = jnp.where(row == col, 1.0, 0.0).astype(F32)
    levels = []
    m = 1
    while m < c_sz:
        levels.append(jnp.logical_and(row // (2 * m) == col // (2 * m), row // m != col // m))
        m *= 2

    w0, a0, k_k, k_a = w0_ref[...], a0_ref[...], kk_ref[...], ka_ref[...]

    def chunk(j, carry):
        jj = jnp.where(fwd, j, n_chunk - 1 - j)
        rows = pl.ds(pl.multiple_of(jj * c_sz, c_sz), c_sz)
        r, k, v = r_ref[rows, :], k_ref[rows, :], v_ref[rows, :]
        w_log = -_softplus(-(w0 + zw_ref[rows, :])) - 0.5
        lw = -jnp.exp(w_log)
        a = jax.nn.sigmoid(a0 + za_ref[rows, :])
        cum = jnp.dot(tri_incl, lw, preferred_element_type=F32, precision=lax.Precision.HIGHEST)
        total = jnp.sum(lw, axis=0, keepdims=True)
        half = 0.5 * total
        cumx = cum - lw
        r_in = r * jnp.exp(cum - half)
        e_ex = jnp.exp(cumx - half)
        e_neg = jnp.exp(half - cum)
        e_end = jnp.exp(total - cum)
        r_abs = r * jnp.exp(cum)
        e_absx = jnp.exp(cumx)
        e_tot = jnp.exp(total)
        kkr = k * k_k
        kdir = k * (1.0 + (a - 1.0) * k_a)
        ys = []
        for h in range(w0.shape[1] // hd):
            sl = slice(h * hd, (h + 1) * hd)
            kk = kkr[:, sl]
            kk = kk * lax.rsqrt(jnp.sum(kk * kk, axis=-1, keepdims=True) + 1e-12)
            b = kk * a[:, sl]
            vh = v[:, sl]
            gram = _bdot_nt(jnp.concatenate([kk * e_ex[:, sl], r_in[:, sl]], axis=0),
                            jnp.concatenate([kdir[:, sl] * e_neg[:, sl], b * e_neg[:, sl]], axis=0))
            l_k = jnp.where(strict, gram[:c_sz, :c_sz], 0.0)
            l_b = jnp.where(strict, gram[:c_sz, c_sz:], 0.0)
            a_rk = jnp.where(incl, gram[c_sz:, :c_sz], 0.0)
            a_rb = jnp.where(incl, gram[c_sz:, c_sz:], 0.0)
            x = eye - jnp.where(levels[0], l_b, 0.0)
            for lvl in levels[1:]:
                x = x - _bdot(_bdot(x, jnp.where(lvl, l_b, 0.0)), x)
            s_prev = st[h]
            u = _bdot(x, _bdot_nt(kk * e_absx[:, sl], s_prev) + _bdot(l_k, vh))
            ys.append(_bdot_nt(r_abs[:, sl], s_prev) + _bdot(a_rk, vh) - _bdot(a_rb, u))
            st[h] = (s_prev * e_tot[:, sl] + _bdot_tn(vh, kdir[:, sl] * e_end[:, sl])
                     - _bdot_tn(u, b * e_end[:, sl]))
        y_ref[rows, :] = jnp.concatenate(ys, axis=1)
        return carry

    lax.fori_loop(0, n_chunk, chunk, 0)

    if emit_state:
        @pl.when(cb == n_blk - 1)
        def _():
            so_ref[...] = st[...]


def _rwkv_scan(rkv, zw, za, w0, a0, k_k, k_a, s0, *, row0, n_seq, seq_len, emit_state, heads_per_step=8):
    _, _, d_model = rkv.shape
    hd = RWKV_HEAD_DIM
    n_heads = d_model // hd
    g = min(heads_per_step, n_heads)
    wg = g * hd
    n_hg = n_heads // g
    tb = min(256, seq_len)
    n_blk = seq_len // tb
    rb0 = row0 // tb

    def blk(c, d):
        return jnp.where(d == 0, c, n_blk - 1 - c)

    def rkv_map(part):
        return lambda s, d, hg, c: (part, rb0 + s * n_blk + blk(c, d), hg)

    dir_map = lambda s, d, hg, c: (d, rb0 + s * n_blk + blk(c, d), hg)
    par_map = lambda s, d, hg, c: (d, 0, hg)
    in_specs = ([pl.BlockSpec((None, tb, wg), rkv_map(p)) for p in range(3)]
                + [pl.BlockSpec((None, tb, wg), dir_map)] * 2
                + [pl.BlockSpec((None, 1, wg), par_map)] * 2
                + [pl.BlockSpec((1, wg), lambda s, d, hg, c: (0, hg))] * 2)
    args = [rkv, rkv, rkv, zw, za, w0.reshape(2, 1, d_model), a0.reshape(2, 1, d_model),
            k_k.reshape(1, d_model), k_a.reshape(1, d_model)]
    if s0 is not None:
        in_specs.append(pl.BlockSpec((None, None, g, hd, hd), lambda s, d, hg, c: (s, d, hg, 0, 0)))
        args.append(s0)
    out_specs = [pl.BlockSpec((None, tb, wg), lambda s, d, hg, c: (d, s * n_blk + blk(c, d), hg))]
    out_shape = [jax.ShapeDtypeStruct((2, n_seq * seq_len, d_model), F32)]
    if emit_state:
        out_specs.append(pl.BlockSpec((None, None, g, hd, hd), lambda s, d, hg, c: (s, d, hg, 0, 0)))
        out_shape.append(jax.ShapeDtypeStruct((n_seq, 2, n_heads, hd, hd), F32))
    body = functools.partial(_rwkv_scan_body, n_chunk=tb // RWKV_CHUNK, n_blk=n_blk, n_hg=n_hg,
                             has_s0=s0 is not None, emit_state=emit_state)
    return pl.pallas_call(
        body, grid=(n_seq, 2, n_hg, n_blk), in_specs=in_specs, out_specs=out_specs, out_shape=out_shape,
        scratch_shapes=[pltpu.VMEM((g, hd, hd), F32)],
        compiler_params=_cparams(4), name="rwkv_scan")(*args)


def _group_sum(x, ones_bd):
    w = ones_bd.shape[0]
    out = []
    for c in range(x.shape[1] // w):
        xs = x[:, c * w:(c + 1) * w]
        hi = xs.astype(BF16)
        lo = (xs - hi.astype(F32)).astype(BF16)
        out.append(jnp.dot(hi, ones_bd, preferred_element_type=F32)
                   + jnp.dot(lo, ones_bd, preferred_element_type=F32))
    return jnp.concatenate(out, axis=1)


def _rwkv_finalize_body(y_ref, rkv_ref, za_ref, g_ref, a0_ref, ka_ref, rk_ref, lnw_ref, lnb_ref, o_ref):
    hd = RWKV_HEAD_DIM
    w = 256
    ri = lax.broadcasted_iota(jnp.int32, (w, w), 0)
    ci = lax.broadcasted_iota(jnp.int32, (w, w), 1)
    ones_bd = jnp.where(ri // hd == ci // hd, 1.0, 0.0).astype(BF16)
    y = y_ref[0] + y_ref[1]
    mean = _group_sum(y, ones_bd) * (1.0 / hd)
    yc = y - mean
    var = _group_sum(yc * yc, ones_bd) * (1.0 / hd)
    yn = yc * lax.rsqrt(var + RWKV_LN_EPS) * lnw_ref[...] + lnb_ref[...]
    r, k, v = rkv_ref[0], rkv_ref[1], rkv_ref[2]
    a_sum = jax.nn.sigmoid(a0_ref[0] + za_ref[0]) + jax.nn.sigmoid(a0_ref[1] + za_ref[1])
    k_sum = k * (2.0 + (a_sum - 2.0) * ka_ref[...])
    bonus = _group_sum(r * k_sum * rk_ref[...], ones_bd) * v
    o_ref[...] = ((yn + bonus) * g_ref[...]).astype(o_ref.dtype)


def _rwkv_finalize(y2, rkv, za, g, a0, k_a, r_k, ln_w, ln_b, tm=256):
    _, n, d = y2.shape
    tm = min(tm, n)
    row = lambda i: (0, 0)
    return pl.pallas_call(
        _rwkv_finalize_body, grid=(n // tm,),
        in_specs=[pl.BlockSpec((2, tm, d), lambda i: (0, i, 0)),
                  pl.BlockSpec((3, tm, d), lambda i: (0, i, 0)),
                  pl.BlockSpec((2, tm, d), lambda i: (0, i, 0)),
                  pl.BlockSpec((tm, d), lambda i: (i, 0)),
                  pl.BlockSpec((2, 1, d), lambda i: (0, 0, 0)),
                  pl.BlockSpec((1, d), row), pl.BlockSpec((1, d), row),
                  pl.BlockSpec((1, d), row), pl.BlockSpec((1, d), row)],
        out_specs=pl.BlockSpec((tm, d), lambda i: (i, 0)),
        out_shape=jax.ShapeDtypeStruct((n, d), BF16),
        compiler_params=_cparams(1), name="rwkv_finalize",
    )(y2, rkv, za, g, a0.reshape(2, 1, d), k_a.reshape(1, d), r_k.reshape(1, d),
      ln_w.reshape(1, d), ln_b.reshape(1, d))


def _moe_up_body(be_ref, na_ref, x_ref, wg_ref, wu_ref, o_ref):
    b = pl.program_id(1)

    @pl.when(b < na_ref[0])
    def _():
        x = x_ref[...]
        o_ref[...] = (_silu(_bdot(x, wg_ref[...])) * _bdot(x, wu_ref[...])).astype(o_ref.dtype)

    @pl.when(b >= na_ref[0])
    def _():
        o_ref[...] = jnp.zeros_like(o_ref)


def _moe_down_body(be_ref, na_ref, x_ref, wd_ref, o_ref):
    b = pl.program_id(1)

    @pl.when(b < na_ref[0])
    def _():
        o_ref[...] = _bdot(x_ref[...], wd_ref[...])

    @pl.when(b >= na_ref[0])
    def _():
        o_ref[...] = jnp.zeros_like(o_ref)


def _moe_experts(xb, block_e, n_active, w_gate, w_up, w_down, tn_up=512, tn_down=512):
    rows, d = xb.shape
    bm = MOE_BLOCK
    n_blk = rows // bm
    d_ff = w_gate.shape[2]
    tn_up, tn_down = min(tn_up, d_ff), min(tn_down, d)
    act = pl.pallas_call(
        _moe_up_body,
        grid_spec=pltpu.PrefetchScalarGridSpec(
            num_scalar_prefetch=2, grid=(d_ff // tn_up, n_blk),
            in_specs=[pl.BlockSpec((bm, d), lambda j, b, be, na: (b, 0)),
                      pl.BlockSpec((None, d, tn_up), lambda j, b, be, na: (be[b], 0, j)),
                      pl.BlockSpec((None, d, tn_up), lambda j, b, be, na: (be[b], 0, j))],
            out_specs=pl.BlockSpec((bm, tn_up), lambda j, b, be, na: (b, j))),
        out_shape=jax.ShapeDtypeStruct((rows, d_ff), BF16),
        compiler_params=_cparams(2), name="moe_up")(block_e, n_active, xb, w_gate, w_up)
    return pl.pallas_call(
        _moe_down_body,
        grid_spec=pltpu.PrefetchScalarGridSpec(
            num_scalar_prefetch=2, grid=(d // tn_down, n_blk),
            in_specs=[pl.BlockSpec((bm, d_ff), lambda j, b, be, na: (b, 0)),
                      pl.BlockSpec((None, d_ff, tn_down), lambda j, b, be, na: (be[b], 0, j))],
            out_specs=pl.BlockSpec((bm, tn_down), lambda j, b, be, na: (b, j))),
        out_shape=jax.ShapeDtypeStruct((rows, d), F32),
        compiler_params=_cparams(2), name="moe_down")(block_e, n_active, act, w_down)


def _moe_combine_body(x_ref, y0_ref, y1_ref, g_ref, gt_ref, o_ref):
    g = g_ref[...]
    y = y0_ref[...] * g[:, 0:1] + y1_ref[...] * g[:, 1:2]
    o_ref[...] = x_ref[...] + gt_ref[...] * y


def _moe_combine(x, y0, y1, gates, mod5, layer, lay, tm=512):
    n, d = x.shape
    tm = min(tm, lay.s_len, lay.p_rows)
    row = pl.BlockSpec((tm, d), lambda i: (i, 0))
    return pl.pallas_call(
        _moe_combine_body, grid=(n // tm,),
        in_specs=[row, row, row, pl.BlockSpec((tm, TOP_K), lambda i: (i, 0)),
                  pl.BlockSpec((None, None, None, 1, d), _mod_spec(layer, 5, tm, lay))],
        out_specs=row, out_shape=jax.ShapeDtypeStruct((n, d), F32),
        compiler_params=_cparams(1), name="moe_combine")(x, y0, y1, gates, mod5)


def _moe(h, logits, w_gate, w_up, w_down):
    n_tok, d = h.shape
    bm = MOE_BLOCK
    top_val, top_idx = lax.top_k(logits, TOP_K)
    gates = jax.nn.softmax(top_val, axis=-1)
    n_assign = n_tok * TOP_K
    flat_e = top_idx.reshape(-1)
    flat_tok = jnp.arange(n_assign, dtype=jnp.int32) // TOP_K
    order = jnp.argsort(flat_e)
    sorted_e = flat_e[order]
    counts = jnp.bincount(flat_e, length=N_EXPERTS)
    padded = ((counts + bm - 1) // bm) * bm
    pad_end = jnp.cumsum(padded)
    pad_start = pad_end - padded
    start = jnp.cumsum(counts) - counts
    dest = (pad_start[sorted_e] + jnp.arange(n_assign, dtype=jnp.int32) - start[sorted_e]).astype(jnp.int32)
    n_blocks = -(-n_assign // bm) + N_EXPERTS
    slot_tok = jnp.full((n_blocks * bm,), n_tok, jnp.int32).at[dest].set(flat_tok[order])
    block_e = jnp.minimum(jnp.searchsorted(pad_end, jnp.arange(n_blocks) * bm, side='right'),
                          N_EXPERTS - 1).astype(jnp.int32)
    n_active = (pad_end[-1] // bm).astype(jnp.int32).reshape(1)
    h_pad = jnp.concatenate([h, jnp.zeros((1, d), h.dtype)], axis=0)
    xb = h_pad[slot_tok]
    yb = _moe_experts(xb, block_e, n_active, w_gate, w_up, w_down)
    slot_of = jnp.zeros((n_assign,), jnp.int32).at[order].set(dest).reshape(n_tok, TOP_K)
    return yb[slot_of[:, 0]], yb[slot_of[:, 1]], gates


def _final_norm_body(x_ref, w_ref, o_ref):
    x = x_ref[...]
    o_ref[...] = x * lax.rsqrt(jnp.mean(x * x, axis=-1, keepdims=True) + NORM_EPS) * w_ref[...]


def _final_norm(x, w, row0, n_rows, tm=512):
    d = x.shape[1]
    tm = min(tm, n_rows)
    rb0 = row0 // tm
    return pl.pallas_call(
        _final_norm_body, grid=(n_rows // tm,),
        in_specs=[pl.BlockSpec((tm, d), lambda i: (rb0 + i, 0)), pl.BlockSpec((1, d), lambda i: (0, 0))],
        out_specs=pl.BlockSpec((tm, d), lambda i: (i, 0)),
        out_shape=jax.ShapeDtypeStruct((n_rows, d), F32),
        compiler_params=_cparams(1), name="final_norm")(x, w.reshape(1, d))


def _rope_tables(seq_len, dim):
    t = jnp.arange(seq_len)
    row = (t // GRID_W).astype(F32)
    col = (t % GRID_W).astype(F32)
    half = dim // 2
    inv = ROPE_BASE ** (-(jnp.arange(0, half, 2, dtype=F32) / half))
    ang = jnp.concatenate([row[:, None] * inv, col[:, None] * inv], axis=-1)
    cos, sin = jnp.cos(ang), jnp.sin(ang)
    return jnp.repeat(cos, 2, axis=-1), jnp.stack([-sin, sin], axis=-1).reshape(seq_len, dim)


def _pad_cols(w, width):
    return jnp.pad(w, ((0, 0), (0, width - w.shape[1])))


def kernel(x_prompt, x_sample, state_l0_ret, state_l1_rwkv, c, c_ctx, ada_w, ada_b, norm_w, final_norm_w,
           l0_ret_w_in, l0_ret_w_out, l0_ret_decay, l0_ffn_w_gate, l0_ffn_w_up, l0_ffn_w_down,
           l1_rwkv_mu, l1_rwkv_w_rkv, l1_rwkv_w0, l1_rwkv_w1, l1_rwkv_w2, l1_rwkv_a0, l1_rwkv_a1, l1_rwkv_a2,
           l1_rwkv_g1, l1_rwkv_g2, l1_rwkv_k_k, l1_rwkv_k_a, l1_rwkv_r_k, l1_rwkv_ln_w, l1_rwkv_ln_b,
           l1_rwkv_w_out, l1_moe_router, l1_moe_w_gate, l1_moe_w_up, l1_moe_w_down):
    pb, p_len, d = x_prompt.shape
    sb, s_len, _ = x_sample.shape
    p_rows, s_rows = pb * p_len, sb * s_len
    lay = _Layout(p_rows, p_len, s_len, p_rows + s_rows)
    n_layers = ada_w.shape[0]

    x = jnp.concatenate([x_prompt.reshape(p_rows, d), x_sample.reshape(s_rows, d)], axis=0)
    cond8 = jnp.concatenate([c_ctx[None, :], c, jnp.zeros((8 - 1 - sb, d), F32)], axis=0)
    mod5 = _ada_mod(cond8, ada_w, ada_b)[:, :1 + sb].reshape(n_layers, 1 + sb, 6, 1, d)

    h = _norm_mod(x, norm_w[0, 0], mod5, 0, 1, 0, lay)
    qkvg = _linear(h, l0_ret_w_in)
    log_gamma = -jnp.exp(l0_ret_decay.astype(F32))
    dh = d // RET_HEADS
    o_p, new_state_l0_ret = _retention(qkvg, log_gamma, None, None, row0=0, n_seq=pb, seq_len=p_len,
                                       emit_state=True)
    (o_s,) = _retention(qkvg, log_gamma, state_l0_ret, _rope_tables(s_len, dh), row0=p_rows, n_seq=sb,
                        seq_len=s_len, emit_state=False)
    og = jnp.concatenate([_ret_finalize(o_p, qkvg, 0), _ret_finalize(o_s, qkvg, p_rows)], axis=0)
    x = _linear_residual(og, l0_ret_w_out, x, mod5, 0, 2, lay)
    h = _norm_mod(x, norm_w[0, 1], mod5, 0, 4, 3, lay)
    act = _swiglu_up(h, l0_ffn_w_gate, l0_ffn_w_up)
    x = _linear_residual(act, l0_ffn_w_down, x, mod5, 0, 5, lay, tn=256)

    xs = _rwkv_mix(x, norm_w[1, 0], mod5, 1, l1_rwkv_mu, lay)
    rkv = jnp.stack([_linear(xs, l1_rwkv_w_rkv, x_lead=n, w_lead=n) for n in range(3)], axis=0)
    lora_w = l1_rwkv_w1.shape[2]
    pad_w = -(-lora_w // 128) * 128
    w1p = jnp.concatenate([_pad_cols(l1_rwkv_w1[0], pad_w), _pad_cols(l1_rwkv_w1[1], pad_w)], axis=1)
    a1p = jnp.concatenate([_pad_cols(l1_rwkv_a1[0], pad_w), _pad_cols(l1_rwkv_a1[1], pad_w)], axis=1)
    t_w = _linear(xs, w1p, x_lead=3, act="tanh", out_dtype=BF16)
    t_a = _linear(xs, a1p, x_lead=4, out_dtype=BF16)
    t_g = _linear(xs, l1_rwkv_g1, x_lead=5, act="sigmoid", out_dtype=BF16)
    w2p = jnp.pad(l1_rwkv_w2, ((0, 0), (0, pad_w - lora_w), (0, 0)))
    a2p = jnp.pad(l1_rwkv_a2, ((0, 0), (0, pad_w - lora_w), (0, 0)))
    zw = jnp.stack([_linear(t_w[:, n * pad_w:(n + 1) * pad_w], w2p, w_lead=n) for n in range(2)], axis=0)
    za = jnp.stack([_linear(t_a[:, n * pad_w:(n + 1) * pad_w], a2p, w_lead=n) for n in range(2)], axis=0)
    g = _linear(t_g, l1_rwkv_g2)
    scan_args = (rkv, zw, za, l1_rwkv_w0, l1_rwkv_a0, l1_rwkv_k_k, l1_rwkv_k_a)
    y_p, new_state_l1_rwkv = _rwkv_scan(*scan_args, None, row0=0, n_seq=pb, seq_len=p_len, emit_state=True)
    (y_s,) = _rwkv_scan(*scan_args, state_l1_rwkv, row0=p_rows, n_seq=sb, seq_len=s_len, emit_state=False)
    y2 = jnp.concatenate([y_p, y_s], axis=1)
    yg = _rwkv_finalize(y2, rkv, za, g, l1_rwkv_a0, l1_rwkv_k_a, l1_rwkv_r_k, l1_rwkv_ln_w, l1_rwkv_ln_b)
    x = _linear_residual(yg, l1_rwkv_w_out, x, mod5, 1, 2, lay)
    ne_pad = 128
    h, logits = _norm_mod(x, norm_w[1, 1], mod5, 1, 4, 3, lay, router_w=_pad_cols(l1_moe_router, ne_pad))
    y0, y1, gates = _moe(h, logits[:, :N_EXPERTS], l1_moe_w_gate, l1_moe_w_up, l1_moe_w_down)
    x = _moe_combine(x, y0, y1, gates, mod5, 1, lay)

    y_prompt = _final_norm(x, final_norm_w, 0, p_rows).reshape(pb, p_len, d)
    y_sample = _final_norm(x, final_norm_w, p_rows, s_rows).reshape(sb, s_len, d)
    return (y_prompt, y_sample, new_state_l0_ret, new_state_l1_rwkv)
```

```python
import functools
from typing import NamedTuple

import jax
import jax.numpy as jnp
from jax import lax
from jax.experimental import pallas as pl
from jax.experimental.pallas import tpu as pltpu

F32 = jnp.float32
BF16 = jnp.bfloat16

GRID_W = 64
RET_HEADS = 8
RET_CHUNK = 128
ROPE_BASE = 10000.0
RWKV_HEAD_DIM = 64
RWKV_CHUNK = 64
RWKV_SLAB = 256
RWKV_LN_EPS = 64e-5
N_EXPERTS = 8
TOP_K = 2
MOE_BLOCK = 512
NORM_EPS = 1e-6

VMEM_LIMIT_BYTES = 56 * 1024 * 1024


class _Layout(NamedTuple):
    p_rows: int
    p_len: int
    s_len: int
    n_rows: int


def _cparams(n_axes):
    return pltpu.CompilerParams(dimension_semantics=("arbitrary",) * n_axes,
                                vmem_limit_bytes=VMEM_LIMIT_BYTES)


def _cond_of_tile(i, tm, lay):
    pt = lay.p_rows // tm
    st = lay.s_len // tm
    return jnp.where(i < pt, 0, 1 + (i - pt) // st)


def _bdot(a, b):
    return jnp.dot(a.astype(BF16), b.astype(BF16), preferred_element_type=F32)


def _bdot_nt(a, b):
    return lax.dot_general(a.astype(BF16), b.astype(BF16), (((1,), (1,)), ((), ())),
                           preferred_element_type=F32)


def _bdot_tn(a, b):
    return lax.dot_general(a.astype(BF16), b.astype(BF16), (((0,), (0,)), ((), ())),
                           preferred_element_type=F32)


def _silu(x):
    return x * jax.nn.sigmoid(x)


def _ada_body(c_ref, w_ref, b_ref, o_ref):
    o_ref[...] = _bdot(_silu(c_ref[...]), w_ref[...]) + b_ref[...]


def _ada_mod(cond8, ada_w, ada_b):
    n_layers, d, d6 = ada_w.shape
    tn = min(1024, d6)
    return pl.pallas_call(
        _ada_body,
        grid=(n_layers, d6 // tn),
        in_specs=[pl.BlockSpec((8, d), lambda l, j: (0, 0)),
                  pl.BlockSpec((None, d, tn), lambda l, j: (l, 0, j)),
                  pl.BlockSpec((None, 1, tn), lambda l, j: (l, 0, j))],
        out_specs=pl.BlockSpec((None, 8, tn), lambda l, j: (l, 0, j)),
        out_shape=jax.ShapeDtypeStruct((n_layers, 8, d6), F32),
        compiler_params=_cparams(2), name="ada_mod",
    )(cond8, ada_w, ada_b.reshape(n_layers, 1, d6))


def _mod_spec(layer, which, tm, lay, n_grid_axes=1):
    def imap(i, *_):
        return (layer, _cond_of_tile(i, tm, lay), which, 0, 0)
    return imap


def _rms_mod(x, nw, sc, sh):
    xn = x * lax.rsqrt(jnp.mean(x * x, axis=-1, keepdims=True) + NORM_EPS) * nw
    return xn * (1.0 + sc) + sh


def _norm_mod_body(x_ref, nw_ref, sc_ref, sh_ref, o_ref):
    o_ref[...] = _rms_mod(x_ref[...], nw_ref[...], sc_ref[...], sh_ref[...]).astype(o_ref.dtype)


def _norm_mod_router_body(x_ref, nw_ref, sc_ref, sh_ref, wr_ref, o_ref, lg_ref):
    h = _rms_mod(x_ref[...], nw_ref[...], sc_ref[...], sh_ref[...])
    o_ref[...] = h.astype(o_ref.dtype)
    lg_ref[...] = jnp.dot(h, wr_ref[...], preferred_element_type=F32, precision=lax.Precision.HIGHEST)


def _norm_mod(x, nw, mod5, layer, which_sc, which_sh, lay, router_w=None, tm=512):
    n, d = x.shape
    tm = min(tm, lay.s_len, lay.p_rows)
    in_specs = [pl.BlockSpec((tm, d), lambda i: (i, 0)),
                pl.BlockSpec((1, d), lambda i: (0, 0)),
                pl.BlockSpec((None, None, None, 1, d), _mod_spec(layer, which_sc, tm, lay)),
                pl.BlockSpec((None, None, None, 1, d), _mod_spec(layer, which_sh, tm, lay))]
    args = [x, nw.reshape(1, d), mod5, mod5]
    if router_w is None:
        return pl.pallas_call(
            _norm_mod_body, grid=(n // tm,), in_specs=in_specs,
            out_specs=pl.BlockSpec((tm, d), lambda i: (i, 0)),
            out_shape=jax.ShapeDtypeStruct((n, d), BF16),
            compiler_params=_cparams(1), name="norm_mod")(*args)
    ne = router_w.shape[1]
    return pl.pallas_call(
        _norm_mod_router_body, grid=(n // tm,),
        in_specs=in_specs + [pl.BlockSpec((d, ne), lambda i: (0, 0))],
        out_specs=[pl.BlockSpec((tm, d), lambda i: (i, 0)), pl.BlockSpec((tm, ne), lambda i: (i, 0))],
        out_shape=[jax.ShapeDtypeStruct((n, d), BF16), jax.ShapeDtypeStruct((n, ne), F32)],
        compiler_params=_cparams(1), name="norm_mod_router")(*args, router_w)


def _linear_body(x_ref, w_ref, o_ref, *, act):
    acc = _bdot(x_ref[...], w_ref[...])
    if act == "tanh":
        acc = jnp.tanh(acc)
    elif act == "sigmoid":
        acc = jax.nn.sigmoid(acc)
    o_ref[...] = acc.astype(o_ref.dtype)


def _linear(x, w, *, x_lead=None, w_lead=None, act=None, out_dtype=F32, tm=1024, tn=512):
    m, k = x.shape[-2:]
    n = w.shape[-1]
    tm, tn = min(tm, m), min(tn, n)
    if x_lead is None:
        x_spec = pl.BlockSpec((tm, k), lambda i, j: (i, 0))
    else:
        x_spec = pl.BlockSpec((None, tm, k), lambda i, j: (x_lead, i, 0))
    if w_lead is None:
        w_spec = pl.BlockSpec((k, tn), lambda i, j: (0, j))
    else:
        w_spec = pl.BlockSpec((None, k, tn), lambda i, j: (w_lead, 0, j))
    return pl.pallas_call(
        functools.partial(_linear_body, act=act),
        grid=(m // tm, n // tn), in_specs=[x_spec, w_spec],
        out_specs=pl.BlockSpec((tm, tn), lambda i, j: (i, j)),
        out_shape=jax.ShapeDtypeStruct((m, n), out_dtype),
        compiler_params=_cparams(2), name="linear")(x, w)


def _linear_res_body(x_ref, w_ref, res_ref, gt_ref, o_ref):
    o_ref[...] = res_ref[...] + gt_ref[...] * _bdot(x_ref[...], w_ref[...])


def _linear_residual(x, w, res, mod5, layer, which_gate, lay, tm=1024, tn=512):
    m, k = x.shape
    n = w.shape[1]
    tm, tn = min(tm, lay.s_len, lay.p_rows), min(tn, n)

    def gmap(i, j):
        return (layer, _cond_of_tile(i, tm, lay), which_gate, 0, j)

    return pl.pallas_call(
        _linear_res_body, grid=(m // tm, n // tn),
        in_specs=[pl.BlockSpec((tm, k), lambda i, j: (i, 0)),
                  pl.BlockSpec((k, tn), lambda i, j: (0, j)),
                  pl.BlockSpec((tm, tn), lambda i, j: (i, j)),
                  pl.BlockSpec((None, None, None, 1, tn), gmap)],
        out_specs=pl.BlockSpec((tm, tn), lambda i, j: (i, j)),
        out_shape=jax.ShapeDtypeStruct((m, n), F32),
        compiler_params=_cparams(2), name="linear_residual")(x, w, res, mod5)


def _swiglu_up_body(x_ref, wg_ref, wu_ref, o_ref):
    x = x_ref[...]
    o_ref[...] = (_silu(_bdot(x, wg_ref[...])) * _bdot(x, wu_ref[...])).astype(o_ref.dtype)


def _swiglu_up(x, wg, wu, tm=1024, tn=512):
    m, k = x.shape
    n = wg.shape[1]
    tm, tn = min(tm, m), min(tn, n)
    return pl.pallas_call(
        _swiglu_up_body, grid=(m // tm, n // tn),
        in_specs=[pl.BlockSpec((tm, k), lambda i, j: (i, 0)),
                  pl.BlockSpec((k, tn), lambda i, j: (0, j)),
                  pl.BlockSpec((k, tn), lambda i, j: (0, j))],
        out_specs=pl.BlockSpec((tm, tn), lambda i, j: (i, j)),
        out_shape=jax.ShapeDtypeStruct((m, n), BF16),
        compiler_params=_cparams(2), name="swiglu_up")(x, wg, wu)


def _rope(x, c, s):
    w = x.shape[-1]
    lane = lax.broadcasted_iota(jnp.int32, x.shape, x.ndim - 1)
    nxt = pltpu.roll(x, w - 1, axis=x.ndim - 1)
    prv = pltpu.roll(x, 1, axis=x.ndim - 1)
    return x * c + jnp.where(lane % 2 == 0, nxt, prv) * s


def _retention_body(lg_ref, *refs, n_chunk, n_blk, has_rope, has_s0, emit_state, scale):
    refs = list(refs)
    q_ref, k_ref, v_ref = refs[:3]
    refs = refs[3:]
    if has_rope:
        cos_ref, sin_ref = refs[:2]
        refs = refs[2:]
    if has_s0:
        s0_ref = refs.pop(0)
    o_ref = refs.pop(0)
    if emit_state:
        so_ref = refs.pop(0)
    st = refs.pop(0)
    c_sz = RET_CHUNK
    dh = q_ref.shape[-1]
    h, d, cb = pl.program_id(1), pl.program_id(2), pl.program_id(3)

    @pl.when(cb == 0)
    def _():
        st[...] = s0_ref[...] if has_s0 else jnp.zeros_like(st)

    lgv = lg_ref[d, h]
    fwd = d == 0
    row = lax.broadcasted_iota(jnp.int32, (c_sz, c_sz), 0)
    col = lax.broadcasted_iota(jnp.int32, (c_sz, c_sz), 1)
    diff = jnp.where(fwd, row - col, col - row).astype(F32)
    intra = jnp.where(diff >= 0, jnp.exp(jnp.maximum(diff, 0.0) * lgv), 0.0)
    pos = lax.broadcasted_iota(jnp.int32, (c_sz, dh), 0)
    npos = jnp.where(fwd, pos, c_sz - 1 - pos).astype(F32)
    q_decay = jnp.exp((npos + 1.0) * lgv)
    k_decay = jnp.exp((c_sz - 1.0 - npos) * lgv)
    chunk_decay = jnp.exp(jnp.full((1, dh), c_sz, F32) * lgv)

    for j in range(n_chunk):
        jj = jnp.where(fwd, j, n_chunk - 1 - j)
        rows = pl.ds(pl.multiple_of(jj * c_sz, c_sz), c_sz)
        q = q_ref[rows, :]
        k = k_ref[rows, :] * scale
        v = v_ref[rows, :]
        if has_rope:
            c, s = cos_ref[rows, :], sin_ref[rows, :]
            q, k = _rope(q, c, s), _rope(k, c, s)
        state = st[...]
        scores = _bdot_nt(q, k) * intra
        o_ref[rows, :] = _bdot(scores, v) + _bdot(q, state) * q_decay
        st[...] = state * chunk_decay + _bdot_tn(k * k_decay, v)

    if emit_state:
        @pl.when(cb == n_blk - 1)
        def _():
            so_ref[...] = st[...]


def _retention(qkvg, log_gamma, s0, rope, *, row0, n_seq, seq_len, emit_state):
    d_model = qkvg.shape[1] // 4
    n_heads = RET_HEADS
    dh = d_model // n_heads
    tb = min(512, seq_len)
    n_blk = seq_len // tb
    rb0 = row0 // tb

    def blk(c, d):
        return jnp.where(d == 0, c, n_blk - 1 - c)

    def in_map(part):
        return lambda s, h, d, c, lg: (rb0 + s * n_blk + blk(c, d), part * n_heads + h)

    in_specs = [pl.BlockSpec((tb, dh), in_map(p)) for p in range(3)]
    args = [qkvg, qkvg, qkvg]
    if rope is not None:
        in_specs += [pl.BlockSpec((tb, dh), lambda s, h, d, c, lg: (blk(c, d), 0))] * 2
        args += list(rope)
    if s0 is not None:
        in_specs.append(pl.BlockSpec((None, None, None, dh, dh), lambda s, h, d, c, lg: (s, d, h, 0, 0)))
        args.append(s0)
    n_out_rows = n_seq * seq_len
    out_specs = [pl.BlockSpec((None, tb, dh), lambda s, h, d, c, lg: (d, s * n_blk + blk(c, d), h))]
    out_shape = [jax.ShapeDtypeStruct((2, n_out_rows, d_model), F32)]
    if emit_state:
        out_specs.append(pl.BlockSpec((None, None, None, dh, dh), lambda s, h, d, c, lg: (s, d, h, 0, 0)))
        out_shape.append(jax.ShapeDtypeStruct((n_seq, 2, n_heads, dh, dh), F32))
    body = functools.partial(_retention_body, n_chunk=tb // RET_CHUNK, n_blk=n_blk, has_rope=rope is not None,
                             has_s0=s0 is not None, emit_state=emit_state, scale=dh ** -0.5)
    return pl.pallas_call(
        body,
        grid_spec=pltpu.PrefetchScalarGridSpec(
            num_scalar_prefetch=1, grid=(n_seq, n_heads, 2, n_blk), in_specs=in_specs, out_specs=out_specs,
            scratch_shapes=[pltpu.VMEM((dh, dh), F32)]),
        out_shape=out_shape, compiler_params=_cparams(4), name="retention")(log_gamma, *args)


def _ret_finalize_body(o_ref, g_ref, out_ref, *, n_heads):
    o = o_ref[0] + o_ref[1]
    g = g_ref[...]
    dh = o.shape[1] // n_heads
    for h in range(n_heads):
        sl = slice(h * dh, (h + 1) * dh)
        oh = o[:, sl]
        oh = oh * lax.rsqrt(jnp.mean(oh * oh, axis=-1, keepdims=True) + NORM_EPS)
        out_ref[:, sl] = (oh * _silu(g[:, sl])).astype(out_ref.dtype)


def _ret_finalize(o2, qkvg, row0, tm=256):
    _, n, d = o2.shape
    tm = min(tm, n)
    rb0 = row0 // tm
    return pl.pallas_call(
        functools.partial(_ret_finalize_body, n_heads=RET_HEADS), grid=(n // tm,),
        in_specs=[pl.BlockSpec((2, tm, d), lambda i: (0, i, 0)),
                  pl.BlockSpec((tm, d), lambda i: (rb0 + i, 3))],
        out_specs=pl.BlockSpec((tm, d), lambda i: (i, 0)),
        out_shape=jax.ShapeDtypeStruct((n, d), BF16),
        compiler_params=_cparams(1), name="ret_finalize")(o2, qkvg)


def _rwkv_mix_body(x_ref, xp_ref, xn_ref, nw_ref, sc_ref, sh_ref, mu_ref, o_ref, hext, *, tm, halo, lay):
    i = pl.program_id(0)
    d = x_ref.shape[1]
    nw, sc, sh = nw_ref[...], sc_ref[...], sh_ref[...]
    hext[0:halo, :] = _rms_mod(xp_ref[...], nw, sc, sh)
    hext[halo:halo + tm, :] = _rms_mod(x_ref[...], nw, sc, sh)
    hext[halo + tm:halo + tm + halo, :] = _rms_mod(xn_ref[...], nw, sc, sh)
    h = hext[halo:halo + tm, :]
    g_row = i * tm + lax.broadcasted_iota(jnp.int32, (tm, 1), 0)

    def emit(h_shift):
        diff = h_shift - h
        for n in range(6):
            o_ref[n] = (h + diff * mu_ref[n:n + 1, :]).astype(o_ref.dtype)

    def shifted(off, lo, hi, keep):
        return jnp.where(keep, hext[halo + off:halo + off + tm, lo:hi], 0.0)

    @pl.when(i < lay.p_rows // tm)
    def _():
        t = g_row % lay.p_len
        hd = d // 2
        emit(jnp.concatenate([shifted(-1, 0, hd, t != 0),
                              shifted(1, hd, d, t != lay.p_len - 1)], axis=1))

    @pl.when(i >= lay.p_rows // tm)
    def _():
        t = (g_row - lay.p_rows) % lay.s_len
        colw = t % GRID_W
        qd = d // 4
        emit(jnp.concatenate([shifted(-1, 0, qd, colw != 0),
                              shifted(1, qd, 2 * qd, colw != GRID_W - 1),
                              shifted(-GRID_W, 2 * qd, 3 * qd, t >= GRID_W),
                              shifted(GRID_W, 3 * qd, d, t < lay.s_len - GRID_W)], axis=1))


def _rwkv_mix(x, nw, mod5, layer, mu, lay, tm=512):
    n, d = x.shape
    halo = GRID_W
    tm = min(tm, lay.s_len, lay.p_rows)
    r = tm // halo
    n_halo_blk = n // halo
    body = functools.partial(_rwkv_mix_body, tm=tm, halo=halo, lay=lay)
    return pl.pallas_call(
        body, grid=(n // tm,),
        in_specs=[pl.BlockSpec((tm, d), lambda i: (i, 0)),
                  pl.BlockSpec((halo, d), lambda i: (jnp.maximum(i * r - 1, 0), 0)),
                  pl.BlockSpec((halo, d), lambda i: (jnp.minimum((i + 1) * r, n_halo_blk - 1), 0)),
                  pl.BlockSpec((1, d), lambda i: (0, 0)),
                  pl.BlockSpec((None, None, None, 1, d), _mod_spec(layer, 1, tm, lay)),
                  pl.BlockSpec((None, None, None, 1, d), _mod_spec(layer, 0, tm, lay)),
                  pl.BlockSpec((6, d), lambda i: (0, 0))],
        out_specs=pl.BlockSpec((6, tm, d), lambda i: (0, i, 0)),
        out_shape=jax.ShapeDtypeStruct((6, n, d), BF16),
        scratch_shapes=[pltpu.VMEM((tm + 2 * halo, d), F32)],
        compiler_params=_cparams(1), name="rwkv_mix")(x, x, x, nw.reshape(1, d), mod5, mod5, mu)


def _softplus(x):
    return jnp.maximum(x, 0.0) + jnp.log1p(jnp.exp(-jnp.abs(x)))


def _rwkv_scan_body(*refs, n_chunk, n_blk, has_s0, emit_state):
    refs = list(refs)
    r_ref, k_ref, v_ref, zw_ref, za_ref, w0_ref, a0_ref, kk_ref, ka_ref = refs[:9]
    refs = refs[9:]
    if has_s0:
        s0_ref = refs.pop(0)
    y_ref = refs.pop(0)
    if emit_state:
        so_ref = refs.pop(0)
    st = refs.pop(0)
    c_sz, hd, sw = RWKV_CHUNK, RWKV_HEAD_DIM, RWKV_SLAB
    n_grp = st.shape[0]
    hps = sw // hd
    d, cb = pl.program_id(1), pl.program_id(3)
    fwd = d == 0

    ri = lax.broadcasted_iota(jnp.int32, (sw, sw), 0)
    ci = lax.broadcasted_iota(jnp.int32, (sw, sw), 1)
    bd_mask = (ri // hd) == (ci // hd)

    def block_diag(slab):
        return jnp.where(bd_mask, jnp.concatenate([slab] * hps, axis=0), 0.0).astype(BF16)

    @pl.when(cb == 0)
    def _():
        if has_s0:
            st[...] = s0_ref[...]
        else:
            st[...] = jnp.zeros_like(st)

    srow = lax.broadcasted_iota(jnp.int32, (c_sz, sw), 0)
    scol = lax.broadcasted_iota(jnp.int32, (c_sz, sw), 1) % hd
    diff = jnp.where(fwd, srow - scol, scol - srow)
    strict = diff > 0
    incl = diff >= 0
    eye = jnp.where(diff == 0, 1.0, 0.0).astype(F32)
    levels = []
    m = 1
    while m < c_sz:
        levels.append(jnp.logical_and(srow // (2 * m) == scol // (2 * m), srow // m != scol // m))
        m *= 2
    trow = lax.broadcasted_iota(jnp.int32, (c_sz, c_sz), 0)
    tcol = lax.broadcasted_iota(jnp.int32, (c_sz, c_sz), 1)
    tri_incl = jnp.where(jnp.where(fwd, trow - tcol, tcol - trow) >= 0, 1.0, 0.0).astype(BF16)
    ones_bd = jnp.where(bd_mask, 1.0, 0.0).astype(BF16)

    def split3(x):
        hi = x.astype(BF16)
        r1 = x - hi.astype(F32)
        mid = r1.astype(BF16)
        return hi, mid, (r1 - mid.astype(F32)).astype(BF16)

    w0, a0, k_k, k_a = w0_ref[...], a0_ref[...], kk_ref[...], ka_ref[...]
    grp = range(n_grp)

    def chunk(j, carry):
        jj = jnp.where(fwd, j, n_chunk - 1 - j)
        rows = pl.ds(pl.multiple_of(jj * c_sz, c_sz), c_sz)
        lanes = [slice(g * sw, (g + 1) * sw) for g in grp]
        v = [v_ref[rows, sl] for sl in lanes]
        kkr = [k_ref[rows, sl] * k_k[:, sl] for sl in lanes]
        sq = [split3(x * x) for x in kkr]
        ssum = [sum(jnp.dot(p, ones_bd, preferred_element_type=F32) for p in s3) for s3 in sq]
        kk = [x * lax.rsqrt(s + 1e-12) for x, s in zip(kkr, ssum)]
        a = [jax.nn.sigmoid(a0[:, sl] + za_ref[rows, sl]) for sl in lanes]
        b = [x * y for x, y in zip(kk, a)]
        kdir = [k_ref[rows, sl] * (1.0 + (ai - 1.0) * k_a[:, sl]) for sl, ai in zip(lanes, a)]
        lw = [-jnp.exp(-_softplus(-(w0[:, sl] + zw_ref[rows, sl])) - 0.5) for sl in lanes]
        cum = [sum(jnp.dot(tri_incl, p, preferred_element_type=F32) for p in split3(x)) for x in lw]
        total = [jnp.sum(x, axis=0, keepdims=True) for x in lw]
        half = [0.5 * t for t in total]
        cumx = [c - x for c, x in zip(cum, lw)]
        r = [r_ref[rows, sl] for sl in lanes]
        lhs_g = [jnp.concatenate([kk[g] * jnp.exp(cumx[g] - half[g]), r[g] * jnp.exp(cum[g] - half[g])], axis=0)
                 for g in grp]
        e_neg = [jnp.exp(half[g] - cum[g]) for g in grp]
        g_k = [_bdot_nt(lhs_g[g], block_diag(kdir[g] * e_neg[g])) for g in grp]
        g_b = [_bdot_nt(lhs_g[g], block_diag(b[g] * e_neg[g])) for g in grp]
        l_k = [jnp.where(strict, x[:c_sz], 0.0) for x in g_k]
        a_rk = [jnp.where(incl, x[c_sz:], 0.0) for x in g_k]
        l_b = [jnp.where(strict, x[:c_sz], 0.0) for x in g_b]
        a_rb = [jnp.where(incl, x[c_sz:], 0.0) for x in g_b]
        x = [eye - jnp.where(levels[0], l, 0.0) for l in l_b]
        for lvl in levels[1:]:
            t = [_bdot(x[g], block_diag(jnp.where(lvl, l_b[g], 0.0))) for g in grp]
            x = [x[g] - _bdot(t[g], block_diag(x[g])) for g in grp]
        bd_v = [block_diag(x) for x in v]
        lkv = [_bdot(l_k[g], bd_v[g]) for g in grp]
        wt = [_bdot(x[g], block_diag(kk[g] * jnp.exp(cumx[g]))) for g in grp]
        vt = [_bdot(x[g], block_diag(lkv[g])) for g in grp]
        s_prev = [st[g] for g in grp]
        su = [_bdot_nt(jnp.concatenate([wt[g], r[g] * jnp.exp(cum[g])], axis=0), s_prev[g]) for g in grp]
        u = [su[g][:c_sz] + vt[g] for g in grp]
        y = [su[g][c_sz:] + _bdot(a_rk[g], bd_v[g]) - _bdot(a_rb[g], block_diag(u[g])) for g in grp]
        e_end = [jnp.exp(total[g] - cum[g]) for g in grp]
        upd = [_bdot_tn(jnp.concatenate([v[g], u[g]], axis=0),
                        jnp.concatenate([kdir[g] * e_end[g], -(b[g] * e_end[g])], axis=0)) for g in grp]
        st[...] = jnp.stack([s_prev[g] * jnp.exp(total[g]) + jnp.where(bd_mask, upd[g], 0.0) for g in grp])
        y_ref[rows, :] = jnp.concatenate(y, axis=1)
        return carry

    lax.fori_loop(0, n_chunk, chunk, 0)

    if emit_state:
        @pl.when(cb == n_blk - 1)
        def _():
            for g in grp:
                s = st[g]
                for h in range(hps):
                    so_ref[g * hps + h] = s[h * hd:(h + 1) * hd, h * hd:(h + 1) * hd]


def _rwkv_scan(rkv, zw, za, w0, a0, k_k, k_a, s0, *, row0, n_seq, seq_len, emit_state, heads_per_step=16):
    _, _, d_model = rkv.shape
    hd, sw = RWKV_HEAD_DIM, RWKV_SLAB
    n_heads = d_model // hd
    g = min(heads_per_step, n_heads)
    wg = g * hd
    n_grp = wg // sw
    n_hg = n_heads // g
    tb = min(256, seq_len)
    n_blk = seq_len // tb
    rb0 = row0 // tb

    def blk(c, d):
        return jnp.where(d == 0, c, n_blk - 1 - c)

    def rkv_map(part):
        return lambda s, d, hg, c: (part, rb0 + s * n_blk + blk(c, d), hg)

    dir_map = lambda s, d, hg, c: (d, rb0 + s * n_blk + blk(c, d), hg)
    par_map = lambda s, d, hg, c: (d, 0, hg)
    in_specs = ([pl.BlockSpec((None, tb, wg), rkv_map(p)) for p in range(3)]
                + [pl.BlockSpec((None, tb, wg), dir_map)] * 2
                + [pl.BlockSpec((None, 1, wg), par_map)] * 2
                + [pl.BlockSpec((1, wg), lambda s, d, hg, c: (0, hg))] * 2)
    args = [rkv, rkv, rkv, zw, za, w0.reshape(2, 1, d_model), a0.reshape(2, 1, d_model),
            k_k.reshape(1, d_model), k_a.reshape(1, d_model)]
    if s0 is not None:
        in_specs.append(pl.BlockSpec((None, None, n_grp, sw, sw), lambda s, d, hg, c: (s, d, hg, 0, 0)))
        args.append(s0)
    out_specs = [pl.BlockSpec((None, tb, wg), lambda s, d, hg, c: (d, s * n_blk + blk(c, d), hg))]
    out_shape = [jax.ShapeDtypeStruct((2, n_seq * seq_len, d_model), F32)]
    if emit_state:
        out_specs.append(pl.BlockSpec((None, None, g, hd, hd), lambda s, d, hg, c: (s, d, hg, 0, 0)))
        out_shape.append(jax.ShapeDtypeStruct((n_seq, 2, n_heads, hd, hd), F32))
    body = functools.partial(_rwkv_scan_body, n_chunk=tb // RWKV_CHUNK, n_blk=n_blk,
                             has_s0=s0 is not None, emit_state=emit_state)
    return pl.pallas_call(
        body, grid=(n_seq, 2, n_hg, n_blk), in_specs=in_specs, out_specs=out_specs, out_shape=out_shape,
        scratch_shapes=[pltpu.VMEM((n_grp, sw, sw), F32)],
        compiler_params=_cparams(4), name="rwkv_scan")(*args)


def _group_sum(x, ones_bd):
    w = ones_bd.shape[0]
    out = []
    for c in range(x.shape[1] // w):
        xs = x[:, c * w:(c + 1) * w]
        hi = xs.astype(BF16)
        lo = (xs - hi.astype(F32)).astype(BF16)
        out.append(jnp.dot(hi, ones_bd, preferred_element_type=F32)
                   + jnp.dot(lo, ones_bd, preferred_element_type=F32))
    return jnp.concatenate(out, axis=1)


def _rwkv_finalize_body(y_ref, rkv_ref, za_ref, g_ref, a0_ref, ka_ref, rk_ref, lnw_ref, lnb_ref, o_ref):
    hd = RWKV_HEAD_DIM
    w = 256
    ri = lax.broadcasted_iota(jnp.int32, (w, w), 0)
    ci = lax.broadcasted_iota(jnp.int32, (w, w), 1)
    ones_bd = jnp.where(ri // hd == ci // hd, 1.0, 0.0).astype(BF16)
    y = y_ref[0] + y_ref[1]
    mean = _group_sum(y, ones_bd) * (1.0 / hd)
    yc = y - mean
    var = _group_sum(yc * yc, ones_bd) * (1.0 / hd)
    yn = yc * lax.rsqrt(var + RWKV_LN_EPS) * lnw_ref[...] + lnb_ref[...]
    r, k, v = rkv_ref[0], rkv_ref[1], rkv_ref[2]
    a_sum = jax.nn.sigmoid(a0_ref[0] + za_ref[0]) + jax.nn.sigmoid(a0_ref[1] + za_ref[1])
    k_sum = k * (2.0 + (a_sum - 2.0) * ka_ref[...])
    bonus = _group_sum(r * k_sum * rk_ref[...], ones_bd) * v
    o_ref[...] = ((yn + bonus) * g_ref[...]).astype(o_ref.dtype)


def _rwkv_finalize(y2, rkv, za, g, a0, k_a, r_k, ln_w, ln_b, tm=256):
    _, n, d = y2.shape
    tm = min(tm, n)
    row = lambda i: (0, 0)
    return pl.pallas_call(
        _rwkv_finalize_body, grid=(n // tm,),
        in_specs=[pl.BlockSpec((2, tm, d), lambda i: (0, i, 0)),
                  pl.BlockSpec((3, tm, d), lambda i: (0, i, 0)),
                  pl.BlockSpec((2, tm, d), lambda i: (0, i, 0)),
                  pl.BlockSpec((tm, d), lambda i: (i, 0)),
                  pl.BlockSpec((2, 1, d), lambda i: (0, 0, 0)),
                  pl.BlockSpec((1, d), row), pl.BlockSpec((1, d), row),
                  pl.BlockSpec((1, d), row), pl.BlockSpec((1, d), row)],
        out_specs=pl.BlockSpec((tm, d), lambda i: (i, 0)),
        out_shape=jax.ShapeDtypeStruct((n, d), BF16),
        compiler_params=_cparams(1), name="rwkv_finalize",
    )(y2, rkv, za, g, a0.reshape(2, 1, d), k_a.reshape(1, d), r_k.reshape(1, d),
      ln_w.reshape(1, d), ln_b.reshape(1, d))


def _moe_up_body(be_ref, na_ref, x_ref, wg_ref, wu_ref, o_ref):
    b = pl.program_id(1)

    @pl.when(b < na_ref[0])
    def _():
        x = x_ref[...]
        o_ref[...] = (_silu(_bdot(x, wg_ref[...])) * _bdot(x, wu_ref[...])).astype(o_ref.dtype)

    @pl.when(b >= na_ref[0])
    def _():
        o_ref[...] = jnp.zeros_like(o_ref)


def _moe_down_body(be_ref, na_ref, x_ref, wd_ref, o_ref):
    b = pl.program_id(1)

    @pl.when(b < na_ref[0])
    def _():
        o_ref[...] = _bdot(x_ref[...], wd_ref[...])

    @pl.when(b >= na_ref[0])
    def _():
        o_ref[...] = jnp.zeros_like(o_ref)


def _moe_experts(xb, block_e, n_active, w_gate, w_up, w_down, tn_up=512, tn_down=512):
    rows, d = xb.shape
    bm = MOE_BLOCK
    n_blk = rows // bm
    d_ff = w_gate.shape[2]
    tn_up, tn_down = min(tn_up, d_ff), min(tn_down, d)
    act = pl.pallas_call(
        _moe_up_body,
        grid_spec=pltpu.PrefetchScalarGridSpec(
            num_scalar_prefetch=2, grid=(d_ff // tn_up, n_blk),
            in_specs=[pl.BlockSpec((bm, d), lambda j, b, be, na: (b, 0)),
                      pl.BlockSpec((None, d, tn_up), lambda j, b, be, na: (be[b], 0, j)),
                      pl.BlockSpec((None, d, tn_up), lambda j, b, be, na: (be[b], 0, j))],
            out_specs=pl.BlockSpec((bm, tn_up), lambda j, b, be, na: (b, j))),
        out_shape=jax.ShapeDtypeStruct((rows, d_ff), BF16),
        compiler_params=_cparams(2), name="moe_up")(block_e, n_active, xb, w_gate, w_up)
    return pl.pallas_call(
        _moe_down_body,
        grid_spec=pltpu.PrefetchScalarGridSpec(
            num_scalar_prefetch=2, grid=(d // tn_down, n_blk),
            in_specs=[pl.BlockSpec((bm, d_ff), lambda j, b, be, na: (b, 0)),
                      pl.BlockSpec((None, d_ff, tn_down), lambda j, b, be, na: (be[b], 0, j))],
            out_specs=pl.BlockSpec((bm, tn_down), lambda j, b, be, na: (b, j))),
        out_shape=jax.ShapeDtypeStruct((rows, d), F32),
        compiler_params=_cparams(2), name="moe_down")(block_e, n_active, act, w_down)


def _moe_combine_body(x_ref, y0_ref, y1_ref, g_ref, gt_ref, o_ref):
    g = g_ref[...]
    y = y0_ref[...] * g[:, 0:1] + y1_ref[...] * g[:, 1:2]
    o_ref[...] = x_ref[...] + gt_ref[...] * y


def _moe_combine(x, y0, y1, gates, mod5, layer, lay, tm=512):
    n, d = x.shape
    tm = min(tm, lay.s_len, lay.p_rows)
    row = pl.BlockSpec((tm, d), lambda i: (i, 0))
    return pl.pallas_call(
        _moe_combine_body, grid=(n // tm,),
        in_specs=[row, row, row, pl.BlockSpec((tm, TOP_K), lambda i: (i, 0)),
                  pl.BlockSpec((None, None, None, 1, d), _mod_spec(layer, 5, tm, lay))],
        out_specs=row, out_shape=jax.ShapeDtypeStruct((n, d), F32),
        compiler_params=_cparams(1), name="moe_combine")(x, y0, y1, gates, mod5)


def _moe(h, logits, w_gate, w_up, w_down):
    n_tok, d = h.shape
    bm = MOE_BLOCK
    top_val, top_idx = lax.top_k(logits, TOP_K)
    gates = jax.nn.softmax(top_val, axis=-1)
    n_assign = n_tok * TOP_K
    flat_e = top_idx.reshape(-1)
    flat_tok = jnp.arange(n_assign, dtype=jnp.int32) // TOP_K
    order = jnp.argsort(flat_e)
    sorted_e = flat_e[order]
    counts = jnp.bincount(flat_e, length=N_EXPERTS)
    padded = ((counts + bm - 1) // bm) * bm
    pad_end = jnp.cumsum(padded)
    pad_start = pad_end - padded
    start = jnp.cumsum(counts) - counts
    dest = (pad_start[sorted_e] + jnp.arange(n_assign, dtype=jnp.int32) - start[sorted_e]).astype(jnp.int32)
    n_blocks = -(-n_assign // bm) + N_EXPERTS
    slot_tok = jnp.full((n_blocks * bm,), n_tok, jnp.int32).at[dest].set(flat_tok[order])
    block_e = jnp.minimum(jnp.searchsorted(pad_end, jnp.arange(n_blocks) * bm, side='right'),
                          N_EXPERTS - 1).astype(jnp.int32)
    n_active = (pad_end[-1] // bm).astype(jnp.int32).reshape(1)
    h_pad = jnp.concatenate([h, jnp.zeros((1, d), h.dtype)], axis=0)
    xb = h_pad[slot_tok]
    yb = _moe_experts(xb, block_e, n_active, w_gate, w_up, w_down)
    slot_of = jnp.zeros((n_assign,), jnp.int32).at[order].set(dest).reshape(n_tok, TOP_K)
    return yb[slot_of[:, 0]], yb[slot_of[:, 1]], gates


def _final_norm_body(x_ref, w_ref, o_ref):
    x = x_ref[...]
    o_ref[...] = x * lax.rsqrt(jnp.mean(x * x, axis=-1, keepdims=True) + NORM_EPS) * w_ref[...]


def _final_norm(x, w, row0, n_rows, tm=512):
    d = x.shape[1]
    tm = min(tm, n_rows)
    rb0 = row0 // tm
    return pl.pallas_call(
        _final_norm_body, grid=(n_rows // tm,),
        in_specs=[pl.BlockSpec((tm, d), lambda i: (rb0 + i, 0)), pl.BlockSpec((1, d), lambda i: (0, 0))],
        out_specs=pl.BlockSpec((tm, d), lambda i: (i, 0)),
        out_shape=jax.ShapeDtypeStruct((n_rows, d), F32),
        compiler_params=_cparams(1), name="final_norm")(x, w.reshape(1, d))


def _rope_tables(seq_len, dim):
    t = jnp.arange(seq_len)
    row = (t // GRID_W).astype(F32)
    col = (t % GRID_W).astype(F32)
    half = dim // 2
    inv = ROPE_BASE ** (-(jnp.arange(0, half, 2, dtype=F32) / half))
    ang = jnp.concatenate([row[:, None] * inv, col[:, None] * inv], axis=-1)
    cos, sin = jnp.cos(ang), jnp.sin(ang)
    return jnp.repeat(cos, 2, axis=-1), jnp.stack([-sin, sin], axis=-1).reshape(seq_len, dim)


def _block_diag_states(s):
    b, two, n_heads, hd, _ = s.shape
    hps = RWKV_SLAB // hd
    s6 = s.reshape(b, two, n_heads // hps, hps, hd, hd)
    bd = jnp.einsum('bdghvk,hi->bdghvik', s6, jnp.eye(hps, dtype=s.dtype))
    return bd.reshape(b, two, n_heads // hps, RWKV_SLAB, RWKV_SLAB)


def _pad_cols(w, width):
    return jnp.pad(w, ((0, 0), (0, width - w.shape[1])))


def kernel(x_prompt, x_sample, state_l0_ret, state_l1_rwkv, c, c_ctx, ada_w, ada_b, norm_w, final_norm_w,
           l0_ret_w_in, l0_ret_w_out, l0_ret_decay, l0_ffn_w_gate, l0_ffn_w_up, l0_ffn_w_down,
           l1_rwkv_mu, l1_rwkv_w_rkv, l1_rwkv_w0, l1_rwkv_w1, l1_rwkv_w2, l1_rwkv_a0, l1_rwkv_a1, l1_rwkv_a2,
           l1_rwkv_g1, l1_rwkv_g2, l1_rwkv_k_k, l1_rwkv_k_a, l1_rwkv_r_k, l1_rwkv_ln_w, l1_rwkv_ln_b,
           l1_rwkv_w_out, l1_moe_router, l1_moe_w_gate, l1_moe_w_up, l1_moe_w_down):
    pb, p_len, d = x_prompt.shape
    sb, s_len, _ = x_sample.shape
    p_rows, s_rows = pb * p_len, sb * s_len
    lay = _Layout(p_rows, p_len, s_len, p_rows + s_rows)
    n_layers = ada_w.shape[0]

    x = jnp.concatenate([x_prompt.reshape(p_rows, d), x_sample.reshape(s_rows, d)], axis=0)
    cond8 = jnp.concatenate([c_ctx[None, :], c, jnp.zeros((8 - 1 - sb, d), F32)], axis=0)
    mod5 = _ada_mod(cond8, ada_w, ada_b)[:, :1 + sb].reshape(n_layers, 1 + sb, 6, 1, d)

    h = _norm_mod(x, norm_w[0, 0], mod5, 0, 1, 0, lay)
    qkvg = _linear(h, l0_ret_w_in)
    log_gamma = -jnp.exp(l0_ret_decay.astype(F32))
    dh = d // RET_HEADS
    o_p, new_state_l0_ret = _retention(qkvg, log_gamma, None, None, row0=0, n_seq=pb, seq_len=p_len,
                                       emit_state=True)
    (o_s,) = _retention(qkvg, log_gamma, state_l0_ret, _rope_tables(s_len, dh), row0=p_rows, n_seq=sb,
                        seq_len=s_len, emit_state=False)
    og = jnp.concatenate([_ret_finalize(o_p, qkvg, 0), _ret_finalize(o_s, qkvg, p_rows)], axis=0)
    x = _linear_residual(og, l0_ret_w_out, x, mod5, 0, 2, lay)
    h = _norm_mod(x, norm_w[0, 1], mod5, 0, 4, 3, lay)
    act = _swiglu_up(h, l0_ffn_w_gate, l0_ffn_w_up)
    x = _linear_residual(act, l0_ffn_w_down, x, mod5, 0, 5, lay, tn=256)

    xs = _rwkv_mix(x, norm_w[1, 0], mod5, 1, l1_rwkv_mu, lay)
    rkv = jnp.stack([_linear(xs, l1_rwkv_w_rkv, x_lead=n, w_lead=n) for n in range(3)], axis=0)
    lora_w = l1_rwkv_w1.shape[2]
    pad_w = -(-lora_w // 128) * 128
    w1p = jnp.concatenate([_pad_cols(l1_rwkv_w1[0], pad_w), _pad_cols(l1_rwkv_w1[1], pad_w)], axis=1)
    a1p = jnp.concatenate([_pad_cols(l1_rwkv_a1[0], pad_w), _pad_cols(l1_rwkv_a1[1], pad_w)], axis=1)
    t_w = _linear(xs, w1p, x_lead=3, act="tanh", out_dtype=BF16)
    t_a = _linear(xs, a1p, x_lead=4, out_dtype=BF16)
    t_g = _linear(xs, l1_rwkv_g1, x_lead=5, act="sigmoid", out_dtype=BF16)
    w2p = jnp.pad(l1_rwkv_w2, ((0, 0), (0, pad_w - lora_w), (0, 0)))
    a2p = jnp.pad(l1_rwkv_a2, ((0, 0), (0, pad_w - lora_w), (0, 0)))
    zw = jnp.stack([_linear(t_w[:, n * pad_w:(n + 1) * pad_w], w2p, w_lead=n) for n in range(2)], axis=0)
    za = jnp.stack([_linear(t_a[:, n * pad_w:(n + 1) * pad_w], a2p, w_lead=n) for n in range(2)], axis=0)
    g = _linear(t_g, l1_rwkv_g2)
    scan_args = (rkv, zw, za, l1_rwkv_w0, l1_rwkv_a0, l1_rwkv_k_k, l1_rwkv_k_a)
    y_p, new_state_l1_rwkv = _rwkv_scan(*scan_args, None, row0=0, n_seq=pb, seq_len=p_len, emit_state=True)
    (y_s,) = _rwkv_scan(*scan_args, _block_diag_states(state_l1_rwkv), row0=p_rows, n_seq=sb, seq_len=s_len,
                        emit_state=False)
    y2 = jnp.concatenate([y_p, y_s], axis=1)
    yg = _rwkv_finalize(y2, rkv, za, g, l1_rwkv_a0, l1_rwkv_k_a, l1_rwkv_r_k, l1_rwkv_ln_w, l1_rwkv_ln_b)
    x = _linear_residual(yg, l1_rwkv_w_out, x, mod5, 1, 2, lay)
    ne_pad = 128
    h, logits = _norm_mod(x, norm_w[1, 1], mod5, 1, 4, 3, lay, router_w=_pad_cols(l1_moe_router, ne_pad))
    y0, y1, gates = _moe(h, logits[:, :N_EXPERTS], l1_moe_w_gate, l1_moe_w_up, l1_moe_w_down)
    x = _moe_combine(x, y0, y1, gates, mod5, 1, lay)

    y_prompt = _final_norm(x, final_norm_w, 0, p_rows).reshape(pb, p_len, d)
    y_sample = _final_norm(x, final_norm_w, p_rows, s_rows).reshape(sb, s_len, d)
    return (y_prompt, y_sample, new_state_l0_ret, new_state_l1_rwkv)
```

```python
import functools
from typing import NamedTuple

import jax
import jax.numpy as jnp
from jax import lax
from jax.experimental import pallas as pl
from jax.experimental.pallas import tpu as pltpu

F32 = jnp.float32
BF16 = jnp.bfloat16

GRID_W = 64
RET_HEADS = 8
RET_CHUNK = 128
ROPE_BASE = 10000.0
RWKV_HEAD_DIM = 64
RWKV_CHUNK = 64
RWKV_SLAB = 256
RWKV_LN_EPS = 64e-5
N_EXPERTS = 8
TOP_K = 2
MOE_BLOCK = 512
NORM_EPS = 1e-6

VMEM_LIMIT_BYTES = 56 * 1024 * 1024


class _Layout(NamedTuple):
    p_rows: int
    p_len: int
    s_len: int
    n_rows: int


def _cparams(n_axes):
    return pltpu.CompilerParams(dimension_semantics=("arbitrary",) * n_axes,
                                vmem_limit_bytes=VMEM_LIMIT_BYTES)


def _cond_of_tile(i, tm, lay):
    pt = lay.p_rows // tm
    st = lay.s_len // tm
    return jnp.where(i < pt, 0, 1 + (i - pt) // st)


def _bdot(a, b):
    return jnp.dot(a.astype(BF16), b.astype(BF16), preferred_element_type=F32)


def _bdot_nt(a, b):
    return lax.dot_general(a.astype(BF16), b.astype(BF16), (((1,), (1,)), ((), ())),
                           preferred_element_type=F32)


def _bdot_tn(a, b):
    return lax.dot_general(a.astype(BF16), b.astype(BF16), (((0,), (0,)), ((), ())),
                           preferred_element_type=F32)


def _silu(x):
    return x * jax.nn.sigmoid(x)


def _ada_body(c_ref, w_ref, b_ref, o_ref):
    o_ref[...] = _bdot(_silu(c_ref[...]), w_ref[...]) + b_ref[...]


def _ada_mod(cond8, ada_w, ada_b):
    n_layers, d, d6 = ada_w.shape
    tn = min(1024, d6)
    return pl.pallas_call(
        _ada_body,
        grid=(n_layers, d6 // tn),
        in_specs=[pl.BlockSpec((8, d), lambda l, j: (0, 0)),
                  pl.BlockSpec((None, d, tn), lambda l, j: (l, 0, j)),
                  pl.BlockSpec((None, 1, tn), lambda l, j: (l, 0, j))],
        out_specs=pl.BlockSpec((None, 8, tn), lambda l, j: (l, 0, j)),
        out_shape=jax.ShapeDtypeStruct((n_layers, 8, d6), F32),
        compiler_params=_cparams(2), name="ada_mod",
    )(cond8, ada_w, ada_b.reshape(n_layers, 1, d6))


def _mod_spec(layer, which, tm, lay, n_grid_axes=1):
    def imap(i, *_):
        return (layer, _cond_of_tile(i, tm, lay), which, 0, 0)
    return imap


def _rms_mod(x, nw, sc, sh):
    xn = x * lax.rsqrt(jnp.mean(x * x, axis=-1, keepdims=True) + NORM_EPS) * nw
    return xn * (1.0 + sc) + sh


def _norm_mod_body(x_ref, nw_ref, sc_ref, sh_ref, o_ref):
    o_ref[...] = _rms_mod(x_ref[...], nw_ref[...], sc_ref[...], sh_ref[...]).astype(o_ref.dtype)


def _norm_mod_router_body(x_ref, nw_ref, sc_ref, sh_ref, wr_ref, o_ref, lg_ref):
    h = _rms_mod(x_ref[...], nw_ref[...], sc_ref[...], sh_ref[...])
    o_ref[...] = h.astype(o_ref.dtype)
    lg_ref[...] = jnp.dot(h, wr_ref[...], preferred_element_type=F32, precision=lax.Precision.HIGHEST)


def _norm_mod(x, nw, mod5, layer, which_sc, which_sh, lay, router_w=None, tm=512):
    n, d = x.shape
    tm = min(tm, lay.s_len, lay.p_rows)
    in_specs = [pl.BlockSpec((tm, d), lambda i: (i, 0)),
                pl.BlockSpec((1, d), lambda i: (0, 0)),
                pl.BlockSpec((None, None, None, 1, d), _mod_spec(layer, which_sc, tm, lay)),
                pl.BlockSpec((None, None, None, 1, d), _mod_spec(layer, which_sh, tm, lay))]
    args = [x, nw.reshape(1, d), mod5, mod5]
    if router_w is None:
        return pl.pallas_call(
            _norm_mod_body, grid=(n // tm,), in_specs=in_specs,
            out_specs=pl.BlockSpec((tm, d), lambda i: (i, 0)),
            out_shape=jax.ShapeDtypeStruct((n, d), BF16),
            compiler_params=_cparams(1), name="norm_mod")(*args)
    ne = router_w.shape[1]
    return pl.pallas_call(
        _norm_mod_router_body, grid=(n // tm,),
        in_specs=in_specs + [pl.BlockSpec((d, ne), lambda i: (0, 0))],
        out_specs=[pl.BlockSpec((tm, d), lambda i: (i, 0)), pl.BlockSpec((tm, ne), lambda i: (i, 0))],
        out_shape=[jax.ShapeDtypeStruct((n, d), BF16), jax.ShapeDtypeStruct((n, ne), F32)],
        compiler_params=_cparams(1), name="norm_mod_router")(*args, router_w)


def _linear_body(x_ref, w_ref, o_ref, *, act):
    acc = _bdot(x_ref[...], w_ref[...])
    if act == "tanh":
        acc = jnp.tanh(acc)
    elif act == "sigmoid":
        acc = jax.nn.sigmoid(acc)
    o_ref[...] = acc.astype(o_ref.dtype)


def _linear(x, w, *, n_batch=None, x_lead=0, act=None, out_dtype=F32, tm=1024, tn=512):
    k, n = w.shape[-2:]
    m = x.shape[-2]
    tm, tn = min(tm, m), min(tn, n)
    if x.ndim == 3:
        x_spec = pl.BlockSpec((None, tm, k), lambda b, i, j: (x_lead + b, i, 0))
    else:
        x_spec = pl.BlockSpec((tm, k), lambda b, i, j: (i, b))
    if w.ndim == 3:
        w_spec = pl.BlockSpec((None, k, tn), lambda b, i, j: (b, 0, j))
    else:
        w_spec = pl.BlockSpec((k, tn), lambda b, i, j: (0, j))
    if n_batch is None:
        out_spec = pl.BlockSpec((tm, tn), lambda b, i, j: (i, j))
        out_shape = jax.ShapeDtypeStruct((m, n), out_dtype)
    else:
        out_spec = pl.BlockSpec((None, tm, tn), lambda b, i, j: (b, i, j))
        out_shape = jax.ShapeDtypeStruct((n_batch, m, n), out_dtype)
    return pl.pallas_call(
        functools.partial(_linear_body, act=act),
        grid=(n_batch or 1, m // tm, n // tn), in_specs=[x_spec, w_spec],
        out_specs=out_spec, out_shape=out_shape,
        compiler_params=_cparams(3), name="linear")(x, w)


def _linear_res_body(x_ref, w_ref, res_ref, gt_ref, o_ref):
    o_ref[...] = res_ref[...] + gt_ref[...] * _bdot(x_ref[...], w_ref[...])


def _linear_residual(x, w, res, mod5, layer, which_gate, lay, tm=1024, tn=512):
    m, k = x.shape
    n = w.shape[1]
    tm, tn = min(tm, lay.s_len, lay.p_rows), min(tn, n)

    def gmap(i, j):
        return (layer, _cond_of_tile(i, tm, lay), which_gate, 0, j)

    return pl.pallas_call(
        _linear_res_body, grid=(m // tm, n // tn),
        in_specs=[pl.BlockSpec((tm, k), lambda i, j: (i, 0)),
                  pl.BlockSpec((k, tn), lambda i, j: (0, j)),
                  pl.BlockSpec((tm, tn), lambda i, j: (i, j)),
                  pl.BlockSpec((None, None, None, 1, tn), gmap)],
        out_specs=pl.BlockSpec((tm, tn), lambda i, j: (i, j)),
        out_shape=jax.ShapeDtypeStruct((m, n), F32),
        compiler_params=_cparams(2), name="linear_residual")(x, w, res, mod5)


def _swiglu_up_body(x_ref, wg_ref, wu_ref, o_ref):
    x = x_ref[...]
    o_ref[...] = (_silu(_bdot(x, wg_ref[...])) * _bdot(x, wu_ref[...])).astype(o_ref.dtype)


def _swiglu_up(x, wg, wu, tm=1024, tn=512):
    m, k = x.shape
    n = wg.shape[1]
    tm, tn = min(tm, m), min(tn, n)
    return pl.pallas_call(
        _swiglu_up_body, grid=(m // tm, n // tn),
        in_specs=[pl.BlockSpec((tm, k), lambda i, j: (i, 0)),
                  pl.BlockSpec((k, tn), lambda i, j: (0, j)),
                  pl.BlockSpec((k, tn), lambda i, j: (0, j))],
        out_specs=pl.BlockSpec((tm, tn), lambda i, j: (i, j)),
        out_shape=jax.ShapeDtypeStruct((m, n), BF16),
        compiler_params=_cparams(2), name="swiglu_up")(x, wg, wu)


def _rope(x, c, s):
    w = x.shape[-1]
    lane = lax.broadcasted_iota(jnp.int32, x.shape, x.ndim - 1)
    nxt = pltpu.roll(x, w - 1, axis=x.ndim - 1)
    prv = pltpu.roll(x, 1, axis=x.ndim - 1)
    return x * c + jnp.where(lane % 2 == 0, nxt, prv) * s


def _retention_body(lg_ref, *refs, n_chunk, n_blk, has_rope, has_s0, has_buf, emit_state, scale):
    refs = list(refs)
    q_ref, k_ref, v_ref = refs[:3]
    refs = refs[3:]
    if has_rope:
        cos_ref, sin_ref = refs[:2]
        refs = refs[2:]
    if has_s0:
        s0_ref = refs.pop(0)
    if has_buf:
        refs.pop(0)
    o_ref = refs.pop(0)
    if emit_state:
        so_ref = refs.pop(0)
    st = refs.pop(0)
    c_sz = RET_CHUNK
    dh = q_ref.shape[-1]
    h, d, cb = pl.program_id(1), pl.program_id(2), pl.program_id(3)

    @pl.when(cb == 0)
    def _():
        st[...] = s0_ref[...] if has_s0 else jnp.zeros_like(st)

    lgv = lg_ref[d, h]
    fwd = d == 0
    row = lax.broadcasted_iota(jnp.int32, (c_sz, c_sz), 0)
    col = lax.broadcasted_iota(jnp.int32, (c_sz, c_sz), 1)
    diff = jnp.where(fwd, row - col, col - row).astype(F32)
    intra = jnp.where(diff >= 0, jnp.exp(jnp.maximum(diff, 0.0) * lgv), 0.0)
    pos = lax.broadcasted_iota(jnp.int32, (c_sz, dh), 0)
    npos = jnp.where(fwd, pos, c_sz - 1 - pos).astype(F32)
    q_decay = jnp.exp((npos + 1.0) * lgv)
    k_decay = jnp.exp((c_sz - 1.0 - npos) * lgv)
    chunk_decay = jnp.exp(jnp.full((1, dh), c_sz, F32) * lgv)

    for j in range(n_chunk):
        jj = jnp.where(fwd, j, n_chunk - 1 - j)
        rows = pl.ds(pl.multiple_of(jj * c_sz, c_sz), c_sz)
        q = q_ref[rows, :]
        k = k_ref[rows, :] * scale
        v = v_ref[rows, :]
        if has_rope:
            c, s = cos_ref[rows, :], sin_ref[rows, :]
            q, k = _rope(q, c, s), _rope(k, c, s)
        state = st[...]
        scores = _bdot_nt(q, k) * intra
        o_ref[rows, :] = _bdot(scores, v) + _bdot(q, state) * q_decay
        st[...] = state * chunk_decay + _bdot_tn(k * k_decay, v)

    if emit_state:
        @pl.when(cb == n_blk - 1)
        def _():
            so_ref[...] = st[...]


def _retention(qkvg, log_gamma, s0, rope, out_buf, *, row0, n_seq, seq_len, emit_state):
    d_model = qkvg.shape[1] // 4
    n_heads = RET_HEADS
    dh = d_model // n_heads
    tb = min(512, seq_len)
    n_blk = seq_len // tb
    rb0 = row0 // tb

    def blk(c, d):
        return jnp.where(d == 0, c, n_blk - 1 - c)

    def in_map(part):
        return lambda s, h, d, c, lg: (rb0 + s * n_blk + blk(c, d), part * n_heads + h)

    in_specs = [pl.BlockSpec((tb, dh), in_map(p)) for p in range(3)]
    args = [qkvg, qkvg, qkvg]
    if rope is not None:
        in_specs += [pl.BlockSpec((tb, dh), lambda s, h, d, c, lg: (blk(c, d), 0))] * 2
        args += list(rope)
    if s0 is not None:
        in_specs.append(pl.BlockSpec((None, None, None, dh, dh), lambda s, h, d, c, lg: (s, d, h, 0, 0)))
        args.append(s0)
    aliases = {}
    if out_buf is not None:
        in_specs.append(pl.BlockSpec(memory_space=pl.ANY))
        args.append(out_buf)
        aliases = {len(args): 0}
    out_specs = [pl.BlockSpec((None, tb, dh), lambda s, h, d, c, lg: (d, rb0 + s * n_blk + blk(c, d), h))]
    out_shape = [jax.ShapeDtypeStruct((2, qkvg.shape[0], d_model), F32)]
    if emit_state:
        out_specs.append(pl.BlockSpec((None, None, None, dh, dh), lambda s, h, d, c, lg: (s, d, h, 0, 0)))
        out_shape.append(jax.ShapeDtypeStruct((n_seq, 2, n_heads, dh, dh), F32))
    body = functools.partial(_retention_body, n_chunk=tb // RET_CHUNK, n_blk=n_blk, has_rope=rope is not None,
                             has_s0=s0 is not None, has_buf=out_buf is not None, emit_state=emit_state,
                             scale=dh ** -0.5)
    return pl.pallas_call(
        body,
        grid_spec=pltpu.PrefetchScalarGridSpec(
            num_scalar_prefetch=1, grid=(n_seq, n_heads, 2, n_blk), in_specs=in_specs, out_specs=out_specs,
            scratch_shapes=[pltpu.VMEM((dh, dh), F32)]),
        out_shape=out_shape, input_output_aliases=aliases,
        compiler_params=_cparams(4), name="retention")(log_gamma, *args)


def _ret_finalize_body(o_ref, g_ref, out_ref, *, n_heads):
    o = o_ref[0] + o_ref[1]
    g = g_ref[...]
    dh = o.shape[1] // n_heads
    for h in range(n_heads):
        sl = slice(h * dh, (h + 1) * dh)
        oh = o[:, sl]
        oh = oh * lax.rsqrt(jnp.mean(oh * oh, axis=-1, keepdims=True) + NORM_EPS)
        out_ref[:, sl] = (oh * _silu(g[:, sl])).astype(out_ref.dtype)


def _ret_finalize(o2, qkvg, tm=256):
    _, n, d = o2.shape
    tm = min(tm, n)
    return pl.pallas_call(
        functools.partial(_ret_finalize_body, n_heads=RET_HEADS), grid=(n // tm,),
        in_specs=[pl.BlockSpec((2, tm, d), lambda i: (0, i, 0)),
                  pl.BlockSpec((tm, d), lambda i: (i, 3))],
        out_specs=pl.BlockSpec((tm, d), lambda i: (i, 0)),
        out_shape=jax.ShapeDtypeStruct((n, d), BF16),
        compiler_params=_cparams(1), name="ret_finalize")(o2, qkvg)


def _rwkv_mix_body(x_ref, xp_ref, xn_ref, nw_ref, sc_ref, sh_ref, mu_ref, o_ref, hext, *, tm, halo, lay):
    i = pl.program_id(0)
    d = x_ref.shape[1]
    nw, sc, sh = nw_ref[...], sc_ref[...], sh_ref[...]
    hext[0:halo, :] = _rms_mod(xp_ref[...], nw, sc, sh)
    hext[halo:halo + tm, :] = _rms_mod(x_ref[...], nw, sc, sh)
    hext[halo + tm:halo + tm + halo, :] = _rms_mod(xn_ref[...], nw, sc, sh)
    h = hext[halo:halo + tm, :]
    g_row = i * tm + lax.broadcasted_iota(jnp.int32, (tm, 1), 0)

    def emit(h_shift):
        diff = h_shift - h
        for n in range(6):
            o_ref[n] = (h + diff * mu_ref[n:n + 1, :]).astype(o_ref.dtype)

    def shifted(off, lo, hi, keep):
        return jnp.where(keep, hext[halo + off:halo + off + tm, lo:hi], 0.0)

    @pl.when(i < lay.p_rows // tm)
    def _():
        t = g_row % lay.p_len
        hd = d // 2
        emit(jnp.concatenate([shifted(-1, 0, hd, t != 0),
                              shifted(1, hd, d, t != lay.p_len - 1)], axis=1))

    @pl.when(i >= lay.p_rows // tm)
    def _():
        t = (g_row - lay.p_rows) % lay.s_len
        colw = t % GRID_W
        qd = d // 4
        emit(jnp.concatenate([shifted(-1, 0, qd, colw != 0),
                              shifted(1, qd, 2 * qd, colw != GRID_W - 1),
                              shifted(-GRID_W, 2 * qd, 3 * qd, t >= GRID_W),
                              shifted(GRID_W, 3 * qd, d, t < lay.s_len - GRID_W)], axis=1))


def _rwkv_mix(x, nw, mod5, layer, mu, lay, tm=512):
    n, d = x.shape
    halo = GRID_W
    tm = min(tm, lay.s_len, lay.p_rows)
    r = tm // halo
    n_halo_blk = n // halo
    body = functools.partial(_rwkv_mix_body, tm=tm, halo=halo, lay=lay)
    return pl.pallas_call(
        body, grid=(n // tm,),
        in_specs=[pl.BlockSpec((tm, d), lambda i: (i, 0)),
                  pl.BlockSpec((halo, d), lambda i: (jnp.maximum(i * r - 1, 0), 0)),
                  pl.BlockSpec((halo, d), lambda i: (jnp.minimum((i + 1) * r, n_halo_blk - 1), 0)),
                  pl.BlockSpec((1, d), lambda i: (0, 0)),
                  pl.BlockSpec((None, None, None, 1, d), _mod_spec(layer, 1, tm, lay)),
                  pl.BlockSpec((None, None, None, 1, d), _mod_spec(layer, 0, tm, lay)),
                  pl.BlockSpec((6, d), lambda i: (0, 0))],
        out_specs=pl.BlockSpec((6, tm, d), lambda i: (0, i, 0)),
        out_shape=jax.ShapeDtypeStruct((6, n, d), BF16),
        scratch_shapes=[pltpu.VMEM((tm + 2 * halo, d), F32)],
        compiler_params=_cparams(1), name="rwkv_mix")(x, x, x, nw.reshape(1, d), mod5, mod5, mu)


def _softplus(x):
    return jnp.maximum(x, 0.0) + jnp.log1p(jnp.exp(-jnp.abs(x)))


def _rwkv_scan_body(*refs, n_chunk, n_blk, has_s0, has_buf, emit_state):
    refs = list(refs)
    r_ref, k_ref, v_ref, zw_ref, za_ref, w0_ref, a0_ref, kk_ref, ka_ref = refs[:9]
    refs = refs[9:]
    if has_s0:
        s0_ref = refs.pop(0)
    if has_buf:
        refs.pop(0)
    y_ref = refs.pop(0)
    if emit_state:
        so_ref = refs.pop(0)
    st = refs.pop(0)
    c_sz, hd, sw = RWKV_CHUNK, RWKV_HEAD_DIM, RWKV_SLAB
    n_grp = st.shape[0]
    hps = sw // hd
    d, cb = pl.program_id(1), pl.program_id(3)
    fwd = d == 0

    ri = lax.broadcasted_iota(jnp.int32, (sw, sw), 0)
    ci = lax.broadcasted_iota(jnp.int32, (sw, sw), 1)
    bd_mask = (ri // hd) == (ci // hd)

    def block_diag(slab):
        return jnp.where(bd_mask, jnp.concatenate([slab] * hps, axis=0), 0.0).astype(BF16)

    @pl.when(cb == 0)
    def _():
        if has_s0:
            st[...] = s0_ref[...]
        else:
            st[...] = jnp.zeros_like(st)

    srow = lax.broadcasted_iota(jnp.int32, (c_sz, sw), 0)
    scol = lax.broadcasted_iota(jnp.int32, (c_sz, sw), 1) % hd
    diff = jnp.where(fwd, srow - scol, scol - srow)
    strict = diff > 0
    incl = diff >= 0
    eye = jnp.where(diff == 0, 1.0, 0.0).astype(F32)
    levels = []
    m = 1
    while m < c_sz:
        levels.append(jnp.logical_and(srow // (2 * m) == scol // (2 * m), srow // m != scol // m))
        m *= 2
    trow = lax.broadcasted_iota(jnp.int32, (c_sz, c_sz), 0)
    tcol = lax.broadcasted_iota(jnp.int32, (c_sz, c_sz), 1)
    tri_incl = jnp.where(jnp.where(fwd, trow - tcol, tcol - trow) >= 0, 1.0, 0.0).astype(BF16)
    ones_bd = jnp.where(bd_mask, 1.0, 0.0).astype(BF16)

    def split3(x):
        hi = x.astype(BF16)
        r1 = x - hi.astype(F32)
        mid = r1.astype(BF16)
        return hi, mid, (r1 - mid.astype(F32)).astype(BF16)

    w0, a0, k_k, k_a = w0_ref[...], a0_ref[...], kk_ref[...], ka_ref[...]
    grp = range(n_grp)

    def chunk(j, carry):
        jj = jnp.where(fwd, j, n_chunk - 1 - j)
        rows = pl.ds(pl.multiple_of(jj * c_sz, c_sz), c_sz)
        lanes = [slice(g * sw, (g + 1) * sw) for g in grp]
        v = [v_ref[rows, sl] for sl in lanes]
        kkr = [k_ref[rows, sl] * k_k[:, sl] for sl in lanes]
        sq = [split3(x * x) for x in kkr]
        ssum = [sum(jnp.dot(p, ones_bd, preferred_element_type=F32) for p in s3) for s3 in sq]
        kk = [x * lax.rsqrt(s + 1e-12) for x, s in zip(kkr, ssum)]
        a = [jax.nn.sigmoid(a0[:, sl] + za_ref[rows, sl]) for sl in lanes]
        b = [x * y for x, y in zip(kk, a)]
        kdir = [k_ref[rows, sl] * (1.0 + (ai - 1.0) * k_a[:, sl]) for sl, ai in zip(lanes, a)]
        lw = [-jnp.exp(-_softplus(-(w0[:, sl] + zw_ref[rows, sl])) - 0.5) for sl in lanes]
        cum = [sum(jnp.dot(tri_incl, p, preferred_element_type=F32) for p in split3(x)) for x in lw]
        total = [jnp.sum(x, axis=0, keepdims=True) for x in lw]
        half = [0.5 * t for t in total]
        cumx = [c - x for c, x in zip(cum, lw)]
        r = [r_ref[rows, sl] for sl in lanes]
        lhs_g = [jnp.concatenate([kk[g] * jnp.exp(cumx[g] - half[g]), r[g] * jnp.exp(cum[g] - half[g])], axis=0)
                 for g in grp]
        e_neg = [jnp.exp(half[g] - cum[g]) for g in grp]
        g_k = [_bdot_nt(lhs_g[g], block_diag(kdir[g] * e_neg[g])) for g in grp]
        g_b = [_bdot_nt(lhs_g[g], block_diag(b[g] * e_neg[g])) for g in grp]
        l_k = [jnp.where(strict, x[:c_sz], 0.0) for x in g_k]
        a_rk = [jnp.where(incl, x[c_sz:], 0.0) for x in g_k]
        l_b = [jnp.where(strict, x[:c_sz], 0.0) for x in g_b]
        a_rb = [jnp.where(incl, x[c_sz:], 0.0) for x in g_b]
        x = [eye - jnp.where(levels[0], l, 0.0) for l in l_b]
        for lvl in levels[1:]:
            t = [_bdot(x[g], block_diag(jnp.where(lvl, l_b[g], 0.0))) for g in grp]
            x = [x[g] - _bdot(t[g], block_diag(x[g])) for g in grp]
        bd_v = [block_diag(x) for x in v]
        lkv = [_bdot(l_k[g], bd_v[g]) for g in grp]
        wt = [_bdot(x[g], block_diag(kk[g] * jnp.exp(cumx[g]))) for g in grp]
        vt = [_bdot(x[g], block_diag(lkv[g])) for g in grp]
        s_prev = [st[g] for g in grp]
        su = [_bdot_nt(jnp.concatenate([wt[g], r[g] * jnp.exp(cum[g])], axis=0), s_prev[g]) for g in grp]
        u = [su[g][:c_sz] + vt[g] for g in grp]
        y = [su[g][c_sz:] + _bdot(a_rk[g], bd_v[g]) - _bdot(a_rb[g], block_diag(u[g])) for g in grp]
        e_end = [jnp.exp(total[g] - cum[g]) for g in grp]
        upd = [_bdot_tn(jnp.concatenate([v[g], u[g]], axis=0),
                        jnp.concatenate([kdir[g] * e_end[g], -(b[g] * e_end[g])], axis=0)) for g in grp]
        st[...] = jnp.stack([s_prev[g] * jnp.exp(total[g]) + jnp.where(bd_mask, upd[g], 0.0) for g in grp])
        y_ref[rows, :] = jnp.concatenate(y, axis=1)
        return carry

    lax.fori_loop(0, n_chunk, chunk, 0)

    if emit_state:
        @pl.when(cb == n_blk - 1)
        def _():
            for g in grp:
                s = st[g]
                for h in range(hps):
                    so_ref[g * hps + h] = s[h * hd:(h + 1) * hd, h * hd:(h + 1) * hd]


def _rwkv_scan(rkv, zw, za, w0, a0, k_k, k_a, s0, out_buf, *, row0, n_seq, seq_len, emit_state,
               heads_per_step=32):
    _, _, d_model = rkv.shape
    hd, sw = RWKV_HEAD_DIM, RWKV_SLAB
    n_heads = d_model // hd
    g = min(heads_per_step, n_heads)
    wg = g * hd
    n_grp = wg // sw
    n_hg = n_heads // g
    tb = min(256, seq_len)
    n_blk = seq_len // tb
    rb0 = row0 // tb

    def blk(c, d):
        return jnp.where(d == 0, c, n_blk - 1 - c)

    def rkv_map(part):
        return lambda s, d, hg, c: (part, rb0 + s * n_blk + blk(c, d), hg)

    dir_map = lambda s, d, hg, c: (d, rb0 + s * n_blk + blk(c, d), hg)
    par_map = lambda s, d, hg, c: (d, 0, hg)
    in_specs = ([pl.BlockSpec((None, tb, wg), rkv_map(p)) for p in range(3)]
                + [pl.BlockSpec((None, tb, wg), dir_map)] * 2
                + [pl.BlockSpec((None, 1, wg), par_map)] * 2
                + [pl.BlockSpec((1, wg), lambda s, d, hg, c: (0, hg))] * 2)
    args = [rkv, rkv, rkv, zw, za, w0.reshape(2, 1, d_model), a0.reshape(2, 1, d_model),
            k_k.reshape(1, d_model), k_a.reshape(1, d_model)]
    if s0 is not None:
        in_specs.append(pl.BlockSpec((None, None, n_grp, sw, sw), lambda s, d, hg, c: (s, d, hg, 0, 0)))
        args.append(s0)
    aliases = {}
    if out_buf is not None:
        in_specs.append(pl.BlockSpec(memory_space=pl.ANY))
        args.append(out_buf)
        aliases = {len(args) - 1: 0}
    out_specs = [pl.BlockSpec((None, tb, wg), lambda s, d, hg, c: (d, rb0 + s * n_blk + blk(c, d), hg))]
    out_shape = [jax.ShapeDtypeStruct((2, rkv.shape[1], d_model), F32)]
    if emit_state:
        out_specs.append(pl.BlockSpec((None, None, g, hd, hd), lambda s, d, hg, c: (s, d, hg, 0, 0)))
        out_shape.append(jax.ShapeDtypeStruct((n_seq, 2, n_heads, hd, hd), F32))
    body = functools.partial(_rwkv_scan_body, n_chunk=tb // RWKV_CHUNK, n_blk=n_blk,
                             has_s0=s0 is not None, has_buf=out_buf is not None, emit_state=emit_state)
    return pl.pallas_call(
        body, grid=(n_seq, 2, n_hg, n_blk), in_specs=in_specs, out_specs=out_specs, out_shape=out_shape,
        scratch_shapes=[pltpu.VMEM((n_grp, sw, sw), F32)], input_output_aliases=aliases,
        compiler_params=_cparams(4), name="rwkv_scan")(*args)


def _group_sum(x, ones_bd):
    w = ones_bd.shape[0]
    out = []
    for c in range(x.shape[1] // w):
        xs = x[:, c * w:(c + 1) * w]
        hi = xs.astype(BF16)
        lo = (xs - hi.astype(F32)).astype(BF16)
        out.append(jnp.dot(hi, ones_bd, preferred_element_type=F32)
                   + jnp.dot(lo, ones_bd, preferred_element_type=F32))
    return jnp.concatenate(out, axis=1)


def _rwkv_finalize_body(y_ref, rkv_ref, za_ref, g_ref, a0_ref, ka_ref, rk_ref, lnw_ref, lnb_ref, o_ref):
    hd = RWKV_HEAD_DIM
    w = 256
    ri = lax.broadcasted_iota(jnp.int32, (w, w), 0)
    ci = lax.broadcasted_iota(jnp.int32, (w, w), 1)
    ones_bd = jnp.where(ri // hd == ci // hd, 1.0, 0.0).astype(BF16)
    y = y_ref[0] + y_ref[1]
    mean = _group_sum(y, ones_bd) * (1.0 / hd)
    yc = y - mean
    var = _group_sum(yc * yc, ones_bd) * (1.0 / hd)
    yn = yc * lax.rsqrt(var + RWKV_LN_EPS) * lnw_ref[...] + lnb_ref[...]
    r, k, v = rkv_ref[0], rkv_ref[1], rkv_ref[2]
    a_sum = jax.nn.sigmoid(a0_ref[0] + za_ref[0]) + jax.nn.sigmoid(a0_ref[1] + za_ref[1])
    k_sum = k * (2.0 + (a_sum - 2.0) * ka_ref[...])
    bonus = _group_sum(r * k_sum * rk_ref[...], ones_bd) * v
    o_ref[...] = ((yn + bonus) * g_ref[...]).astype(o_ref.dtype)


def _rwkv_finalize(y2, rkv, za, g, a0, k_a, r_k, ln_w, ln_b, tm=256):
    _, n, d = y2.shape
    tm = min(tm, n)
    row = lambda i: (0, 0)
    return pl.pallas_call(
        _rwkv_finalize_body, grid=(n // tm,),
        in_specs=[pl.BlockSpec((2, tm, d), lambda i: (0, i, 0)),
                  pl.BlockSpec((3, tm, d), lambda i: (0, i, 0)),
                  pl.BlockSpec((2, tm, d), lambda i: (0, i, 0)),
                  pl.BlockSpec((tm, d), lambda i: (i, 0)),
                  pl.BlockSpec((2, 1, d), lambda i: (0, 0, 0)),
                  pl.BlockSpec((1, d), row), pl.BlockSpec((1, d), row),
                  pl.BlockSpec((1, d), row), pl.BlockSpec((1, d), row)],
        out_specs=pl.BlockSpec((tm, d), lambda i: (i, 0)),
        out_shape=jax.ShapeDtypeStruct((n, d), BF16),
        compiler_params=_cparams(1), name="rwkv_finalize",
    )(y2, rkv, za, g, a0.reshape(2, 1, d), k_a.reshape(1, d), r_k.reshape(1, d),
      ln_w.reshape(1, d), ln_b.reshape(1, d))


def _moe_up_body(be_ref, na_ref, x_ref, wg_ref, wu_ref, o_ref):
    b = pl.program_id(1)

    @pl.when(b < na_ref[0])
    def _():
        x = x_ref[...]
        o_ref[...] = (_silu(_bdot(x, wg_ref[...])) * _bdot(x, wu_ref[...])).astype(o_ref.dtype)

    @pl.when(b >= na_ref[0])
    def _():
        o_ref[...] = jnp.zeros_like(o_ref)


def _moe_down_body(be_ref, na_ref, x_ref, wd_ref, o_ref):
    b = pl.program_id(1)

    @pl.when(b < na_ref[0])
    def _():
        o_ref[...] = _bdot(x_ref[...], wd_ref[...])

    @pl.when(b >= na_ref[0])
    def _():
        o_ref[...] = jnp.zeros_like(o_ref)


def _moe_experts(xb, block_e, n_active, w_gate, w_up, w_down, tn_up=512, tn_down=512):
    rows, d = xb.shape
    bm = MOE_BLOCK
    n_blk = rows // bm
    d_ff = w_gate.shape[2]
    tn_up, tn_down = min(tn_up, d_ff), min(tn_down, d)
    act = pl.pallas_call(
        _moe_up_body,
        grid_spec=pltpu.PrefetchScalarGridSpec(
            num_scalar_prefetch=2, grid=(d_ff // tn_up, n_blk),
            in_specs=[pl.BlockSpec((bm, d), lambda j, b, be, na: (b, 0)),
                      pl.BlockSpec((None, d, tn_up), lambda j, b, be, na: (be[b], 0, j)),
                      pl.BlockSpec((None, d, tn_up), lambda j, b, be, na: (be[b], 0, j))],
            out_specs=pl.BlockSpec((bm, tn_up), lambda j, b, be, na: (b, j))),
        out_shape=jax.ShapeDtypeStruct((rows, d_ff), BF16),
        compiler_params=_cparams(2), name="moe_up")(block_e, n_active, xb, w_gate, w_up)
    return pl.pallas_call(
        _moe_down_body,
        grid_spec=pltpu.PrefetchScalarGridSpec(
            num_scalar_prefetch=2, grid=(d // tn_down, n_blk),
            in_specs=[pl.BlockSpec((bm, d_ff), lambda j, b, be, na: (b, 0)),
                      pl.BlockSpec((None, d_ff, tn_down), lambda j, b, be, na: (be[b], 0, j))],
            out_specs=pl.BlockSpec((bm, tn_down), lambda j, b, be, na: (b, j))),
        out_shape=jax.ShapeDtypeStruct((rows, d), F32),
        compiler_params=_cparams(2), name="moe_down")(block_e, n_active, act, w_down)


def _moe_combine_body(x_ref, y0_ref, y1_ref, g_ref, gt_ref, o_ref):
    g = g_ref[...]
    y = y0_ref[...] * g[:, 0:1] + y1_ref[...] * g[:, 1:2]
    o_ref[...] = x_ref[...] + gt_ref[...] * y


def _moe_combine(x, y0, y1, gates, mod5, layer, lay, tm=512):
    n, d = x.shape
    tm = min(tm, lay.s_len, lay.p_rows)
    row = pl.BlockSpec((tm, d), lambda i: (i, 0))
    return pl.pallas_call(
        _moe_combine_body, grid=(n // tm,),
        in_specs=[row, row, row, pl.BlockSpec((tm, TOP_K), lambda i: (i, 0)),
                  pl.BlockSpec((None, None, None, 1, d), _mod_spec(layer, 5, tm, lay))],
        out_specs=row, out_shape=jax.ShapeDtypeStruct((n, d), F32),
        compiler_params=_cparams(1), name="moe_combine")(x, y0, y1, gates, mod5)


def _moe(h, logits, w_gate, w_up, w_down):
    n_tok, d = h.shape
    bm = MOE_BLOCK
    top_val, top_idx = lax.top_k(logits, TOP_K)
    gates = jax.nn.softmax(top_val, axis=-1)
    n_assign = n_tok * TOP_K
    flat_e = top_idx.reshape(-1)
    flat_tok = jnp.arange(n_assign, dtype=jnp.int32) // TOP_K
    order = jnp.argsort(flat_e)
    sorted_e = flat_e[order]
    counts = jnp.bincount(flat_e, length=N_EXPERTS)
    padded = ((counts + bm - 1) // bm) * bm
    pad_end = jnp.cumsum(padded)
    pad_start = pad_end - padded
    start = jnp.cumsum(counts) - counts
    dest = (pad_start[sorted_e] + jnp.arange(n_assign, dtype=jnp.int32) - start[sorted_e]).astype(jnp.int32)
    n_blocks = -(-n_assign // bm) + N_EXPERTS
    slot_tok = jnp.full((n_blocks * bm,), n_tok, jnp.int32).at[dest].set(flat_tok[order])
    block_e = jnp.minimum(jnp.searchsorted(pad_end, jnp.arange(n_blocks) * bm, side='right'),
                          N_EXPERTS - 1).astype(jnp.int32)
    n_active = (pad_end[-1] // bm).astype(jnp.int32).reshape(1)
    h_pad = jnp.concatenate([h, jnp.zeros((1, d), h.dtype)], axis=0)
    xb = h_pad[slot_tok]
    yb = _moe_experts(xb, block_e, n_active, w_gate, w_up, w_down)
    slot_of = jnp.zeros((n_assign,), jnp.int32).at[order].set(dest).reshape(n_tok, TOP_K)
    return yb[slot_of[:, 0]], yb[slot_of[:, 1]], gates


def _final_norm_body(x_ref, w_ref, o_ref):
    x = x_ref[...]
    o_ref[...] = x * lax.rsqrt(jnp.mean(x * x, axis=-1, keepdims=True) + NORM_EPS) * w_ref[...]


def _final_norm(x, w, row0, n_rows, tm=512):
    d = x.shape[1]
    tm = min(tm, n_rows)
    rb0 = row0 // tm
    return pl.pallas_call(
        _final_norm_body, grid=(n_rows // tm,),
        in_specs=[pl.BlockSpec((tm, d), lambda i: (rb0 + i, 0)), pl.BlockSpec((1, d), lambda i: (0, 0))],
        out_specs=pl.BlockSpec((tm, d), lambda i: (i, 0)),
        out_shape=jax.ShapeDtypeStruct((n_rows, d), F32),
        compiler_params=_cparams(1), name="final_norm")(x, w.reshape(1, d))


def _rope_tables(seq_len, dim):
    t = jnp.arange(seq_len)
    row = (t // GRID_W).astype(F32)
    col = (t % GRID_W).astype(F32)
    half = dim // 2
    inv = ROPE_BASE ** (-(jnp.arange(0, half, 2, dtype=F32) / half))
    ang = jnp.concatenate([row[:, None] * inv, col[:, None] * inv], axis=-1)
    cos, sin = jnp.cos(ang), jnp.sin(ang)
    return jnp.repeat(cos, 2, axis=-1), jnp.stack([-sin, sin], axis=-1).reshape(seq_len, dim)


def _block_diag_states(s):
    b, two, n_heads, hd, _ = s.shape
    hps = RWKV_SLAB // hd
    s6 = s.reshape(b, two, n_heads // hps, hps, hd, hd)
    bd = jnp.einsum('bdghvk,hi->bdghvik', s6, jnp.eye(hps, dtype=s.dtype))
    return bd.reshape(b, two, n_heads // hps, RWKV_SLAB, RWKV_SLAB)


def _pad_cols(w, width):
    return jnp.pad(w, ((0, 0), (0, width - w.shape[1])))


def kernel(x_prompt, x_sample, state_l0_ret, state_l1_rwkv, c, c_ctx, ada_w, ada_b, norm_w, final_norm_w,
           l0_ret_w_in, l0_ret_w_out, l0_ret_decay, l0_ffn_w_gate, l0_ffn_w_up, l0_ffn_w_down,
           l1_rwkv_mu, l1_rwkv_w_rkv, l1_rwkv_w0, l1_rwkv_w1, l1_rwkv_w2, l1_rwkv_a0, l1_rwkv_a1, l1_rwkv_a2,
           l1_rwkv_g1, l1_rwkv_g2, l1_rwkv_k_k, l1_rwkv_k_a, l1_rwkv_r_k, l1_rwkv_ln_w, l1_rwkv_ln_b,
           l1_rwkv_w_out, l1_moe_router, l1_moe_w_gate, l1_moe_w_up, l1_moe_w_down):
    pb, p_len, d = x_prompt.shape
    sb, s_len, _ = x_sample.shape
    p_rows, s_rows = pb * p_len, sb * s_len
    lay = _Layout(p_rows, p_len, s_len, p_rows + s_rows)
    n_layers = ada_w.shape[0]

    x = jnp.concatenate([x_prompt.reshape(p_rows, d), x_sample.reshape(s_rows, d)], axis=0)
    cond8 = jnp.concatenate([c_ctx[None, :], c, jnp.zeros((8 - 1 - sb, d), F32)], axis=0)
    mod5 = _ada_mod(cond8, ada_w, ada_b)[:, :1 + sb].reshape(n_layers, 1 + sb, 6, 1, d)

    h = _norm_mod(x, norm_w[0, 0], mod5, 0, 1, 0, lay)
    qkvg = _linear(h, l0_ret_w_in)
    log_gamma = -jnp.exp(l0_ret_decay.astype(F32))
    dh = d // RET_HEADS
    o2, new_state_l0_ret = _retention(qkvg, log_gamma, None, None, None, row0=0, n_seq=pb, seq_len=p_len,
                                      emit_state=True)
    (o2,) = _retention(qkvg, log_gamma, state_l0_ret, _rope_tables(s_len, dh), o2, row0=p_rows, n_seq=sb,
                       seq_len=s_len, emit_state=False)
    x = _linear_residual(_ret_finalize(o2, qkvg), l0_ret_w_out, x, mod5, 0, 2, lay)
    h = _norm_mod(x, norm_w[0, 1], mod5, 0, 4, 3, lay)
    act = _swiglu_up(h, l0_ffn_w_gate, l0_ffn_w_up)
    x = _linear_residual(act, l0_ffn_w_down, x, mod5, 0, 5, lay, tn=256)

    xs = _rwkv_mix(x, norm_w[1, 0], mod5, 1, l1_rwkv_mu, lay)
    rkv = _linear(xs, l1_rwkv_w_rkv, n_batch=3)
    lora_w = l1_rwkv_w1.shape[2]
    pad_w = -(-lora_w // 128) * 128
    w1p = jnp.concatenate([_pad_cols(l1_rwkv_w1[0], pad_w), _pad_cols(l1_rwkv_w1[1], pad_w)], axis=1)
    a1p = jnp.concatenate([_pad_cols(l1_rwkv_a1[0], pad_w), _pad_cols(l1_rwkv_a1[1], pad_w)], axis=1)
    t_w = _linear(xs, w1p, x_lead=3, act="tanh", out_dtype=BF16)
    t_a = _linear(xs, a1p, x_lead=4, out_dtype=BF16)
    t_g = _linear(xs, l1_rwkv_g1, x_lead=5, act="sigmoid", out_dtype=BF16)
    w2p = jnp.pad(l1_rwkv_w2, ((0, 0), (0, pad_w - lora_w), (0, 0)))
    a2p = jnp.pad(l1_rwkv_a2, ((0, 0), (0, pad_w - lora_w), (0, 0)))
    zw = _linear(t_w, w2p, n_batch=2)
    za = _linear(t_a, a2p, n_batch=2)
    g = _linear(t_g, l1_rwkv_g2)
    scan_args = (rkv, zw, za, l1_rwkv_w0, l1_rwkv_a0, l1_rwkv_k_k, l1_rwkv_k_a)
    y2, new_state_l1_rwkv = _rwkv_scan(*scan_args, None, None, row0=0, n_seq=pb, seq_len=p_len, emit_state=True)
    (y2,) = _rwkv_scan(*scan_args, _block_diag_states(state_l1_rwkv), y2, row0=p_rows, n_seq=sb, seq_len=s_len,
                       emit_state=False)
    yg = _rwkv_finalize(y2, rkv, za, g, l1_rwkv_a0, l1_rwkv_k_a, l1_rwkv_r_k, l1_rwkv_ln_w, l1_rwkv_ln_b)
    x = _linear_residual(yg, l1_rwkv_w_out, x, mod5, 1, 2, lay)
    ne_pad = 128
    h, logits = _norm_mod(x, norm_w[1, 1], mod5, 1, 4, 3, lay, router_w=_pad_cols(l1_moe_router, ne_pad))
    y0, y1, gates = _moe(h, logits[:, :N_EXPERTS], l1_moe_w_gate, l1_moe_w_up, l1_moe_w_down)
    x = _moe_combine(x, y0, y1, gates, mod5, 1, lay)

    y_prompt = _final_norm(x, final_norm_w, 0, p_rows).reshape(pb, p_len, d)
    y_sample = _final_norm(x, final_norm_w, p_rows, s_rows).reshape(sb, s_len, d)
    return (y_prompt, y_sample, new_state_l0_ret, new_state_l1_rwkv)
```

```python
import functools
from typing import NamedTuple

import jax
import jax.numpy as jnp
from jax import lax
from jax.experimental import pallas as pl
from jax.experimental.pallas import tpu as pltpu

F32 = jnp.float32
BF16 = jnp.bfloat16

GRID_W = 64
RET_HEADS = 8
RET_CHUNK = 128
ROPE_BASE = 10000.0
RWKV_HEAD_DIM = 64
RWKV_CHUNK = 64
RWKV_SLAB = 256
RWKV_LN_EPS = 64e-5
N_EXPERTS = 8
TOP_K = 2
MOE_BLOCK = 512
NORM_EPS = 1e-6

VMEM_LIMIT_BYTES = 56 * 1024 * 1024


class _Layout(NamedTuple):
    p_rows: int
    p_len: int
    s_len: int
    n_rows: int


def _cparams(n_axes):
    return pltpu.CompilerParams(dimension_semantics=("arbitrary",) * n_axes,
                                vmem_limit_bytes=VMEM_LIMIT_BYTES)


def _cond_of_tile(i, tm, lay):
    pt = lay.p_rows // tm
    st = lay.s_len // tm
    return jnp.where(i < pt, 0, 1 + (i - pt) // st)


def _bdot(a, b):
    return jnp.dot(a.astype(BF16), b.astype(BF16), preferred_element_type=F32)


def _bdot_nt(a, b):
    return lax.dot_general(a.astype(BF16), b.astype(BF16), (((1,), (1,)), ((), ())),
                           preferred_element_type=F32)


def _bdot_tn(a, b):
    return lax.dot_general(a.astype(BF16), b.astype(BF16), (((0,), (0,)), ((), ())),
                           preferred_element_type=F32)


def _silu(x):
    return x * jax.nn.sigmoid(x)


def _ada_body(c_ref, w_ref, b_ref, o_ref):
    o_ref[...] = _bdot(_silu(c_ref[...]), w_ref[...]) + b_ref[...]


def _ada_mod(cond8, ada_w, ada_b):
    n_layers, d, d6 = ada_w.shape
    tn = min(1024, d6)
    return pl.pallas_call(
        _ada_body,
        grid=(n_layers, d6 // tn),
        in_specs=[pl.BlockSpec((8, d), lambda l, j: (0, 0)),
                  pl.BlockSpec((None, d, tn), lambda l, j: (l, 0, j)),
                  pl.BlockSpec((None, 1, tn), lambda l, j: (l, 0, j))],
        out_specs=pl.BlockSpec((None, 8, tn), lambda l, j: (l, 0, j)),
        out_shape=jax.ShapeDtypeStruct((n_layers, 8, d6), F32),
        compiler_params=_cparams(2), name="ada_mod",
    )(cond8, ada_w, ada_b.reshape(n_layers, 1, d6))


def _mod_spec(layer, which, tm, lay, n_grid_axes=1):
    def imap(i, *_):
        return (layer, _cond_of_tile(i, tm, lay), which, 0, 0)
    return imap


def _rms_mod(x, nw, sc, sh):
    xn = x * lax.rsqrt(jnp.mean(x * x, axis=-1, keepdims=True) + NORM_EPS) * nw
    return xn * (1.0 + sc) + sh


def _norm_mod_body(x_ref, nw_ref, sc_ref, sh_ref, o_ref):
    o_ref[...] = _rms_mod(x_ref[...], nw_ref[...], sc_ref[...], sh_ref[...]).astype(o_ref.dtype)


def _norm_mod_router_body(x_ref, nw_ref, sc_ref, sh_ref, wr_ref, o_ref, lg_ref):
    h = _rms_mod(x_ref[...], nw_ref[...], sc_ref[...], sh_ref[...])
    o_ref[...] = h.astype(o_ref.dtype)
    lg_ref[...] = jnp.dot(h, wr_ref[...], preferred_element_type=F32, precision=lax.Precision.HIGHEST)


def _norm_mod(x, nw, mod5, layer, which_sc, which_sh, lay, router_w=None, tm=512):
    n, d = x.shape
    tm = min(tm, lay.s_len, lay.p_rows)
    in_specs = [pl.BlockSpec((tm, d), lambda i: (i, 0)),
                pl.BlockSpec((1, d), lambda i: (0, 0)),
                pl.BlockSpec((None, None, None, 1, d), _mod_spec(layer, which_sc, tm, lay)),
                pl.BlockSpec((None, None, None, 1, d), _mod_spec(layer, which_sh, tm, lay))]
    args = [x, nw.reshape(1, d), mod5, mod5]
    if router_w is None:
        return pl.pallas_call(
            _norm_mod_body, grid=(n // tm,), in_specs=in_specs,
            out_specs=pl.BlockSpec((tm, d), lambda i: (i, 0)),
            out_shape=jax.ShapeDtypeStruct((n, d), BF16),
            compiler_params=_cparams(1), name="norm_mod")(*args)
    ne = router_w.shape[1]
    return pl.pallas_call(
        _norm_mod_router_body, grid=(n // tm,),
        in_specs=in_specs + [pl.BlockSpec((d, ne), lambda i: (0, 0))],
        out_specs=[pl.BlockSpec((tm, d), lambda i: (i, 0)), pl.BlockSpec((tm, ne), lambda i: (i, 0))],
        out_shape=[jax.ShapeDtypeStruct((n, d), BF16), jax.ShapeDtypeStruct((n, ne), F32)],
        compiler_params=_cparams(1), name="norm_mod_router")(*args, router_w)


def _cast_weights_once(row_axis, pairs):
    @pl.when(pl.program_id(row_axis) == 0)
    def _():
        for w_ref, w_bf in pairs:
            w_bf[...] = w_ref[...].astype(BF16)


def _linear_body(x_ref, w_ref, o_ref, w_bf, *, act):
    _cast_weights_once(2, [(w_ref, w_bf)])
    acc = jnp.dot(x_ref[...].astype(BF16), w_bf[...], preferred_element_type=F32)
    if act == "tanh":
        acc = jnp.tanh(acc)
    elif act == "sigmoid":
        acc = jax.nn.sigmoid(acc)
    o_ref[...] = acc.astype(o_ref.dtype)


def _linear(x, w, *, n_batch=None, x_lead=0, act=None, out_dtype=F32, tm=1024, tn=512):
    k, n = w.shape[-2:]
    m = x.shape[-2]
    tm, tn = min(tm, m), min(tn, n)
    if x.ndim == 3:
        x_spec = pl.BlockSpec((None, tm, k), lambda b, j, i: (x_lead + b, i, 0))
    else:
        x_spec = pl.BlockSpec((tm, k), lambda b, j, i: (i, b))
    if w.ndim == 3:
        w_spec = pl.BlockSpec((None, k, tn), lambda b, j, i: (b, 0, j))
    else:
        w_spec = pl.BlockSpec((k, tn), lambda b, j, i: (0, j))
    if n_batch is None:
        out_spec = pl.BlockSpec((tm, tn), lambda b, j, i: (i, j))
        out_shape = jax.ShapeDtypeStruct((m, n), out_dtype)
    else:
        out_spec = pl.BlockSpec((None, tm, tn), lambda b, j, i: (b, i, j))
        out_shape = jax.ShapeDtypeStruct((n_batch, m, n), out_dtype)
    return pl.pallas_call(
        functools.partial(_linear_body, act=act),
        grid=(n_batch or 1, n // tn, m // tm), in_specs=[x_spec, w_spec],
        out_specs=out_spec, out_shape=out_shape,
        scratch_shapes=[pltpu.VMEM((k, tn), BF16)],
        compiler_params=_cparams(3), name="linear")(x, w)


def _linear_res_body(x_ref, w_ref, res_ref, gt_ref, o_ref, w_bf):
    _cast_weights_once(1, [(w_ref, w_bf)])
    o_ref[...] = res_ref[...] + gt_ref[...] * jnp.dot(x_ref[...], w_bf[...], preferred_element_type=F32)


def _linear_residual(x, w, res, mod5, layer, which_gate, lay, tm=1024, tn=1024):
    m, k = x.shape
    n = w.shape[1]
    tm, tn = min(tm, lay.s_len, lay.p_rows), min(tn, n)

    def gmap(j, i):
        return (layer, _cond_of_tile(i, tm, lay), which_gate, 0, j)

    return pl.pallas_call(
        _linear_res_body, grid=(n // tn, m // tm),
        in_specs=[pl.BlockSpec((tm, k), lambda j, i: (i, 0)),
                  pl.BlockSpec((k, tn), lambda j, i: (0, j)),
                  pl.BlockSpec((tm, tn), lambda j, i: (i, j)),
                  pl.BlockSpec((None, None, None, 1, tn), gmap)],
        out_specs=pl.BlockSpec((tm, tn), lambda j, i: (i, j)),
        out_shape=jax.ShapeDtypeStruct((m, n), F32),
        scratch_shapes=[pltpu.VMEM((k, tn), BF16)],
        compiler_params=_cparams(2), name="linear_residual")(x, w, res, mod5)


def _swiglu_up_body(x_ref, wg_ref, wu_ref, o_ref, wg_bf, wu_bf):
    _cast_weights_once(1, [(wg_ref, wg_bf), (wu_ref, wu_bf)])
    x = x_ref[...]
    gate = jnp.dot(x, wg_bf[...], preferred_element_type=F32)
    up = jnp.dot(x, wu_bf[...], preferred_element_type=F32)
    o_ref[...] = (_silu(gate) * up).astype(o_ref.dtype)


def _swiglu_up(x, wg, wu, tm=1024, tn=512):
    m, k = x.shape
    n = wg.shape[1]
    tm, tn = min(tm, m), min(tn, n)
    return pl.pallas_call(
        _swiglu_up_body, grid=(n // tn, m // tm),
        in_specs=[pl.BlockSpec((tm, k), lambda j, i: (i, 0)),
                  pl.BlockSpec((k, tn), lambda j, i: (0, j)),
                  pl.BlockSpec((k, tn), lambda j, i: (0, j))],
        out_specs=pl.BlockSpec((tm, tn), lambda j, i: (i, j)),
        out_shape=jax.ShapeDtypeStruct((m, n), BF16),
        scratch_shapes=[pltpu.VMEM((k, tn), BF16)] * 2,
        compiler_params=_cparams(2), name="swiglu_up")(x, wg, wu)


def _rope(x, c, s):
    w = x.shape[-1]
    lane = lax.broadcasted_iota(jnp.int32, x.shape, x.ndim - 1)
    nxt = pltpu.roll(x, w - 1, axis=x.ndim - 1)
    prv = pltpu.roll(x, 1, axis=x.ndim - 1)
    return x * c + jnp.where(lane % 2 == 0, nxt, prv) * s


def _retention_body(lg_ref, *refs, n_chunk, n_blk, has_rope, has_s0, has_buf, emit_state, scale):
    refs = list(refs)
    q_ref, k_ref, v_ref = refs[:3]
    refs = refs[3:]
    if has_rope:
        cos_ref, sin_ref = refs[:2]
        refs = refs[2:]
    if has_s0:
        s0_ref = refs.pop(0)
    if has_buf:
        refs.pop(0)
    o_ref = refs.pop(0)
    if emit_state:
        so_ref = refs.pop(0)
    st = refs.pop(0)
    c_sz = RET_CHUNK
    dh = q_ref.shape[-1]
    h, d, cb = pl.program_id(1), pl.program_id(2), pl.program_id(3)

    @pl.when(cb == 0)
    def _():
        st[...] = s0_ref[...] if has_s0 else jnp.zeros_like(st)

    lgv = lg_ref[d, h]
    fwd = d == 0
    row = lax.broadcasted_iota(jnp.int32, (c_sz, c_sz), 0)
    col = lax.broadcasted_iota(jnp.int32, (c_sz, c_sz), 1)
    diff = jnp.where(fwd, row - col, col - row).astype(F32)
    intra = jnp.where(diff >= 0, jnp.exp(jnp.maximum(diff, 0.0) * lgv), 0.0)
    pos = lax.broadcasted_iota(jnp.int32, (c_sz, dh), 0)
    npos = jnp.where(fwd, pos, c_sz - 1 - pos).astype(F32)
    q_decay = jnp.exp((npos + 1.0) * lgv)
    k_decay = jnp.exp((c_sz - 1.0 - npos) * lgv)
    chunk_decay = jnp.exp(jnp.full((1, dh), c_sz, F32) * lgv)

    for j in range(n_chunk):
        jj = jnp.where(fwd, j, n_chunk - 1 - j)
        rows = pl.ds(pl.multiple_of(jj * c_sz, c_sz), c_sz)
        q = q_ref[rows, :]
        k = k_ref[rows, :] * scale
        v = v_ref[rows, :]
        if has_rope:
            c, s = cos_ref[rows, :], sin_ref[rows, :]
            q, k = _rope(q, c, s), _rope(k, c, s)
        state = st[...]
        scores = _bdot_nt(q, k) * intra
        o_ref[rows, :] = _bdot(scores, v) + _bdot(q, state) * q_decay
        st[...] = state * chunk_decay + _bdot_tn(k * k_decay, v)

    if emit_state:
        @pl.when(cb == n_blk - 1)
        def _():
            so_ref[...] = st[...]


def _retention(qkvg, log_gamma, s0, rope, out_buf, *, row0, n_seq, seq_len, emit_state):
    d_model = qkvg.shape[1] // 4
    n_heads = RET_HEADS
    dh = d_model // n_heads
    tb = min(512, seq_len)
    n_blk = seq_len // tb
    rb0 = row0 // tb

    def blk(c, d):
        return jnp.where(d == 0, c, n_blk - 1 - c)

    def in_map(part):
        return lambda s, h, d, c, lg: (rb0 + s * n_blk + blk(c, d), part * n_heads + h)

    in_specs = [pl.BlockSpec((tb, dh), in_map(p)) for p in range(3)]
    args = [qkvg, qkvg, qkvg]
    if rope is not None:
        in_specs += [pl.BlockSpec((tb, dh), lambda s, h, d, c, lg: (blk(c, d), 0))] * 2
        args += list(rope)
    if s0 is not None:
        in_specs.append(pl.BlockSpec((None, None, None, dh, dh), lambda s, h, d, c, lg: (s, d, h, 0, 0)))
        args.append(s0)
    aliases = {}
    if out_buf is not None:
        in_specs.append(pl.BlockSpec(memory_space=pl.ANY))
        args.append(out_buf)
        aliases = {len(args): 0}
    out_specs = [pl.BlockSpec((None, tb, dh), lambda s, h, d, c, lg: (d, rb0 + s * n_blk + blk(c, d), h))]
    out_shape = [jax.ShapeDtypeStruct((2, qkvg.shape[0], d_model), F32)]
    if emit_state:
        out_specs.append(pl.BlockSpec((None, None, None, dh, dh), lambda s, h, d, c, lg: (s, d, h, 0, 0)))
        out_shape.append(jax.ShapeDtypeStruct((n_seq, 2, n_heads, dh, dh), F32))
    body = functools.partial(_retention_body, n_chunk=tb // RET_CHUNK, n_blk=n_blk, has_rope=rope is not None,
                             has_s0=s0 is not None, has_buf=out_buf is not None, emit_state=emit_state,
                             scale=dh ** -0.5)
    return pl.pallas_call(
        body,
        grid_spec=pltpu.PrefetchScalarGridSpec(
            num_scalar_prefetch=1, grid=(n_seq, n_heads, 2, n_blk), in_specs=in_specs, out_specs=out_specs,
            scratch_shapes=[pltpu.VMEM((dh, dh), F32)]),
        out_shape=out_shape, input_output_aliases=aliases,
        compiler_params=_cparams(4), name="retention")(log_gamma, *args)


def _ret_finalize_body(o_ref, g_ref, out_ref, *, n_heads):
    o = o_ref[0] + o_ref[1]
    g = g_ref[...]
    dh = o.shape[1] // n_heads
    for h in range(n_heads):
        sl = slice(h * dh, (h + 1) * dh)
        oh = o[:, sl]
        oh = oh * lax.rsqrt(jnp.mean(oh * oh, axis=-1, keepdims=True) + NORM_EPS)
        out_ref[:, sl] = (oh * _silu(g[:, sl])).astype(out_ref.dtype)


def _ret_finalize(o2, qkvg, tm=256):
    _, n, d = o2.shape
    tm = min(tm, n)
    return pl.pallas_call(
        functools.partial(_ret_finalize_body, n_heads=RET_HEADS), grid=(n // tm,),
        in_specs=[pl.BlockSpec((2, tm, d), lambda i: (0, i, 0)),
                  pl.BlockSpec((tm, d), lambda i: (i, 3))],
        out_specs=pl.BlockSpec((tm, d), lambda i: (i, 0)),
        out_shape=jax.ShapeDtypeStruct((n, d), BF16),
        compiler_params=_cparams(1), name="ret_finalize")(o2, qkvg)


def _rwkv_mix_body(x_ref, xp_ref, xn_ref, nw_ref, sc_ref, sh_ref, mu_ref, o_ref, hext, *, tm, halo, lay):
    i = pl.program_id(0)
    d = x_ref.shape[1]
    nw, sc, sh = nw_ref[...], sc_ref[...], sh_ref[...]
    hext[0:halo, :] = _rms_mod(xp_ref[...], nw, sc, sh)
    hext[halo:halo + tm, :] = _rms_mod(x_ref[...], nw, sc, sh)
    hext[halo + tm:halo + tm + halo, :] = _rms_mod(xn_ref[...], nw, sc, sh)
    h = hext[halo:halo + tm, :]
    g_row = i * tm + lax.broadcasted_iota(jnp.int32, (tm, 1), 0)

    def emit(h_shift):
        diff = h_shift - h
        for n in range(6):
            o_ref[n] = (h + diff * mu_ref[n:n + 1, :]).astype(o_ref.dtype)

    def shifted(off, lo, hi, keep):
        return jnp.where(keep, hext[halo + off:halo + off + tm, lo:hi], 0.0)

    @pl.when(i < lay.p_rows // tm)
    def _():
        t = g_row % lay.p_len
        hd = d // 2
        emit(jnp.concatenate([shifted(-1, 0, hd, t != 0),
                              shifted(1, hd, d, t != lay.p_len - 1)], axis=1))

    @pl.when(i >= lay.p_rows // tm)
    def _():
        t = (g_row - lay.p_rows) % lay.s_len
        colw = t % GRID_W
        qd = d // 4
        emit(jnp.concatenate([shifted(-1, 0, qd, colw != 0),
                              shifted(1, qd, 2 * qd, colw != GRID_W - 1),
                              shifted(-GRID_W, 2 * qd, 3 * qd, t >= GRID_W),
                              shifted(GRID_W, 3 * qd, d, t < lay.s_len - GRID_W)], axis=1))


def _rwkv_mix(x, nw, mod5, layer, mu, lay, tm=512):
    n, d = x.shape
    halo = GRID_W
    tm = min(tm, lay.s_len, lay.p_rows)
    r = tm // halo
    n_halo_blk = n // halo
    body = functools.partial(_rwkv_mix_body, tm=tm, halo=halo, lay=lay)
    return pl.pallas_call(
        body, grid=(n // tm,),
        in_specs=[pl.BlockSpec((tm, d), lambda i: (i, 0)),
                  pl.BlockSpec((halo, d), lambda i: (jnp.maximum(i * r - 1, 0), 0)),
                  pl.BlockSpec((halo, d), lambda i: (jnp.minimum((i + 1) * r, n_halo_blk - 1), 0)),
                  pl.BlockSpec((1, d), lambda i: (0, 0)),
                  pl.BlockSpec((None, None, None, 1, d), _mod_spec(layer, 1, tm, lay)),
                  pl.BlockSpec((None, None, None, 1, d), _mod_spec(layer, 0, tm, lay)),
                  pl.BlockSpec((6, d), lambda i: (0, 0))],
        out_specs=pl.BlockSpec((6, tm, d), lambda i: (0, i, 0)),
        out_shape=jax.ShapeDtypeStruct((6, n, d), BF16),
        scratch_shapes=[pltpu.VMEM((tm + 2 * halo, d), F32)],
        compiler_params=_cparams(1), name="rwkv_mix")(x, x, x, nw.reshape(1, d), mod5, mod5, mu)


def _softplus(x):
    return jnp.maximum(x, 0.0) + jnp.log1p(jnp.exp(-jnp.abs(x)))


def _rwkv_scan_body(*refs, n_chunk, n_blk, has_s0, has_buf, emit_state):
    refs = list(refs)
    r_ref, k_ref, v_ref, zw_ref, za_ref, w0_ref, a0_ref, kk_ref, ka_ref = refs[:9]
    refs = refs[9:]
    if has_s0:
        s0_ref = refs.pop(0)
    if has_buf:
        refs.pop(0)
    y_ref = refs.pop(0)
    if emit_state:
        so_ref = refs.pop(0)
    st = refs.pop(0)
    c_sz, hd, sw = RWKV_CHUNK, RWKV_HEAD_DIM, RWKV_SLAB
    n_grp = st.shape[0]
    hps = sw // hd
    d, cb = pl.program_id(1), pl.program_id(3)
    fwd = d == 0

    ri = lax.broadcasted_iota(jnp.int32, (sw, sw), 0)
    ci = lax.broadcasted_iota(jnp.int32, (sw, sw), 1)
    bd_mask = (ri // hd) == (ci // hd)

    def block_diag(slab):
        return jnp.where(bd_mask, jnp.concatenate([slab] * hps, axis=0), 0.0).astype(BF16)

    @pl.when(cb == 0)
    def _():
        if has_s0:
            st[...] = s0_ref[...]
        else:
            st[...] = jnp.zeros_like(st)

    srow = lax.broadcasted_iota(jnp.int32, (c_sz, sw), 0)
    scol = lax.broadcasted_iota(jnp.int32, (c_sz, sw), 1) % hd
    diff = jnp.where(fwd, srow - scol, scol - srow)
    strict = diff > 0
    incl = diff >= 0
    eye = jnp.where(diff == 0, 1.0, 0.0).astype(F32)
    levels = []
    m = 1
    while m < c_sz:
        levels.append(jnp.logical_and(srow // (2 * m) == scol // (2 * m), srow // m != scol // m))
        m *= 2
    trow = lax.broadcasted_iota(jnp.int32, (c_sz, c_sz), 0)
    tcol = lax.broadcasted_iota(jnp.int32, (c_sz, c_sz), 1)
    tri_incl = jnp.where(jnp.where(fwd, trow - tcol, tcol - trow) >= 0, 1.0, 0.0).astype(BF16)
    ones_bd = jnp.where(bd_mask, 1.0, 0.0).astype(BF16)

    def split3(x):
        hi = x.astype(BF16)
        r1 = x - hi.astype(F32)
        mid = r1.astype(BF16)
        return hi, mid, (r1 - mid.astype(F32)).astype(BF16)

    w0, a0, k_k, k_a = w0_ref[...], a0_ref[...], kk_ref[...], ka_ref[...]
    grp = range(n_grp)

    def chunk(j, carry):
        jj = jnp.where(fwd, j, n_chunk - 1 - j)
        rows = pl.ds(pl.multiple_of(jj * c_sz, c_sz), c_sz)
        lanes = [slice(g * sw, (g + 1) * sw) for g in grp]
        v = [v_ref[rows, sl] for sl in lanes]
        kkr = [k_ref[rows, sl] * k_k[:, sl] for sl in lanes]
        sq = [split3(x * x) for x in kkr]
        ssum = [sum(jnp.dot(p, ones_bd, preferred_element_type=F32) for p in s3) for s3 in sq]
        kk = [x * lax.rsqrt(s + 1e-12) for x, s in zip(kkr, ssum)]
        a = [jax.nn.sigmoid(a0[:, sl] + za_ref[rows, sl]) for sl in lanes]
        b = [x * y for x, y in zip(kk, a)]
        kdir = [k_ref[rows, sl] * (1.0 + (ai - 1.0) * k_a[:, sl]) for sl, ai in zip(lanes, a)]
        lw = [-jnp.exp(-_softplus(-(w0[:, sl] + zw_ref[rows, sl])) - 0.5) for sl in lanes]
        cum = [sum(jnp.dot(tri_incl, p, preferred_element_type=F32) for p in split3(x)) for x in lw]
        total = [jnp.sum(x, axis=0, keepdims=True) for x in lw]
        half = [0.5 * t for t in total]
        cumx = [c - x for c, x in zip(cum, lw)]
        r = [r_ref[rows, sl] for sl in lanes]
        lhs_g = [jnp.concatenate([kk[g] * jnp.exp(cumx[g] - half[g]), r[g] * jnp.exp(cum[g] - half[g])], axis=0)
                 for g in grp]
        e_neg = [jnp.exp(half[g] - cum[g]) for g in grp]
        g_k = [_bdot_nt(lhs_g[g], block_diag(kdir[g] * e_neg[g])) for g in grp]
        g_b = [_bdot_nt(lhs_g[g], block_diag(b[g] * e_neg[g])) for g in grp]
        l_k = [jnp.where(strict, x[:c_sz], 0.0) for x in g_k]
        a_rk = [jnp.where(incl, x[c_sz:], 0.0) for x in g_k]
        l_b = [jnp.where(strict, x[:c_sz], 0.0) for x in g_b]
        a_rb = [jnp.where(incl, x[c_sz:], 0.0) for x in g_b]
        x = [eye - jnp.where(levels[0], l, 0.0) for l in l_b]
        for lvl in levels[1:]:
            t = [_bdot(x[g], block_diag(jnp.where(lvl, l_b[g], 0.0))) for g in grp]
            x = [x[g] - _bdot(t[g], block_diag(x[g])) for g in grp]
        bd_v = [block_diag(x) for x in v]
        lkv = [_bdot(l_k[g], bd_v[g]) for g in grp]
        wt = [_bdot(x[g], block_diag(kk[g] * jnp.exp(cumx[g]))) for g in grp]
        vt = [_bdot(x[g], block_diag(lkv[g])) for g in grp]
        s_prev = [st[g] for g in grp]
        su = [_bdot_nt(jnp.concatenate([wt[g], r[g] * jnp.exp(cum[g])], axis=0), s_prev[g]) for g in grp]
        u = [su[g][:c_sz] + vt[g] for g in grp]
        y = [su[g][c_sz:] + _bdot(a_rk[g], bd_v[g]) - _bdot(a_rb[g], block_diag(u[g])) for g in grp]
        e_end = [jnp.exp(total[g] - cum[g]) for g in grp]
        upd = [_bdot_tn(jnp.concatenate([v[g], u[g]], axis=0),
                        jnp.concatenate([kdir[g] * e_end[g], -(b[g] * e_end[g])], axis=0)) for g in grp]
        st[...] = jnp.stack([s_prev[g] * jnp.exp(total[g]) + jnp.where(bd_mask, upd[g], 0.0) for g in grp])
        y_ref[rows, :] = jnp.concatenate(y, axis=1)
        return carry

    lax.fori_loop(0, n_chunk, chunk, 0)

    if emit_state:
        @pl.when(cb == n_blk - 1)
        def _():
            for g in grp:
                s = st[g]
                for h in range(hps):
                    so_ref[g * hps + h] = s[h * hd:(h + 1) * hd, h * hd:(h + 1) * hd]


def _rwkv_scan(rkv, zw, za, w0, a0, k_k, k_a, s0, out_buf, *, row0, n_seq, seq_len, emit_state,
               heads_per_step=32):
    _, _, d_model = rkv.shape
    hd, sw = RWKV_HEAD_DIM, RWKV_SLAB
    n_heads = d_model // hd
    g = min(heads_per_step, n_heads)
    wg = g * hd
    n_grp = wg // sw
    n_hg = n_heads // g
    tb = min(256, seq_len)
    n_blk = seq_len // tb
    rb0 = row0 // tb

    def blk(c, d):
        return jnp.where(d == 0, c, n_blk - 1 - c)

    def rkv_map(part):
        return lambda s, d, hg, c: (part, rb0 + s * n_blk + blk(c, d), hg)

    dir_map = lambda s, d, hg, c: (d, rb0 + s * n_blk + blk(c, d), hg)
    par_map = lambda s, d, hg, c: (d, 0, hg)
    in_specs = ([pl.BlockSpec((None, tb, wg), rkv_map(p)) for p in range(3)]
                + [pl.BlockSpec((None, tb, wg), dir_map)] * 2
                + [pl.BlockSpec((None, 1, wg), par_map)] * 2
                + [pl.BlockSpec((1, wg), lambda s, d, hg, c: (0, hg))] * 2)
    args = [rkv, rkv, rkv, zw, za, w0.reshape(2, 1, d_model), a0.reshape(2, 1, d_model),
            k_k.reshape(1, d_model), k_a.reshape(1, d_model)]
    if s0 is not None:
        in_specs.append(pl.BlockSpec((None, None, n_grp, sw, sw), lambda s, d, hg, c: (s, d, hg, 0, 0)))
        args.append(s0)
    aliases = {}
    if out_buf is not None:
        in_specs.append(pl.BlockSpec(memory_space=pl.ANY))
        args.append(out_buf)
        aliases = {len(args) - 1: 0}
    out_specs = [pl.BlockSpec((None, tb, wg), lambda s, d, hg, c: (d, rb0 + s * n_blk + blk(c, d), hg))]
    out_shape = [jax.ShapeDtypeStruct((2, rkv.shape[1], d_model), F32)]
    if emit_state:
        out_specs.append(pl.BlockSpec((None, None, g, hd, hd), lambda s, d, hg, c: (s, d, hg, 0, 0)))
        out_shape.append(jax.ShapeDtypeStruct((n_seq, 2, n_heads, hd, hd), F32))
    body = functools.partial(_rwkv_scan_body, n_chunk=tb // RWKV_CHUNK, n_blk=n_blk,
                             has_s0=s0 is not None, has_buf=out_buf is not None, emit_state=emit_state)
    return pl.pallas_call(
        body, grid=(n_seq, 2, n_hg, n_blk), in_specs=in_specs, out_specs=out_specs, out_shape=out_shape,
        scratch_shapes=[pltpu.VMEM((n_grp, sw, sw), F32)], input_output_aliases=aliases,
        compiler_params=_cparams(4), name="rwkv_scan")(*args)


def _group_sum(x, ones_bd):
    w = ones_bd.shape[0]
    out = []
    for c in range(x.shape[1] // w):
        xs = x[:, c * w:(c + 1) * w]
        hi = xs.astype(BF16)
        lo = (xs - hi.astype(F32)).astype(BF16)
        out.append(jnp.dot(hi, ones_bd, preferred_element_type=F32)
                   + jnp.dot(lo, ones_bd, preferred_element_type=F32))
    return jnp.concatenate(out, axis=1)


def _rwkv_finalize_body(y_ref, rkv_ref, za_ref, g_ref, a0_ref, ka_ref, rk_ref, lnw_ref, lnb_ref, o_ref):
    hd = RWKV_HEAD_DIM
    w = 256
    ri = lax.broadcasted_iota(jnp.int32, (w, w), 0)
    ci = lax.broadcasted_iota(jnp.int32, (w, w), 1)
    ones_bd = jnp.where(ri // hd == ci // hd, 1.0, 0.0).astype(BF16)
    y = y_ref[0] + y_ref[1]
    mean = _group_sum(y, ones_bd) * (1.0 / hd)
    yc = y - mean
    var = _group_sum(yc * yc, ones_bd) * (1.0 / hd)
    yn = yc * lax.rsqrt(var + RWKV_LN_EPS) * lnw_ref[...] + lnb_ref[...]
    r, k, v = rkv_ref[0], rkv_ref[1], rkv_ref[2]
    a_sum = jax.nn.sigmoid(a0_ref[0] + za_ref[0]) + jax.nn.sigmoid(a0_ref[1] + za_ref[1])
    k_sum = k * (2.0 + (a_sum - 2.0) * ka_ref[...])
    bonus = _group_sum(r * k_sum * rk_ref[...], ones_bd) * v
    o_ref[...] = ((yn + bonus) * g_ref[...]).astype(o_ref.dtype)


def _rwkv_finalize(y2, rkv, za, g, a0, k_a, r_k, ln_w, ln_b, tm=256):
    _, n, d = y2.shape
    tm = min(tm, n)
    row = lambda i: (0, 0)
    return pl.pallas_call(
        _rwkv_finalize_body, grid=(n // tm,),
        in_specs=[pl.BlockSpec((2, tm, d), lambda i: (0, i, 0)),
                  pl.BlockSpec((3, tm, d), lambda i: (0, i, 0)),
                  pl.BlockSpec((2, tm, d), lambda i: (0, i, 0)),
                  pl.BlockSpec((tm, d), lambda i: (i, 0)),
                  pl.BlockSpec((2, 1, d), lambda i: (0, 0, 0)),
                  pl.BlockSpec((1, d), row), pl.BlockSpec((1, d), row),
                  pl.BlockSpec((1, d), row), pl.BlockSpec((1, d), row)],
        out_specs=pl.BlockSpec((tm, d), lambda i: (i, 0)),
        out_shape=jax.ShapeDtypeStruct((n, d), BF16),
        compiler_params=_cparams(1), name="rwkv_finalize",
    )(y2, rkv, za, g, a0.reshape(2, 1, d), k_a.reshape(1, d), r_k.reshape(1, d),
      ln_w.reshape(1, d), ln_b.reshape(1, d))


def _cast_expert_weights(be_ref, pairs):
    b = pl.program_id(1)

    @pl.when(jnp.logical_or(b == 0, be_ref[b] != be_ref[jnp.maximum(b - 1, 0)]))
    def _():
        for w_ref, w_bf in pairs:
            w_bf[...] = w_ref[...].astype(BF16)


def _moe_up_body(be_ref, na_ref, x_ref, wg_ref, wu_ref, o_ref, wg_bf, wu_bf):
    b = pl.program_id(1)
    _cast_expert_weights(be_ref, [(wg_ref, wg_bf), (wu_ref, wu_bf)])

    @pl.when(b < na_ref[0])
    def _():
        x = x_ref[...]
        gate = jnp.dot(x, wg_bf[...], preferred_element_type=F32)
        up = jnp.dot(x, wu_bf[...], preferred_element_type=F32)
        o_ref[...] = (_silu(gate) * up).astype(o_ref.dtype)

    @pl.when(b >= na_ref[0])
    def _():
        o_ref[...] = jnp.zeros_like(o_ref)


def _moe_down_body(be_ref, na_ref, x_ref, wd_ref, o_ref, wd_bf):
    b = pl.program_id(1)
    _cast_expert_weights(be_ref, [(wd_ref, wd_bf)])

    @pl.when(b < na_ref[0])
    def _():
        o_ref[...] = jnp.dot(x_ref[...], wd_bf[...], preferred_element_type=F32)

    @pl.when(b >= na_ref[0])
    def _():
        o_ref[...] = jnp.zeros_like(o_ref)


def _moe_experts(xb, block_e, n_active, w_gate, w_up, w_down, tn_up=512, tn_down=512):
    rows, d = xb.shape
    bm = MOE_BLOCK
    n_blk = rows // bm
    d_ff = w_gate.shape[2]
    tn_up, tn_down = min(tn_up, d_ff), min(tn_down, d)
    act = pl.pallas_call(
        _moe_up_body,
        grid_spec=pltpu.PrefetchScalarGridSpec(
            num_scalar_prefetch=2, grid=(d_ff // tn_up, n_blk),
            in_specs=[pl.BlockSpec((bm, d), lambda j, b, be, na: (b, 0)),
                      pl.BlockSpec((None, d, tn_up), lambda j, b, be, na: (be[b], 0, j)),
                      pl.BlockSpec((None, d, tn_up), lambda j, b, be, na: (be[b], 0, j))],
            out_specs=pl.BlockSpec((bm, tn_up), lambda j, b, be, na: (b, j)),
            scratch_shapes=[pltpu.VMEM((d, tn_up), BF16)] * 2),
        out_shape=jax.ShapeDtypeStruct((rows, d_ff), BF16),
        compiler_params=_cparams(2), name="moe_up")(block_e, n_active, xb, w_gate, w_up)
    return pl.pallas_call(
        _moe_down_body,
        grid_spec=pltpu.PrefetchScalarGridSpec(
            num_scalar_prefetch=2, grid=(d // tn_down, n_blk),
            in_specs=[pl.BlockSpec((bm, d_ff), lambda j, b, be, na: (b, 0)),
                      pl.BlockSpec((None, d_ff, tn_down), lambda j, b, be, na: (be[b], 0, j))],
            out_specs=pl.BlockSpec((bm, tn_down), lambda j, b, be, na: (b, j)),
            scratch_shapes=[pltpu.VMEM((d_ff, tn_down), BF16)]),
        out_shape=jax.ShapeDtypeStruct((rows, d), F32),
        compiler_params=_cparams(2), name="moe_down")(block_e, n_active, act, w_down)


def _moe_combine_body(x_ref, y0_ref, y1_ref, g_ref, gt_ref, o_ref):
    g = g_ref[...]
    y = y0_ref[...] * g[:, 0:1] + y1_ref[...] * g[:, 1:2]
    o_ref[...] = x_ref[...] + gt_ref[...] * y


def _moe_combine(x, y0, y1, gates, mod5, layer, lay, tm=512):
    n, d = x.shape
    tm = min(tm, lay.s_len, lay.p_rows)
    row = pl.BlockSpec((tm, d), lambda i: (i, 0))
    return pl.pallas_call(
        _moe_combine_body, grid=(n // tm,),
        in_specs=[row, row, row, pl.BlockSpec((tm, TOP_K), lambda i: (i, 0)),
                  pl.BlockSpec((None, None, None, 1, d), _mod_spec(layer, 5, tm, lay))],
        out_specs=row, out_shape=jax.ShapeDtypeStruct((n, d), F32),
        compiler_params=_cparams(1), name="moe_combine")(x, y0, y1, gates, mod5)


def _moe(h, logits, w_gate, w_up, w_down):
    n_tok, d = h.shape
    bm = MOE_BLOCK
    top_val, top_idx = lax.top_k(logits, TOP_K)
    gates = jax.nn.softmax(top_val, axis=-1)
    n_assign = n_tok * TOP_K
    flat_e = top_idx.reshape(-1)
    flat_tok = jnp.arange(n_assign, dtype=jnp.int32) // TOP_K
    order = jnp.argsort(flat_e)
    sorted_e = flat_e[order]
    counts = jnp.bincount(flat_e, length=N_EXPERTS)
    padded = ((counts + bm - 1) // bm) * bm
    pad_end = jnp.cumsum(padded)
    pad_start = pad_end - padded
    start = jnp.cumsum(counts) - counts
    dest = (pad_start[sorted_e] + jnp.arange(n_assign, dtype=jnp.int32) - start[sorted_e]).astype(jnp.int32)
    n_blocks = -(-n_assign // bm) + N_EXPERTS
    slot_tok = jnp.full((n_blocks * bm,), n_tok, jnp.int32).at[dest].set(flat_tok[order])
    block_e = jnp.minimum(jnp.searchsorted(pad_end, jnp.arange(n_blocks) * bm, side='right'),
                          N_EXPERTS - 1).astype(jnp.int32)
    n_active = (pad_end[-1] // bm).astype(jnp.int32).reshape(1)
    h_pad = jnp.concatenate([h, jnp.zeros((1, d), h.dtype)], axis=0)
    xb = h_pad[slot_tok]
    yb = _moe_experts(xb, block_e, n_active, w_gate, w_up, w_down)
    slot_of = jnp.zeros((n_assign,), jnp.int32).at[order].set(dest).reshape(n_tok, TOP_K)
    return yb[slot_of[:, 0]], yb[slot_of[:, 1]], gates


def _final_norm_body(x_ref, w_ref, o_ref):
    x = x_ref[...]
    o_ref[...] = x * lax.rsqrt(jnp.mean(x * x, axis=-1, keepdims=True) + NORM_EPS) * w_ref[...]


def _final_norm(x, w, row0, n_rows, tm=512):
    d = x.shape[1]
    tm = min(tm, n_rows)
    rb0 = row0 // tm
    return pl.pallas_call(
        _final_norm_body, grid=(n_rows // tm,),
        in_specs=[pl.BlockSpec((tm, d), lambda i: (rb0 + i, 0)), pl.BlockSpec((1, d), lambda i: (0, 0))],
        out_specs=pl.BlockSpec((tm, d), lambda i: (i, 0)),
        out_shape=jax.ShapeDtypeStruct((n_rows, d), F32),
        compiler_params=_cparams(1), name="final_norm")(x, w.reshape(1, d))


def _rope_tables(seq_len, dim):
    t = jnp.arange(seq_len)
    row = (t // GRID_W).astype(F32)
    col = (t % GRID_W).astype(F32)
    half = dim // 2
    inv = ROPE_BASE ** (-(jnp.arange(0, half, 2, dtype=F32) / half))
    ang = jnp.concatenate([row[:, None] * inv, col[:, None] * inv], axis=-1)
    cos, sin = jnp.cos(ang), jnp.sin(ang)
    return jnp.repeat(cos, 2, axis=-1), jnp.stack([-sin, sin], axis=-1).reshape(seq_len, dim)


def _block_diag_states(s):
    b, two, n_heads, hd, _ = s.shape
    hps = RWKV_SLAB // hd
    s6 = s.reshape(b, two, n_heads // hps, hps, hd, hd)
    bd = jnp.einsum('bdghvk,hi->bdghvik', s6, jnp.eye(hps, dtype=s.dtype))
    return bd.reshape(b, two, n_heads // hps, RWKV_SLAB, RWKV_SLAB)


def _pad_cols(w, width):
    return jnp.pad(w, ((0, 0), (0, width - w.shape[1])))


def kernel(x_prompt, x_sample, state_l0_ret, state_l1_rwkv, c, c_ctx, ada_w, ada_b, norm_w, final_norm_w,
           l0_ret_w_in, l0_ret_w_out, l0_ret_decay, l0_ffn_w_gate, l0_ffn_w_up, l0_ffn_w_down,
           l1_rwkv_mu, l1_rwkv_w_rkv, l1_rwkv_w0, l1_rwkv_w1, l1_rwkv_w2, l1_rwkv_a0, l1_rwkv_a1, l1_rwkv_a2,
           l1_rwkv_g1, l1_rwkv_g2, l1_rwkv_k_k, l1_rwkv_k_a, l1_rwkv_r_k, l1_rwkv_ln_w, l1_rwkv_ln_b,
           l1_rwkv_w_out, l1_moe_router, l1_moe_w_gate, l1_moe_w_up, l1_moe_w_down):
    pb, p_len, d = x_prompt.shape
    sb, s_len, _ = x_sample.shape
    p_rows, s_rows = pb * p_len, sb * s_len
    lay = _Layout(p_rows, p_len, s_len, p_rows + s_rows)
    n_layers = ada_w.shape[0]

    x = jnp.concatenate([x_prompt.reshape(p_rows, d), x_sample.reshape(s_rows, d)], axis=0)
    cond8 = jnp.concatenate([c_ctx[None, :], c, jnp.zeros((8 - 1 - sb, d), F32)], axis=0)
    mod5 = _ada_mod(cond8, ada_w, ada_b)[:, :1 + sb].reshape(n_layers, 1 + sb, 6, 1, d)

    h = _norm_mod(x, norm_w[0, 0], mod5, 0, 1, 0, lay)
    qkvg = _linear(h, l0_ret_w_in, tn=1024)
    log_gamma = -jnp.exp(l0_ret_decay.astype(F32))
    dh = d // RET_HEADS
    o2, new_state_l0_ret = _retention(qkvg, log_gamma, None, None, None, row0=0, n_seq=pb, seq_len=p_len,
                                      emit_state=True)
    (o2,) = _retention(qkvg, log_gamma, state_l0_ret, _rope_tables(s_len, dh), o2, row0=p_rows, n_seq=sb,
                       seq_len=s_len, emit_state=False)
    x = _linear_residual(_ret_finalize(o2, qkvg), l0_ret_w_out, x, mod5, 0, 2, lay)
    h = _norm_mod(x, norm_w[0, 1], mod5, 0, 4, 3, lay)
    act = _swiglu_up(h, l0_ffn_w_gate, l0_ffn_w_up)
    x = _linear_residual(act, l0_ffn_w_down, x, mod5, 0, 5, lay, tm=512, tn=512)

    xs = _rwkv_mix(x, norm_w[1, 0], mod5, 1, l1_rwkv_mu, lay)
    rkv = _linear(xs, l1_rwkv_w_rkv, n_batch=3, tn=1024)
    lora_w = l1_rwkv_w1.shape[2]
    pad_w = -(-lora_w // 128) * 128
    w1p = jnp.concatenate([_pad_cols(l1_rwkv_w1[0], pad_w), _pad_cols(l1_rwkv_w1[1], pad_w)], axis=1)
    a1p = jnp.concatenate([_pad_cols(l1_rwkv_a1[0], pad_w), _pad_cols(l1_rwkv_a1[1], pad_w)], axis=1)
    t_w = _linear(xs, w1p, x_lead=3, act="tanh", out_dtype=BF16)
    t_a = _linear(xs, a1p, x_lead=4, out_dtype=BF16)
    t_g = _linear(xs, l1_rwkv_g1, x_lead=5, act="sigmoid", out_dtype=BF16)
    w2p = jnp.pad(l1_rwkv_w2, ((0, 0), (0, pad_w - lora_w), (0, 0)))
    a2p = jnp.pad(l1_rwkv_a2, ((0, 0), (0, pad_w - lora_w), (0, 0)))
    zw = _linear(t_w, w2p, n_batch=2)
    za = _linear(t_a, a2p, n_batch=2)
    g = _linear(t_g, l1_rwkv_g2)
    scan_args = (rkv, zw, za, l1_rwkv_w0, l1_rwkv_a0, l1_rwkv_k_k, l1_rwkv_k_a)
    y2, new_state_l1_rwkv = _rwkv_scan(*scan_args, None, None, row0=0, n_seq=pb, seq_len=p_len, emit_state=True)
    (y2,) = _rwkv_scan(*scan_args, _block_diag_states(state_l1_rwkv), y2, row0=p_rows, n_seq=sb, seq_len=s_len,
                       emit_state=False)
    yg = _rwkv_finalize(y2, rkv, za, g, l1_rwkv_a0, l1_rwkv_k_a, l1_rwkv_r_k, l1_rwkv_ln_w, l1_rwkv_ln_b)
    x = _linear_residual(yg, l1_rwkv_w_out, x, mod5, 1, 2, lay)
    ne_pad = 128
    h, logits = _norm_mod(x, norm_w[1, 1], mod5, 1, 4, 3, lay, router_w=_pad_cols(l1_moe_router, ne_pad))
    y0, y1, gates = _moe(h, logits[:, :N_EXPERTS], l1_moe_w_gate, l1_moe_w_up, l1_moe_w_down)
    x = _moe_combine(x, y0, y1, gates, mod5, 1, lay)

    y_prompt = _final_norm(x, final_norm_w, 0, p_rows).reshape(pb, p_len, d)
    y_sample = _final_norm(x, final_norm_w, p_rows, s_rows).reshape(sb, s_len, d)
    return (y_prompt, y_sample, new_state_l0_ret, new_state_l1_rwkv)
```

```python
import functools
from typing import NamedTuple

import jax
import jax.numpy as jnp
from jax import lax
from jax.experimental import pallas as pl
from jax.experimental.pallas import tpu as pltpu

F32 = jnp.float32
BF16 = jnp.bfloat16

GRID_W = 64
RET_HEADS = 8
RET_CHUNK = 128
ROPE_BASE = 10000.0
RWKV_HEAD_DIM = 64
RWKV_CHUNK = 64
RWKV_SLAB = 256
RWKV_CHUNKS_PER_ITER = 2
RWKV_LN_EPS = 64e-5
N_EXPERTS = 8
TOP_K = 2
MOE_BLOCK = 512
NORM_EPS = 1e-6

VMEM_LIMIT_BYTES = 56 * 1024 * 1024


class _Layout(NamedTuple):
    p_rows: int
    p_len: int
    s_len: int
    n_rows: int


def _cparams(n_axes):
    return pltpu.CompilerParams(dimension_semantics=("arbitrary",) * n_axes,
                                vmem_limit_bytes=VMEM_LIMIT_BYTES)


def _cond_of_tile(i, tm, lay):
    pt = lay.p_rows // tm
    st = lay.s_len // tm
    return jnp.where(i < pt, 0, 1 + (i - pt) // st)


def _bdot(a, b):
    return jnp.dot(a.astype(BF16), b.astype(BF16), preferred_element_type=F32)


def _bdot_nt(a, b):
    return lax.dot_general(a.astype(BF16), b.astype(BF16), (((1,), (1,)), ((), ())),
                           preferred_element_type=F32)


def _bdot_tn(a, b):
    return lax.dot_general(a.astype(BF16), b.astype(BF16), (((0,), (0,)), ((), ())),
                           preferred_element_type=F32)


def _silu(x):
    return x * jax.nn.sigmoid(x)


def _ada_body(c_ref, w_ref, b_ref, o_ref):
    o_ref[...] = _bdot(_silu(c_ref[...]), w_ref[...]) + b_ref[...]


def _ada_mod(cond8, ada_w, ada_b):
    n_layers, d, d6 = ada_w.shape
    tn = min(1024, d6)
    return pl.pallas_call(
        _ada_body,
        grid=(n_layers, d6 // tn),
        in_specs=[pl.BlockSpec((8, d), lambda l, j: (0, 0)),
                  pl.BlockSpec((None, d, tn), lambda l, j: (l, 0, j)),
                  pl.BlockSpec((None, 1, tn), lambda l, j: (l, 0, j))],
        out_specs=pl.BlockSpec((None, 8, tn), lambda l, j: (l, 0, j)),
        out_shape=jax.ShapeDtypeStruct((n_layers, 8, d6), F32),
        compiler_params=_cparams(2), name="ada_mod",
    )(cond8, ada_w, ada_b.reshape(n_layers, 1, d6))


def _mod_spec(layer, which, tm, lay, n_grid_axes=1):
    def imap(i, *_):
        return (layer, _cond_of_tile(i, tm, lay), which, 0, 0)
    return imap


def _rms_mod(x, nw, sc, sh):
    xn = x * lax.rsqrt(jnp.mean(x * x, axis=-1, keepdims=True) + NORM_EPS) * nw
    return xn * (1.0 + sc) + sh


def _norm_mod_body(x_ref, nw_ref, sc_ref, sh_ref, o_ref):
    o_ref[...] = _rms_mod(x_ref[...], nw_ref[...], sc_ref[...], sh_ref[...]).astype(o_ref.dtype)


def _norm_mod_router_body(x_ref, nw_ref, sc_ref, sh_ref, wr_ref, o_ref, lg_ref):
    h = _rms_mod(x_ref[...], nw_ref[...], sc_ref[...], sh_ref[...])
    o_ref[...] = h.astype(o_ref.dtype)
    lg_ref[...] = jnp.dot(h, wr_ref[...], preferred_element_type=F32, precision=lax.Precision.HIGHEST)


def _norm_mod(x, nw, mod5, layer, which_sc, which_sh, lay, router_w=None, tm=512):
    n, d = x.shape
    tm = min(tm, lay.s_len, lay.p_rows)
    in_specs = [pl.BlockSpec((tm, d), lambda i: (i, 0)),
                pl.BlockSpec((1, d), lambda i: (0, 0)),
                pl.BlockSpec((None, None, None, 1, d), _mod_spec(layer, which_sc, tm, lay)),
                pl.BlockSpec((None, None, None, 1, d), _mod_spec(layer, which_sh, tm, lay))]
    args = [x, nw.reshape(1, d), mod5, mod5]
    if router_w is None:
        return pl.pallas_call(
            _norm_mod_body, grid=(n // tm,), in_specs=in_specs,
            out_specs=pl.BlockSpec((tm, d), lambda i: (i, 0)),
            out_shape=jax.ShapeDtypeStruct((n, d), BF16),
            compiler_params=_cparams(1), name="norm_mod")(*args)
    ne = router_w.shape[1]
    return pl.pallas_call(
        _norm_mod_router_body, grid=(n // tm,),
        in_specs=in_specs + [pl.BlockSpec((d, ne), lambda i: (0, 0))],
        out_specs=[pl.BlockSpec((tm, d), lambda i: (i, 0)), pl.BlockSpec((tm, ne), lambda i: (i, 0))],
        out_shape=[jax.ShapeDtypeStruct((n, d), BF16), jax.ShapeDtypeStruct((n, ne), F32)],
        compiler_params=_cparams(1), name="norm_mod_router")(*args, router_w)


def _cast_weights_once(row_axis, pairs):
    @pl.when(pl.program_id(row_axis) == 0)
    def _():
        for w_ref, w_bf in pairs:
            w_bf[...] = w_ref[...].astype(BF16)


def _softplus(x):
    return jnp.maximum(x, 0.0) + jnp.log1p(jnp.exp(-jnp.abs(x)))


def _linear_body(x_ref, w_ref, *refs, act, has_bias):
    refs = list(refs)
    bias_ref = refs.pop(0) if has_bias else None
    o_ref, w_bf = refs
    _cast_weights_once(2, [(w_ref, w_bf)])
    acc = jnp.dot(x_ref[...].astype(BF16), w_bf[...], preferred_element_type=F32)
    if has_bias:
        acc = acc + bias_ref[...]
    if act == "tanh":
        acc = jnp.tanh(acc)
    elif act == "sigmoid":
        acc = jax.nn.sigmoid(acc)
    elif act == "log_decay":
        acc = -jnp.exp(-_softplus(-acc) - 0.5)
    o_ref[...] = acc.astype(o_ref.dtype)


def _linear(x, w, *, n_batch=None, x_lead=0, bias=None, act=None, out_dtype=F32, tm=1024, tn=512):
    k, n = w.shape[-2:]
    m = x.shape[-2]
    tm, tn = min(tm, m), min(tn, n)
    if x.ndim == 3:
        x_spec = pl.BlockSpec((None, tm, k), lambda b, j, i: (x_lead + b, i, 0))
    else:
        x_spec = pl.BlockSpec((tm, k), lambda b, j, i: (i, b))
    if w.ndim == 3:
        w_spec = pl.BlockSpec((None, k, tn), lambda b, j, i: (b, 0, j))
    else:
        w_spec = pl.BlockSpec((k, tn), lambda b, j, i: (0, j))
    if n_batch is None:
        out_spec = pl.BlockSpec((tm, tn), lambda b, j, i: (i, j))
        out_shape = jax.ShapeDtypeStruct((m, n), out_dtype)
    else:
        out_spec = pl.BlockSpec((None, tm, tn), lambda b, j, i: (b, i, j))
        out_shape = jax.ShapeDtypeStruct((n_batch, m, n), out_dtype)
    in_specs, args = [x_spec, w_spec], [x, w]
    if bias is not None:
        in_specs.append(pl.BlockSpec((None, 1, tn), lambda b, j, i: (b, 0, j)))
        args.append(bias.reshape(bias.shape[0], 1, n))
    return pl.pallas_call(
        functools.partial(_linear_body, act=act, has_bias=bias is not None),
        grid=(n_batch or 1, n // tn, m // tm), in_specs=in_specs,
        out_specs=out_spec, out_shape=out_shape,
        scratch_shapes=[pltpu.VMEM((k, tn), BF16)],
        compiler_params=_cparams(3), name="linear")(*args)


def _linear_res_body(x_ref, w_ref, res_ref, gt_ref, o_ref, w_bf):
    _cast_weights_once(1, [(w_ref, w_bf)])
    o_ref[...] = res_ref[...] + gt_ref[...] * jnp.dot(x_ref[...], w_bf[...], preferred_element_type=F32)


def _linear_residual(x, w, res, mod5, layer, which_gate, lay, tm=1024, tn=1024):
    m, k = x.shape
    n = w.shape[1]
    tm, tn = min(tm, lay.s_len, lay.p_rows), min(tn, n)

    def gmap(j, i):
        return (layer, _cond_of_tile(i, tm, lay), which_gate, 0, j)

    return pl.pallas_call(
        _linear_res_body, grid=(n // tn, m // tm),
        in_specs=[pl.BlockSpec((tm, k), lambda j, i: (i, 0)),
                  pl.BlockSpec((k, tn), lambda j, i: (0, j)),
                  pl.BlockSpec((tm, tn), lambda j, i: (i, j)),
                  pl.BlockSpec((None, None, None, 1, tn), gmap)],
        out_specs=pl.BlockSpec((tm, tn), lambda j, i: (i, j)),
        out_shape=jax.ShapeDtypeStruct((m, n), F32),
        scratch_shapes=[pltpu.VMEM((k, tn), BF16)],
        compiler_params=_cparams(2), name="linear_residual")(x, w, res, mod5)


def _swiglu_up_body(x_ref, wg_ref, wu_ref, o_ref, wg_bf, wu_bf):
    _cast_weights_once(1, [(wg_ref, wg_bf), (wu_ref, wu_bf)])
    x = x_ref[...]
    gate = jnp.dot(x, wg_bf[...], preferred_element_type=F32)
    up = jnp.dot(x, wu_bf[...], preferred_element_type=F32)
    o_ref[...] = (_silu(gate) * up).astype(o_ref.dtype)


def _swiglu_up(x, wg, wu, tm=1024, tn=512):
    m, k = x.shape
    n = wg.shape[1]
    tm, tn = min(tm, m), min(tn, n)
    return pl.pallas_call(
        _swiglu_up_body, grid=(n // tn, m // tm),
        in_specs=[pl.BlockSpec((tm, k), lambda j, i: (i, 0)),
                  pl.BlockSpec((k, tn), lambda j, i: (0, j)),
                  pl.BlockSpec((k, tn), lambda j, i: (0, j))],
        out_specs=pl.BlockSpec((tm, tn), lambda j, i: (i, j)),
        out_shape=jax.ShapeDtypeStruct((m, n), BF16),
        scratch_shapes=[pltpu.VMEM((k, tn), BF16)] * 2,
        compiler_params=_cparams(2), name="swiglu_up")(x, wg, wu)


def _rope(x, c, s):
    w = x.shape[-1]
    lane = lax.broadcasted_iota(jnp.int32, x.shape, x.ndim - 1)
    nxt = pltpu.roll(x, w - 1, axis=x.ndim - 1)
    prv = pltpu.roll(x, 1, axis=x.ndim - 1)
    return x * c + jnp.where(lane % 2 == 0, nxt, prv) * s


def _retention_body(lg_ref, *refs, n_chunk, n_blk, has_rope, has_s0, has_buf, emit_state, scale):
    refs = list(refs)
    q_ref, k_ref, v_ref = refs[:3]
    refs = refs[3:]
    if has_rope:
        cos_ref, sin_ref = refs[:2]
        refs = refs[2:]
    if has_s0:
        s0_ref = refs.pop(0)
    if has_buf:
        refs.pop(0)
    o_ref = refs.pop(0)
    if emit_state:
        so_ref = refs.pop(0)
    st = refs.pop(0)
    c_sz = RET_CHUNK
    dh = q_ref.shape[-1]
    h, d, cb = pl.program_id(1), pl.program_id(2), pl.program_id(3)

    @pl.when(cb == 0)
    def _():
        st[...] = s0_ref[...] if has_s0 else jnp.zeros_like(st)

    lgv = lg_ref[d, h]
    fwd = d == 0
    row = lax.broadcasted_iota(jnp.int32, (c_sz, c_sz), 0)
    col = lax.broadcasted_iota(jnp.int32, (c_sz, c_sz), 1)
    diff = jnp.where(fwd, row - col, col - row).astype(F32)
    intra = jnp.where(diff >= 0, jnp.exp(jnp.maximum(diff, 0.0) * lgv), 0.0)
    pos = lax.broadcasted_iota(jnp.int32, (c_sz, dh), 0)
    npos = jnp.where(fwd, pos, c_sz - 1 - pos).astype(F32)
    q_decay = jnp.exp((npos + 1.0) * lgv)
    k_decay = jnp.exp((c_sz - 1.0 - npos) * lgv)
    chunk_decay = jnp.exp(jnp.full((1, dh), c_sz, F32) * lgv)

    for j in range(n_chunk):
        jj = jnp.where(fwd, j, n_chunk - 1 - j)
        rows = pl.ds(pl.multiple_of(jj * c_sz, c_sz), c_sz)
        q = q_ref[rows, :]
        k = k_ref[rows, :] * scale
        v = v_ref[rows, :]
        if has_rope:
            c, s = cos_ref[rows, :], sin_ref[rows, :]
            q, k = _rope(q, c, s), _rope(k, c, s)
        state = st[...]
        scores = _bdot_nt(q, k) * intra
        o_ref[rows, :] = _bdot(scores, v) + _bdot(q, state) * q_decay
        st[...] = state * chunk_decay + _bdot_tn(k * k_decay, v)

    if emit_state:
        @pl.when(cb == n_blk - 1)
        def _():
            so_ref[...] = st[...]


def _retention(qkvg, log_gamma, s0, rope, out_buf, *, row0, n_seq, seq_len, emit_state):
    d_model = qkvg.shape[1] // 4
    n_heads = RET_HEADS
    dh = d_model // n_heads
    tb = min(512, seq_len)
    n_blk = seq_len // tb
    rb0 = row0 // tb

    def blk(c, d):
        return jnp.where(d == 0, c, n_blk - 1 - c)

    def in_map(part):
        return lambda s, h, d, c, lg: (rb0 + s * n_blk + blk(c, d), part * n_heads + h)

    in_specs = [pl.BlockSpec((tb, dh), in_map(p)) for p in range(3)]
    args = [qkvg, qkvg, qkvg]
    if rope is not None:
        in_specs += [pl.BlockSpec((tb, dh), lambda s, h, d, c, lg: (blk(c, d), 0))] * 2
        args += list(rope)
    if s0 is not None:
        in_specs.append(pl.BlockSpec((None, None, None, dh, dh), lambda s, h, d, c, lg: (s, d, h, 0, 0)))
        args.append(s0)
    aliases = {}
    if out_buf is not None:
        in_specs.append(pl.BlockSpec(memory_space=pl.ANY))
        args.append(out_buf)
        aliases = {len(args): 0}
    out_specs = [pl.BlockSpec((None, tb, dh), lambda s, h, d, c, lg: (d, rb0 + s * n_blk + blk(c, d), h))]
    out_shape = [jax.ShapeDtypeStruct((2, qkvg.shape[0], d_model), F32)]
    if emit_state:
        out_specs.append(pl.BlockSpec((None, None, None, dh, dh), lambda s, h, d, c, lg: (s, d, h, 0, 0)))
        out_shape.append(jax.ShapeDtypeStruct((n_seq, 2, n_heads, dh, dh), F32))
    body = functools.partial(_retention_body, n_chunk=tb // RET_CHUNK, n_blk=n_blk, has_rope=rope is not None,
                             has_s0=s0 is not None, has_buf=out_buf is not None, emit_state=emit_state,
                             scale=dh ** -0.5)
    return pl.pallas_call(
        body,
        grid_spec=pltpu.PrefetchScalarGridSpec(
            num_scalar_prefetch=1, grid=(n_seq, n_heads, 2, n_blk), in_specs=in_specs, out_specs=out_specs,
            scratch_shapes=[pltpu.VMEM((dh, dh), F32)]),
        out_shape=out_shape, input_output_aliases=aliases,
        compiler_params=_cparams(4), name="retention")(log_gamma, *args)


def _ret_finalize_body(o_ref, g_ref, out_ref, *, n_heads):
    o = o_ref[0] + o_ref[1]
    g = g_ref[...]
    dh = o.shape[1] // n_heads
    for h in range(n_heads):
        sl = slice(h * dh, (h + 1) * dh)
        oh = o[:, sl]
        oh = oh * lax.rsqrt(jnp.mean(oh * oh, axis=-1, keepdims=True) + NORM_EPS)
        out_ref[:, sl] = (oh * _silu(g[:, sl])).astype(out_ref.dtype)


def _ret_finalize(o2, qkvg, tm=256):
    _, n, d = o2.shape
    tm = min(tm, n)
    return pl.pallas_call(
        functools.partial(_ret_finalize_body, n_heads=RET_HEADS), grid=(n // tm,),
        in_specs=[pl.BlockSpec((2, tm, d), lambda i: (0, i, 0)),
                  pl.BlockSpec((tm, d), lambda i: (i, 3))],
        out_specs=pl.BlockSpec((tm, d), lambda i: (i, 0)),
        out_shape=jax.ShapeDtypeStruct((n, d), BF16),
        compiler_params=_cparams(1), name="ret_finalize")(o2, qkvg)


def _rwkv_mix_body(x_ref, xp_ref, xn_ref, nw_ref, sc_ref, sh_ref, mu_ref, o_ref, hext, *, tm, halo, lay):
    i = pl.program_id(0)
    d = x_ref.shape[1]
    nw, sc, sh = nw_ref[...], sc_ref[...], sh_ref[...]
    hext[0:halo, :] = _rms_mod(xp_ref[...], nw, sc, sh)
    hext[halo:halo + tm, :] = _rms_mod(x_ref[...], nw, sc, sh)
    hext[halo + tm:halo + tm + halo, :] = _rms_mod(xn_ref[...], nw, sc, sh)
    h = hext[halo:halo + tm, :]
    g_row = i * tm + lax.broadcasted_iota(jnp.int32, (tm, 1), 0)

    def emit(h_shift):
        diff = h_shift - h
        for n in range(6):
            o_ref[n] = (h + diff * mu_ref[n:n + 1, :]).astype(o_ref.dtype)

    def shifted(off, lo, hi, keep):
        return jnp.where(keep, hext[halo + off:halo + off + tm, lo:hi], 0.0)

    @pl.when(i < lay.p_rows // tm)
    def _():
        t = g_row % lay.p_len
        hd = d // 2
        emit(jnp.concatenate([shifted(-1, 0, hd, t != 0),
                              shifted(1, hd, d, t != lay.p_len - 1)], axis=1))

    @pl.when(i >= lay.p_rows // tm)
    def _():
        t = (g_row - lay.p_rows) % lay.s_len
        colw = t % GRID_W
        qd = d // 4
        emit(jnp.concatenate([shifted(-1, 0, qd, colw != 0),
                              shifted(1, qd, 2 * qd, colw != GRID_W - 1),
                              shifted(-GRID_W, 2 * qd, 3 * qd, t >= GRID_W),
                              shifted(GRID_W, 3 * qd, d, t < lay.s_len - GRID_W)], axis=1))


def _rwkv_mix(x, nw, mod5, layer, mu, lay, tm=512):
    n, d = x.shape
    halo = GRID_W
    tm = min(tm, lay.s_len, lay.p_rows)
    r = tm // halo
    n_halo_blk = n // halo
    body = functools.partial(_rwkv_mix_body, tm=tm, halo=halo, lay=lay)
    return pl.pallas_call(
        body, grid=(n // tm,),
        in_specs=[pl.BlockSpec((tm, d), lambda i: (i, 0)),
                  pl.BlockSpec((halo, d), lambda i: (jnp.maximum(i * r - 1, 0), 0)),
                  pl.BlockSpec((halo, d), lambda i: (jnp.minimum((i + 1) * r, n_halo_blk - 1), 0)),
                  pl.BlockSpec((1, d), lambda i: (0, 0)),
                  pl.BlockSpec((None, None, None, 1, d), _mod_spec(layer, 1, tm, lay)),
                  pl.BlockSpec((None, None, None, 1, d), _mod_spec(layer, 0, tm, lay)),
                  pl.BlockSpec((6, d), lambda i: (0, 0))],
        out_specs=pl.BlockSpec((6, tm, d), lambda i: (0, i, 0)),
        out_shape=jax.ShapeDtypeStruct((6, n, d), BF16),
        scratch_shapes=[pltpu.VMEM((tm + 2 * halo, d), F32)],
        compiler_params=_cparams(1), name="rwkv_mix")(x, x, x, nw.reshape(1, d), mod5, mod5, mu)


def _rwkv_scan_body(*refs, n_chunk, n_blk, has_s0, has_buf, emit_state):
    refs = list(refs)
    r_ref, k_ref, v_ref, lw_ref, a_ref, kk_ref, ka_ref = refs[:7]
    refs = refs[7:]
    if has_s0:
        s0_ref = refs.pop(0)
    if has_buf:
        refs.pop(0)
    y_ref = refs.pop(0)
    if emit_state:
        so_ref = refs.pop(0)
    st = refs.pop(0)
    c_sz, hd, sw = RWKV_CHUNK, RWKV_HEAD_DIM, RWKV_SLAB
    n_grp = st.shape[0]
    hps = sw // hd
    d, cb = pl.program_id(1), pl.program_id(3)
    fwd = d == 0

    ri = lax.broadcasted_iota(jnp.int32, (sw, sw), 0)
    ci = lax.broadcasted_iota(jnp.int32, (sw, sw), 1)
    bd_mask = (ri // hd) == (ci // hd)

    def block_diag(slab):
        return jnp.where(bd_mask, jnp.concatenate([slab] * hps, axis=0), 0.0).astype(BF16)

    @pl.when(cb == 0)
    def _():
        if has_s0:
            st[...] = s0_ref[...]
        else:
            st[...] = jnp.zeros_like(st)

    srow = lax.broadcasted_iota(jnp.int32, (c_sz, sw), 0)
    scol = lax.broadcasted_iota(jnp.int32, (c_sz, sw), 1) % hd
    diff = jnp.where(fwd, srow - scol, scol - srow)
    strict = diff > 0
    incl = diff >= 0
    eye = jnp.where(diff == 0, 1.0, 0.0).astype(F32)
    levels = []
    m = 1
    while m < c_sz:
        levels.append(jnp.logical_and(srow // (2 * m) == scol // (2 * m), srow // m != scol // m))
        m *= 2
    trow = lax.broadcasted_iota(jnp.int32, (c_sz, c_sz), 0)
    tcol = lax.broadcasted_iota(jnp.int32, (c_sz, c_sz), 1)
    tri_incl = jnp.where(jnp.where(fwd, trow - tcol, tcol - trow) >= 0, 1.0, 0.0).astype(BF16)
    ones_bd = jnp.where(bd_mask, 1.0, 0.0).astype(BF16)

    def split(x, n_parts):
        parts = []
        for _ in range(n_parts - 1):
            p = x.astype(BF16)
            parts.append(p)
            x = x - p.astype(F32)
        return parts + [x.astype(BF16)]

    k_k, k_a = kk_ref[...], ka_ref[...]
    grp = range(n_grp)

    n_sub = min(RWKV_CHUNKS_PER_ITER, n_chunk)

    def chunks(j, carry):
        rows_of = []
        for cc in range(n_sub):
            jj = j * n_sub + cc
            jj = jnp.where(fwd, jj, n_chunk - 1 - jj)
            rows_of.append(pl.ds(pl.multiple_of(jj * c_sz, c_sz), c_sz))
        units = [(rows_of[cc], slice(g * sw, (g + 1) * sw)) for cc in range(n_sub) for g in grp]
        un = range(len(units))
        v = [v_ref[rows, sl] for rows, sl in units]
        kkr = [k_ref[rows, sl] * k_k[:, sl] for rows, sl in units]
        sq = [split(x * x, 2) for x in kkr]
        ssum = [sum(jnp.dot(p, ones_bd, preferred_element_type=F32) for p in s2) for s2 in sq]
        kk = [x * lax.rsqrt(s + 1e-12) for x, s in zip(kkr, ssum)]
        a = [a_ref[rows, sl] for rows, sl in units]
        b = [x * y for x, y in zip(kk, a)]
        kdir = [k_ref[rows, sl] * (1.0 + (ai - 1.0) * k_a[:, sl]) for (rows, sl), ai in zip(units, a)]
        lw = [lw_ref[rows, sl] for rows, sl in units]
        cum = [sum(jnp.dot(tri_incl, p, preferred_element_type=F32) for p in split(x, 3)) for x in lw]
        total = [jnp.sum(x, axis=0, keepdims=True) for x in lw]
        half = [0.5 * t for t in total]
        cumx = [c - x for c, x in zip(cum, lw)]
        r = [r_ref[rows, sl] for rows, sl in units]
        lhs_g = [jnp.concatenate([kk[i] * jnp.exp(cumx[i] - half[i]), r[i] * jnp.exp(cum[i] - half[i])], axis=0)
                 for i in un]
        e_neg = [jnp.exp(half[i] - cum[i]) for i in un]
        g_k = [_bdot_nt(lhs_g[i], block_diag(kdir[i] * e_neg[i])) for i in un]
        g_b = [_bdot_nt(lhs_g[i], block_diag(b[i] * e_neg[i])) for i in un]
        l_k = [jnp.where(strict, x[:c_sz], 0.0) for x in g_k]
        a_rk = [jnp.where(incl, x[c_sz:], 0.0) for x in g_k]
        l_b = [jnp.where(strict, x[:c_sz], 0.0) for x in g_b]
        a_rb = [jnp.where(incl, x[c_sz:], 0.0) for x in g_b]
        x = [eye - jnp.where(levels[0], l, 0.0) for l in l_b]
        for lvl in levels[1:]:
            t = [_bdot(x[i], block_diag(jnp.where(lvl, l_b[i], 0.0))) for i in un]
            x = [x[i] - _bdot(t[i], block_diag(x[i])) for i in un]
        bd_v = [block_diag(x) for x in v]
        lkv = [_bdot(l_k[i], bd_v[i]) for i in un]
        wt = [_bdot(x[i], block_diag(kk[i] * jnp.exp(cumx[i]))) for i in un]
        vt = [_bdot(x[i], block_diag(lkv[i])) for i in un]
        r_abs = [r[i] * jnp.exp(cum[i]) for i in un]
        e_end = [jnp.exp(total[i] - cum[i]) for i in un]
        kb_end = [jnp.concatenate([kdir[i] * e_end[i], -(b[i] * e_end[i])], axis=0) for i in un]
        state = [st[g] for g in grp]
        for cc in range(n_sub):
            ids = [cc * n_grp + g for g in grp]
            su = [_bdot_nt(jnp.concatenate([wt[i], r_abs[i]], axis=0), state[g]) for g, i in zip(grp, ids)]
            u = [su[g][:c_sz] + vt[i] for g, i in zip(grp, ids)]
            y = [su[g][c_sz:] + _bdot(a_rk[i], bd_v[i]) - _bdot(a_rb[i], block_diag(u[g]))
                 for g, i in zip(grp, ids)]
            upd = [_bdot_tn(jnp.concatenate([v[i], u[g]], axis=0), kb_end[i]) for g, i in zip(grp, ids)]
            state = [state[g] * jnp.exp(total[i]) + jnp.where(bd_mask, upd[g], 0.0) for g, i in zip(grp, ids)]
            y_ref[rows_of[cc], :] = jnp.concatenate(y, axis=1)
        st[...] = jnp.stack(state)
        return carry

    lax.fori_loop(0, n_chunk // n_sub, chunks, 0)

    if emit_state:
        @pl.when(cb == n_blk - 1)
        def _():
            for g in grp:
                s = st[g]
                for h in range(hps):
                    so_ref[g * hps + h] = s[h * hd:(h + 1) * hd, h * hd:(h + 1) * hd]


def _rwkv_scan(rkv, lw, a, k_k, k_a, s0, out_buf, *, row0, n_seq, seq_len, emit_state, heads_per_step=32):
    _, _, d_model = rkv.shape
    hd, sw = RWKV_HEAD_DIM, RWKV_SLAB
    n_heads = d_model // hd
    g = min(heads_per_step, n_heads)
    wg = g * hd
    n_grp = wg // sw
    n_hg = n_heads // g
    tb = min(256, seq_len)
    n_blk = seq_len // tb
    rb0 = row0 // tb

    def blk(c, d):
        return jnp.where(d == 0, c, n_blk - 1 - c)

    def rkv_map(part):
        return lambda s, d, hg, c: (part, rb0 + s * n_blk + blk(c, d), hg)

    dir_map = lambda s, d, hg, c: (d, rb0 + s * n_blk + blk(c, d), hg)
    in_specs = ([pl.BlockSpec((None, tb, wg), rkv_map(p)) for p in range(3)]
                + [pl.BlockSpec((None, tb, wg), dir_map)] * 2
                + [pl.BlockSpec((1, wg), lambda s, d, hg, c: (0, hg))] * 2)
    args = [rkv, rkv, rkv, lw, a, k_k.reshape(1, d_model), k_a.reshape(1, d_model)]
    if s0 is not None:
        in_specs.append(pl.BlockSpec((None, None, n_grp, sw, sw), lambda s, d, hg, c: (s, d, hg, 0, 0)))
        args.append(s0)
    aliases = {}
    if out_buf is not None:
        in_specs.append(pl.BlockSpec(memory_space=pl.ANY))
        args.append(out_buf)
        aliases = {len(args) - 1: 0}
    out_specs = [pl.BlockSpec((None, tb, wg), lambda s, d, hg, c: (d, rb0 + s * n_blk + blk(c, d), hg))]
    out_shape = [jax.ShapeDtypeStruct((2, rkv.shape[1], d_model), F32)]
    if emit_state:
        out_specs.append(pl.BlockSpec((None, None, g, hd, hd), lambda s, d, hg, c: (s, d, hg, 0, 0)))
        out_shape.append(jax.ShapeDtypeStruct((n_seq, 2, n_heads, hd, hd), F32))
    body = functools.partial(_rwkv_scan_body, n_chunk=tb // RWKV_CHUNK, n_blk=n_blk,
                             has_s0=s0 is not None, has_buf=out_buf is not None, emit_state=emit_state)
    return pl.pallas_call(
        body, grid=(n_seq, 2, n_hg, n_blk), in_specs=in_specs, out_specs=out_specs, out_shape=out_shape,
        scratch_shapes=[pltpu.VMEM((n_grp, sw, sw), F32)], input_output_aliases=aliases,
        compiler_params=_cparams(4), name="rwkv_scan")(*args)


def _group_sum(x, ones_bd):
    w = ones_bd.shape[0]
    out = []
    for c in range(x.shape[1] // w):
        xs = x[:, c * w:(c + 1) * w]
        hi = xs.astype(BF16)
        lo = (xs - hi.astype(F32)).astype(BF16)
        out.append(jnp.dot(hi, ones_bd, preferred_element_type=F32)
                   + jnp.dot(lo, ones_bd, preferred_element_type=F32))
    return jnp.concatenate(out, axis=1)


def _rwkv_finalize_body(y_ref, rkv_ref, a_ref, g_ref, ka_ref, rk_ref, lnw_ref, lnb_ref, o_ref):
    hd = RWKV_HEAD_DIM
    w = 256
    ri = lax.broadcasted_iota(jnp.int32, (w, w), 0)
    ci = lax.broadcasted_iota(jnp.int32, (w, w), 1)
    ones_bd = jnp.where(ri // hd == ci // hd, 1.0, 0.0).astype(BF16)
    y = y_ref[0] + y_ref[1]
    mean = _group_sum(y, ones_bd) * (1.0 / hd)
    yc = y - mean
    var = _group_sum(yc * yc, ones_bd) * (1.0 / hd)
    yn = yc * lax.rsqrt(var + RWKV_LN_EPS) * lnw_ref[...] + lnb_ref[...]
    r, k, v = rkv_ref[0], rkv_ref[1], rkv_ref[2]
    a_sum = a_ref[0] + a_ref[1]
    k_sum = k * (2.0 + (a_sum - 2.0) * ka_ref[...])
    bonus = _group_sum(r * k_sum * rk_ref[...], ones_bd) * v
    o_ref[...] = ((yn + bonus) * g_ref[...]).astype(o_ref.dtype)


def _rwkv_finalize(y2, rkv, a, g, k_a, r_k, ln_w, ln_b, tm=256):
    _, n, d = y2.shape
    tm = min(tm, n)
    row = lambda i: (0, 0)
    return pl.pallas_call(
        _rwkv_finalize_body, grid=(n // tm,),
        in_specs=[pl.BlockSpec((2, tm, d), lambda i: (0, i, 0)),
                  pl.BlockSpec((3, tm, d), lambda i: (0, i, 0)),
                  pl.BlockSpec((2, tm, d), lambda i: (0, i, 0)),
                  pl.BlockSpec((tm, d), lambda i: (i, 0)),
                  pl.BlockSpec((1, d), row), pl.BlockSpec((1, d), row),
                  pl.BlockSpec((1, d), row), pl.BlockSpec((1, d), row)],
        out_specs=pl.BlockSpec((tm, d), lambda i: (i, 0)),
        out_shape=jax.ShapeDtypeStruct((n, d), BF16),
        compiler_params=_cparams(1), name="rwkv_finalize",
    )(y2, rkv, a, g, k_a.reshape(1, d), r_k.reshape(1, d), ln_w.reshape(1, d), ln_b.reshape(1, d))


def _cast_expert_weights(be_ref, pairs):
    b = pl.program_id(1)

    @pl.when(jnp.logical_or(b == 0, be_ref[b] != be_ref[jnp.maximum(b - 1, 0)]))
    def _():
        for w_ref, w_bf in pairs:
            w_bf[...] = w_ref[...].astype(BF16)


def _moe_up_body(be_ref, na_ref, x_ref, wg_ref, wu_ref, o_ref, wg_bf, wu_bf):
    b = pl.program_id(1)
    _cast_expert_weights(be_ref, [(wg_ref, wg_bf), (wu_ref, wu_bf)])

    @pl.when(b < na_ref[0])
    def _():
        x = x_ref[...]
        gate = jnp.dot(x, wg_bf[...], preferred_element_type=F32)
        up = jnp.dot(x, wu_bf[...], preferred_element_type=F32)
        o_ref[...] = (_silu(gate) * up).astype(o_ref.dtype)

    @pl.when(b >= na_ref[0])
    def _():
        o_ref[...] = jnp.zeros_like(o_ref)


def _moe_down_body(be_ref, na_ref, x_ref, wd_ref, o_ref, wd_bf):
    b = pl.program_id(1)
    _cast_expert_weights(be_ref, [(wd_ref, wd_bf)])

    @pl.when(b < na_ref[0])
    def _():
        o_ref[...] = jnp.dot(x_ref[...], wd_bf[...], preferred_element_type=F32)

    @pl.when(b >= na_ref[0])
    def _():
        o_ref[...] = jnp.zeros_like(o_ref)


def _moe_experts(xb, block_e, n_active, w_gate, w_up, w_down, tn_up=512, tn_down=512):
    rows, d = xb.shape
    bm = MOE_BLOCK
    n_blk = rows // bm
    d_ff = w_gate.shape[2]
    tn_up, tn_down = min(tn_up, d_ff), min(tn_down, d)
    act = pl.pallas_call(
        _moe_up_body,
        grid_spec=pltpu.PrefetchScalarGridSpec(
            num_scalar_prefetch=2, grid=(d_ff // tn_up, n_blk),
            in_specs=[pl.BlockSpec((bm, d), lambda j, b, be, na: (b, 0)),
                      pl.BlockSpec((None, d, tn_up), lambda j, b, be, na: (be[b], 0, j)),
                      pl.BlockSpec((None, d, tn_up), lambda j, b, be, na: (be[b], 0, j))],
            out_specs=pl.BlockSpec((bm, tn_up), lambda j, b, be, na: (b, j)),
            scratch_shapes=[pltpu.VMEM((d, tn_up), BF16)] * 2),
        out_shape=jax.ShapeDtypeStruct((rows, d_ff), BF16),
        compiler_params=_cparams(2), name="moe_up")(block_e, n_active, xb, w_gate, w_up)
    return pl.pallas_call(
        _moe_down_body,
        grid_spec=pltpu.PrefetchScalarGridSpec(
            num_scalar_prefetch=2, grid=(d // tn_down, n_blk),
            in_specs=[pl.BlockSpec((bm, d_ff), lambda j, b, be, na: (b, 0)),
                      pl.BlockSpec((None, d_ff, tn_down), lambda j, b, be, na: (be[b], 0, j))],
            out_specs=pl.BlockSpec((bm, tn_down), lambda j, b, be, na: (b, j)),
            scratch_shapes=[pltpu.VMEM((d_ff, tn_down), BF16)]),
        out_shape=jax.ShapeDtypeStruct((rows, d), F32),
        compiler_params=_cparams(2), name="moe_down")(block_e, n_active, act, w_down)


def _moe_combine_body(x_ref, y0_ref, y1_ref, g_ref, gt_ref, o_ref):
    g = g_ref[...]
    y = y0_ref[...] * g[:, 0:1] + y1_ref[...] * g[:, 1:2]
    o_ref[...] = x_ref[...] + gt_ref[...] * y


def _moe_combine(x, y0, y1, gates, mod5, layer, lay, tm=512):
    n, d = x.shape
    tm = min(tm, lay.s_len, lay.p_rows)
    row = pl.BlockSpec((tm, d), lambda i: (i, 0))
    return pl.pallas_call(
        _moe_combine_body, grid=(n // tm,),
        in_specs=[row, row, row, pl.BlockSpec((tm, TOP_K), lambda i: (i, 0)),
                  pl.BlockSpec((None, None, None, 1, d), _mod_spec(layer, 5, tm, lay))],
        out_specs=row, out_shape=jax.ShapeDtypeStruct((n, d), F32),
        compiler_params=_cparams(1), name="moe_combine")(x, y0, y1, gates, mod5)


def _moe(h, logits, w_gate, w_up, w_down):
    n_tok, d = h.shape
    bm = MOE_BLOCK
    top_val, top_idx = lax.top_k(logits, TOP_K)
    gates = jax.nn.softmax(top_val, axis=-1)
    n_assign = n_tok * TOP_K
    flat_e = top_idx.reshape(-1).astype(jnp.int32)
    order = jnp.argsort(flat_e).astype(jnp.int32)
    rank = jnp.argsort(order).astype(jnp.int32)
    counts = jnp.sum(flat_e[:, None] == jnp.arange(N_EXPERTS, dtype=jnp.int32)[None, :], axis=0,
                     dtype=jnp.int32)
    padded = ((counts + bm - 1) // bm) * bm
    pad_end = jnp.cumsum(padded)
    pad_start = pad_end - padded
    start = jnp.cumsum(counts) - counts
    n_blocks = -(-n_assign // bm) + N_EXPERTS
    block_e = jnp.minimum(jnp.searchsorted(pad_end, jnp.arange(n_blocks) * bm, side='right'),
                          N_EXPERTS - 1).astype(jnp.int32)
    n_active = (pad_end[-1] // bm).astype(jnp.int32).reshape(1)
    slot = jnp.arange(n_blocks * bm, dtype=jnp.int32)
    slot_e = block_e[slot // bm]
    pos = slot - pad_start[slot_e]
    src = jnp.clip(start[slot_e] + pos, 0, n_assign - 1)
    slot_tok = jnp.where(pos < counts[slot_e], order[src] // TOP_K, 0)
    xb = h[slot_tok]
    yb = _moe_experts(xb, block_e, n_active, w_gate, w_up, w_down)
    slot_of = (pad_start[flat_e] + rank - start[flat_e]).reshape(n_tok, TOP_K)
    return yb[slot_of[:, 0]], yb[slot_of[:, 1]], gates


def _final_norm_body(x_ref, w_ref, o_ref):
    x = x_ref[...]
    o_ref[...] = x * lax.rsqrt(jnp.mean(x * x, axis=-1, keepdims=True) + NORM_EPS) * w_ref[...]


def _final_norm(x, w, row0, n_rows, tm=512):
    d = x.shape[1]
    tm = min(tm, n_rows)
    rb0 = row0 // tm
    return pl.pallas_call(
        _final_norm_body, grid=(n_rows // tm,),
        in_specs=[pl.BlockSpec((tm, d), lambda i: (rb0 + i, 0)), pl.BlockSpec((1, d), lambda i: (0, 0))],
        out_specs=pl.BlockSpec((tm, d), lambda i: (i, 0)),
        out_shape=jax.ShapeDtypeStruct((n_rows, d), F32),
        compiler_params=_cparams(1), name="final_norm")(x, w.reshape(1, d))


def _rope_tables(seq_len, dim):
    t = jnp.arange(seq_len)
    row = (t // GRID_W).astype(F32)
    col = (t % GRID_W).astype(F32)
    half = dim // 2
    inv = ROPE_BASE ** (-(jnp.arange(0, half, 2, dtype=F32) / half))
    ang = jnp.concatenate([row[:, None] * inv, col[:, None] * inv], axis=-1)
    cos, sin = jnp.cos(ang), jnp.sin(ang)
    return jnp.repeat(cos, 2, axis=-1), jnp.stack([-sin, sin], axis=-1).reshape(seq_len, dim)


def _block_diag_states(s):
    b, two, n_heads, hd, _ = s.shape
    hps = RWKV_SLAB // hd
    s6 = s.reshape(b, two, n_heads // hps, hps, hd, hd)
    bd = jnp.einsum('bdghvk,hi->bdghvik', s6, jnp.eye(hps, dtype=s.dtype))
    return bd.reshape(b, two, n_heads // hps, RWKV_SLAB, RWKV_SLAB)


def _pad_cols(w, width):
    return jnp.pad(w, ((0, 0), (0, width - w.shape[1])))


def kernel(x_prompt, x_sample, state_l0_ret, state_l1_rwkv, c, c_ctx, ada_w, ada_b, norm_w, final_norm_w,
           l0_ret_w_in, l0_ret_w_out, l0_ret_decay, l0_ffn_w_gate, l0_ffn_w_up, l0_ffn_w_down,
           l1_rwkv_mu, l1_rwkv_w_rkv, l1_rwkv_w0, l1_rwkv_w1, l1_rwkv_w2, l1_rwkv_a0, l1_rwkv_a1, l1_rwkv_a2,
           l1_rwkv_g1, l1_rwkv_g2, l1_rwkv_k_k, l1_rwkv_k_a, l1_rwkv_r_k, l1_rwkv_ln_w, l1_rwkv_ln_b,
           l1_rwkv_w_out, l1_moe_router, l1_moe_w_gate, l1_moe_w_up, l1_moe_w_down):
    pb, p_len, d = x_prompt.shape
    sb, s_len, _ = x_sample.shape
    p_rows, s_rows = pb * p_len, sb * s_len
    lay = _Layout(p_rows, p_len, s_len, p_rows + s_rows)
    n_layers = ada_w.shape[0]

    x = jnp.concatenate([x_prompt.reshape(p_rows, d), x_sample.reshape(s_rows, d)], axis=0)
    cond8 = jnp.concatenate([c_ctx[None, :], c, jnp.zeros((8 - 1 - sb, d), F32)], axis=0)
    mod5 = _ada_mod(cond8, ada_w, ada_b)[:, :1 + sb].reshape(n_layers, 1 + sb, 6, 1, d)

    h = _norm_mod(x, norm_w[0, 0], mod5, 0, 1, 0, lay)
    qkvg = _linear(h, l0_ret_w_in, tn=1024)
    log_gamma = -jnp.exp(l0_ret_decay.astype(F32))
    dh = d // RET_HEADS
    o2, new_state_l0_ret = _retention(qkvg, log_gamma, None, None, None, row0=0, n_seq=pb, seq_len=p_len,
                                      emit_state=True)
    (o2,) = _retention(qkvg, log_gamma, state_l0_ret, _rope_tables(s_len, dh), o2, row0=p_rows, n_seq=sb,
                       seq_len=s_len, emit_state=False)
    x = _linear_residual(_ret_finalize(o2, qkvg), l0_ret_w_out, x, mod5, 0, 2, lay)
    h = _norm_mod(x, norm_w[0, 1], mod5, 0, 4, 3, lay)
    act = _swiglu_up(h, l0_ffn_w_gate, l0_ffn_w_up)
    x = _linear_residual(act, l0_ffn_w_down, x, mod5, 0, 5, lay, tm=512, tn=512)

    xs = _rwkv_mix(x, norm_w[1, 0], mod5, 1, l1_rwkv_mu, lay)
    rkv = _linear(xs, l1_rwkv_w_rkv, n_batch=3, tn=1024)
    lora_w = l1_rwkv_w1.shape[2]
    pad_w = -(-lora_w // 128) * 128
    w1p = jnp.concatenate([_pad_cols(l1_rwkv_w1[0], pad_w), _pad_cols(l1_rwkv_w1[1], pad_w)], axis=1)
    a1p = jnp.concatenate([_pad_cols(l1_rwkv_a1[0], pad_w), _pad_cols(l1_rwkv_a1[1], pad_w)], axis=1)
    t_w = _linear(xs, w1p, x_lead=3, act="tanh", out_dtype=BF16)
    t_a = _linear(xs, a1p, x_lead=4, out_dtype=BF16)
    t_g = _linear(xs, l1_rwkv_g1, x_lead=5, act="sigmoid", out_dtype=BF16)
    w2p = jnp.pad(l1_rwkv_w2, ((0, 0), (0, pad_w - lora_w), (0, 0)))
    a2p = jnp.pad(l1_rwkv_a2, ((0, 0), (0, pad_w - lora_w), (0, 0)))
    lw = _linear(t_w, w2p, n_batch=2, bias=l1_rwkv_w0, act="log_decay")
    a_lr = _linear(t_a, a2p, n_batch=2, bias=l1_rwkv_a0, act="sigmoid")
    g = _linear(t_g, l1_rwkv_g2)
    scan_args = (rkv, lw, a_lr, l1_rwkv_k_k, l1_rwkv_k_a)
    y2, new_state_l1_rwkv = _rwkv_scan(*scan_args, None, None, row0=0, n_seq=pb, seq_len=p_len, emit_state=True)
    (y2,) = _rwkv_scan(*scan_args, _block_diag_states(state_l1_rwkv), y2, row0=p_rows, n_seq=sb, seq_len=s_len,
                       emit_state=False)
    yg = _rwkv_finalize(y2, rkv, a_lr, g, l1_rwkv_k_a, l1_rwkv_r_k, l1_rwkv_ln_w, l1_rwkv_ln_b)
    x = _linear_residual(yg, l1_rwkv_w_out, x, mod5, 1, 2, lay)
    ne_pad = 128
    h, logits = _norm_mod(x, norm_w[1, 1], mod5, 1, 4, 3, lay, router_w=_pad_cols(l1_moe_router, ne_pad))
    y0, y1, gates = _moe(h, logits[:, :N_EXPERTS], l1_moe_w_gate, l1_moe_w_up, l1_moe_w_down)
    x = _moe_combine(x, y0, y1, gates, mod5, 1, lay)

    y_prompt = _final_norm(x, final_norm_w, 0, p_rows).reshape(pb, p_len, d)
    y_sample = _final_norm(x, final_norm_w, p_rows, s_rows).reshape(sb, s_len, d)
    return (y_prompt, y_sample, new_state_l0_ret, new_state_l1_rwkv)
```

```python
import functools
from typing import NamedTuple

import jax
import jax.numpy as jnp
from jax import lax
from jax.experimental import pallas as pl
from jax.experimental.pallas import tpu as pltpu

F32 = jnp.float32
BF16 = jnp.bfloat16

GRID_W = 64
RET_HEADS = 8
RET_CHUNK = 128
RET_HEADS_PER_STEP = 2
ROPE_BASE = 10000.0
RWKV_HEAD_DIM = 64
RWKV_CHUNK = 64
RWKV_SLAB = 256
RWKV_CHUNKS_PER_ITER = 2
RWKV_LN_EPS = 64e-5
N_EXPERTS = 8
TOP_K = 2
MOE_BLOCK = 512
NORM_EPS = 1e-6

VMEM_LIMIT_BYTES = 56 * 1024 * 1024


class _Layout(NamedTuple):
    p_rows: int
    p_len: int
    s_len: int
    n_rows: int


def _cparams(n_axes):
    return pltpu.CompilerParams(dimension_semantics=("arbitrary",) * n_axes,
                                vmem_limit_bytes=VMEM_LIMIT_BYTES)


def _cond_of_tile(i, tm, lay):
    pt = lay.p_rows // tm
    st = lay.s_len // tm
    return jnp.where(i < pt, 0, 1 + (i - pt) // st)


def _bdot(a, b):
    return jnp.dot(a.astype(BF16), b.astype(BF16), preferred_element_type=F32)


def _bdot_nt(a, b):
    return lax.dot_general(a.astype(BF16), b.astype(BF16), (((1,), (1,)), ((), ())),
                           preferred_element_type=F32)


def _bdot_tn(a, b):
    return lax.dot_general(a.astype(BF16), b.astype(BF16), (((0,), (0,)), ((), ())),
                           preferred_element_type=F32)


def _silu(x):
    return x * jax.nn.sigmoid(x)


def _ada_body(c_ref, w_ref, b_ref, o_ref):
    o_ref[...] = _bdot(_silu(c_ref[...]), w_ref[...]) + b_ref[...]


def _ada_mod(cond8, ada_w, ada_b):
    n_layers, d, d6 = ada_w.shape
    tn = min(1024, d6)
    return pl.pallas_call(
        _ada_body,
        grid=(n_layers, d6 // tn),
        in_specs=[pl.BlockSpec((8, d), lambda l, j: (0, 0)),
                  pl.BlockSpec((None, d, tn), lambda l, j: (l, 0, j)),
                  pl.BlockSpec((None, 1, tn), lambda l, j: (l, 0, j))],
        out_specs=pl.BlockSpec((None, 8, tn), lambda l, j: (l, 0, j)),
        out_shape=jax.ShapeDtypeStruct((n_layers, 8, d6), F32),
        compiler_params=_cparams(2), name="ada_mod",
    )(cond8, ada_w, ada_b.reshape(n_layers, 1, d6))


def _mod_spec(layer, which, tm, lay, n_grid_axes=1):
    def imap(i, *_):
        return (layer, _cond_of_tile(i, tm, lay), which, 0, 0)
    return imap


def _rms_mod(x, nw, sc, sh):
    xn = x * lax.rsqrt(jnp.mean(x * x, axis=-1, keepdims=True) + NORM_EPS) * nw
    return xn * (1.0 + sc) + sh


def _norm_mod_body(x_ref, nw_ref, sc_ref, sh_ref, o_ref):
    o_ref[...] = _rms_mod(x_ref[...], nw_ref[...], sc_ref[...], sh_ref[...]).astype(o_ref.dtype)


def _norm_mod_router_body(x_ref, nw_ref, sc_ref, sh_ref, wr_ref, o_ref, lg_ref):
    h = _rms_mod(x_ref[...], nw_ref[...], sc_ref[...], sh_ref[...])
    o_ref[...] = h.astype(o_ref.dtype)
    lg_ref[...] = jnp.dot(h, wr_ref[...], preferred_element_type=F32, precision=lax.Precision.HIGHEST)


def _norm_mod(x, nw, mod5, layer, which_sc, which_sh, lay, router_w=None, tm=512):
    n, d = x.shape
    tm = min(tm, lay.s_len, lay.p_rows)
    in_specs = [pl.BlockSpec((tm, d), lambda i: (i, 0)),
                pl.BlockSpec((1, d), lambda i: (0, 0)),
                pl.BlockSpec((None, None, None, 1, d), _mod_spec(layer, which_sc, tm, lay)),
                pl.BlockSpec((None, None, None, 1, d), _mod_spec(layer, which_sh, tm, lay))]
    args = [x, nw.reshape(1, d), mod5, mod5]
    if router_w is None:
        return pl.pallas_call(
            _norm_mod_body, grid=(n // tm,), in_specs=in_specs,
            out_specs=pl.BlockSpec((tm, d), lambda i: (i, 0)),
            out_shape=jax.ShapeDtypeStruct((n, d), BF16),
            compiler_params=_cparams(1), name="norm_mod")(*args)
    ne = router_w.shape[1]
    return pl.pallas_call(
        _norm_mod_router_body, grid=(n // tm,),
        in_specs=in_specs + [pl.BlockSpec((d, ne), lambda i: (0, 0))],
        out_specs=[pl.BlockSpec((tm, d), lambda i: (i, 0)), pl.BlockSpec((tm, ne), lambda i: (i, 0))],
        out_shape=[jax.ShapeDtypeStruct((n, d), BF16), jax.ShapeDtypeStruct((n, ne), F32)],
        compiler_params=_cparams(1), name="norm_mod_router")(*args, router_w)


def _cast_weights_once(row_axis, pairs):
    @pl.when(pl.program_id(row_axis) == 0)
    def _():
        for w_ref, w_bf in pairs:
            w_bf[...] = w_ref[...].astype(BF16)


def _softplus(x):
    return jnp.maximum(x, 0.0) + jnp.log1p(jnp.exp(-jnp.abs(x)))


def _linear_body(x_ref, w_ref, *refs, act, has_bias):
    refs = list(refs)
    bias_ref = refs.pop(0) if has_bias else None
    o_ref, w_bf = refs
    _cast_weights_once(2, [(w_ref, w_bf)])
    acc = jnp.dot(x_ref[...].astype(BF16), w_bf[...], preferred_element_type=F32)
    if has_bias:
        acc = acc + bias_ref[...]
    if act == "tanh":
        acc = jnp.tanh(acc)
    elif act == "sigmoid":
        acc = jax.nn.sigmoid(acc)
    elif act == "log_decay":
        acc = -jnp.exp(-_softplus(-acc) - 0.5)
    o_ref[...] = acc.astype(o_ref.dtype)


def _linear(x, w, *, n_batch=None, x_lead=0, bias=None, act=None, out_dtype=F32, tm=1024, tn=512):
    k, n = w.shape[-2:]
    m = x.shape[-2]
    tm, tn = min(tm, m), min(tn, n)
    if x.ndim == 3:
        x_spec = pl.BlockSpec((None, tm, k), lambda b, j, i: (x_lead + b, i, 0))
    else:
        x_spec = pl.BlockSpec((tm, k), lambda b, j, i: (i, b))
    if w.ndim == 3:
        w_spec = pl.BlockSpec((None, k, tn), lambda b, j, i: (b, 0, j))
    else:
        w_spec = pl.BlockSpec((k, tn), lambda b, j, i: (0, j))
    if n_batch is None:
        out_spec = pl.BlockSpec((tm, tn), lambda b, j, i: (i, j))
        out_shape = jax.ShapeDtypeStruct((m, n), out_dtype)
    else:
        out_spec = pl.BlockSpec((None, tm, tn), lambda b, j, i: (b, i, j))
        out_shape = jax.ShapeDtypeStruct((n_batch, m, n), out_dtype)
    in_specs, args = [x_spec, w_spec], [x, w]
    if bias is not None:
        in_specs.append(pl.BlockSpec((None, 1, tn), lambda b, j, i: (b, 0, j)))
        args.append(bias.reshape(bias.shape[0], 1, n))
    return pl.pallas_call(
        functools.partial(_linear_body, act=act, has_bias=bias is not None),
        grid=(n_batch or 1, n // tn, m // tm), in_specs=in_specs,
        out_specs=out_spec, out_shape=out_shape,
        scratch_shapes=[pltpu.VMEM((k, tn), BF16)],
        compiler_params=_cparams(3), name="linear")(*args)


def _linear_res_body(x_ref, w_ref, res_ref, gt_ref, o_ref, w_bf):
    _cast_weights_once(1, [(w_ref, w_bf)])
    o_ref[...] = res_ref[...] + gt_ref[...] * jnp.dot(x_ref[...], w_bf[...], preferred_element_type=F32)


def _linear_residual(x, w, res, mod5, layer, which_gate, lay, tm=1024, tn=1024):
    m, k = x.shape
    n = w.shape[1]
    tm, tn = min(tm, lay.s_len, lay.p_rows), min(tn, n)

    def gmap(j, i):
        return (layer, _cond_of_tile(i, tm, lay), which_gate, 0, j)

    return pl.pallas_call(
        _linear_res_body, grid=(n // tn, m // tm),
        in_specs=[pl.BlockSpec((tm, k), lambda j, i: (i, 0)),
                  pl.BlockSpec((k, tn), lambda j, i: (0, j)),
                  pl.BlockSpec((tm, tn), lambda j, i: (i, j)),
                  pl.BlockSpec((None, None, None, 1, tn), gmap)],
        out_specs=pl.BlockSpec((tm, tn), lambda j, i: (i, j)),
        out_shape=jax.ShapeDtypeStruct((m, n), F32),
        scratch_shapes=[pltpu.VMEM((k, tn), BF16)],
        compiler_params=_cparams(2), name="linear_residual")(x, w, res, mod5)


def _swiglu_up_body(x_ref, wg_ref, wu_ref, o_ref, wg_bf, wu_bf):
    _cast_weights_once(1, [(wg_ref, wg_bf), (wu_ref, wu_bf)])
    x = x_ref[...]
    gate = jnp.dot(x, wg_bf[...], preferred_element_type=F32)
    up = jnp.dot(x, wu_bf[...], preferred_element_type=F32)
    o_ref[...] = (_silu(gate) * up).astype(o_ref.dtype)


def _swiglu_up(x, wg, wu, tm=1024, tn=512):
    m, k = x.shape
    n = wg.shape[1]
    tm, tn = min(tm, m), min(tn, n)
    return pl.pallas_call(
        _swiglu_up_body, grid=(n // tn, m // tm),
        in_specs=[pl.BlockSpec((tm, k), lambda j, i: (i, 0)),
                  pl.BlockSpec((k, tn), lambda j, i: (0, j)),
                  pl.BlockSpec((k, tn), lambda j, i: (0, j))],
        out_specs=pl.BlockSpec((tm, tn), lambda j, i: (i, j)),
        out_shape=jax.ShapeDtypeStruct((m, n), BF16),
        scratch_shapes=[pltpu.VMEM((k, tn), BF16)] * 2,
        compiler_params=_cparams(2), name="swiglu_up")(x, wg, wu)


def _rope(x, c, s):
    w = x.shape[-1]
    lane = lax.broadcasted_iota(jnp.int32, x.shape, x.ndim - 1)
    nxt = pltpu.roll(x, w - 1, axis=x.ndim - 1)
    prv = pltpu.roll(x, 1, axis=x.ndim - 1)
    return x * c + jnp.where(lane % 2 == 0, nxt, prv) * s


def _retention_body(lg_ref, *refs, n_chunk, n_blk, has_rope, has_s0, emit_state, scale):
    refs = list(refs)
    q_ref, k_ref, v_ref = refs[:3]
    refs = refs[3:]
    if has_rope:
        cos_ref, sin_ref = refs[:2]
        refs = refs[2:]
    if has_s0:
        s0_ref = refs.pop(0)
    o_ref = refs.pop(0)
    if emit_state:
        so_ref = refs.pop(0)
    st = refs.pop(0)
    c_sz = RET_CHUNK
    hp, dh = st.shape[0], st.shape[1]
    hg, d, cb = pl.program_id(1), pl.program_id(2), pl.program_id(3)
    heads = range(hp)
    lanes = [slice(h * dh, (h + 1) * dh) for h in heads]

    @pl.when(cb == 0)
    def _():
        st[...] = s0_ref[...] if has_s0 else jnp.zeros_like(st)

    lgv = [lg_ref[d, hg * hp + h] for h in heads]
    fwd = d == 0
    row = lax.broadcasted_iota(jnp.int32, (c_sz, c_sz), 0)
    col = lax.broadcasted_iota(jnp.int32, (c_sz, c_sz), 1)
    diff = jnp.where(fwd, row - col, col - row).astype(F32)
    intra = [jnp.where(diff >= 0, jnp.exp(jnp.maximum(diff, 0.0) * lg), 0.0) for lg in lgv]
    pos = lax.broadcasted_iota(jnp.int32, (c_sz, dh), 0)
    npos = jnp.where(fwd, pos, c_sz - 1 - pos).astype(F32)
    q_decay = [jnp.exp((npos + 1.0) * lg) for lg in lgv]
    k_decay = [jnp.exp((c_sz - 1.0 - npos) * lg) for lg in lgv]
    chunk_decay = [jnp.exp(jnp.full((1, dh), c_sz, F32) * lg) for lg in lgv]

    for j in range(n_chunk):
        jj = jnp.where(fwd, j, n_chunk - 1 - j)
        rows = pl.ds(pl.multiple_of(jj * c_sz, c_sz), c_sz)
        q = [q_ref[rows, sl] for sl in lanes]
        k = [k_ref[rows, sl] * scale for sl in lanes]
        v = [v_ref[rows, sl] for sl in lanes]
        if has_rope:
            c, s = cos_ref[rows, :], sin_ref[rows, :]
            q = [_rope(x, c, s) for x in q]
            k = [_rope(x, c, s) for x in k]
        state = [st[h] for h in heads]
        scores = [_bdot_nt(q[h], k[h]) * intra[h] for h in heads]
        o = [_bdot(scores[h], v[h]) + _bdot(q[h], state[h]) * q_decay[h] for h in heads]
        new_state = [state[h] * chunk_decay[h] + _bdot_tn(k[h] * k_decay[h], v[h]) for h in heads]
        o_ref[rows, :] = jnp.concatenate(o, axis=1)
        st[...] = jnp.stack(new_state)

    if emit_state:
        @pl.when(cb == n_blk - 1)
        def _():
            so_ref[...] = st[...]


def _retention(qkvg, log_gamma, s0, rope, *, row0, n_seq, seq_len, emit_state):
    d_model = qkvg.shape[1] // 4
    n_heads = RET_HEADS
    dh = d_model // n_heads
    tb = min(512, seq_len)
    n_blk = seq_len // tb
    rb0 = row0 // tb

    def blk(c, d):
        return jnp.where(d == 0, c, n_blk - 1 - c)

    hp = RET_HEADS_PER_STEP
    n_hg = n_heads // hp

    def in_map(part):
        return lambda s, h, d, c, lg: (rb0 + s * n_blk + blk(c, d), part * n_hg + h)

    in_specs = [pl.BlockSpec((tb, hp * dh), in_map(p)) for p in range(3)]
    args = [qkvg, qkvg, qkvg]
    if rope is not None:
        in_specs += [pl.BlockSpec((tb, dh), lambda s, h, d, c, lg: (blk(c, d), 0))] * 2
        args += list(rope)
    if s0 is not None:
        in_specs.append(pl.BlockSpec((None, None, hp, dh, dh), lambda s, h, d, c, lg: (s, d, h, 0, 0)))
        args.append(s0)
    out_specs = [pl.BlockSpec((None, tb, hp * dh), lambda s, h, d, c, lg: (d, s * n_blk + blk(c, d), h))]
    out_shape = [jax.ShapeDtypeStruct((2, n_seq * seq_len, d_model), F32)]
    if emit_state:
        out_specs.append(pl.BlockSpec((None, None, hp, dh, dh), lambda s, h, d, c, lg: (s, d, h, 0, 0)))
        out_shape.append(jax.ShapeDtypeStruct((n_seq, 2, n_heads, dh, dh), F32))
    body = functools.partial(_retention_body, n_chunk=tb // RET_CHUNK, n_blk=n_blk, has_rope=rope is not None,
                             has_s0=s0 is not None, emit_state=emit_state, scale=dh ** -0.5)
    return pl.pallas_call(
        body,
        grid_spec=pltpu.PrefetchScalarGridSpec(
            num_scalar_prefetch=1, grid=(n_seq, n_hg, 2, n_blk), in_specs=in_specs, out_specs=out_specs,
            scratch_shapes=[pltpu.VMEM((hp, dh, dh), F32)]),
        out_shape=out_shape, compiler_params=_cparams(4), name="retention")(log_gamma, *args)


def _two_trunk_specs(a_p, a_s, tm):
    pt = a_p.shape[1] // tm
    d = a_p.shape[2]
    return [pl.BlockSpec((2, tm, d), lambda i: (0, jnp.minimum(i, pt - 1), 0)),
            pl.BlockSpec((2, tm, d), lambda i: (0, jnp.maximum(i - pt, 0), 0))], pt


def _for_trunk_of_tile(pt, p_ref, s_ref, fn):
    i = pl.program_id(0)

    @pl.when(i < pt)
    def _():
        fn(p_ref)

    @pl.when(i >= pt)
    def _():
        fn(s_ref)


def _ret_finalize_body(op_ref, os_ref, g_ref, out_ref, *, n_heads, pt):
    def finalize(o_ref):
        o = o_ref[0] + o_ref[1]
        g = g_ref[...]
        dh = o.shape[1] // n_heads
        for h in range(n_heads):
            sl = slice(h * dh, (h + 1) * dh)
            oh = o[:, sl]
            oh = oh * lax.rsqrt(jnp.mean(oh * oh, axis=-1, keepdims=True) + NORM_EPS)
            out_ref[:, sl] = (oh * _silu(g[:, sl])).astype(out_ref.dtype)

    _for_trunk_of_tile(pt, op_ref, os_ref, finalize)


def _ret_finalize(o_p, o_s, qkvg, tm=256):
    n, d = o_p.shape[1] + o_s.shape[1], o_p.shape[2]
    specs, pt = _two_trunk_specs(o_p, o_s, tm)
    return pl.pallas_call(
        functools.partial(_ret_finalize_body, n_heads=RET_HEADS, pt=pt), grid=(n // tm,),
        in_specs=specs + [pl.BlockSpec((tm, d), lambda i: (i, 3))],
        out_specs=pl.BlockSpec((tm, d), lambda i: (i, 0)),
        out_shape=jax.ShapeDtypeStruct((n, d), BF16),
        compiler_params=_cparams(1), name="ret_finalize")(o_p, o_s, qkvg)


def _rwkv_mix_body(x_ref, xp_ref, xn_ref, nw_ref, sc_ref, sh_ref, mu_ref, o_ref, hext, *, tm, halo, lay):
    i = pl.program_id(0)
    d = x_ref.shape[1]
    nw, sc, sh = nw_ref[...], sc_ref[...], sh_ref[...]
    hext[0:halo, :] = _rms_mod(xp_ref[...], nw, sc, sh)
    hext[halo:halo + tm, :] = _rms_mod(x_ref[...], nw, sc, sh)
    hext[halo + tm:halo + tm + halo, :] = _rms_mod(xn_ref[...], nw, sc, sh)
    h = hext[halo:halo + tm, :]
    g_row = i * tm + lax.broadcasted_iota(jnp.int32, (tm, 1), 0)

    def emit(h_shift):
        diff = h_shift - h
        for n in range(6):
            o_ref[n] = (h + diff * mu_ref[n:n + 1, :]).astype(o_ref.dtype)

    def shifted(off, lo, hi, keep):
        return jnp.where(keep, hext[halo + off:halo + off + tm, lo:hi], 0.0)

    @pl.when(i < lay.p_rows // tm)
    def _():
        t = g_row % lay.p_len
        hd = d // 2
        emit(jnp.concatenate([shifted(-1, 0, hd, t != 0),
                              shifted(1, hd, d, t != lay.p_len - 1)], axis=1))

    @pl.when(i >= lay.p_rows // tm)
    def _():
        t = (g_row - lay.p_rows) % lay.s_len
        colw = t % GRID_W
        qd = d // 4
        emit(jnp.concatenate([shifted(-1, 0, qd, colw != 0),
                              shifted(1, qd, 2 * qd, colw != GRID_W - 1),
                              shifted(-GRID_W, 2 * qd, 3 * qd, t >= GRID_W),
                              shifted(GRID_W, 3 * qd, d, t < lay.s_len - GRID_W)], axis=1))


def _rwkv_mix(x, nw, mod5, layer, mu, lay, tm=512):
    n, d = x.shape
    halo = GRID_W
    tm = min(tm, lay.s_len, lay.p_rows)
    r = tm // halo
    n_halo_blk = n // halo
    body = functools.partial(_rwkv_mix_body, tm=tm, halo=halo, lay=lay)
    return pl.pallas_call(
        body, grid=(n // tm,),
        in_specs=[pl.BlockSpec((tm, d), lambda i: (i, 0)),
                  pl.BlockSpec((halo, d), lambda i: (jnp.maximum(i * r - 1, 0), 0)),
                  pl.BlockSpec((halo, d), lambda i: (jnp.minimum((i + 1) * r, n_halo_blk - 1), 0)),
                  pl.BlockSpec((1, d), lambda i: (0, 0)),
                  pl.BlockSpec((None, None, None, 1, d), _mod_spec(layer, 1, tm, lay)),
                  pl.BlockSpec((None, None, None, 1, d), _mod_spec(layer, 0, tm, lay)),
                  pl.BlockSpec((6, d), lambda i: (0, 0))],
        out_specs=pl.BlockSpec((6, tm, d), lambda i: (0, i, 0)),
        out_shape=jax.ShapeDtypeStruct((6, n, d), BF16),
        scratch_shapes=[pltpu.VMEM((tm + 2 * halo, d), F32)],
        compiler_params=_cparams(1), name="rwkv_mix")(x, x, x, nw.reshape(1, d), mod5, mod5, mu)


def _rwkv_scan_body(*refs, n_chunk, n_blk, has_s0, emit_state):
    refs = list(refs)
    r_ref, k_ref, v_ref, lw_ref, a_ref, kk_ref, ka_ref = refs[:7]
    refs = refs[7:]
    if has_s0:
        s0_ref = refs.pop(0)
    y_ref = refs.pop(0)
    if emit_state:
        so_ref = refs.pop(0)
    st = refs.pop(0)
    c_sz, hd, sw = RWKV_CHUNK, RWKV_HEAD_DIM, RWKV_SLAB
    n_grp = st.shape[0]
    hps = sw // hd
    d, cb = pl.program_id(1), pl.program_id(3)
    fwd = d == 0

    ri = lax.broadcasted_iota(jnp.int32, (sw, sw), 0)
    ci = lax.broadcasted_iota(jnp.int32, (sw, sw), 1)
    bd_mask = (ri // hd) == (ci // hd)

    def block_diag(slab):
        return jnp.where(bd_mask, jnp.concatenate([slab] * hps, axis=0), 0.0).astype(BF16)

    @pl.when(cb == 0)
    def _():
        if has_s0:
            st[...] = s0_ref[...]
        else:
            st[...] = jnp.zeros_like(st)

    srow = lax.broadcasted_iota(jnp.int32, (c_sz, sw), 0)
    scol = lax.broadcasted_iota(jnp.int32, (c_sz, sw), 1) % hd
    diff = jnp.where(fwd, srow - scol, scol - srow)
    strict = diff > 0
    incl = diff >= 0
    eye = jnp.where(diff == 0, 1.0, 0.0).astype(F32)
    levels = []
    m = 1
    while m < c_sz:
        levels.append(jnp.logical_and(srow // (2 * m) == scol // (2 * m), srow // m != scol // m))
        m *= 2
    trow = lax.broadcasted_iota(jnp.int32, (c_sz, c_sz), 0)
    tcol = lax.broadcasted_iota(jnp.int32, (c_sz, c_sz), 1)
    tri_incl = jnp.where(jnp.where(fwd, trow - tcol, tcol - trow) >= 0, 1.0, 0.0).astype(BF16)
    ones_bd = jnp.where(bd_mask, 1.0, 0.0).astype(BF16)

    def split(x, n_parts):
        parts = []
        for _ in range(n_parts - 1):
            p = x.astype(BF16)
            parts.append(p)
            x = x - p.astype(F32)
        return parts + [x.astype(BF16)]

    k_k, k_a = kk_ref[...], ka_ref[...]
    grp = range(n_grp)

    n_sub = min(RWKV_CHUNKS_PER_ITER, n_chunk)

    def chunks(j, carry):
        rows_of = []
        for cc in range(n_sub):
            jj = j * n_sub + cc
            jj = jnp.where(fwd, jj, n_chunk - 1 - jj)
            rows_of.append(pl.ds(pl.multiple_of(jj * c_sz, c_sz), c_sz))
        units = [(rows_of[cc], slice(g * sw, (g + 1) * sw)) for cc in range(n_sub) for g in grp]
        un = range(len(units))
        v = [v_ref[rows, sl] for rows, sl in units]
        kkr = [k_ref[rows, sl] * k_k[:, sl] for rows, sl in units]
        sq = [split(x * x, 2) for x in kkr]
        ssum = [sum(jnp.dot(p, ones_bd, preferred_element_type=F32) for p in s2) for s2 in sq]
        kk = [x * lax.rsqrt(s + 1e-12) for x, s in zip(kkr, ssum)]
        a = [a_ref[rows, sl] for rows, sl in units]
        b = [x * y for x, y in zip(kk, a)]
        kdir = [k_ref[rows, sl] * (1.0 + (ai - 1.0) * k_a[:, sl]) for (rows, sl), ai in zip(units, a)]
        lw = [lw_ref[rows, sl] for rows, sl in units]
        cum = [sum(jnp.dot(tri_incl, p, preferred_element_type=F32) for p in split(x, 3)) for x in lw]
        total = [jnp.sum(x, axis=0, keepdims=True) for x in lw]
        half = [0.5 * t for t in total]
        cumx = [c - x for c, x in zip(cum, lw)]
        r = [r_ref[rows, sl] for rows, sl in units]
        lhs_g = [jnp.concatenate([kk[i] * jnp.exp(cumx[i] - half[i]), r[i] * jnp.exp(cum[i] - half[i])], axis=0)
                 for i in un]
        e_neg = [jnp.exp(half[i] - cum[i]) for i in un]
        g_k = [_bdot_nt(lhs_g[i], block_diag(kdir[i] * e_neg[i])) for i in un]
        g_b = [_bdot_nt(lhs_g[i], block_diag(b[i] * e_neg[i])) for i in un]
        l_k = [jnp.where(strict, x[:c_sz], 0.0) for x in g_k]
        a_rk = [jnp.where(incl, x[c_sz:], 0.0) for x in g_k]
        l_b = [jnp.where(strict, x[:c_sz], 0.0) for x in g_b]
        a_rb = [jnp.where(incl, x[c_sz:], 0.0) for x in g_b]
        x = [eye - jnp.where(levels[0], l, 0.0) for l in l_b]
        for lvl in levels[1:]:
            t = [_bdot(x[i], block_diag(jnp.where(lvl, l_b[i], 0.0))) for i in un]
            x = [x[i] - _bdot(t[i], block_diag(x[i])) for i in un]
        bd_v = [block_diag(x) for x in v]
        lkv = [_bdot(l_k[i], bd_v[i]) for i in un]
        wt = [_bdot(x[i], block_diag(kk[i] * jnp.exp(cumx[i]))) for i in un]
        vt = [_bdot(x[i], block_diag(lkv[i])) for i in un]
        r_abs = [r[i] * jnp.exp(cum[i]) for i in un]
        e_end = [jnp.exp(total[i] - cum[i]) for i in un]
        kb_end = [jnp.concatenate([kdir[i] * e_end[i], -(b[i] * e_end[i])], axis=0) for i in un]
        state = [st[g] for g in grp]
        for cc in range(n_sub):
            ids = [cc * n_grp + g for g in grp]
            su = [_bdot_nt(jnp.concatenate([wt[i], r_abs[i]], axis=0), state[g]) for g, i in zip(grp, ids)]
            u = [su[g][:c_sz] + vt[i] for g, i in zip(grp, ids)]
            y = [su[g][c_sz:] + _bdot(a_rk[i], bd_v[i]) - _bdot(a_rb[i], block_diag(u[g]))
                 for g, i in zip(grp, ids)]
            upd = [_bdot_tn(jnp.concatenate([v[i], u[g]], axis=0), kb_end[i]) for g, i in zip(grp, ids)]
            state = [state[g] * jnp.exp(total[i]) + jnp.where(bd_mask, upd[g], 0.0) for g, i in zip(grp, ids)]
            y_ref[rows_of[cc], :] = jnp.concatenate(y, axis=1)
        st[...] = jnp.stack(state)
        return carry

    lax.fori_loop(0, n_chunk // n_sub, chunks, 0)

    if emit_state:
        @pl.when(cb == n_blk - 1)
        def _():
            for g in grp:
                s = st[g]
                for h in range(hps):
                    so_ref[g * hps + h] = s[h * hd:(h + 1) * hd, h * hd:(h + 1) * hd]


def _rwkv_scan(rkv, lw, a, k_k, k_a, s0, *, row0, n_seq, seq_len, emit_state, heads_per_step=32):
    _, _, d_model = rkv.shape
    hd, sw = RWKV_HEAD_DIM, RWKV_SLAB
    n_heads = d_model // hd
    g = min(heads_per_step, n_heads)
    wg = g * hd
    n_grp = wg // sw
    n_hg = n_heads // g
    tb = min(256, seq_len)
    n_blk = seq_len // tb
    rb0 = row0 // tb

    def blk(c, d):
        return jnp.where(d == 0, c, n_blk - 1 - c)

    def rkv_map(part):
        return lambda s, d, hg, c: (part, rb0 + s * n_blk + blk(c, d), hg)

    dir_map = lambda s, d, hg, c: (d, rb0 + s * n_blk + blk(c, d), hg)
    in_specs = ([pl.BlockSpec((None, tb, wg), rkv_map(p)) for p in range(3)]
                + [pl.BlockSpec((None, tb, wg), dir_map)] * 2
                + [pl.BlockSpec((1, wg), lambda s, d, hg, c: (0, hg))] * 2)
    args = [rkv, rkv, rkv, lw, a, k_k.reshape(1, d_model), k_a.reshape(1, d_model)]
    if s0 is not None:
        in_specs.append(pl.BlockSpec((None, None, n_grp, sw, sw), lambda s, d, hg, c: (s, d, hg, 0, 0)))
        args.append(s0)
    out_specs = [pl.BlockSpec((None, tb, wg), lambda s, d, hg, c: (d, s * n_blk + blk(c, d), hg))]
    out_shape = [jax.ShapeDtypeStruct((2, n_seq * seq_len, d_model), F32)]
    if emit_state:
        out_specs.append(pl.BlockSpec((None, None, g, hd, hd), lambda s, d, hg, c: (s, d, hg, 0, 0)))
        out_shape.append(jax.ShapeDtypeStruct((n_seq, 2, n_heads, hd, hd), F32))
    body = functools.partial(_rwkv_scan_body, n_chunk=tb // RWKV_CHUNK, n_blk=n_blk,
                             has_s0=s0 is not None, emit_state=emit_state)
    return pl.pallas_call(
        body, grid=(n_seq, 2, n_hg, n_blk), in_specs=in_specs, out_specs=out_specs, out_shape=out_shape,
        scratch_shapes=[pltpu.VMEM((n_grp, sw, sw), F32)],
        compiler_params=_cparams(4), name="rwkv_scan")(*args)


def _group_sum(x, ones_bd):
    w = ones_bd.shape[0]
    out = []
    for c in range(x.shape[1] // w):
        xs = x[:, c * w:(c + 1) * w]
        hi = xs.astype(BF16)
        lo = (xs - hi.astype(F32)).astype(BF16)
        out.append(jnp.dot(hi, ones_bd, preferred_element_type=F32)
                   + jnp.dot(lo, ones_bd, preferred_element_type=F32))
    return jnp.concatenate(out, axis=1)


def _rwkv_finalize_body(yp_ref, ys_ref, rkv_ref, a_ref, g_ref, ka_ref, rk_ref, lnw_ref, lnb_ref, o_ref, *, pt):
    hd = RWKV_HEAD_DIM
    w = RWKV_SLAB
    ri = lax.broadcasted_iota(jnp.int32, (w, w), 0)
    ci = lax.broadcasted_iota(jnp.int32, (w, w), 1)
    ones_bd = jnp.where(ri // hd == ci // hd, 1.0, 0.0).astype(BF16)

    def finalize(y_ref):
        y = y_ref[0] + y_ref[1]
        mean = _group_sum(y, ones_bd) * (1.0 / hd)
        yc = y - mean
        var = _group_sum(yc * yc, ones_bd) * (1.0 / hd)
        yn = yc * lax.rsqrt(var + RWKV_LN_EPS) * lnw_ref[...] + lnb_ref[...]
        r, k, v = rkv_ref[0], rkv_ref[1], rkv_ref[2]
        a_sum = a_ref[0] + a_ref[1]
        k_sum = k * (2.0 + (a_sum - 2.0) * ka_ref[...])
        bonus = _group_sum(r * k_sum * rk_ref[...], ones_bd) * v
        o_ref[...] = ((yn + bonus) * g_ref[...]).astype(o_ref.dtype)

    _for_trunk_of_tile(pt, yp_ref, ys_ref, finalize)


def _rwkv_finalize(y_p, y_s, rkv, a, g, k_a, r_k, ln_w, ln_b, tm=256):
    n, d = y_p.shape[1] + y_s.shape[1], y_p.shape[2]
    specs, pt = _two_trunk_specs(y_p, y_s, tm)
    row = lambda i: (0, 0)
    return pl.pallas_call(
        functools.partial(_rwkv_finalize_body, pt=pt), grid=(n // tm,),
        in_specs=specs + [
                  pl.BlockSpec((3, tm, d), lambda i: (0, i, 0)),
                  pl.BlockSpec((2, tm, d), lambda i: (0, i, 0)),
                  pl.BlockSpec((tm, d), lambda i: (i, 0)),
                  pl.BlockSpec((1, d), row), pl.BlockSpec((1, d), row),
                  pl.BlockSpec((1, d), row), pl.BlockSpec((1, d), row)],
        out_specs=pl.BlockSpec((tm, d), lambda i: (i, 0)),
        out_shape=jax.ShapeDtypeStruct((n, d), BF16),
        compiler_params=_cparams(1), name="rwkv_finalize",
    )(y_p, y_s, rkv, a, g, k_a.reshape(1, d), r_k.reshape(1, d), ln_w.reshape(1, d), ln_b.reshape(1, d))


def _cast_expert_weights(be_ref, pairs):
    b = pl.program_id(1)

    @pl.when(jnp.logical_or(b == 0, be_ref[b] != be_ref[jnp.maximum(b - 1, 0)]))
    def _():
        for w_ref, w_bf in pairs:
            w_bf[...] = w_ref[...].astype(BF16)


def _moe_up_body(be_ref, na_ref, x_ref, wg_ref, wu_ref, o_ref, wg_bf, wu_bf):
    b = pl.program_id(1)
    _cast_expert_weights(be_ref, [(wg_ref, wg_bf), (wu_ref, wu_bf)])

    @pl.when(b < na_ref[0])
    def _():
        x = x_ref[...]
        gate = jnp.dot(x, wg_bf[...], preferred_element_type=F32)
        up = jnp.dot(x, wu_bf[...], preferred_element_type=F32)
        o_ref[...] = (_silu(gate) * up).astype(o_ref.dtype)

    @pl.when(b >= na_ref[0])
    def _():
        o_ref[...] = jnp.zeros_like(o_ref)


def _moe_down_body(be_ref, na_ref, x_ref, wd_ref, o_ref, wd_bf):
    b = pl.program_id(1)
    _cast_expert_weights(be_ref, [(wd_ref, wd_bf)])

    @pl.when(b < na_ref[0])
    def _():
        o_ref[...] = jnp.dot(x_ref[...], wd_bf[...], preferred_element_type=F32)

    @pl.when(b >= na_ref[0])
    def _():
        o_ref[...] = jnp.zeros_like(o_ref)


def _moe_experts(xb, block_e, n_active, w_gate, w_up, w_down, tn_up=512, tn_down=512):
    rows, d = xb.shape
    bm = MOE_BLOCK
    n_blk = rows // bm
    d_ff = w_gate.shape[2]
    tn_up, tn_down = min(tn_up, d_ff), min(tn_down, d)
    act = pl.pallas_call(
        _moe_up_body,
        grid_spec=pltpu.PrefetchScalarGridSpec(
            num_scalar_prefetch=2, grid=(d_ff // tn_up, n_blk),
            in_specs=[pl.BlockSpec((bm, d), lambda j, b, be, na: (b, 0)),
                      pl.BlockSpec((None, d, tn_up), lambda j, b, be, na: (be[b], 0, j)),
                      pl.BlockSpec((None, d, tn_up), lambda j, b, be, na: (be[b], 0, j))],
            out_specs=pl.BlockSpec((bm, tn_up), lambda j, b, be, na: (b, j)),
            scratch_shapes=[pltpu.VMEM((d, tn_up), BF16)] * 2),
        out_shape=jax.ShapeDtypeStruct((rows, d_ff), BF16),
        compiler_params=_cparams(2), name="moe_up")(block_e, n_active, xb, w_gate, w_up)
    return pl.pallas_call(
        _moe_down_body,
        grid_spec=pltpu.PrefetchScalarGridSpec(
            num_scalar_prefetch=2, grid=(d // tn_down, n_blk),
            in_specs=[pl.BlockSpec((bm, d_ff), lambda j, b, be, na: (b, 0)),
                      pl.BlockSpec((None, d_ff, tn_down), lambda j, b, be, na: (be[b], 0, j))],
            out_specs=pl.BlockSpec((bm, tn_down), lambda j, b, be, na: (b, j)),
            scratch_shapes=[pltpu.VMEM((d_ff, tn_down), BF16)]),
        out_shape=jax.ShapeDtypeStruct((rows, d), F32),
        compiler_params=_cparams(2), name="moe_down")(block_e, n_active, act, w_down)


def _moe_combine_body(x_ref, y0_ref, y1_ref, g_ref, gt_ref, o_ref):
    g = g_ref[...]
    y = y0_ref[...] * g[:, 0:1] + y1_ref[...] * g[:, 1:2]
    o_ref[...] = x_ref[...] + gt_ref[...] * y


def _moe_combine(x, y0, y1, gates, mod5, layer, lay, tm=512):
    n, d = x.shape
    tm = min(tm, lay.s_len, lay.p_rows)
    row = pl.BlockSpec((tm, d), lambda i: (i, 0))
    return pl.pallas_call(
        _moe_combine_body, grid=(n // tm,),
        in_specs=[row, row, row, pl.BlockSpec((tm, TOP_K), lambda i: (i, 0)),
                  pl.BlockSpec((None, None, None, 1, d), _mod_spec(layer, 5, tm, lay))],
        out_specs=row, out_shape=jax.ShapeDtypeStruct((n, d), F32),
        compiler_params=_cparams(1), name="moe_combine")(x, y0, y1, gates, mod5)


def _moe(h, logits, w_gate, w_up, w_down):
    n_tok, d = h.shape
    bm = MOE_BLOCK
    top_val, top_idx = lax.top_k(logits, TOP_K)
    gates = jax.nn.softmax(top_val, axis=-1)
    n_assign = n_tok * TOP_K
    flat_e = top_idx.reshape(-1).astype(jnp.int32)
    order = jnp.argsort(flat_e).astype(jnp.int32)
    rank = jnp.argsort(order).astype(jnp.int32)
    counts = jnp.sum(flat_e[:, None] == jnp.arange(N_EXPERTS, dtype=jnp.int32)[None, :], axis=0,
                     dtype=jnp.int32)
    padded = ((counts + bm - 1) // bm) * bm
    pad_end = jnp.cumsum(padded)
    pad_start = pad_end - padded
    start = jnp.cumsum(counts) - counts
    n_blocks = -(-n_assign // bm) + N_EXPERTS
    block_e = jnp.minimum(jnp.searchsorted(pad_end, jnp.arange(n_blocks) * bm, side='right'),
                          N_EXPERTS - 1).astype(jnp.int32)
    n_active = (pad_end[-1] // bm).astype(jnp.int32).reshape(1)
    experts = jnp.arange(N_EXPERTS, dtype=jnp.int32)

    def lookup(table, e):
        return jnp.sum(jnp.where(e[..., None] == experts, table, 0), axis=-1, dtype=jnp.int32)

    slot = jnp.arange(n_blocks * bm, dtype=jnp.int32)
    slot_e = jnp.repeat(block_e, bm)
    pos = slot - lookup(pad_start, slot_e)
    src = jnp.clip(lookup(start, slot_e) + pos, 0, n_assign - 1)
    slot_tok = jnp.where(pos < lookup(counts, slot_e),
                         jnp.take(order, src, indices_are_sorted=True) // TOP_K, slot % n_tok)
    xb = h[slot_tok]
    yb = _moe_experts(xb, block_e, n_active, w_gate, w_up, w_down)
    slot_of = (lookup(pad_start - start, flat_e) + rank).reshape(n_tok, TOP_K)
    return yb[slot_of[:, 0]], yb[slot_of[:, 1]], gates


def _final_norm_body(x_ref, w_ref, o_ref):
    x = x_ref[...]
    o_ref[...] = x * lax.rsqrt(jnp.mean(x * x, axis=-1, keepdims=True) + NORM_EPS) * w_ref[...]


def _final_norm(x, w, row0, n_rows, tm=512):
    d = x.shape[1]
    tm = min(tm, n_rows)
    rb0 = row0 // tm
    return pl.pallas_call(
        _final_norm_body, grid=(n_rows // tm,),
        in_specs=[pl.BlockSpec((tm, d), lambda i: (rb0 + i, 0)), pl.BlockSpec((1, d), lambda i: (0, 0))],
        out_specs=pl.BlockSpec((tm, d), lambda i: (i, 0)),
        out_shape=jax.ShapeDtypeStruct((n_rows, d), F32),
        compiler_params=_cparams(1), name="final_norm")(x, w.reshape(1, d))


def _rope_tables(seq_len, dim):
    t = jnp.arange(seq_len)
    row = (t // GRID_W).astype(F32)
    col = (t % GRID_W).astype(F32)
    half = dim // 2
    inv = ROPE_BASE ** (-(jnp.arange(0, half, 2, dtype=F32) / half))
    ang = jnp.concatenate([row[:, None] * inv, col[:, None] * inv], axis=-1)
    cos, sin = jnp.cos(ang), jnp.sin(ang)
    return jnp.repeat(cos, 2, axis=-1), jnp.stack([-sin, sin], axis=-1).reshape(seq_len, dim)


def _block_diag_states(s):
    b, two, n_heads, hd, _ = s.shape
    hps = RWKV_SLAB // hd
    s6 = s.reshape(b, two, n_heads // hps, hps, hd, hd)
    bd = jnp.einsum('bdghvk,hi->bdghvik', s6, jnp.eye(hps, dtype=s.dtype))
    return bd.reshape(b, two, n_heads // hps, RWKV_SLAB, RWKV_SLAB)


def _pad_cols(w, width):
    return jnp.pad(w, ((0, 0), (0, width - w.shape[1])))


def kernel(x_prompt, x_sample, state_l0_ret, state_l1_rwkv, c, c_ctx, ada_w, ada_b, norm_w, final_norm_w,
           l0_ret_w_in, l0_ret_w_out, l0_ret_decay, l0_ffn_w_gate, l0_ffn_w_up, l0_ffn_w_down,
           l1_rwkv_mu, l1_rwkv_w_rkv, l1_rwkv_w0, l1_rwkv_w1, l1_rwkv_w2, l1_rwkv_a0, l1_rwkv_a1, l1_rwkv_a2,
           l1_rwkv_g1, l1_rwkv_g2, l1_rwkv_k_k, l1_rwkv_k_a, l1_rwkv_r_k, l1_rwkv_ln_w, l1_rwkv_ln_b,
           l1_rwkv_w_out, l1_moe_router, l1_moe_w_gate, l1_moe_w_up, l1_moe_w_down):
    pb, p_len, d = x_prompt.shape
    sb, s_len, _ = x_sample.shape
    p_rows, s_rows = pb * p_len, sb * s_len
    lay = _Layout(p_rows, p_len, s_len, p_rows + s_rows)
    n_layers = ada_w.shape[0]

    x = jnp.concatenate([x_prompt.reshape(p_rows, d), x_sample.reshape(s_rows, d)], axis=0)
    cond8 = jnp.concatenate([c_ctx[None, :], c, jnp.zeros((8 - 1 - sb, d), F32)], axis=0)
    mod5 = _ada_mod(cond8, ada_w, ada_b)[:, :1 + sb].reshape(n_layers, 1 + sb, 6, 1, d)

    h = _norm_mod(x, norm_w[0, 0], mod5, 0, 1, 0, lay)
    qkvg = _linear(h, l0_ret_w_in, tn=1024)
    log_gamma = -jnp.exp(l0_ret_decay.astype(F32))
    dh = d // RET_HEADS
    o_p, new_state_l0_ret = _retention(qkvg, log_gamma, None, None, row0=0, n_seq=pb, seq_len=p_len,
                                       emit_state=True)
    (o_s,) = _retention(qkvg, log_gamma, state_l0_ret, _rope_tables(s_len, dh), row0=p_rows, n_seq=sb,
                        seq_len=s_len, emit_state=False)
    x = _linear_residual(_ret_finalize(o_p, o_s, qkvg), l0_ret_w_out, x, mod5, 0, 2, lay)
    h = _norm_mod(x, norm_w[0, 1], mod5, 0, 4, 3, lay)
    act = _swiglu_up(h, l0_ffn_w_gate, l0_ffn_w_up)
    x = _linear_residual(act, l0_ffn_w_down, x, mod5, 0, 5, lay, tm=512, tn=512)

    xs = _rwkv_mix(x, norm_w[1, 0], mod5, 1, l1_rwkv_mu, lay)
    rkv = _linear(xs, l1_rwkv_w_rkv, n_batch=3, tn=1024)
    lora_w = l1_rwkv_w1.shape[2]
    pad_w = -(-lora_w // 128) * 128
    w1p = jnp.concatenate([_pad_cols(l1_rwkv_w1[0], pad_w), _pad_cols(l1_rwkv_w1[1], pad_w)], axis=1)
    a1p = jnp.concatenate([_pad_cols(l1_rwkv_a1[0], pad_w), _pad_cols(l1_rwkv_a1[1], pad_w)], axis=1)
    t_w = _linear(xs, w1p, x_lead=3, act="tanh", out_dtype=BF16)
    t_a = _linear(xs, a1p, x_lead=4, out_dtype=BF16)
    t_g = _linear(xs, l1_rwkv_g1, x_lead=5, act="sigmoid", out_dtype=BF16)
    w2p = jnp.pad(l1_rwkv_w2, ((0, 0), (0, pad_w - lora_w), (0, 0)))
    a2p = jnp.pad(l1_rwkv_a2, ((0, 0), (0, pad_w - lora_w), (0, 0)))
    lw = _linear(t_w, w2p, n_batch=2, bias=l1_rwkv_w0, act="log_decay")
    a_lr = _linear(t_a, a2p, n_batch=2, bias=l1_rwkv_a0, act="sigmoid")
    g = _linear(t_g, l1_rwkv_g2)
    scan_args = (rkv, lw, a_lr, l1_rwkv_k_k, l1_rwkv_k_a)
    y_p, new_state_l1_rwkv = _rwkv_scan(*scan_args, None, row0=0, n_seq=pb, seq_len=p_len, emit_state=True)
    (y_s,) = _rwkv_scan(*scan_args, _block_diag_states(state_l1_rwkv), row0=p_rows, n_seq=sb, seq_len=s_len,
                        emit_state=False)
    yg = _rwkv_finalize(y_p, y_s, rkv, a_lr, g, l1_rwkv_k_a, l1_rwkv_r_k, l1_rwkv_ln_w, l1_rwkv_ln_b)
    x = _linear_residual(yg, l1_rwkv_w_out, x, mod5, 1, 2, lay)
    ne_pad = 128
    h, logits = _norm_mod(x, norm_w[1, 1], mod5, 1, 4, 3, lay, router_w=_pad_cols(l1_moe_router, ne_pad))
    y0, y1, gates = _moe(h, logits[:, :N_EXPERTS], l1_moe_w_gate, l1_moe_w_up, l1_moe_w_down)
    x = _moe_combine(x, y0, y1, gates, mod5, 1, lay)

    y_prompt = _final_norm(x, final_norm_w, 0, p_rows).reshape(pb, p_len, d)
    y_sample = _final_norm(x, final_norm_w, p_rows, s_rows).reshape(sb, s_len, d)
    return (y_prompt, y_sample, new_state_l0_ret, new_state_l1_rwkv)
```

```python
import functools
from typing import NamedTuple

import jax
import jax.numpy as jnp
from jax import lax
from jax.experimental import pallas as pl
from jax.experimental.pallas import tpu as pltpu

F32 = jnp.float32
BF16 = jnp.bfloat16

GRID_W = 64
RET_HEADS = 8
RET_CHUNK = 128
RET_HEADS_PER_STEP = 2
ROPE_BASE = 10000.0
RWKV_HEAD_DIM = 64
RWKV_CHUNK = 64
RWKV_SLAB = 256
RWKV_CHUNKS_PER_ITER = 2
RWKV_LN_EPS = 64e-5
N_EXPERTS = 8
TOP_K = 2
MOE_BLOCK = 512
NORM_EPS = 1e-6

VMEM_LIMIT_BYTES = 56 * 1024 * 1024


class _Layout(NamedTuple):
    p_rows: int
    p_len: int
    s_len: int
    n_rows: int


def _cparams(n_axes):
    return pltpu.CompilerParams(dimension_semantics=("arbitrary",) * n_axes,
                                vmem_limit_bytes=VMEM_LIMIT_BYTES)


def _cond_of_tile(i, tm, lay):
    pt = lay.p_rows // tm
    st = lay.s_len // tm
    return jnp.where(i < pt, 0, 1 + (i - pt) // st)


def _bdot(a, b):
    return jnp.dot(a.astype(BF16), b.astype(BF16), preferred_element_type=F32)


def _bdot_nt(a, b):
    return lax.dot_general(a.astype(BF16), b.astype(BF16), (((1,), (1,)), ((), ())),
                           preferred_element_type=F32)


def _bdot_tn(a, b):
    return lax.dot_general(a.astype(BF16), b.astype(BF16), (((0,), (0,)), ((), ())),
                           preferred_element_type=F32)


def _silu(x):
    return x * jax.nn.sigmoid(x)


def _ada_body(c_ref, w_ref, b_ref, o_ref):
    o_ref[...] = _bdot(_silu(c_ref[...]), w_ref[...]) + b_ref[...]


def _ada_mod(cond8, ada_w, ada_b):
    n_layers, d, d6 = ada_w.shape
    tn = min(1024, d6)
    return pl.pallas_call(
        _ada_body,
        grid=(n_layers, d6 // tn),
        in_specs=[pl.BlockSpec((8, d), lambda l, j: (0, 0)),
                  pl.BlockSpec((None, d, tn), lambda l, j: (l, 0, j)),
                  pl.BlockSpec((None, 1, tn), lambda l, j: (l, 0, j))],
        out_specs=pl.BlockSpec((None, 8, tn), lambda l, j: (l, 0, j)),
        out_shape=jax.ShapeDtypeStruct((n_layers, 8, d6), F32),
        compiler_params=_cparams(2), name="ada_mod",
    )(cond8, ada_w, ada_b.reshape(n_layers, 1, d6))


def _mod_spec(layer, which, tm, lay, n_grid_axes=1):
    def imap(i, *_):
        return (layer, _cond_of_tile(i, tm, lay), which, 0, 0)
    return imap


def _rms_mod(x, nw, sc, sh):
    xn = x * lax.rsqrt(jnp.mean(x * x, axis=-1, keepdims=True) + NORM_EPS) * nw
    return xn * (1.0 + sc) + sh


def _norm_mod_body(x_ref, nw_ref, sc_ref, sh_ref, o_ref):
    o_ref[...] = _rms_mod(x_ref[...], nw_ref[...], sc_ref[...], sh_ref[...]).astype(o_ref.dtype)


def _norm_mod_router_body(x_ref, nw_ref, sc_ref, sh_ref, wr_ref, o_ref, lg_ref):
    h = _rms_mod(x_ref[...], nw_ref[...], sc_ref[...], sh_ref[...])
    o_ref[...] = h.astype(o_ref.dtype)
    lg_ref[...] = jnp.dot(h, wr_ref[...], preferred_element_type=F32, precision=lax.Precision.HIGHEST)


def _norm_mod(x, nw, mod5, layer, which_sc, which_sh, lay, router_w=None, tm=512):
    n, d = x.shape
    tm = min(tm, lay.s_len, lay.p_rows)
    in_specs = [pl.BlockSpec((tm, d), lambda i: (i, 0)),
                pl.BlockSpec((1, d), lambda i: (0, 0)),
                pl.BlockSpec((None, None, None, 1, d), _mod_spec(layer, which_sc, tm, lay)),
                pl.BlockSpec((None, None, None, 1, d), _mod_spec(layer, which_sh, tm, lay))]
    args = [x, nw.reshape(1, d), mod5, mod5]
    if router_w is None:
        return pl.pallas_call(
            _norm_mod_body, grid=(n // tm,), in_specs=in_specs,
            out_specs=pl.BlockSpec((tm, d), lambda i: (i, 0)),
            out_shape=jax.ShapeDtypeStruct((n, d), BF16),
            compiler_params=_cparams(1), name="norm_mod")(*args)
    ne = router_w.shape[1]
    return pl.pallas_call(
        _norm_mod_router_body, grid=(n // tm,),
        in_specs=in_specs + [pl.BlockSpec((d, ne), lambda i: (0, 0))],
        out_specs=[pl.BlockSpec((tm, d), lambda i: (i, 0)), pl.BlockSpec((tm, ne), lambda i: (i, 0))],
        out_shape=[jax.ShapeDtypeStruct((n, d), BF16), jax.ShapeDtypeStruct((n, ne), F32)],
        compiler_params=_cparams(1), name="norm_mod_router")(*args, router_w)


def _cast_weights_once(row_axis, pairs):
    @pl.when(pl.program_id(row_axis) == 0)
    def _():
        for w_ref, w_bf in pairs:
            w_bf[...] = w_ref[...].astype(BF16)


def _softplus(x):
    return jnp.maximum(x, 0.0) + jnp.log1p(jnp.exp(-jnp.abs(x)))


def _linear_body(x_ref, w_ref, *refs, act, has_bias):
    refs = list(refs)
    bias_ref = refs.pop(0) if has_bias else None
    o_ref, w_bf = refs
    _cast_weights_once(2, [(w_ref, w_bf)])
    acc = jnp.dot(x_ref[...].astype(BF16), w_bf[...], preferred_element_type=F32)
    if has_bias:
        acc = acc + bias_ref[...]
    if act == "tanh":
        acc = jnp.tanh(acc)
    elif act == "sigmoid":
        acc = jax.nn.sigmoid(acc)
    elif act == "log_decay":
        acc = -jnp.exp(-_softplus(-acc) - 0.5)
    o_ref[...] = acc.astype(o_ref.dtype)


def _linear(x, w, *, n_batch=None, x_lead=0, bias=None, act=None, out_dtype=F32, tm=1024, tn=512):
    k, n = w.shape[-2:]
    m = x.shape[-2]
    tm, tn = min(tm, m), min(tn, n)
    if x.ndim == 3:
        x_spec = pl.BlockSpec((None, tm, k), lambda b, j, i: (x_lead + b, i, 0))
    else:
        x_spec = pl.BlockSpec((tm, k), lambda b, j, i: (i, b))
    if w.ndim == 3:
        w_spec = pl.BlockSpec((None, k, tn), lambda b, j, i: (b, 0, j))
    else:
        w_spec = pl.BlockSpec((k, tn), lambda b, j, i: (0, j))
    if n_batch is None:
        out_spec = pl.BlockSpec((tm, tn), lambda b, j, i: (i, j))
        out_shape = jax.ShapeDtypeStruct((m, n), out_dtype)
    else:
        out_spec = pl.BlockSpec((None, tm, tn), lambda b, j, i: (b, i, j))
        out_shape = jax.ShapeDtypeStruct((n_batch, m, n), out_dtype)
    in_specs, args = [x_spec, w_spec], [x, w]
    if bias is not None:
        in_specs.append(pl.BlockSpec((None, 1, tn), lambda b, j, i: (b, 0, j)))
        args.append(bias.reshape(bias.shape[0], 1, n))
    return pl.pallas_call(
        functools.partial(_linear_body, act=act, has_bias=bias is not None),
        grid=(n_batch or 1, n // tn, m // tm), in_specs=in_specs,
        out_specs=out_spec, out_shape=out_shape,
        scratch_shapes=[pltpu.VMEM((k, tn), BF16)],
        compiler_params=_cparams(3), name="linear")(*args)


def _linear_res_body(x_ref, w_ref, res_ref, gt_ref, o_ref, w_bf):
    _cast_weights_once(1, [(w_ref, w_bf)])
    o_ref[...] = res_ref[...] + gt_ref[...] * jnp.dot(x_ref[...], w_bf[...], preferred_element_type=F32)


def _linear_residual(x, w, res, mod5, layer, which_gate, lay, tm=1024, tn=1024):
    m, k = x.shape
    n = w.shape[1]
    tm, tn = min(tm, lay.s_len, lay.p_rows), min(tn, n)

    def gmap(j, i):
        return (layer, _cond_of_tile(i, tm, lay), which_gate, 0, j)

    return pl.pallas_call(
        _linear_res_body, grid=(n // tn, m // tm),
        in_specs=[pl.BlockSpec((tm, k), lambda j, i: (i, 0)),
                  pl.BlockSpec((k, tn), lambda j, i: (0, j)),
                  pl.BlockSpec((tm, tn), lambda j, i: (i, j)),
                  pl.BlockSpec((None, None, None, 1, tn), gmap)],
        out_specs=pl.BlockSpec((tm, tn), lambda j, i: (i, j)),
        out_shape=jax.ShapeDtypeStruct((m, n), F32),
        scratch_shapes=[pltpu.VMEM((k, tn), BF16)],
        compiler_params=_cparams(2), name="linear_residual")(x, w, res, mod5)


def _swiglu_up_body(x_ref, wg_ref, wu_ref, o_ref, wg_bf, wu_bf):
    _cast_weights_once(1, [(wg_ref, wg_bf), (wu_ref, wu_bf)])
    x = x_ref[...]
    gate = jnp.dot(x, wg_bf[...], preferred_element_type=F32)
    up = jnp.dot(x, wu_bf[...], preferred_element_type=F32)
    o_ref[...] = (_silu(gate) * up).astype(o_ref.dtype)


def _swiglu_up(x, wg, wu, tm=1024, tn=512):
    m, k = x.shape
    n = wg.shape[1]
    tm, tn = min(tm, m), min(tn, n)
    return pl.pallas_call(
        _swiglu_up_body, grid=(n // tn, m // tm),
        in_specs=[pl.BlockSpec((tm, k), lambda j, i: (i, 0)),
                  pl.BlockSpec((k, tn), lambda j, i: (0, j)),
                  pl.BlockSpec((k, tn), lambda j, i: (0, j))],
        out_specs=pl.BlockSpec((tm, tn), lambda j, i: (i, j)),
        out_shape=jax.ShapeDtypeStruct((m, n), BF16),
        scratch_shapes=[pltpu.VMEM((k, tn), BF16)] * 2,
        compiler_params=_cparams(2), name="swiglu_up")(x, wg, wu)


def _rope(x, c, s):
    w = x.shape[-1]
    lane = lax.broadcasted_iota(jnp.int32, x.shape, x.ndim - 1)
    nxt = pltpu.roll(x, w - 1, axis=x.ndim - 1)
    prv = pltpu.roll(x, 1, axis=x.ndim - 1)
    return x * c + jnp.where(lane % 2 == 0, nxt, prv) * s


def _retention_body(lg_ref, *refs, n_chunk, n_blk, has_rope, has_s0, emit_state, scale):
    refs = list(refs)
    q_ref, k_ref, v_ref = refs[:3]
    refs = refs[3:]
    if has_rope:
        cos_ref, sin_ref = refs[:2]
        refs = refs[2:]
    if has_s0:
        s0_ref = refs.pop(0)
    o_ref = refs.pop(0)
    if emit_state:
        so_ref = refs.pop(0)
    st = refs.pop(0)
    c_sz = RET_CHUNK
    hp, dh = st.shape[0], st.shape[1]
    hg, d, cb = pl.program_id(1), pl.program_id(2), pl.program_id(3)
    heads = range(hp)
    lanes = [slice(h * dh, (h + 1) * dh) for h in heads]

    @pl.when(cb == 0)
    def _():
        st[...] = s0_ref[...] if has_s0 else jnp.zeros_like(st)

    lgv = [lg_ref[d, hg * hp + h] for h in heads]
    fwd = d == 0
    row = lax.broadcasted_iota(jnp.int32, (c_sz, c_sz), 0)
    col = lax.broadcasted_iota(jnp.int32, (c_sz, c_sz), 1)
    diff = jnp.where(fwd, row - col, col - row).astype(F32)
    intra = [jnp.where(diff >= 0, jnp.exp(jnp.maximum(diff, 0.0) * lg), 0.0) for lg in lgv]
    pos = lax.broadcasted_iota(jnp.int32, (c_sz, dh), 0)
    npos = jnp.where(fwd, pos, c_sz - 1 - pos).astype(F32)
    q_decay = [jnp.exp((npos + 1.0) * lg) for lg in lgv]
    k_decay = [jnp.exp((c_sz - 1.0 - npos) * lg) for lg in lgv]
    chunk_decay = [jnp.exp(jnp.full((1, dh), c_sz, F32) * lg) for lg in lgv]

    for j in range(n_chunk):
        jj = jnp.where(fwd, j, n_chunk - 1 - j)
        rows = pl.ds(pl.multiple_of(jj * c_sz, c_sz), c_sz)
        q = [q_ref[rows, sl] for sl in lanes]
        k = [k_ref[rows, sl] * scale for sl in lanes]
        v = [v_ref[rows, sl] for sl in lanes]
        if has_rope:
            c, s = cos_ref[rows, :], sin_ref[rows, :]
            q = [_rope(x, c, s) for x in q]
            k = [_rope(x, c, s) for x in k]
        state = [st[h] for h in heads]
        scores = [_bdot_nt(q[h], k[h]) * intra[h] for h in heads]
        o = [_bdot(scores[h], v[h]) + _bdot(q[h], state[h]) * q_decay[h] for h in heads]
        new_state = [state[h] * chunk_decay[h] + _bdot_tn(k[h] * k_decay[h], v[h]) for h in heads]
        o_ref[rows, :] = jnp.concatenate(o, axis=1)
        st[...] = jnp.stack(new_state)

    if emit_state:
        @pl.when(cb == n_blk - 1)
        def _():
            so_ref[...] = st[...]


def _retention(qkvg, log_gamma, s0, rope, *, row0, n_seq, seq_len, emit_state):
    d_model = qkvg.shape[1] // 4
    n_heads = RET_HEADS
    dh = d_model // n_heads
    tb = min(512, seq_len)
    n_blk = seq_len // tb
    rb0 = row0 // tb

    def blk(c, d):
        return jnp.where(d == 0, c, n_blk - 1 - c)

    hp = RET_HEADS_PER_STEP
    n_hg = n_heads // hp

    def in_map(part):
        return lambda s, h, d, c, lg: (rb0 + s * n_blk + blk(c, d), part * n_hg + h)

    in_specs = [pl.BlockSpec((tb, hp * dh), in_map(p)) for p in range(3)]
    args = [qkvg, qkvg, qkvg]
    if rope is not None:
        in_specs += [pl.BlockSpec((tb, dh), lambda s, h, d, c, lg: (blk(c, d), 0))] * 2
        args += list(rope)
    if s0 is not None:
        in_specs.append(pl.BlockSpec((None, None, hp, dh, dh), lambda s, h, d, c, lg: (s, d, h, 0, 0)))
        args.append(s0)
    out_specs = [pl.BlockSpec((None, tb, hp * dh), lambda s, h, d, c, lg: (d, s * n_blk + blk(c, d), h))]
    out_shape = [jax.ShapeDtypeStruct((2, n_seq * seq_len, d_model), F32)]
    if emit_state:
        out_specs.append(pl.BlockSpec((None, None, hp, dh, dh), lambda s, h, d, c, lg: (s, d, h, 0, 0)))
        out_shape.append(jax.ShapeDtypeStruct((n_seq, 2, n_heads, dh, dh), F32))
    body = functools.partial(_retention_body, n_chunk=tb // RET_CHUNK, n_blk=n_blk, has_rope=rope is not None,
                             has_s0=s0 is not None, emit_state=emit_state, scale=dh ** -0.5)
    return pl.pallas_call(
        body,
        grid_spec=pltpu.PrefetchScalarGridSpec(
            num_scalar_prefetch=1, grid=(n_seq, n_hg, 2, n_blk), in_specs=in_specs, out_specs=out_specs,
            scratch_shapes=[pltpu.VMEM((hp, dh, dh), F32)]),
        out_shape=out_shape, compiler_params=_cparams(4), name="retention")(log_gamma, *args)


def _two_trunk_specs(a_p, a_s, tm):
    pt = a_p.shape[1] // tm
    d = a_p.shape[2]
    return [pl.BlockSpec((2, tm, d), lambda i: (0, jnp.minimum(i, pt - 1), 0)),
            pl.BlockSpec((2, tm, d), lambda i: (0, jnp.maximum(i - pt, 0), 0))], pt


def _for_trunk_of_tile(pt, p_ref, s_ref, fn):
    i = pl.program_id(0)

    @pl.when(i < pt)
    def _():
        fn(p_ref)

    @pl.when(i >= pt)
    def _():
        fn(s_ref)


def _ret_finalize_body(op_ref, os_ref, g_ref, out_ref, *, n_heads, pt):
    def finalize(o_ref):
        o = o_ref[0] + o_ref[1]
        g = g_ref[...]
        dh = o.shape[1] // n_heads
        for h in range(n_heads):
            sl = slice(h * dh, (h + 1) * dh)
            oh = o[:, sl]
            oh = oh * lax.rsqrt(jnp.mean(oh * oh, axis=-1, keepdims=True) + NORM_EPS)
            out_ref[:, sl] = (oh * _silu(g[:, sl])).astype(out_ref.dtype)

    _for_trunk_of_tile(pt, op_ref, os_ref, finalize)


def _ret_finalize(o_p, o_s, qkvg, tm=256):
    n, d = o_p.shape[1] + o_s.shape[1], o_p.shape[2]
    specs, pt = _two_trunk_specs(o_p, o_s, tm)
    return pl.pallas_call(
        functools.partial(_ret_finalize_body, n_heads=RET_HEADS, pt=pt), grid=(n // tm,),
        in_specs=specs + [pl.BlockSpec((tm, d), lambda i: (i, 3))],
        out_specs=pl.BlockSpec((tm, d), lambda i: (i, 0)),
        out_shape=jax.ShapeDtypeStruct((n, d), BF16),
        compiler_params=_cparams(1), name="ret_finalize")(o_p, o_s, qkvg)


def _rwkv_mix_body(x_ref, xp_ref, xn_ref, nw_ref, sc_ref, sh_ref, mu_ref, o_ref, hext, *, tm, halo, lay):
    i = pl.program_id(0)
    d = x_ref.shape[1]
    nw, sc, sh = nw_ref[...], sc_ref[...], sh_ref[...]
    hext[0:halo, :] = _rms_mod(xp_ref[...], nw, sc, sh)
    hext[halo:halo + tm, :] = _rms_mod(x_ref[...], nw, sc, sh)
    hext[halo + tm:halo + tm + halo, :] = _rms_mod(xn_ref[...], nw, sc, sh)
    h = hext[halo:halo + tm, :]
    g_row = i * tm + lax.broadcasted_iota(jnp.int32, (tm, 1), 0)

    def emit(h_shift):
        diff = h_shift - h
        for n in range(6):
            o_ref[n] = (h + diff * mu_ref[n:n + 1, :]).astype(o_ref.dtype)

    def shifted(off, lo, hi, keep):
        return jnp.where(keep, hext[halo + off:halo + off + tm, lo:hi], 0.0)

    @pl.when(i < lay.p_rows // tm)
    def _():
        t = g_row % lay.p_len
        hd = d // 2
        emit(jnp.concatenate([shifted(-1, 0, hd, t != 0),
                              shifted(1, hd, d, t != lay.p_len - 1)], axis=1))

    @pl.when(i >= lay.p_rows // tm)
    def _():
        t = (g_row - lay.p_rows) % lay.s_len
        colw = t % GRID_W
        qd = d // 4
        emit(jnp.concatenate([shifted(-1, 0, qd, colw != 0),
                              shifted(1, qd, 2 * qd, colw != GRID_W - 1),
                              shifted(-GRID_W, 2 * qd, 3 * qd, t >= GRID_W),
                              shifted(GRID_W, 3 * qd, d, t < lay.s_len - GRID_W)], axis=1))


def _rwkv_mix(x, nw, mod5, layer, mu, lay, tm=512):
    n, d = x.shape
    halo = GRID_W
    tm = min(tm, lay.s_len, lay.p_rows)
    r = tm // halo
    n_halo_blk = n // halo
    body = functools.partial(_rwkv_mix_body, tm=tm, halo=halo, lay=lay)
    return pl.pallas_call(
        body, grid=(n // tm,),
        in_specs=[pl.BlockSpec((tm, d), lambda i: (i, 0)),
                  pl.BlockSpec((halo, d), lambda i: (jnp.maximum(i * r - 1, 0), 0)),
                  pl.BlockSpec((halo, d), lambda i: (jnp.minimum((i + 1) * r, n_halo_blk - 1), 0)),
                  pl.BlockSpec((1, d), lambda i: (0, 0)),
                  pl.BlockSpec((None, None, None, 1, d), _mod_spec(layer, 1, tm, lay)),
                  pl.BlockSpec((None, None, None, 1, d), _mod_spec(layer, 0, tm, lay)),
                  pl.BlockSpec((6, d), lambda i: (0, 0))],
        out_specs=pl.BlockSpec((6, tm, d), lambda i: (0, i, 0)),
        out_shape=jax.ShapeDtypeStruct((6, n, d), BF16),
        scratch_shapes=[pltpu.VMEM((tm + 2 * halo, d), F32)],
        compiler_params=_cparams(1), name="rwkv_mix")(x, x, x, nw.reshape(1, d), mod5, mod5, mu)


def _rwkv_scan_body(*refs, n_chunk, n_blk, has_s0, emit_state):
    refs = list(refs)
    r_ref, k_ref, v_ref, lw_ref, a_ref, kk_ref, ka_ref = refs[:7]
    refs = refs[7:]
    if has_s0:
        s0_ref = refs.pop(0)
    y_ref = refs.pop(0)
    if emit_state:
        so_ref = refs.pop(0)
    st = refs.pop(0)
    c_sz, hd, sw = RWKV_CHUNK, RWKV_HEAD_DIM, RWKV_SLAB
    n_grp = st.shape[0]
    hps = sw // hd
    d, cb = pl.program_id(1), pl.program_id(3)
    fwd = d == 0

    ri = lax.broadcasted_iota(jnp.int32, (sw, sw), 0)
    ci = lax.broadcasted_iota(jnp.int32, (sw, sw), 1)
    bd_mask = (ri // hd) == (ci // hd)

    def block_diag(slab):
        return jnp.where(bd_mask, jnp.concatenate([slab] * hps, axis=0), 0.0).astype(BF16)

    @pl.when(cb == 0)
    def _():
        if has_s0:
            st[...] = s0_ref[...]
        else:
            st[...] = jnp.zeros_like(st)

    srow = lax.broadcasted_iota(jnp.int32, (c_sz, sw), 0)
    scol = lax.broadcasted_iota(jnp.int32, (c_sz, sw), 1) % hd
    diff = jnp.where(fwd, srow - scol, scol - srow)
    strict = diff > 0
    incl = diff >= 0
    eye = jnp.where(diff == 0, 1.0, 0.0).astype(F32)
    levels = []
    m = 1
    while m < c_sz:
        levels.append(jnp.logical_and(srow // (2 * m) == scol // (2 * m), srow // m != scol // m))
        m *= 2
    trow = lax.broadcasted_iota(jnp.int32, (c_sz, c_sz), 0)
    tcol = lax.broadcasted_iota(jnp.int32, (c_sz, c_sz), 1)
    tri_incl = jnp.where(jnp.where(fwd, trow - tcol, tcol - trow) >= 0, 1.0, 0.0).astype(BF16)
    ones_bd = jnp.where(bd_mask, 1.0, 0.0).astype(BF16)

    def split(x, n_parts):
        parts = []
        for _ in range(n_parts - 1):
            p = x.astype(BF16)
            parts.append(p)
            x = x - p.astype(F32)
        return parts + [x.astype(BF16)]

    k_k, k_a = kk_ref[...], ka_ref[...]
    grp = range(n_grp)

    n_sub = min(RWKV_CHUNKS_PER_ITER, n_chunk)

    def chunks(j, carry):
        rows_of = []
        for cc in range(n_sub):
            jj = j * n_sub + cc
            jj = jnp.where(fwd, jj, n_chunk - 1 - jj)
            rows_of.append(pl.ds(pl.multiple_of(jj * c_sz, c_sz), c_sz))
        units = [(rows_of[cc], slice(g * sw, (g + 1) * sw)) for cc in range(n_sub) for g in grp]
        un = range(len(units))
        v = [v_ref[rows, sl] for rows, sl in units]
        kkr = [k_ref[rows, sl] * k_k[:, sl] for rows, sl in units]
        sq = [split(x * x, 2) for x in kkr]
        ssum = [sum(jnp.dot(p, ones_bd, preferred_element_type=F32) for p in s2) for s2 in sq]
        kk = [x * lax.rsqrt(s + 1e-12) for x, s in zip(kkr, ssum)]
        a = [a_ref[rows, sl] for rows, sl in units]
        b = [x * y for x, y in zip(kk, a)]
        kdir = [k_ref[rows, sl] * (1.0 + (ai - 1.0) * k_a[:, sl]) for (rows, sl), ai in zip(units, a)]
        lw = [lw_ref[rows, sl] for rows, sl in units]
        cum = [sum(jnp.dot(tri_incl, p, preferred_element_type=F32) for p in split(x, 3)) for x in lw]
        total = [jnp.sum(x, axis=0, keepdims=True) for x in lw]
        half = [0.5 * t for t in total]
        cumx = [c - x for c, x in zip(cum, lw)]
        r = [r_ref[rows, sl] for rows, sl in units]
        lhs_g = [jnp.concatenate([kk[i] * jnp.exp(cumx[i] - half[i]), r[i] * jnp.exp(cum[i] - half[i])], axis=0)
                 for i in un]
        e_neg = [jnp.exp(half[i] - cum[i]) for i in un]
        g_k = [_bdot_nt(lhs_g[i], block_diag(kdir[i] * e_neg[i])) for i in un]
        g_b = [_bdot_nt(lhs_g[i], block_diag(b[i] * e_neg[i])) for i in un]
        l_k = [jnp.where(strict, x[:c_sz], 0.0) for x in g_k]
        a_rk = [jnp.where(incl, x[c_sz:], 0.0) for x in g_k]
        l_b = [jnp.where(strict, x[:c_sz], 0.0) for x in g_b]
        a_rb = [jnp.where(incl, x[c_sz:], 0.0) for x in g_b]
        x = [eye - jnp.where(levels[0], l, 0.0) for l in l_b]
        for lvl in levels[1:]:
            t = [_bdot(x[i], block_diag(jnp.where(lvl, l_b[i], 0.0))) for i in un]
            x = [x[i] - _bdot(t[i], block_diag(x[i])) for i in un]
        bd_v = [block_diag(x) for x in v]
        lkv = [_bdot(l_k[i], bd_v[i]) for i in un]
        wt = [_bdot(x[i], block_diag(kk[i] * jnp.exp(cumx[i]))) for i in un]
        vt = [_bdot(x[i], block_diag(lkv[i])) for i in un]
        r_abs = [r[i] * jnp.exp(cum[i]) for i in un]
        e_end = [jnp.exp(total[i] - cum[i]) for i in un]
        kb_end = [jnp.concatenate([kdir[i] * e_end[i], -(b[i] * e_end[i])], axis=0) for i in un]
        state = [st[g] for g in grp]
        for cc in range(n_sub):
            ids = [cc * n_grp + g for g in grp]
            su = [_bdot_nt(jnp.concatenate([wt[i], r_abs[i]], axis=0), state[g]) for g, i in zip(grp, ids)]
            u = [su[g][:c_sz] + vt[i] for g, i in zip(grp, ids)]
            y = [su[g][c_sz:] + _bdot(a_rk[i], bd_v[i]) - _bdot(a_rb[i], block_diag(u[g]))
                 for g, i in zip(grp, ids)]
            upd = [_bdot_tn(jnp.concatenate([v[i], u[g]], axis=0), kb_end[i]) for g, i in zip(grp, ids)]
            state = [state[g] * jnp.exp(total[i]) + jnp.where(bd_mask, upd[g], 0.0) for g, i in zip(grp, ids)]
            y_ref[rows_of[cc], :] = jnp.concatenate(y, axis=1)
        st[...] = jnp.stack(state)
        return carry

    lax.fori_loop(0, n_chunk // n_sub, chunks, 0)

    if emit_state:
        @pl.when(cb == n_blk - 1)
        def _():
            for g in grp:
                s = st[g]
                for h in range(hps):
                    so_ref[g * hps + h] = s[h * hd:(h + 1) * hd, h * hd:(h + 1) * hd]


def _rwkv_scan(rkv, lw, a, k_k, k_a, s0, *, row0, n_seq, seq_len, emit_state, heads_per_step=32):
    _, _, d_model = rkv.shape
    hd, sw = RWKV_HEAD_DIM, RWKV_SLAB
    n_heads = d_model // hd
    g = min(heads_per_step, n_heads)
    wg = g * hd
    n_grp = wg // sw
    n_hg = n_heads // g
    tb = min(256, seq_len)
    n_blk = seq_len // tb
    rb0 = row0 // tb

    def blk(c, d):
        return jnp.where(d == 0, c, n_blk - 1 - c)

    def rkv_map(part):
        return lambda s, d, hg, c: (part, rb0 + s * n_blk + blk(c, d), hg)

    dir_map = lambda s, d, hg, c: (d, rb0 + s * n_blk + blk(c, d), hg)
    in_specs = ([pl.BlockSpec((None, tb, wg), rkv_map(p)) for p in range(3)]
                + [pl.BlockSpec((None, tb, wg), dir_map)] * 2
                + [pl.BlockSpec((1, wg), lambda s, d, hg, c: (0, hg))] * 2)
    args = [rkv, rkv, rkv, lw, a, k_k.reshape(1, d_model), k_a.reshape(1, d_model)]
    if s0 is not None:
        in_specs.append(pl.BlockSpec((None, None, n_grp, sw, sw), lambda s, d, hg, c: (s, d, hg, 0, 0)))
        args.append(s0)
    out_specs = [pl.BlockSpec((None, tb, wg), lambda s, d, hg, c: (d, s * n_blk + blk(c, d), hg))]
    out_shape = [jax.ShapeDtypeStruct((2, n_seq * seq_len, d_model), F32)]
    if emit_state:
        out_specs.append(pl.BlockSpec((None, None, g, hd, hd), lambda s, d, hg, c: (s, d, hg, 0, 0)))
        out_shape.append(jax.ShapeDtypeStruct((n_seq, 2, n_heads, hd, hd), F32))
    body = functools.partial(_rwkv_scan_body, n_chunk=tb // RWKV_CHUNK, n_blk=n_blk,
                             has_s0=s0 is not None, emit_state=emit_state)
    return pl.pallas_call(
        body, grid=(n_seq, 2, n_hg, n_blk), in_specs=in_specs, out_specs=out_specs, out_shape=out_shape,
        scratch_shapes=[pltpu.VMEM((n_grp, sw, sw), F32)],
        compiler_params=_cparams(4), name="rwkv_scan")(*args)


def _group_sum(x, ones_bd):
    w = ones_bd.shape[0]
    out = []
    for c in range(x.shape[1] // w):
        xs = x[:, c * w:(c + 1) * w]
        hi = xs.astype(BF16)
        lo = (xs - hi.astype(F32)).astype(BF16)
        out.append(jnp.dot(hi, ones_bd, preferred_element_type=F32)
                   + jnp.dot(lo, ones_bd, preferred_element_type=F32))
    return jnp.concatenate(out, axis=1)


def _rwkv_finalize_body(yp_ref, ys_ref, rkv_ref, a_ref, g_ref, ka_ref, rk_ref, lnw_ref, lnb_ref, o_ref, *, pt):
    hd = RWKV_HEAD_DIM
    w = RWKV_SLAB
    ri = lax.broadcasted_iota(jnp.int32, (w, w), 0)
    ci = lax.broadcasted_iota(jnp.int32, (w, w), 1)
    ones_bd = jnp.where(ri // hd == ci // hd, 1.0, 0.0).astype(BF16)

    def finalize(y_ref):
        y = y_ref[0] + y_ref[1]
        mean = _group_sum(y, ones_bd) * (1.0 / hd)
        yc = y - mean
        var = _group_sum(yc * yc, ones_bd) * (1.0 / hd)
        yn = yc * lax.rsqrt(var + RWKV_LN_EPS) * lnw_ref[...] + lnb_ref[...]
        r, k, v = rkv_ref[0], rkv_ref[1], rkv_ref[2]
        a_sum = a_ref[0] + a_ref[1]
        k_sum = k * (2.0 + (a_sum - 2.0) * ka_ref[...])
        bonus = _group_sum(r * k_sum * rk_ref[...], ones_bd) * v
        o_ref[...] = ((yn + bonus) * g_ref[...]).astype(o_ref.dtype)

    _for_trunk_of_tile(pt, yp_ref, ys_ref, finalize)


def _rwkv_finalize(y_p, y_s, rkv, a, g, k_a, r_k, ln_w, ln_b, tm=256):
    n, d = y_p.shape[1] + y_s.shape[1], y_p.shape[2]
    specs, pt = _two_trunk_specs(y_p, y_s, tm)
    row = lambda i: (0, 0)
    return pl.pallas_call(
        functools.partial(_rwkv_finalize_body, pt=pt), grid=(n // tm,),
        in_specs=specs + [
                  pl.BlockSpec((3, tm, d), lambda i: (0, i, 0)),
                  pl.BlockSpec((2, tm, d), lambda i: (0, i, 0)),
                  pl.BlockSpec((tm, d), lambda i: (i, 0)),
                  pl.BlockSpec((1, d), row), pl.BlockSpec((1, d), row),
                  pl.BlockSpec((1, d), row), pl.BlockSpec((1, d), row)],
        out_specs=pl.BlockSpec((tm, d), lambda i: (i, 0)),
        out_shape=jax.ShapeDtypeStruct((n, d), BF16),
        compiler_params=_cparams(1), name="rwkv_finalize",
    )(y_p, y_s, rkv, a, g, k_a.reshape(1, d), r_k.reshape(1, d), ln_w.reshape(1, d), ln_b.reshape(1, d))


def _cast_expert_weights(be_ref, pairs):
    b = pl.program_id(1)

    @pl.when(jnp.logical_or(b == 0, be_ref[b] != be_ref[jnp.maximum(b - 1, 0)]))
    def _():
        for w_ref, w_bf in pairs:
            w_bf[...] = w_ref[...].astype(BF16)


def _moe_up_body(be_ref, na_ref, x_ref, wg_ref, wu_ref, o_ref, wg_bf, wu_bf):
    b = pl.program_id(1)
    _cast_expert_weights(be_ref, [(wg_ref, wg_bf), (wu_ref, wu_bf)])

    @pl.when(b < na_ref[0])
    def _():
        x = x_ref[...]
        gate = jnp.dot(x, wg_bf[...], preferred_element_type=F32)
        up = jnp.dot(x, wu_bf[...], preferred_element_type=F32)
        o_ref[...] = (_silu(gate) * up).astype(o_ref.dtype)

    @pl.when(b >= na_ref[0])
    def _():
        o_ref[...] = jnp.zeros_like(o_ref)


def _moe_down_body(be_ref, na_ref, x_ref, wd_ref, o_ref, wd_bf):
    b = pl.program_id(1)
    _cast_expert_weights(be_ref, [(wd_ref, wd_bf)])

    @pl.when(b < na_ref[0])
    def _():
        o_ref[...] = jnp.dot(x_ref[...], wd_bf[...], preferred_element_type=F32)

    @pl.when(b >= na_ref[0])
    def _():
        o_ref[...] = jnp.zeros_like(o_ref)


def _moe_experts(xb, block_e, n_active, w_gate, w_up, w_down, tn_up=512, tn_down=512):
    rows, d = xb.shape
    bm = MOE_BLOCK
    n_blk = rows // bm
    d_ff = w_gate.shape[2]
    tn_up, tn_down = min(tn_up, d_ff), min(tn_down, d)
    act = pl.pallas_call(
        _moe_up_body,
        grid_spec=pltpu.PrefetchScalarGridSpec(
            num_scalar_prefetch=2, grid=(d_ff // tn_up, n_blk),
            in_specs=[pl.BlockSpec((bm, d), lambda j, b, be, na: (b, 0)),
                      pl.BlockSpec((None, d, tn_up), lambda j, b, be, na: (be[b], 0, j)),
                      pl.BlockSpec((None, d, tn_up), lambda j, b, be, na: (be[b], 0, j))],
            out_specs=pl.BlockSpec((bm, tn_up), lambda j, b, be, na: (b, j)),
            scratch_shapes=[pltpu.VMEM((d, tn_up), BF16)] * 2),
        out_shape=jax.ShapeDtypeStruct((rows, d_ff), BF16),
        compiler_params=_cparams(2), name="moe_up")(block_e, n_active, xb, w_gate, w_up)
    return pl.pallas_call(
        _moe_down_body,
        grid_spec=pltpu.PrefetchScalarGridSpec(
            num_scalar_prefetch=2, grid=(d // tn_down, n_blk),
            in_specs=[pl.BlockSpec((bm, d_ff), lambda j, b, be, na: (b, 0)),
                      pl.BlockSpec((None, d_ff, tn_down), lambda j, b, be, na: (be[b], 0, j))],
            out_specs=pl.BlockSpec((bm, tn_down), lambda j, b, be, na: (b, j)),
            scratch_shapes=[pltpu.VMEM((d_ff, tn_down), BF16)]),
        out_shape=jax.ShapeDtypeStruct((rows, d), F32),
        compiler_params=_cparams(2), name="moe_down")(block_e, n_active, act, w_down)


def _row_copy(src_hbm, src_row, dst, dst_row, sem):
    return pltpu.make_async_copy(src_hbm.at[pl.ds(src_row, 1), :], dst.at[pl.ds(dst_row, 1), :], sem)


def _moe_gather_body(tok_ref, h_hbm, o_hbm, sem, *, bm):
    base = pl.program_id(0) * bm

    def start(r, carry):
        _row_copy(h_hbm, tok_ref[0, r], o_hbm, base + r, sem).start()
        return carry

    def wait(r, carry):
        _row_copy(h_hbm, 0, o_hbm, base + r, sem).wait()
        return carry

    lax.fori_loop(0, bm, start, 0, unroll=8)
    lax.fori_loop(0, bm, wait, 0, unroll=8)


def _moe_gather(h, slot_tok):
    n_tok, d = h.shape
    bm = MOE_BLOCK
    n_blk = slot_tok.shape[0] // bm
    pack = 4 // h.dtype.itemsize
    h32 = lax.bitcast_convert_type(h.reshape(n_tok, d // pack, pack), jnp.uint32) if pack > 1 else h
    xb32 = pl.pallas_call(
        functools.partial(_moe_gather_body, bm=bm), grid=(n_blk,),
        in_specs=[pl.BlockSpec((None, 1, bm), lambda b: (b, 0, 0), memory_space=pltpu.SMEM),
                  pl.BlockSpec(memory_space=pl.ANY)],
        out_specs=pl.BlockSpec(memory_space=pl.ANY),
        out_shape=jax.ShapeDtypeStruct((n_blk * bm, d // pack), h32.dtype),
        scratch_shapes=[pltpu.SemaphoreType.DMA(())],
        compiler_params=_cparams(1), name="moe_gather")(slot_tok.reshape(n_blk, 1, bm), h32)
    return lax.bitcast_convert_type(xb32, h.dtype).reshape(n_blk * bm, d) if pack > 1 else xb32


def _moe_combine_body(slot_ref, x_ref, g_ref, gt_ref, yb_hbm, o_ref, ybuf, sem, *, tm):
    def start(r, carry):
        for k in range(TOP_K):
            _row_copy(yb_hbm, slot_ref[0, TOP_K * r + k], ybuf.at[k], r, sem).start()
        return carry

    def wait(r, carry):
        for k in range(TOP_K):
            _row_copy(yb_hbm, 0, ybuf.at[k], r, sem).wait()
        return carry

    lax.fori_loop(0, tm, start, 0, unroll=4)
    lax.fori_loop(0, tm, wait, 0, unroll=4)
    g = g_ref[...]
    y = ybuf[0] * g[:, 0:1] + ybuf[1] * g[:, 1:2]
    o_ref[...] = x_ref[...] + gt_ref[...] * y


def _moe_combine(x, yb, slot_of, gates, mod5, layer, lay, tm=256):
    n, d = x.shape
    tm = min(tm, lay.s_len, lay.p_rows)
    row = pl.BlockSpec((tm, d), lambda i: (i, 0))
    return pl.pallas_call(
        functools.partial(_moe_combine_body, tm=tm), grid=(n // tm,),
        in_specs=[pl.BlockSpec((None, 1, TOP_K * tm), lambda i: (i, 0, 0), memory_space=pltpu.SMEM),
                  row, pl.BlockSpec((tm, TOP_K), lambda i: (i, 0)),
                  pl.BlockSpec((None, None, None, 1, d), _mod_spec(layer, 5, tm, lay)),
                  pl.BlockSpec(memory_space=pl.ANY)],
        out_specs=row, out_shape=jax.ShapeDtypeStruct((n, d), F32),
        scratch_shapes=[pltpu.VMEM((TOP_K, tm, d), F32), pltpu.SemaphoreType.DMA(())],
        compiler_params=_cparams(1), name="moe_combine",
    )(slot_of.reshape(n // tm, 1, TOP_K * tm), x, gates, mod5, yb)


def _moe(h, logits, w_gate, w_up, w_down):
    n_tok, d = h.shape
    bm = MOE_BLOCK
    top_val, top_idx = lax.top_k(logits, TOP_K)
    gates = jax.nn.softmax(top_val, axis=-1)
    n_assign = n_tok * TOP_K
    flat_e = top_idx.reshape(-1).astype(jnp.int32)
    order = jnp.argsort(flat_e).astype(jnp.int32)
    rank = jnp.argsort(order).astype(jnp.int32)
    counts = jnp.sum(flat_e[:, None] == jnp.arange(N_EXPERTS, dtype=jnp.int32)[None, :], axis=0,
                     dtype=jnp.int32)
    padded = ((counts + bm - 1) // bm) * bm
    pad_end = jnp.cumsum(padded)
    pad_start = pad_end - padded
    start = jnp.cumsum(counts) - counts
    n_blocks = -(-n_assign // bm) + N_EXPERTS
    block_e = jnp.minimum(jnp.searchsorted(pad_end, jnp.arange(n_blocks) * bm, side='right'),
                          N_EXPERTS - 1).astype(jnp.int32)
    n_active = (pad_end[-1] // bm).astype(jnp.int32).reshape(1)
    experts = jnp.arange(N_EXPERTS, dtype=jnp.int32)

    def lookup(table, e):
        return jnp.sum(jnp.where(e[..., None] == experts, table, 0), axis=-1, dtype=jnp.int32)

    slot = jnp.arange(n_blocks * bm, dtype=jnp.int32)
    slot_e = jnp.repeat(block_e, bm)
    pos = slot - lookup(pad_start, slot_e)
    src = jnp.clip(lookup(start, slot_e) + pos, 0, n_assign - 1)
    slot_tok = jnp.where(pos < lookup(counts, slot_e),
                         jnp.take(order, src, indices_are_sorted=True) // TOP_K, slot % n_tok)
    yb = _moe_experts(_moe_gather(h, slot_tok), block_e, n_active, w_gate, w_up, w_down)
    slot_of = (lookup(pad_start - start, flat_e) + rank).reshape(n_tok, TOP_K)
    return yb, slot_of, gates


def _final_norm_body(x_ref, w_ref, o_ref):
    x = x_ref[...]
    o_ref[...] = x * lax.rsqrt(jnp.mean(x * x, axis=-1, keepdims=True) + NORM_EPS) * w_ref[...]


def _final_norm(x, w, row0, n_rows, tm=512):
    d = x.shape[1]
    tm = min(tm, n_rows)
    rb0 = row0 // tm
    return pl.pallas_call(
        _final_norm_body, grid=(n_rows // tm,),
        in_specs=[pl.BlockSpec((tm, d), lambda i: (rb0 + i, 0)), pl.BlockSpec((1, d), lambda i: (0, 0))],
        out_specs=pl.BlockSpec((tm, d), lambda i: (i, 0)),
        out_shape=jax.ShapeDtypeStruct((n_rows, d), F32),
        compiler_params=_cparams(1), name="final_norm")(x, w.reshape(1, d))


def _rope_tables(seq_len, dim):
    t = jnp.arange(seq_len)
    row = (t // GRID_W).astype(F32)
    col = (t % GRID_W).astype(F32)
    half = dim // 2
    inv = ROPE_BASE ** (-(jnp.arange(0, half, 2, dtype=F32) / half))
    ang = jnp.concatenate([row[:, None] * inv, col[:, None] * inv], axis=-1)
    cos, sin = jnp.cos(ang), jnp.sin(ang)
    return jnp.repeat(cos, 2, axis=-1), jnp.stack([-sin, sin], axis=-1).reshape(seq_len, dim)


def _block_diag_states(s):
    b, two, n_heads, hd, _ = s.shape
    hps = RWKV_SLAB // hd
    s6 = s.reshape(b, two, n_heads // hps, hps, hd, hd)
    bd = jnp.einsum('bdghvk,hi->bdghvik', s6, jnp.eye(hps, dtype=s.dtype))
    return bd.reshape(b, two, n_heads // hps, RWKV_SLAB, RWKV_SLAB)


def _pad_cols(w, width):
    return jnp.pad(w, ((0, 0), (0, width - w.shape[1])))


def kernel(x_prompt, x_sample, state_l0_ret, state_l1_rwkv, c, c_ctx, ada_w, ada_b, norm_w, final_norm_w,
           l0_ret_w_in, l0_ret_w_out, l0_ret_decay, l0_ffn_w_gate, l0_ffn_w_up, l0_ffn_w_down,
           l1_rwkv_mu, l1_rwkv_w_rkv, l1_rwkv_w0, l1_rwkv_w1, l1_rwkv_w2, l1_rwkv_a0, l1_rwkv_a1, l1_rwkv_a2,
           l1_rwkv_g1, l1_rwkv_g2, l1_rwkv_k_k, l1_rwkv_k_a, l1_rwkv_r_k, l1_rwkv_ln_w, l1_rwkv_ln_b,
           l1_rwkv_w_out, l1_moe_router, l1_moe_w_gate, l1_moe_w_up, l1_moe_w_down):
    pb, p_len, d = x_prompt.shape
    sb, s_len, _ = x_sample.shape
    p_rows, s_rows = pb * p_len, sb * s_len
    lay = _Layout(p_rows, p_len, s_len, p_rows + s_rows)
    n_layers = ada_w.shape[0]

    x = jnp.concatenate([x_prompt.reshape(p_rows, d), x_sample.reshape(s_rows, d)], axis=0)
    cond8 = jnp.concatenate([c_ctx[None, :], c, jnp.zeros((8 - 1 - sb, d), F32)], axis=0)
    mod5 = _ada_mod(cond8, ada_w, ada_b)[:, :1 + sb].reshape(n_layers, 1 + sb, 6, 1, d)

    h = _norm_mod(x, norm_w[0, 0], mod5, 0, 1, 0, lay)
    qkvg = _linear(h, l0_ret_w_in, tn=1024)
    log_gamma = -jnp.exp(l0_ret_decay.astype(F32))
    dh = d // RET_HEADS
    o_p, new_state_l0_ret = _retention(qkvg, log_gamma, None, None, row0=0, n_seq=pb, seq_len=p_len,
                                       emit_state=True)
    (o_s,) = _retention(qkvg, log_gamma, state_l0_ret, _rope_tables(s_len, dh), row0=p_rows, n_seq=sb,
                        seq_len=s_len, emit_state=False)
    x = _linear_residual(_ret_finalize(o_p, o_s, qkvg), l0_ret_w_out, x, mod5, 0, 2, lay)
    h = _norm_mod(x, norm_w[0, 1], mod5, 0, 4, 3, lay)
    act = _swiglu_up(h, l0_ffn_w_gate, l0_ffn_w_up)
    x = _linear_residual(act, l0_ffn_w_down, x, mod5, 0, 5, lay, tm=512, tn=512)

    xs = _rwkv_mix(x, norm_w[1, 0], mod5, 1, l1_rwkv_mu, lay)
    rkv = _linear(xs, l1_rwkv_w_rkv, n_batch=3, tn=1024)
    lora_w = l1_rwkv_w1.shape[2]
    pad_w = -(-lora_w // 128) * 128
    w1p = jnp.concatenate([_pad_cols(l1_rwkv_w1[0], pad_w), _pad_cols(l1_rwkv_w1[1], pad_w)], axis=1)
    a1p = jnp.concatenate([_pad_cols(l1_rwkv_a1[0], pad_w), _pad_cols(l1_rwkv_a1[1], pad_w)], axis=1)
    t_w = _linear(xs, w1p, x_lead=3, act="tanh", out_dtype=BF16)
    t_a = _linear(xs, a1p, x_lead=4, out_dtype=BF16)
    t_g = _linear(xs, l1_rwkv_g1, x_lead=5, act="sigmoid", out_dtype=BF16)
    w2p = jnp.pad(l1_rwkv_w2, ((0, 0), (0, pad_w - lora_w), (0, 0)))
    a2p = jnp.pad(l1_rwkv_a2, ((0, 0), (0, pad_w - lora_w), (0, 0)))
    lw = _linear(t_w, w2p, n_batch=2, bias=l1_rwkv_w0, act="log_decay")
    a_lr = _linear(t_a, a2p, n_batch=2, bias=l1_rwkv_a0, act="sigmoid")
    g = _linear(t_g, l1_rwkv_g2)
    scan_args = (rkv, lw, a_lr, l1_rwkv_k_k, l1_rwkv_k_a)
    y_p, new_state_l1_rwkv = _rwkv_scan(*scan_args, None, row0=0, n_seq=pb, seq_len=p_len, emit_state=True)
    (y_s,) = _rwkv_scan(*scan_args, _block_diag_states(state_l1_rwkv), row0=p_rows, n_seq=sb, seq_len=s_len,
                        emit_state=False)
    yg = _rwkv_finalize(y_p, y_s, rkv, a_lr, g, l1_rwkv_k_a, l1_rwkv_r_k, l1_rwkv_ln_w, l1_rwkv_ln_b)
    x = _linear_residual(yg, l1_rwkv_w_out, x, mod5, 1, 2, lay)
    ne_pad = 128
    h, logits = _norm_mod(x, norm_w[1, 1], mod5, 1, 4, 3, lay, router_w=_pad_cols(l1_moe_router, ne_pad))
    yb, slot_of, gates = _moe(h, logits[:, :N_EXPERTS], l1_moe_w_gate, l1_moe_w_up, l1_moe_w_down)
    x = _moe_combine(x, yb, slot_of, gates, mod5, 1, lay)

    y_prompt = _final_norm(x, final_norm_w, 0, p_rows).reshape(pb, p_len, d)
    y_sample = _final_norm(x, final_norm_w, p_rows, s_rows).reshape(sb, s_len, d)
    return (y_prompt, y_sample, new_state_l0_ret, new_state_l1_rwkv)
```

```python
import functools
from typing import NamedTuple

import jax
import jax.numpy as jnp
from jax import lax
from jax.experimental import pallas as pl
from jax.experimental.pallas import tpu as pltpu

F32 = jnp.float32
BF16 = jnp.bfloat16

GRID_W = 64
RET_HEADS = 8
RET_CHUNK = 128
RET_HEADS_PER_STEP = 2
ROPE_BASE = 10000.0
RWKV_HEAD_DIM = 64
RWKV_CHUNK = 64
RWKV_SLAB = 256
RWKV_CHUNKS_PER_ITER = 2
RWKV_LN_EPS = 64e-5
N_EXPERTS = 8
TOP_K = 2
MOE_BLOCK = 512
NORM_EPS = 1e-6

VMEM_LIMIT_BYTES = 56 * 1024 * 1024


class _Layout(NamedTuple):
    p_rows: int
    p_len: int
    s_len: int
    n_rows: int


def _cparams(n_axes):
    return pltpu.CompilerParams(dimension_semantics=("arbitrary",) * n_axes,
                                vmem_limit_bytes=VMEM_LIMIT_BYTES)


def _cond_of_tile(i, tm, lay):
    pt = lay.p_rows // tm
    st = lay.s_len // tm
    return jnp.where(i < pt, 0, 1 + (i - pt) // st)


def _bdot(a, b):
    return jnp.dot(a.astype(BF16), b.astype(BF16), preferred_element_type=F32)


def _bdot_nt(a, b):
    return lax.dot_general(a.astype(BF16), b.astype(BF16), (((1,), (1,)), ((), ())),
                           preferred_element_type=F32)


def _bdot_tn(a, b):
    return lax.dot_general(a.astype(BF16), b.astype(BF16), (((0,), (0,)), ((), ())),
                           preferred_element_type=F32)


def _silu(x):
    return x * jax.nn.sigmoid(x)


def _ada_body(c_ref, w_ref, b_ref, o_ref):
    o_ref[...] = _bdot(_silu(c_ref[...]), w_ref[...]) + b_ref[...]


def _ada_mod(cond8, ada_w, ada_b):
    n_layers, d, d6 = ada_w.shape
    tn = min(1024, d6)
    return pl.pallas_call(
        _ada_body,
        grid=(n_layers, d6 // tn),
        in_specs=[pl.BlockSpec((8, d), lambda l, j: (0, 0)),
                  pl.BlockSpec((None, d, tn), lambda l, j: (l, 0, j)),
                  pl.BlockSpec((None, 1, tn), lambda l, j: (l, 0, j))],
        out_specs=pl.BlockSpec((None, 8, tn), lambda l, j: (l, 0, j)),
        out_shape=jax.ShapeDtypeStruct((n_layers, 8, d6), F32),
        compiler_params=_cparams(2), name="ada_mod",
    )(cond8, ada_w, ada_b.reshape(n_layers, 1, d6))


def _mod_spec(layer, which, tm, lay, n_grid_axes=1):
    def imap(i, *_):
        return (layer, _cond_of_tile(i, tm, lay), which, 0, 0)
    return imap


def _rms_mod(x, nw, sc, sh):
    xn = x * lax.rsqrt(jnp.mean(x * x, axis=-1, keepdims=True) + NORM_EPS) * nw
    return xn * (1.0 + sc) + sh


def _norm_mod_body(x_ref, nw_ref, sc_ref, sh_ref, o_ref):
    o_ref[...] = _rms_mod(x_ref[...], nw_ref[...], sc_ref[...], sh_ref[...]).astype(o_ref.dtype)


def _norm_mod_router_body(x_ref, nw_ref, sc_ref, sh_ref, wr_ref, o_ref, lg_ref):
    h = _rms_mod(x_ref[...], nw_ref[...], sc_ref[...], sh_ref[...])
    o_ref[...] = h.astype(o_ref.dtype)
    lg_ref[...] = jnp.dot(h, wr_ref[...], preferred_element_type=F32, precision=lax.Precision.HIGHEST)


def _norm_mod(x, nw, mod5, layer, which_sc, which_sh, lay, router_w=None, tm=512):
    n, d = x.shape
    tm = min(tm, lay.s_len, lay.p_rows)
    in_specs = [pl.BlockSpec((tm, d), lambda i: (i, 0)),
                pl.BlockSpec((1, d), lambda i: (0, 0)),
                pl.BlockSpec((None, None, None, 1, d), _mod_spec(layer, which_sc, tm, lay)),
                pl.BlockSpec((None, None, None, 1, d), _mod_spec(layer, which_sh, tm, lay))]
    args = [x, nw.reshape(1, d), mod5, mod5]
    if router_w is None:
        return pl.pallas_call(
            _norm_mod_body, grid=(n // tm,), in_specs=in_specs,
            out_specs=pl.BlockSpec((tm, d), lambda i: (i, 0)),
            out_shape=jax.ShapeDtypeStruct((n, d), BF16),
            compiler_params=_cparams(1), name="norm_mod")(*args)
    ne = router_w.shape[1]
    return pl.pallas_call(
        _norm_mod_router_body, grid=(n // tm,),
        in_specs=in_specs + [pl.BlockSpec((d, ne), lambda i: (0, 0))],
        out_specs=[pl.BlockSpec((tm, d), lambda i: (i, 0)), pl.BlockSpec((tm, ne), lambda i: (i, 0))],
        out_shape=[jax.ShapeDtypeStruct((n, d), F32), jax.ShapeDtypeStruct((n, ne), F32)],
        compiler_params=_cparams(1), name="norm_mod_router")(*args, router_w)


def _cast_weights_once(row_axis, pairs):
    @pl.when(pl.program_id(row_axis) == 0)
    def _():
        for w_ref, w_bf in pairs:
            w_bf[...] = w_ref[...].astype(BF16)


def _softplus(x):
    return jnp.maximum(x, 0.0) + jnp.log1p(jnp.exp(-jnp.abs(x)))


def _linear_body(x_ref, w_ref, *refs, act, has_bias):
    refs = list(refs)
    bias_ref = refs.pop(0) if has_bias else None
    o_ref, w_bf = refs
    _cast_weights_once(2, [(w_ref, w_bf)])
    acc = jnp.dot(x_ref[...].astype(BF16), w_bf[...], preferred_element_type=F32)
    if has_bias:
        acc = acc + bias_ref[...]
    if act == "tanh":
        acc = jnp.tanh(acc)
    elif act == "sigmoid":
        acc = jax.nn.sigmoid(acc)
    elif act == "log_decay":
        acc = -jnp.exp(-_softplus(-acc) - 0.5)
    o_ref[...] = acc.astype(o_ref.dtype)


def _linear(x, w, *, n_batch=None, x_lead=0, bias=None, act=None, out_dtype=F32, tm=1024, tn=512):
    k, n = w.shape[-2:]
    m = x.shape[-2]
    tm, tn = min(tm, m), min(tn, n)
    if x.ndim == 3:
        x_spec = pl.BlockSpec((None, tm, k), lambda b, j, i: (x_lead + b, i, 0))
    else:
        x_spec = pl.BlockSpec((tm, k), lambda b, j, i: (i, b))
    if w.ndim == 3:
        w_spec = pl.BlockSpec((None, k, tn), lambda b, j, i: (b, 0, j))
    else:
        w_spec = pl.BlockSpec((k, tn), lambda b, j, i: (0, j))
    if n_batch is None:
        out_spec = pl.BlockSpec((tm, tn), lambda b, j, i: (i, j))
        out_shape = jax.ShapeDtypeStruct((m, n), out_dtype)
    else:
        out_spec = pl.BlockSpec((None, tm, tn), lambda b, j, i: (b, i, j))
        out_shape = jax.ShapeDtypeStruct((n_batch, m, n), out_dtype)
    in_specs, args = [x_spec, w_spec], [x, w]
    if bias is not None:
        in_specs.append(pl.BlockSpec((None, 1, tn), lambda b, j, i: (b, 0, j)))
        args.append(bias.reshape(bias.shape[0], 1, n))
    return pl.pallas_call(
        functools.partial(_linear_body, act=act, has_bias=bias is not None),
        grid=(n_batch or 1, n // tn, m // tm), in_specs=in_specs,
        out_specs=out_spec, out_shape=out_shape,
        scratch_shapes=[pltpu.VMEM((k, tn), BF16)],
        compiler_params=_cparams(3), name="linear")(*args)


def _linear_res_body(x_ref, w_ref, res_ref, gt_ref, o_ref, w_bf):
    _cast_weights_once(1, [(w_ref, w_bf)])
    o_ref[...] = res_ref[...] + gt_ref[...] * jnp.dot(x_ref[...], w_bf[...], preferred_element_type=F32)


def _linear_residual(x, w, res, mod5, layer, which_gate, lay, tm=1024, tn=1024):
    m, k = x.shape
    n = w.shape[1]
    tm, tn = min(tm, lay.s_len, lay.p_rows), min(tn, n)

    def gmap(j, i):
        return (layer, _cond_of_tile(i, tm, lay), which_gate, 0, j)

    return pl.pallas_call(
        _linear_res_body, grid=(n // tn, m // tm),
        in_specs=[pl.BlockSpec((tm, k), lambda j, i: (i, 0)),
                  pl.BlockSpec((k, tn), lambda j, i: (0, j)),
                  pl.BlockSpec((tm, tn), lambda j, i: (i, j)),
                  pl.BlockSpec((None, None, None, 1, tn), gmap)],
        out_specs=pl.BlockSpec((tm, tn), lambda j, i: (i, j)),
        out_shape=jax.ShapeDtypeStruct((m, n), F32),
        scratch_shapes=[pltpu.VMEM((k, tn), BF16)],
        compiler_params=_cparams(2), name="linear_residual")(x, w, res, mod5)


def _swiglu_up_body(x_ref, wg_ref, wu_ref, o_ref, wg_bf, wu_bf):
    _cast_weights_once(1, [(wg_ref, wg_bf), (wu_ref, wu_bf)])
    x = x_ref[...]
    gate = jnp.dot(x, wg_bf[...], preferred_element_type=F32)
    up = jnp.dot(x, wu_bf[...], preferred_element_type=F32)
    o_ref[...] = (_silu(gate) * up).astype(o_ref.dtype)


def _swiglu_up(x, wg, wu, tm=1024, tn=512):
    m, k = x.shape
    n = wg.shape[1]
    tm, tn = min(tm, m), min(tn, n)
    return pl.pallas_call(
        _swiglu_up_body, grid=(n // tn, m // tm),
        in_specs=[pl.BlockSpec((tm, k), lambda j, i: (i, 0)),
                  pl.BlockSpec((k, tn), lambda j, i: (0, j)),
                  pl.BlockSpec((k, tn), lambda j, i: (0, j))],
        out_specs=pl.BlockSpec((tm, tn), lambda j, i: (i, j)),
        out_shape=jax.ShapeDtypeStruct((m, n), BF16),
        scratch_shapes=[pltpu.VMEM((k, tn), BF16)] * 2,
        compiler_params=_cparams(2), name="swiglu_up")(x, wg, wu)


def _rope(x, c, s):
    w = x.shape[-1]
    lane = lax.broadcasted_iota(jnp.int32, x.shape, x.ndim - 1)
    nxt = pltpu.roll(x, w - 1, axis=x.ndim - 1)
    prv = pltpu.roll(x, 1, axis=x.ndim - 1)
    return x * c + jnp.where(lane % 2 == 0, nxt, prv) * s


def _retention_body(lg_ref, *refs, n_chunk, n_blk, has_rope, has_s0, emit_state, scale):
    refs = list(refs)
    q_ref, k_ref, v_ref = refs[:3]
    refs = refs[3:]
    if has_rope:
        cos_ref, sin_ref = refs[:2]
        refs = refs[2:]
    if has_s0:
        s0_ref = refs.pop(0)
    o_ref = refs.pop(0)
    if emit_state:
        so_ref = refs.pop(0)
    st = refs.pop(0)
    c_sz = RET_CHUNK
    hp, dh = st.shape[0], st.shape[1]
    hg, d, cb = pl.program_id(1), pl.program_id(2), pl.program_id(3)
    heads = range(hp)
    lanes = [slice(h * dh, (h + 1) * dh) for h in heads]

    @pl.when(cb == 0)
    def _():
        st[...] = s0_ref[...] if has_s0 else jnp.zeros_like(st)

    lgv = [lg_ref[d, hg * hp + h] for h in heads]
    fwd = d == 0
    row = lax.broadcasted_iota(jnp.int32, (c_sz, c_sz), 0)
    col = lax.broadcasted_iota(jnp.int32, (c_sz, c_sz), 1)
    diff = jnp.where(fwd, row - col, col - row).astype(F32)
    intra = [jnp.where(diff >= 0, jnp.exp(jnp.maximum(diff, 0.0) * lg), 0.0) for lg in lgv]
    pos = lax.broadcasted_iota(jnp.int32, (c_sz, dh), 0)
    npos = jnp.where(fwd, pos, c_sz - 1 - pos).astype(F32)
    q_decay = [jnp.exp((npos + 1.0) * lg) for lg in lgv]
    k_decay = [jnp.exp((c_sz - 1.0 - npos) * lg) for lg in lgv]
    chunk_decay = [jnp.exp(jnp.full((1, dh), c_sz, F32) * lg) for lg in lgv]

    for j in range(n_chunk):
        jj = jnp.where(fwd, j, n_chunk - 1 - j)
        rows = pl.ds(pl.multiple_of(jj * c_sz, c_sz), c_sz)
        q = [q_ref[rows, sl] for sl in lanes]
        k = [k_ref[rows, sl] * scale for sl in lanes]
        v = [v_ref[rows, sl] for sl in lanes]
        if has_rope:
            c, s = cos_ref[rows, :], sin_ref[rows, :]
            q = [_rope(x, c, s) for x in q]
            k = [_rope(x, c, s) for x in k]
        state = [st[h] for h in heads]
        scores = [_bdot_nt(q[h], k[h]) * intra[h] for h in heads]
        o = [_bdot(scores[h], v[h]) + _bdot(q[h], state[h]) * q_decay[h] for h in heads]
        new_state = [state[h] * chunk_decay[h] + _bdot_tn(k[h] * k_decay[h], v[h]) for h in heads]
        o_ref[rows, :] = jnp.concatenate(o, axis=1)
        st[...] = jnp.stack(new_state)

    if emit_state:
        @pl.when(cb == n_blk - 1)
        def _():
            so_ref[...] = st[...]


def _retention(qkvg, log_gamma, s0, rope, *, row0, n_seq, seq_len, emit_state):
    d_model = qkvg.shape[1] // 4
    n_heads = RET_HEADS
    dh = d_model // n_heads
    tb = min(512, seq_len)
    n_blk = seq_len // tb
    rb0 = row0 // tb

    def blk(c, d):
        return jnp.where(d == 0, c, n_blk - 1 - c)

    hp = RET_HEADS_PER_STEP
    n_hg = n_heads // hp

    def in_map(part):
        return lambda s, h, d, c, lg: (rb0 + s * n_blk + blk(c, d), part * n_hg + h)

    in_specs = [pl.BlockSpec((tb, hp * dh), in_map(p)) for p in range(3)]
    args = [qkvg, qkvg, qkvg]
    if rope is not None:
        in_specs += [pl.BlockSpec((tb, dh), lambda s, h, d, c, lg: (blk(c, d), 0))] * 2
        args += list(rope)
    if s0 is not None:
        in_specs.append(pl.BlockSpec((None, None, hp, dh, dh), lambda s, h, d, c, lg: (s, d, h, 0, 0)))
        args.append(s0)
    out_specs = [pl.BlockSpec((None, tb, hp * dh), lambda s, h, d, c, lg: (d, s * n_blk + blk(c, d), h))]
    out_shape = [jax.ShapeDtypeStruct((2, n_seq * seq_len, d_model), F32)]
    if emit_state:
        out_specs.append(pl.BlockSpec((None, None, hp, dh, dh), lambda s, h, d, c, lg: (s, d, h, 0, 0)))
        out_shape.append(jax.ShapeDtypeStruct((n_seq, 2, n_heads, dh, dh), F32))
    body = functools.partial(_retention_body, n_chunk=tb // RET_CHUNK, n_blk=n_blk, has_rope=rope is not None,
                             has_s0=s0 is not None, emit_state=emit_state, scale=dh ** -0.5)
    return pl.pallas_call(
        body,
        grid_spec=pltpu.PrefetchScalarGridSpec(
            num_scalar_prefetch=1, grid=(n_seq, n_hg, 2, n_blk), in_specs=in_specs, out_specs=out_specs,
            scratch_shapes=[pltpu.VMEM((hp, dh, dh), F32)]),
        out_shape=out_shape, compiler_params=_cparams(4), name="retention")(log_gamma, *args)


def _two_trunk_specs(a_p, a_s, tm):
    pt = a_p.shape[1] // tm
    d = a_p.shape[2]
    return [pl.BlockSpec((2, tm, d), lambda i: (0, jnp.minimum(i, pt - 1), 0)),
            pl.BlockSpec((2, tm, d), lambda i: (0, jnp.maximum(i - pt, 0), 0))], pt


def _for_trunk_of_tile(pt, p_ref, s_ref, fn):
    i = pl.program_id(0)

    @pl.when(i < pt)
    def _():
        fn(p_ref)

    @pl.when(i >= pt)
    def _():
        fn(s_ref)


def _ret_finalize_body(op_ref, os_ref, g_ref, out_ref, *, n_heads, pt):
    def finalize(o_ref):
        o = o_ref[0] + o_ref[1]
        g = g_ref[...]
        dh = o.shape[1] // n_heads
        for h in range(n_heads):
            sl = slice(h * dh, (h + 1) * dh)
            oh = o[:, sl]
            oh = oh * lax.rsqrt(jnp.mean(oh * oh, axis=-1, keepdims=True) + NORM_EPS)
            out_ref[:, sl] = (oh * _silu(g[:, sl])).astype(out_ref.dtype)

    _for_trunk_of_tile(pt, op_ref, os_ref, finalize)


def _ret_finalize(o_p, o_s, qkvg, tm=256):
    n, d = o_p.shape[1] + o_s.shape[1], o_p.shape[2]
    specs, pt = _two_trunk_specs(o_p, o_s, tm)
    return pl.pallas_call(
        functools.partial(_ret_finalize_body, n_heads=RET_HEADS, pt=pt), grid=(n // tm,),
        in_specs=specs + [pl.BlockSpec((tm, d), lambda i: (i, 3))],
        out_specs=pl.BlockSpec((tm, d), lambda i: (i, 0)),
        out_shape=jax.ShapeDtypeStruct((n, d), BF16),
        compiler_params=_cparams(1), name="ret_finalize")(o_p, o_s, qkvg)


def _rwkv_mix_body(x_ref, xp_ref, xn_ref, nw_ref, sc_ref, sh_ref, mu_ref, o_ref, hext, *, tm, halo, lay):
    i = pl.program_id(0)
    d = x_ref.shape[1]
    nw, sc, sh = nw_ref[...], sc_ref[...], sh_ref[...]
    hext[0:halo, :] = _rms_mod(xp_ref[...], nw, sc, sh)
    hext[halo:halo + tm, :] = _rms_mod(x_ref[...], nw, sc, sh)
    hext[halo + tm:halo + tm + halo, :] = _rms_mod(xn_ref[...], nw, sc, sh)
    h = hext[halo:halo + tm, :]
    g_row = i * tm + lax.broadcasted_iota(jnp.int32, (tm, 1), 0)

    def emit(h_shift):
        diff = h_shift - h
        for n in range(6):
            o_ref[n] = (h + diff * mu_ref[n:n + 1, :]).astype(o_ref.dtype)

    def shifted(off, lo, hi, keep):
        return jnp.where(keep, hext[halo + off:halo + off + tm, lo:hi], 0.0)

    @pl.when(i < lay.p_rows // tm)
    def _():
        t = g_row % lay.p_len
        hd = d // 2
        emit(jnp.concatenate([shifted(-1, 0, hd, t != 0),
                              shifted(1, hd, d, t != lay.p_len - 1)], axis=1))

    @pl.when(i >= lay.p_rows // tm)
    def _():
        t = (g_row - lay.p_rows) % lay.s_len
        colw = t % GRID_W
        qd = d // 4
        emit(jnp.concatenate([shifted(-1, 0, qd, colw != 0),
                              shifted(1, qd, 2 * qd, colw != GRID_W - 1),
                              shifted(-GRID_W, 2 * qd, 3 * qd, t >= GRID_W),
                              shifted(GRID_W, 3 * qd, d, t < lay.s_len - GRID_W)], axis=1))


def _rwkv_mix(x, nw, mod5, layer, mu, lay, tm=512):
    n, d = x.shape
    halo = GRID_W
    tm = min(tm, lay.s_len, lay.p_rows)
    r = tm // halo
    n_halo_blk = n // halo
    body = functools.partial(_rwkv_mix_body, tm=tm, halo=halo, lay=lay)
    return pl.pallas_call(
        body, grid=(n // tm,),
        in_specs=[pl.BlockSpec((tm, d), lambda i: (i, 0)),
                  pl.BlockSpec((halo, d), lambda i: (jnp.maximum(i * r - 1, 0), 0)),
                  pl.BlockSpec((halo, d), lambda i: (jnp.minimum((i + 1) * r, n_halo_blk - 1), 0)),
                  pl.BlockSpec((1, d), lambda i: (0, 0)),
                  pl.BlockSpec((None, None, None, 1, d), _mod_spec(layer, 1, tm, lay)),
                  pl.BlockSpec((None, None, None, 1, d), _mod_spec(layer, 0, tm, lay)),
                  pl.BlockSpec((6, d), lambda i: (0, 0))],
        out_specs=pl.BlockSpec((6, tm, d), lambda i: (0, i, 0)),
        out_shape=jax.ShapeDtypeStruct((6, n, d), BF16),
        scratch_shapes=[pltpu.VMEM((tm + 2 * halo, d), F32)],
        compiler_params=_cparams(1), name="rwkv_mix")(x, x, x, nw.reshape(1, d), mod5, mod5, mu)


def _rwkv_scan_body(*refs, n_chunk, n_blk, has_s0, emit_state):
    refs = list(refs)
    r_ref, k_ref, v_ref, lw_ref, a_ref, kk_ref, ka_ref = refs[:7]
    refs = refs[7:]
    if has_s0:
        s0_ref = refs.pop(0)
    y_ref = refs.pop(0)
    if emit_state:
        so_ref = refs.pop(0)
    st = refs.pop(0)
    c_sz, hd, sw = RWKV_CHUNK, RWKV_HEAD_DIM, RWKV_SLAB
    n_grp = st.shape[0]
    hps = sw // hd
    d, cb = pl.program_id(1), pl.program_id(3)
    fwd = d == 0

    ri = lax.broadcasted_iota(jnp.int32, (sw, sw), 0)
    ci = lax.broadcasted_iota(jnp.int32, (sw, sw), 1)
    bd_mask = (ri // hd) == (ci // hd)

    def block_diag(slab):
        return jnp.where(bd_mask, jnp.concatenate([slab] * hps, axis=0), 0.0).astype(BF16)

    @pl.when(cb == 0)
    def _():
        if has_s0:
            st[...] = s0_ref[...]
        else:
            st[...] = jnp.zeros_like(st)

    srow = lax.broadcasted_iota(jnp.int32, (c_sz, sw), 0)
    scol = lax.broadcasted_iota(jnp.int32, (c_sz, sw), 1) % hd
    diff = jnp.where(fwd, srow - scol, scol - srow)
    strict = diff > 0
    incl = diff >= 0
    eye = jnp.where(diff == 0, 1.0, 0.0).astype(F32)
    levels = []
    m = 1
    while m < c_sz:
        levels.append(jnp.logical_and(srow // (2 * m) == scol // (2 * m), srow // m != scol // m))
        m *= 2
    trow = lax.broadcasted_iota(jnp.int32, (c_sz, c_sz), 0)
    tcol = lax.broadcasted_iota(jnp.int32, (c_sz, c_sz), 1)
    tri_incl = jnp.where(jnp.where(fwd, trow - tcol, tcol - trow) >= 0, 1.0, 0.0).astype(BF16)
    ones_bd = jnp.where(bd_mask, 1.0, 0.0).astype(BF16)

    def split(x, n_parts):
        parts = []
        for _ in range(n_parts - 1):
            p = x.astype(BF16)
            parts.append(p)
            x = x - p.astype(F32)
        return parts + [x.astype(BF16)]

    k_k, k_a = kk_ref[...], ka_ref[...]
    grp = range(n_grp)

    n_sub = min(RWKV_CHUNKS_PER_ITER, n_chunk)

    def chunks(j, carry):
        rows_of = []
        for cc in range(n_sub):
            jj = j * n_sub + cc
            jj = jnp.where(fwd, jj, n_chunk - 1 - jj)
            rows_of.append(pl.ds(pl.multiple_of(jj * c_sz, c_sz), c_sz))
        units = [(rows_of[cc], slice(g * sw, (g + 1) * sw)) for cc in range(n_sub) for g in grp]
        un = range(len(units))
        v = [v_ref[rows, sl] for rows, sl in units]
        kkr = [k_ref[rows, sl] * k_k[:, sl] for rows, sl in units]
        sq = [split(x * x, 2) for x in kkr]
        ssum = [sum(jnp.dot(p, ones_bd, preferred_element_type=F32) for p in s2) for s2 in sq]
        kk = [x * lax.rsqrt(s + 1e-12) for x, s in zip(kkr, ssum)]
        a = [a_ref[rows, sl] for rows, sl in units]
        b = [x * y for x, y in zip(kk, a)]
        kdir = [k_ref[rows, sl] * (1.0 + (ai - 1.0) * k_a[:, sl]) for (rows, sl), ai in zip(units, a)]
        lw = [lw_ref[rows, sl] for rows, sl in units]
        cum = [sum(jnp.dot(tri_incl, p, preferred_element_type=F32) for p in split(x, 3)) for x in lw]
        total = [jnp.sum(x, axis=0, keepdims=True) for x in lw]
        half = [0.5 * t for t in total]
        cumx = [c - x for c, x in zip(cum, lw)]
        r = [r_ref[rows, sl] for rows, sl in units]
        lhs_g = [jnp.concatenate([kk[i] * jnp.exp(cumx[i] - half[i]), r[i] * jnp.exp(cum[i] - half[i])], axis=0)
                 for i in un]
        e_neg = [jnp.exp(half[i] - cum[i]) for i in un]
        g_k = [_bdot_nt(lhs_g[i], block_diag(kdir[i] * e_neg[i])) for i in un]
        g_b = [_bdot_nt(lhs_g[i], block_diag(b[i] * e_neg[i])) for i in un]
        l_k = [jnp.where(strict, x[:c_sz], 0.0) for x in g_k]
        a_rk = [jnp.where(incl, x[c_sz:], 0.0) for x in g_k]
        l_b = [jnp.where(strict, x[:c_sz], 0.0) for x in g_b]
        a_rb = [jnp.where(incl, x[c_sz:], 0.0) for x in g_b]
        x = [eye - jnp.where(levels[0], l, 0.0) for l in l_b]
        for lvl in levels[1:]:
            t = [_bdot(x[i], block_diag(jnp.where(lvl, l_b[i], 0.0))) for i in un]
            x = [x[i] - _bdot(t[i], block_diag(x[i])) for i in un]
        bd_v = [block_diag(x) for x in v]
        lkv = [_bdot(l_k[i], bd_v[i]) for i in un]
        wt = [_bdot(x[i], block_diag(kk[i] * jnp.exp(cumx[i]))) for i in un]
        vt = [_bdot(x[i], block_diag(lkv[i])) for i in un]
        r_abs = [r[i] * jnp.exp(cum[i]) for i in un]
        e_end = [jnp.exp(total[i] - cum[i]) for i in un]
        kb_end = [jnp.concatenate([kdir[i] * e_end[i], -(b[i] * e_end[i])], axis=0) for i in un]
        state = [st[g] for g in grp]
        for cc in range(n_sub):
            ids = [cc * n_grp + g for g in grp]
            su = [_bdot_nt(jnp.concatenate([wt[i], r_abs[i]], axis=0), state[g]) for g, i in zip(grp, ids)]
            u = [su[g][:c_sz] + vt[i] for g, i in zip(grp, ids)]
            y = [su[g][c_sz:] + _bdot(a_rk[i], bd_v[i]) - _bdot(a_rb[i], block_diag(u[g]))
                 for g, i in zip(grp, ids)]
            upd = [_bdot_tn(jnp.concatenate([v[i], u[g]], axis=0), kb_end[i]) for g, i in zip(grp, ids)]
            state = [state[g] * jnp.exp(total[i]) + jnp.where(bd_mask, upd[g], 0.0) for g, i in zip(grp, ids)]
            y_ref[rows_of[cc], :] = jnp.concatenate(y, axis=1)
        st[...] = jnp.stack(state)
        return carry

    lax.fori_loop(0, n_chunk // n_sub, chunks, 0)

    if emit_state:
        @pl.when(cb == n_blk - 1)
        def _():
            for g in grp:
                s = st[g]
                for h in range(hps):
                    so_ref[g * hps + h] = s[h * hd:(h + 1) * hd, h * hd:(h + 1) * hd]


def _rwkv_scan(rkv, lw, a, k_k, k_a, s0, *, row0, n_seq, seq_len, emit_state, heads_per_step=32):
    _, _, d_model = rkv.shape
    hd, sw = RWKV_HEAD_DIM, RWKV_SLAB
    n_heads = d_model // hd
    g = min(heads_per_step, n_heads)
    wg = g * hd
    n_grp = wg // sw
    n_hg = n_heads // g
    tb = min(256, seq_len)
    n_blk = seq_len // tb
    rb0 = row0 // tb

    def blk(c, d):
        return jnp.where(d == 0, c, n_blk - 1 - c)

    def rkv_map(part):
        return lambda s, d, hg, c: (part, rb0 + s * n_blk + blk(c, d), hg)

    dir_map = lambda s, d, hg, c: (d, rb0 + s * n_blk + blk(c, d), hg)
    in_specs = ([pl.BlockSpec((None, tb, wg), rkv_map(p)) for p in range(3)]
                + [pl.BlockSpec((None, tb, wg), dir_map)] * 2
                + [pl.BlockSpec((1, wg), lambda s, d, hg, c: (0, hg))] * 2)
    args = [rkv, rkv, rkv, lw, a, k_k.reshape(1, d_model), k_a.reshape(1, d_model)]
    if s0 is not None:
        in_specs.append(pl.BlockSpec((None, None, n_grp, sw, sw), lambda s, d, hg, c: (s, d, hg, 0, 0)))
        args.append(s0)
    out_specs = [pl.BlockSpec((None, tb, wg), lambda s, d, hg, c: (d, s * n_blk + blk(c, d), hg))]
    out_shape = [jax.ShapeDtypeStruct((2, n_seq * seq_len, d_model), F32)]
    if emit_state:
        out_specs.append(pl.BlockSpec((None, None, g, hd, hd), lambda s, d, hg, c: (s, d, hg, 0, 0)))
        out_shape.append(jax.ShapeDtypeStruct((n_seq, 2, n_heads, hd, hd), F32))
    body = functools.partial(_rwkv_scan_body, n_chunk=tb // RWKV_CHUNK, n_blk=n_blk,
                             has_s0=s0 is not None, emit_state=emit_state)
    return pl.pallas_call(
        body, grid=(n_seq, 2, n_hg, n_blk), in_specs=in_specs, out_specs=out_specs, out_shape=out_shape,
        scratch_shapes=[pltpu.VMEM((n_grp, sw, sw), F32)],
        compiler_params=_cparams(4), name="rwkv_scan")(*args)


def _group_sum(x, ones_bd):
    w = ones_bd.shape[0]
    out = []
    for c in range(x.shape[1] // w):
        xs = x[:, c * w:(c + 1) * w]
        hi = xs.astype(BF16)
        lo = (xs - hi.astype(F32)).astype(BF16)
        out.append(jnp.dot(hi, ones_bd, preferred_element_type=F32)
                   + jnp.dot(lo, ones_bd, preferred_element_type=F32))
    return jnp.concatenate(out, axis=1)


def _rwkv_finalize_body(yp_ref, ys_ref, rkv_ref, a_ref, g_ref, ka_ref, rk_ref, lnw_ref, lnb_ref, o_ref, *, pt):
    hd = RWKV_HEAD_DIM
    w = RWKV_SLAB
    ri = lax.broadcasted_iota(jnp.int32, (w, w), 0)
    ci = lax.broadcasted_iota(jnp.int32, (w, w), 1)
    ones_bd = jnp.where(ri // hd == ci // hd, 1.0, 0.0).astype(BF16)

    def finalize(y_ref):
        y = y_ref[0] + y_ref[1]
        mean = _group_sum(y, ones_bd) * (1.0 / hd)
        yc = y - mean
        var = _group_sum(yc * yc, ones_bd) * (1.0 / hd)
        yn = yc * lax.rsqrt(var + RWKV_LN_EPS) * lnw_ref[...] + lnb_ref[...]
        r, k, v = rkv_ref[0], rkv_ref[1], rkv_ref[2]
        a_sum = a_ref[0] + a_ref[1]
        k_sum = k * (2.0 + (a_sum - 2.0) * ka_ref[...])
        bonus = _group_sum(r * k_sum * rk_ref[...], ones_bd) * v
        o_ref[...] = ((yn + bonus) * g_ref[...]).astype(o_ref.dtype)

    _for_trunk_of_tile(pt, yp_ref, ys_ref, finalize)


def _rwkv_finalize(y_p, y_s, rkv, a, g, k_a, r_k, ln_w, ln_b, tm=256):
    n, d = y_p.shape[1] + y_s.shape[1], y_p.shape[2]
    specs, pt = _two_trunk_specs(y_p, y_s, tm)
    row = lambda i: (0, 0)
    return pl.pallas_call(
        functools.partial(_rwkv_finalize_body, pt=pt), grid=(n // tm,),
        in_specs=specs + [
                  pl.BlockSpec((3, tm, d), lambda i: (0, i, 0)),
                  pl.BlockSpec((2, tm, d), lambda i: (0, i, 0)),
                  pl.BlockSpec((tm, d), lambda i: (i, 0)),
                  pl.BlockSpec((1, d), row), pl.BlockSpec((1, d), row),
                  pl.BlockSpec((1, d), row), pl.BlockSpec((1, d), row)],
        out_specs=pl.BlockSpec((tm, d), lambda i: (i, 0)),
        out_shape=jax.ShapeDtypeStruct((n, d), BF16),
        compiler_params=_cparams(1), name="rwkv_finalize",
    )(y_p, y_s, rkv, a, g, k_a.reshape(1, d), r_k.reshape(1, d), ln_w.reshape(1, d), ln_b.reshape(1, d))


def _cast_expert_weights(be_ref, pairs):
    b = pl.program_id(1)

    @pl.when(jnp.logical_or(b == 0, be_ref[b] != be_ref[jnp.maximum(b - 1, 0)]))
    def _():
        for w_ref, w_bf in pairs:
            w_bf[...] = w_ref[...].astype(BF16)


def _moe_up_body(be_ref, na_ref, x_ref, wg_ref, wu_ref, o_ref, wg_bf, wu_bf):
    b = pl.program_id(1)
    _cast_expert_weights(be_ref, [(wg_ref, wg_bf), (wu_ref, wu_bf)])

    @pl.when(b < na_ref[0])
    def _():
        x = x_ref[...]
        gate = jnp.dot(x, wg_bf[...], preferred_element_type=F32)
        up = jnp.dot(x, wu_bf[...], preferred_element_type=F32)
        o_ref[...] = (_silu(gate) * up).astype(o_ref.dtype)

    @pl.when(b >= na_ref[0])
    def _():
        o_ref[...] = jnp.zeros_like(o_ref)


def _moe_down_body(be_ref, na_ref, x_ref, wd_ref, o_ref, wd_bf):
    b = pl.program_id(1)
    _cast_expert_weights(be_ref, [(wd_ref, wd_bf)])

    @pl.when(b < na_ref[0])
    def _():
        o_ref[...] = jnp.dot(x_ref[...], wd_bf[...], preferred_element_type=F32)

    @pl.when(b >= na_ref[0])
    def _():
        o_ref[...] = jnp.zeros_like(o_ref)


def _moe_experts(xb, block_e, n_active, w_gate, w_up, w_down, tn_up=512, tn_down=512):
    rows, d = xb.shape
    bm = MOE_BLOCK
    n_blk = rows // bm
    d_ff = w_gate.shape[2]
    tn_up, tn_down = min(tn_up, d_ff), min(tn_down, d)
    act = pl.pallas_call(
        _moe_up_body,
        grid_spec=pltpu.PrefetchScalarGridSpec(
            num_scalar_prefetch=2, grid=(d_ff // tn_up, n_blk),
            in_specs=[pl.BlockSpec((bm, d), lambda j, b, be, na: (b, 0)),
                      pl.BlockSpec((None, d, tn_up), lambda j, b, be, na: (be[b], 0, j)),
                      pl.BlockSpec((None, d, tn_up), lambda j, b, be, na: (be[b], 0, j))],
            out_specs=pl.BlockSpec((bm, tn_up), lambda j, b, be, na: (b, j)),
            scratch_shapes=[pltpu.VMEM((d, tn_up), BF16)] * 2),
        out_shape=jax.ShapeDtypeStruct((rows, d_ff), BF16),
        compiler_params=_cparams(2), name="moe_up")(block_e, n_active, xb, w_gate, w_up)
    return pl.pallas_call(
        _moe_down_body,
        grid_spec=pltpu.PrefetchScalarGridSpec(
            num_scalar_prefetch=2, grid=(d // tn_down, n_blk),
            in_specs=[pl.BlockSpec((bm, d_ff), lambda j, b, be, na: (b, 0)),
                      pl.BlockSpec((None, d_ff, tn_down), lambda j, b, be, na: (be[b], 0, j))],
            out_specs=pl.BlockSpec((bm, tn_down), lambda j, b, be, na: (b, j)),
            scratch_shapes=[pltpu.VMEM((d_ff, tn_down), BF16)]),
        out_shape=jax.ShapeDtypeStruct((rows, d), F32),
        compiler_params=_cparams(2), name="moe_down")(block_e, n_active, act, w_down)


def _row_copy(src_hbm, src_row, dst, dst_row, sem):
    return pltpu.make_async_copy(src_hbm.at[pl.ds(src_row, 1), :], dst.at[pl.ds(dst_row, 1), :], sem)


def _moe_gather_body(tok_ref, h_hbm, o_ref, rows, sem, *, bm):
    def start(r, carry):
        _row_copy(h_hbm, tok_ref[0, r], rows, r, sem).start()
        return carry

    def wait(r, carry):
        _row_copy(h_hbm, 0, rows, r, sem).wait()
        return carry

    lax.fori_loop(0, bm, start, 0, unroll=8)
    lax.fori_loop(0, bm, wait, 0, unroll=8)
    o_ref[...] = rows[...].astype(o_ref.dtype)


def _moe_gather(h, slot_tok):
    _, d = h.shape
    bm = MOE_BLOCK
    n_blk = slot_tok.shape[0] // bm
    return pl.pallas_call(
        functools.partial(_moe_gather_body, bm=bm), grid=(n_blk,),
        in_specs=[pl.BlockSpec((None, 1, bm), lambda b: (b, 0, 0), memory_space=pltpu.SMEM),
                  pl.BlockSpec(memory_space=pl.ANY)],
        out_specs=pl.BlockSpec((bm, d), lambda b: (b, 0)),
        out_shape=jax.ShapeDtypeStruct((n_blk * bm, d), BF16),
        scratch_shapes=[pltpu.VMEM((bm, d), h.dtype), pltpu.SemaphoreType.DMA(())],
        compiler_params=_cparams(1), name="moe_gather")(slot_tok.reshape(n_blk, 1, bm), h)


def _moe_combine_body(slot_ref, x_ref, g_ref, gt_ref, yb_hbm, o_ref, ybuf, sem, *, tm):
    def start(r, carry):
        for k in range(TOP_K):
            _row_copy(yb_hbm, slot_ref[0, TOP_K * r + k], ybuf.at[k], r, sem).start()
        return carry

    def wait(r, carry):
        for k in range(TOP_K):
            _row_copy(yb_hbm, 0, ybuf.at[k], r, sem).wait()
        return carry

    lax.fori_loop(0, tm, start, 0, unroll=4)
    lax.fori_loop(0, tm, wait, 0, unroll=4)
    g = g_ref[...]
    y = ybuf[0] * g[:, 0:1] + ybuf[1] * g[:, 1:2]
    o_ref[...] = x_ref[...] + gt_ref[...] * y


def _moe_combine(x, yb, slot_of, gates, mod5, layer, lay, tm=256):
    n, d = x.shape
    tm = min(tm, lay.s_len, lay.p_rows)
    row = pl.BlockSpec((tm, d), lambda i: (i, 0))
    return pl.pallas_call(
        functools.partial(_moe_combine_body, tm=tm), grid=(n // tm,),
        in_specs=[pl.BlockSpec((None, 1, TOP_K * tm), lambda i: (i, 0, 0), memory_space=pltpu.SMEM),
                  row, pl.BlockSpec((tm, TOP_K), lambda i: (i, 0)),
                  pl.BlockSpec((None, None, None, 1, d), _mod_spec(layer, 5, tm, lay)),
                  pl.BlockSpec(memory_space=pl.ANY)],
        out_specs=row, out_shape=jax.ShapeDtypeStruct((n, d), F32),
        scratch_shapes=[pltpu.VMEM((TOP_K, tm, d), F32), pltpu.SemaphoreType.DMA(())],
        compiler_params=_cparams(1), name="moe_combine",
    )(slot_of.reshape(n // tm, 1, TOP_K * tm), x, gates, mod5, yb)


def _moe(h, logits, w_gate, w_up, w_down):
    n_tok, d = h.shape
    bm = MOE_BLOCK
    top_val, top_idx = lax.top_k(logits, TOP_K)
    gates = jax.nn.softmax(top_val, axis=-1)
    n_assign = n_tok * TOP_K
    flat_e = top_idx.reshape(-1).astype(jnp.int32)
    order = jnp.argsort(flat_e).astype(jnp.int32)
    rank = jnp.argsort(order).astype(jnp.int32)
    counts = jnp.sum(flat_e[:, None] == jnp.arange(N_EXPERTS, dtype=jnp.int32)[None, :], axis=0,
                     dtype=jnp.int32)
    padded = ((counts + bm - 1) // bm) * bm
    pad_end = jnp.cumsum(padded)
    pad_start = pad_end - padded
    start = jnp.cumsum(counts) - counts
    n_blocks = -(-n_assign // bm) + N_EXPERTS
    block_e = jnp.minimum(jnp.searchsorted(pad_end, jnp.arange(n_blocks) * bm, side='right'),
                          N_EXPERTS - 1).astype(jnp.int32)
    n_active = (pad_end[-1] // bm).astype(jnp.int32).reshape(1)
    experts = jnp.arange(N_EXPERTS, dtype=jnp.int32)

    def lookup(table, e):
        return jnp.sum(jnp.where(e[..., None] == experts, table, 0), axis=-1, dtype=jnp.int32)

    slot = jnp.arange(n_blocks * bm, dtype=jnp.int32)
    slot_e = jnp.repeat(block_e, bm)
    pos = slot - lookup(pad_start, slot_e)
    src = jnp.clip(lookup(start, slot_e) + pos, 0, n_assign - 1)
    slot_tok = jnp.where(pos < lookup(counts, slot_e),
                         jnp.take(order, src, indices_are_sorted=True) // TOP_K, slot % n_tok)
    yb = _moe_experts(_moe_gather(h, slot_tok), block_e, n_active, w_gate, w_up, w_down)
    slot_of = (lookup(pad_start - start, flat_e) + rank).reshape(n_tok, TOP_K)
    return yb, slot_of, gates


def _final_norm_body(x_ref, w_ref, o_ref):
    x = x_ref[...]
    o_ref[...] = x * lax.rsqrt(jnp.mean(x * x, axis=-1, keepdims=True) + NORM_EPS) * w_ref[...]


def _final_norm(x, w, row0, n_rows, tm=512):
    d = x.shape[1]
    tm = min(tm, n_rows)
    rb0 = row0 // tm
    return pl.pallas_call(
        _final_norm_body, grid=(n_rows // tm,),
        in_specs=[pl.BlockSpec((tm, d), lambda i: (rb0 + i, 0)), pl.BlockSpec((1, d), lambda i: (0, 0))],
        out_specs=pl.BlockSpec((tm, d), lambda i: (i, 0)),
        out_shape=jax.ShapeDtypeStruct((n_rows, d), F32),
        compiler_params=_cparams(1), name="final_norm")(x, w.reshape(1, d))


def _rope_tables(seq_len, dim):
    t = jnp.arange(seq_len)
    row = (t // GRID_W).astype(F32)
    col = (t % GRID_W).astype(F32)
    half = dim // 2
    inv = ROPE_BASE ** (-(jnp.arange(0, half, 2, dtype=F32) / half))
    ang = jnp.concatenate([row[:, None] * inv, col[:, None] * inv], axis=-1)
    cos, sin = jnp.cos(ang), jnp.sin(ang)
    return jnp.repeat(cos, 2, axis=-1), jnp.stack([-sin, sin], axis=-1).reshape(seq_len, dim)


def _block_diag_states(s):
    b, two, n_heads, hd, _ = s.shape
    hps = RWKV_SLAB // hd
    s6 = s.reshape(b, two, n_heads // hps, hps, hd, hd)
    bd = jnp.einsum('bdghvk,hi->bdghvik', s6, jnp.eye(hps, dtype=s.dtype))
    return bd.reshape(b, two, n_heads // hps, RWKV_SLAB, RWKV_SLAB)


def _pad_cols(w, width):
    return jnp.pad(w, ((0, 0), (0, width - w.shape[1])))


def kernel(x_prompt, x_sample, state_l0_ret, state_l1_rwkv, c, c_ctx, ada_w, ada_b, norm_w, final_norm_w,
           l0_ret_w_in, l0_ret_w_out, l0_ret_decay, l0_ffn_w_gate, l0_ffn_w_up, l0_ffn_w_down,
           l1_rwkv_mu, l1_rwkv_w_rkv, l1_rwkv_w0, l1_rwkv_w1, l1_rwkv_w2, l1_rwkv_a0, l1_rwkv_a1, l1_rwkv_a2,
           l1_rwkv_g1, l1_rwkv_g2, l1_rwkv_k_k, l1_rwkv_k_a, l1_rwkv_r_k, l1_rwkv_ln_w, l1_rwkv_ln_b,
           l1_rwkv_w_out, l1_moe_router, l1_moe_w_gate, l1_moe_w_up, l1_moe_w_down):
    pb, p_len, d = x_prompt.shape
    sb, s_len, _ = x_sample.shape
    p_rows, s_rows = pb * p_len, sb * s_len
    lay = _Layout(p_rows, p_len, s_len, p_rows + s_rows)
    n_layers = ada_w.shape[0]

    x = jnp.concatenate([x_prompt.reshape(p_rows, d), x_sample.reshape(s_rows, d)], axis=0)
    cond8 = jnp.concatenate([c_ctx[None, :], c, jnp.zeros((8 - 1 - sb, d), F32)], axis=0)
    mod5 = _ada_mod(cond8, ada_w, ada_b)[:, :1 + sb].reshape(n_layers, 1 + sb, 6, 1, d)

    h = _norm_mod(x, norm_w[0, 0], mod5, 0, 1, 0, lay)
    qkvg = _linear(h, l0_ret_w_in, tn=1024)
    log_gamma = -jnp.exp(l0_ret_decay.astype(F32))
    dh = d // RET_HEADS
    o_p, new_state_l0_ret = _retention(qkvg, log_gamma, None, None, row0=0, n_seq=pb, seq_len=p_len,
                                       emit_state=True)
    (o_s,) = _retention(qkvg, log_gamma, state_l0_ret, _rope_tables(s_len, dh), row0=p_rows, n_seq=sb,
                        seq_len=s_len, emit_state=False)
    x = _linear_residual(_ret_finalize(o_p, o_s, qkvg), l0_ret_w_out, x, mod5, 0, 2, lay)
    h = _norm_mod(x, norm_w[0, 1], mod5, 0, 4, 3, lay)
    act = _swiglu_up(h, l0_ffn_w_gate, l0_ffn_w_up)
    x = _linear_residual(act, l0_ffn_w_down, x, mod5, 0, 5, lay, tm=512, tn=512)

    xs = _rwkv_mix(x, norm_w[1, 0], mod5, 1, l1_rwkv_mu, lay)
    rkv = _linear(xs, l1_rwkv_w_rkv, n_batch=3, tn=1024)
    lora_w = l1_rwkv_w1.shape[2]
    pad_w = -(-lora_w // 128) * 128
    w1p = jnp.concatenate([_pad_cols(l1_rwkv_w1[0], pad_w), _pad_cols(l1_rwkv_w1[1], pad_w)], axis=1)
    a1p = jnp.concatenate([_pad_cols(l1_rwkv_a1[0], pad_w), _pad_cols(l1_rwkv_a1[1], pad_w)], axis=1)
    t_w = _linear(xs, w1p, x_lead=3, act="tanh", out_dtype=BF16)
    t_a = _linear(xs, a1p, x_lead=4, out_dtype=BF16)
    t_g = _linear(xs, l1_rwkv_g1, x_lead=5, act="sigmoid", out_dtype=BF16)
    w2p = jnp.pad(l1_rwkv_w2, ((0, 0), (0, pad_w - lora_w), (0, 0)))
    a2p = jnp.pad(l1_rwkv_a2, ((0, 0), (0, pad_w - lora_w), (0, 0)))
    lw = _linear(t_w, w2p, n_batch=2, bias=l1_rwkv_w0, act="log_decay")
    a_lr = _linear(t_a, a2p, n_batch=2, bias=l1_rwkv_a0, act="sigmoid")
    g = _linear(t_g, l1_rwkv_g2)
    scan_args = (rkv, lw, a_lr, l1_rwkv_k_k, l1_rwkv_k_a)
    y_p, new_state_l1_rwkv = _rwkv_scan(*scan_args, None, row0=0, n_seq=pb, seq_len=p_len, emit_state=True)
    (y_s,) = _rwkv_scan(*scan_args, _block_diag_states(state_l1_rwkv), row0=p_rows, n_seq=sb, seq_len=s_len,
                        emit_state=False)
    yg = _rwkv_finalize(y_p, y_s, rkv, a_lr, g, l1_rwkv_k_a, l1_rwkv_r_k, l1_rwkv_ln_w, l1_rwkv_ln_b)
    x = _linear_residual(yg, l1_rwkv_w_out, x, mod5, 1, 2, lay)
    ne_pad = 128
    h, logits = _norm_mod(x, norm_w[1, 1], mod5, 1, 4, 3, lay, router_w=_pad_cols(l1_moe_router, ne_pad))
    yb, slot_of, gates = _moe(h, logits[:, :N_EXPERTS], l1_moe_w_gate, l1_moe_w_up, l1_moe_w_down)
    x = _moe_combine(x, yb, slot_of, gates, mod5, 1, lay)

    y_prompt = _final_norm(x, final_norm_w, 0, p_rows).reshape(pb, p_len, d)
    y_sample = _final_norm(x, final_norm_w, p_rows, s_rows).reshape(sb, s_len, d)
    return (y_prompt, y_sample, new_state_l0_ret, new_state_l1_rwkv)
```

```python
import functools
from typing import NamedTuple

import jax
import jax.numpy as jnp
from jax import lax
from jax.experimental import pallas as pl
from jax.experimental.pallas import tpu as pltpu

F32 = jnp.float32
BF16 = jnp.bfloat16

GRID_W = 64
RET_HEADS = 8
RET_CHUNK = 128
RET_HEADS_PER_STEP = 2
ROPE_BASE = 10000.0
RWKV_HEAD_DIM = 64
RWKV_CHUNK = 64
RWKV_SLAB = 256
RWKV_CHUNKS_PER_ITER = 2
RWKV_LN_EPS = 64e-5
N_EXPERTS = 8
TOP_K = 2
MOE_BLOCK = 512
NORM_EPS = 1e-6

VMEM_LIMIT_BYTES = 56 * 1024 * 1024


class _Layout(NamedTuple):
    p_rows: int
    p_len: int
    s_len: int
    n_rows: int


def _cparams(n_axes):
    return pltpu.CompilerParams(dimension_semantics=("arbitrary",) * n_axes,
                                vmem_limit_bytes=VMEM_LIMIT_BYTES)


def _cond_of_tile(i, tm, lay):
    pt = lay.p_rows // tm
    st = lay.s_len // tm
    return jnp.where(i < pt, 0, 1 + (i - pt) // st)


def _bdot(a, b):
    return jnp.dot(a.astype(BF16), b.astype(BF16), preferred_element_type=F32)


def _bdot_nt(a, b):
    return lax.dot_general(a.astype(BF16), b.astype(BF16), (((1,), (1,)), ((), ())),
                           preferred_element_type=F32)


def _bdot_tn(a, b):
    return lax.dot_general(a.astype(BF16), b.astype(BF16), (((0,), (0,)), ((), ())),
                           preferred_element_type=F32)


def _silu(x):
    return x * jax.nn.sigmoid(x)


def _ada_body(c_ref, w_ref, b_ref, o_ref):
    o_ref[...] = _bdot(_silu(c_ref[...]), w_ref[...]) + b_ref[...]


def _ada_mod(cond8, ada_w, ada_b):
    n_layers, d, d6 = ada_w.shape
    tn = min(1024, d6)
    return pl.pallas_call(
        _ada_body,
        grid=(n_layers, d6 // tn),
        in_specs=[pl.BlockSpec((8, d), lambda l, j: (0, 0)),
                  pl.BlockSpec((None, d, tn), lambda l, j: (l, 0, j)),
                  pl.BlockSpec((None, 1, tn), lambda l, j: (l, 0, j))],
        out_specs=pl.BlockSpec((None, 8, tn), lambda l, j: (l, 0, j)),
        out_shape=jax.ShapeDtypeStruct((n_layers, 8, d6), F32),
        compiler_params=_cparams(2), name="ada_mod",
    )(cond8, ada_w, ada_b.reshape(n_layers, 1, d6))


def _mod_spec(layer, which, tm, lay, n_grid_axes=1):
    def imap(i, *_):
        return (layer, _cond_of_tile(i, tm, lay), which, 0, 0)
    return imap


def _rms_mod(x, nw, sc, sh):
    xn = x * lax.rsqrt(jnp.mean(x * x, axis=-1, keepdims=True) + NORM_EPS) * nw
    return xn * (1.0 + sc) + sh


def _norm_mod_body(x_ref, nw_ref, sc_ref, sh_ref, o_ref):
    o_ref[...] = _rms_mod(x_ref[...], nw_ref[...], sc_ref[...], sh_ref[...]).astype(o_ref.dtype)


def _norm_mod_router_body(x_ref, nw_ref, sc_ref, sh_ref, wr_ref, o_ref, lg_ref):
    h = _rms_mod(x_ref[...], nw_ref[...], sc_ref[...], sh_ref[...])
    o_ref[...] = h.astype(o_ref.dtype)
    lg_ref[...] = jnp.dot(h, wr_ref[...], preferred_element_type=F32, precision=lax.Precision.HIGHEST)


def _norm_mod(x, nw, mod5, layer, which_sc, which_sh, lay, router_w=None, tm=512):
    n, d = x.shape
    tm = min(tm, lay.s_len, lay.p_rows)
    in_specs = [pl.BlockSpec((tm, d), lambda i: (i, 0)),
                pl.BlockSpec((1, d), lambda i: (0, 0)),
                pl.BlockSpec((None, None, None, 1, d), _mod_spec(layer, which_sc, tm, lay)),
                pl.BlockSpec((None, None, None, 1, d), _mod_spec(layer, which_sh, tm, lay))]
    args = [x, nw.reshape(1, d), mod5, mod5]
    if router_w is None:
        return pl.pallas_call(
            _norm_mod_body, grid=(n // tm,), in_specs=in_specs,
            out_specs=pl.BlockSpec((tm, d), lambda i: (i, 0)),
            out_shape=jax.ShapeDtypeStruct((n, d), BF16),
            compiler_params=_cparams(1), name="norm_mod")(*args)
    ne = router_w.shape[1]
    return pl.pallas_call(
        _norm_mod_router_body, grid=(n // tm,),
        in_specs=in_specs + [pl.BlockSpec((d, ne), lambda i: (0, 0))],
        out_specs=[pl.BlockSpec((tm, d), lambda i: (i, 0)), pl.BlockSpec((tm, ne), lambda i: (i, 0))],
        out_shape=[jax.ShapeDtypeStruct((n, d), F32), jax.ShapeDtypeStruct((n, ne), F32)],
        compiler_params=_cparams(1), name="norm_mod_router")(*args, router_w)


def _cast_weights_once(row_axis, pairs):
    @pl.when(pl.program_id(row_axis) == 0)
    def _():
        for w_ref, w_bf in pairs:
            w_bf[...] = w_ref[...].astype(BF16)


def _softplus(x):
    return jnp.maximum(x, 0.0) + jnp.log1p(jnp.exp(-jnp.abs(x)))


def _linear_body(x_ref, w_ref, *refs, act, has_bias):
    refs = list(refs)
    bias_ref = refs.pop(0) if has_bias else None
    o_ref, w_bf = refs
    _cast_weights_once(2, [(w_ref, w_bf)])
    acc = jnp.dot(x_ref[...].astype(BF16), w_bf[...], preferred_element_type=F32)
    if has_bias:
        acc = acc + bias_ref[...]
    if act == "tanh":
        acc = jnp.tanh(acc)
    elif act == "sigmoid":
        acc = jax.nn.sigmoid(acc)
    elif act == "log_decay":
        acc = -jnp.exp(-_softplus(-acc) - 0.5)
    o_ref[...] = acc.astype(o_ref.dtype)


def _linear(x, w, *, n_batch=None, x_lead=0, bias=None, act=None, out_dtype=F32, tm=1024, tn=512):
    k, n = w.shape[-2:]
    m = x.shape[-2]
    tm, tn = min(tm, m), min(tn, n)
    if x.ndim == 3:
        x_spec = pl.BlockSpec((None, tm, k), lambda b, j, i: (x_lead + b, i, 0))
    else:
        x_spec = pl.BlockSpec((tm, k), lambda b, j, i: (i, b))
    if w.ndim == 3:
        w_spec = pl.BlockSpec((None, k, tn), lambda b, j, i: (b, 0, j))
    else:
        w_spec = pl.BlockSpec((k, tn), lambda b, j, i: (0, j))
    if n_batch is None:
        out_spec = pl.BlockSpec((tm, tn), lambda b, j, i: (i, j))
        out_shape = jax.ShapeDtypeStruct((m, n), out_dtype)
    else:
        out_spec = pl.BlockSpec((None, tm, tn), lambda b, j, i: (b, i, j))
        out_shape = jax.ShapeDtypeStruct((n_batch, m, n), out_dtype)
    in_specs, args = [x_spec, w_spec], [x, w]
    if bias is not None:
        in_specs.append(pl.BlockSpec((None, 1, tn), lambda b, j, i: (b, 0, j)))
        args.append(bias.reshape(bias.shape[0], 1, n))
    return pl.pallas_call(
        functools.partial(_linear_body, act=act, has_bias=bias is not None),
        grid=(n_batch or 1, n // tn, m // tm), in_specs=in_specs,
        out_specs=out_spec, out_shape=out_shape,
        scratch_shapes=[pltpu.VMEM((k, tn), BF16)],
        compiler_params=_cparams(3), name="linear")(*args)


def _linear_res_body(x_ref, w_ref, res_ref, gt_ref, o_ref, w_bf):
    _cast_weights_once(1, [(w_ref, w_bf)])
    o_ref[...] = res_ref[...] + gt_ref[...] * jnp.dot(x_ref[...], w_bf[...], preferred_element_type=F32)


def _linear_residual(x, w, res, mod5, layer, which_gate, lay, tm=1024, tn=1024):
    m, k = x.shape
    n = w.shape[1]
    tm, tn = min(tm, lay.s_len, lay.p_rows), min(tn, n)

    def gmap(j, i):
        return (layer, _cond_of_tile(i, tm, lay), which_gate, 0, j)

    return pl.pallas_call(
        _linear_res_body, grid=(n // tn, m // tm),
        in_specs=[pl.BlockSpec((tm, k), lambda j, i: (i, 0)),
                  pl.BlockSpec((k, tn), lambda j, i: (0, j)),
                  pl.BlockSpec((tm, tn), lambda j, i: (i, j)),
                  pl.BlockSpec((None, None, None, 1, tn), gmap)],
        out_specs=pl.BlockSpec((tm, tn), lambda j, i: (i, j)),
        out_shape=jax.ShapeDtypeStruct((m, n), F32),
        scratch_shapes=[pltpu.VMEM((k, tn), BF16)],
        compiler_params=_cparams(2), name="linear_residual")(x, w, res, mod5)


def _swiglu_up_body(x_ref, wg_ref, wu_ref, o_ref, wg_bf, wu_bf):
    _cast_weights_once(1, [(wg_ref, wg_bf), (wu_ref, wu_bf)])
    x = x_ref[...]
    gate = jnp.dot(x, wg_bf[...], preferred_element_type=F32)
    up = jnp.dot(x, wu_bf[...], preferred_element_type=F32)
    o_ref[...] = (_silu(gate) * up).astype(o_ref.dtype)


def _swiglu_up(x, wg, wu, tm=1024, tn=512):
    m, k = x.shape
    n = wg.shape[1]
    tm, tn = min(tm, m), min(tn, n)
    return pl.pallas_call(
        _swiglu_up_body, grid=(n // tn, m // tm),
        in_specs=[pl.BlockSpec((tm, k), lambda j, i: (i, 0)),
                  pl.BlockSpec((k, tn), lambda j, i: (0, j)),
                  pl.BlockSpec((k, tn), lambda j, i: (0, j))],
        out_specs=pl.BlockSpec((tm, tn), lambda j, i: (i, j)),
        out_shape=jax.ShapeDtypeStruct((m, n), BF16),
        scratch_shapes=[pltpu.VMEM((k, tn), BF16)] * 2,
        compiler_params=_cparams(2), name="swiglu_up")(x, wg, wu)


def _rope(x, c, s):
    w = x.shape[-1]
    lane = lax.broadcasted_iota(jnp.int32, x.shape, x.ndim - 1)
    nxt = pltpu.roll(x, w - 1, axis=x.ndim - 1)
    prv = pltpu.roll(x, 1, axis=x.ndim - 1)
    return x * c + jnp.where(lane % 2 == 0, nxt, prv) * s


def _retention_body(lg_ref, *refs, n_chunk, n_blk, has_rope, has_s0, emit_state, scale):
    refs = list(refs)
    q_ref, k_ref, v_ref = refs[:3]
    refs = refs[3:]
    if has_rope:
        cos_ref, sin_ref = refs[:2]
        refs = refs[2:]
    if has_s0:
        s0_ref = refs.pop(0)
    o_ref = refs.pop(0)
    if emit_state:
        so_ref = refs.pop(0)
    st = refs.pop(0)
    c_sz = RET_CHUNK
    hp, dh = st.shape[0], st.shape[1]
    hg, d, cb = pl.program_id(1), pl.program_id(2), pl.program_id(3)
    heads = range(hp)
    lanes = [slice(h * dh, (h + 1) * dh) for h in heads]

    @pl.when(cb == 0)
    def _():
        st[...] = s0_ref[...] if has_s0 else jnp.zeros_like(st)

    lgv = [lg_ref[d, hg * hp + h] for h in heads]
    fwd = d == 0
    row = lax.broadcasted_iota(jnp.int32, (c_sz, c_sz), 0)
    col = lax.broadcasted_iota(jnp.int32, (c_sz, c_sz), 1)
    diff = jnp.where(fwd, row - col, col - row).astype(F32)
    intra = [jnp.where(diff >= 0, jnp.exp(jnp.maximum(diff, 0.0) * lg), 0.0) for lg in lgv]
    pos = lax.broadcasted_iota(jnp.int32, (c_sz, dh), 0)
    npos = jnp.where(fwd, pos, c_sz - 1 - pos).astype(F32)
    q_decay = [jnp.exp((npos + 1.0) * lg) for lg in lgv]
    k_decay = [jnp.exp((c_sz - 1.0 - npos) * lg) for lg in lgv]
    chunk_decay = [jnp.exp(jnp.full((1, dh), c_sz, F32) * lg) for lg in lgv]

    for j in range(n_chunk):
        jj = jnp.where(fwd, j, n_chunk - 1 - j)
        rows = pl.ds(pl.multiple_of(jj * c_sz, c_sz), c_sz)
        q = [q_ref[rows, sl] for sl in lanes]
        k = [k_ref[rows, sl] * scale for sl in lanes]
        v = [v_ref[rows, sl] for sl in lanes]
        if has_rope:
            c, s = cos_ref[rows, :], sin_ref[rows, :]
            q = [_rope(x, c, s) for x in q]
            k = [_rope(x, c, s) for x in k]
        state = [st[h] for h in heads]
        scores = [_bdot_nt(q[h], k[h]) * intra[h] for h in heads]
        o = [_bdot(scores[h], v[h]) + _bdot(q[h], state[h]) * q_decay[h] for h in heads]
        new_state = [state[h] * chunk_decay[h] + _bdot_tn(k[h] * k_decay[h], v[h]) for h in heads]
        o_ref[rows, :] = jnp.concatenate(o, axis=1)
        st[...] = jnp.stack(new_state)

    if emit_state:
        @pl.when(cb == n_blk - 1)
        def _():
            so_ref[...] = st[...]


def _retention(qkvg, log_gamma, s0, rope, *, row0, n_seq, seq_len, emit_state):
    d_model = qkvg.shape[1] // 4
    n_heads = RET_HEADS
    dh = d_model // n_heads
    tb = min(512, seq_len)
    n_blk = seq_len // tb
    rb0 = row0 // tb

    def blk(c, d):
        return jnp.where(d == 0, c, n_blk - 1 - c)

    hp = RET_HEADS_PER_STEP
    n_hg = n_heads // hp

    def in_map(part):
        return lambda s, h, d, c, lg: (rb0 + s * n_blk + blk(c, d), part * n_hg + h)

    in_specs = [pl.BlockSpec((tb, hp * dh), in_map(p)) for p in range(3)]
    args = [qkvg, qkvg, qkvg]
    if rope is not None:
        in_specs += [pl.BlockSpec((tb, dh), lambda s, h, d, c, lg: (blk(c, d), 0))] * 2
        args += list(rope)
    if s0 is not None:
        in_specs.append(pl.BlockSpec((None, None, hp, dh, dh), lambda s, h, d, c, lg: (s, d, h, 0, 0)))
        args.append(s0)
    out_specs = [pl.BlockSpec((None, tb, hp * dh), lambda s, h, d, c, lg: (d, s * n_blk + blk(c, d), h))]
    out_shape = [jax.ShapeDtypeStruct((2, n_seq * seq_len, d_model), F32)]
    if emit_state:
        out_specs.append(pl.BlockSpec((None, None, hp, dh, dh), lambda s, h, d, c, lg: (s, d, h, 0, 0)))
        out_shape.append(jax.ShapeDtypeStruct((n_seq, 2, n_heads, dh, dh), F32))
    body = functools.partial(_retention_body, n_chunk=tb // RET_CHUNK, n_blk=n_blk, has_rope=rope is not None,
                             has_s0=s0 is not None, emit_state=emit_state, scale=dh ** -0.5)
    return pl.pallas_call(
        body,
        grid_spec=pltpu.PrefetchScalarGridSpec(
            num_scalar_prefetch=1, grid=(n_seq, n_hg, 2, n_blk), in_specs=in_specs, out_specs=out_specs,
            scratch_shapes=[pltpu.VMEM((hp, dh, dh), F32)]),
        out_shape=out_shape, compiler_params=_cparams(4), name="retention")(log_gamma, *args)


def _two_trunk_specs(a_p, a_s, tm):
    pt = a_p.shape[1] // tm
    d = a_p.shape[2]
    return [pl.BlockSpec((2, tm, d), lambda i: (0, jnp.minimum(i, pt - 1), 0)),
            pl.BlockSpec((2, tm, d), lambda i: (0, jnp.maximum(i - pt, 0), 0))], pt


def _for_trunk_of_tile(pt, p_ref, s_ref, fn):
    i = pl.program_id(0)

    @pl.when(i < pt)
    def _():
        fn(p_ref)

    @pl.when(i >= pt)
    def _():
        fn(s_ref)


def _ret_finalize_body(op_ref, os_ref, g_ref, out_ref, *, n_heads, pt):
    def finalize(o_ref):
        o = o_ref[0] + o_ref[1]
        g = g_ref[...]
        dh = o.shape[1] // n_heads
        for h in range(n_heads):
            sl = slice(h * dh, (h + 1) * dh)
            oh = o[:, sl]
            oh = oh * lax.rsqrt(jnp.mean(oh * oh, axis=-1, keepdims=True) + NORM_EPS)
            out_ref[:, sl] = (oh * _silu(g[:, sl])).astype(out_ref.dtype)

    _for_trunk_of_tile(pt, op_ref, os_ref, finalize)


def _ret_finalize(o_p, o_s, qkvg, tm=256):
    n, d = o_p.shape[1] + o_s.shape[1], o_p.shape[2]
    specs, pt = _two_trunk_specs(o_p, o_s, tm)
    return pl.pallas_call(
        functools.partial(_ret_finalize_body, n_heads=RET_HEADS, pt=pt), grid=(n // tm,),
        in_specs=specs + [pl.BlockSpec((tm, d), lambda i: (i, 3))],
        out_specs=pl.BlockSpec((tm, d), lambda i: (i, 0)),
        out_shape=jax.ShapeDtypeStruct((n, d), BF16),
        compiler_params=_cparams(1), name="ret_finalize")(o_p, o_s, qkvg)


def _rwkv_mix_body(x_ref, xp_ref, xn_ref, nw_ref, sc_ref, sh_ref, mu_ref, o_ref, hext, *, tm, halo, lay):
    i = pl.program_id(0)
    d = x_ref.shape[1]
    nw, sc, sh = nw_ref[...], sc_ref[...], sh_ref[...]
    hext[0:halo, :] = _rms_mod(xp_ref[...], nw, sc, sh)
    hext[halo:halo + tm, :] = _rms_mod(x_ref[...], nw, sc, sh)
    hext[halo + tm:halo + tm + halo, :] = _rms_mod(xn_ref[...], nw, sc, sh)
    h = hext[halo:halo + tm, :]
    g_row = i * tm + lax.broadcasted_iota(jnp.int32, (tm, 1), 0)

    def emit(h_shift):
        diff = h_shift - h
        for n in range(6):
            o_ref[n] = (h + diff * mu_ref[n:n + 1, :]).astype(o_ref.dtype)

    def shifted(off, lo, hi, keep):
        return jnp.where(keep, hext[halo + off:halo + off + tm, lo:hi], 0.0)

    @pl.when(i < lay.p_rows // tm)
    def _():
        t = g_row % lay.p_len
        hd = d // 2
        emit(jnp.concatenate([shifted(-1, 0, hd, t != 0),
                              shifted(1, hd, d, t != lay.p_len - 1)], axis=1))

    @pl.when(i >= lay.p_rows // tm)
    def _():
        t = (g_row - lay.p_rows) % lay.s_len
        colw = t % GRID_W
        qd = d // 4
        emit(jnp.concatenate([shifted(-1, 0, qd, colw != 0),
                              shifted(1, qd, 2 * qd, colw != GRID_W - 1),
                              shifted(-GRID_W, 2 * qd, 3 * qd, t >= GRID_W),
                              shifted(GRID_W, 3 * qd, d, t < lay.s_len - GRID_W)], axis=1))


def _rwkv_mix(x, nw, mod5, layer, mu, lay, tm=512):
    n, d = x.shape
    halo = GRID_W
    tm = min(tm, lay.s_len, lay.p_rows)
    r = tm // halo
    n_halo_blk = n // halo
    body = functools.partial(_rwkv_mix_body, tm=tm, halo=halo, lay=lay)
    return pl.pallas_call(
        body, grid=(n // tm,),
        in_specs=[pl.BlockSpec((tm, d), lambda i: (i, 0)),
                  pl.BlockSpec((halo, d), lambda i: (jnp.maximum(i * r - 1, 0), 0)),
                  pl.BlockSpec((halo, d), lambda i: (jnp.minimum((i + 1) * r, n_halo_blk - 1), 0)),
                  pl.BlockSpec((1, d), lambda i: (0, 0)),
                  pl.BlockSpec((None, None, None, 1, d), _mod_spec(layer, 1, tm, lay)),
                  pl.BlockSpec((None, None, None, 1, d), _mod_spec(layer, 0, tm, lay)),
                  pl.BlockSpec((6, d), lambda i: (0, 0))],
        out_specs=pl.BlockSpec((6, tm, d), lambda i: (0, i, 0)),
        out_shape=jax.ShapeDtypeStruct((6, n, d), BF16),
        scratch_shapes=[pltpu.VMEM((tm + 2 * halo, d), F32)],
        compiler_params=_cparams(1), name="rwkv_mix")(x, x, x, nw.reshape(1, d), mod5, mod5, mu)


def _rwkv_scan_body(*refs, n_chunk, n_blk, has_s0, emit_state):
    refs = list(refs)
    r_ref, k_ref, v_ref, lw_ref, a_ref, kk_ref, ka_ref = refs[:7]
    refs = refs[7:]
    if has_s0:
        s0_ref = refs.pop(0)
    y_ref = refs.pop(0)
    if emit_state:
        so_ref = refs.pop(0)
    st = refs.pop(0)
    c_sz, hd, sw = RWKV_CHUNK, RWKV_HEAD_DIM, RWKV_SLAB
    n_grp = st.shape[0]
    hps = sw // hd
    d, cb = pl.program_id(1), pl.program_id(3)
    fwd = d == 0

    ri = lax.broadcasted_iota(jnp.int32, (sw, sw), 0)
    ci = lax.broadcasted_iota(jnp.int32, (sw, sw), 1)
    bd_mask = (ri // hd) == (ci // hd)

    def block_diag(slab):
        return jnp.where(bd_mask, jnp.concatenate([slab] * hps, axis=0), 0.0).astype(BF16)

    @pl.when(cb == 0)
    def _():
        if has_s0:
            st[...] = s0_ref[...]
        else:
            st[...] = jnp.zeros_like(st)

    srow = lax.broadcasted_iota(jnp.int32, (c_sz, sw), 0)
    scol = lax.broadcasted_iota(jnp.int32, (c_sz, sw), 1) % hd
    diff = jnp.where(fwd, srow - scol, scol - srow)
    strict = diff > 0
    incl = diff >= 0
    eye = jnp.where(diff == 0, 1.0, 0.0).astype(F32)
    levels = []
    m = 1
    while m < c_sz:
        levels.append(jnp.logical_and(srow // (2 * m) == scol // (2 * m), srow // m != scol // m))
        m *= 2
    trow = lax.broadcasted_iota(jnp.int32, (c_sz, c_sz), 0)
    tcol = lax.broadcasted_iota(jnp.int32, (c_sz, c_sz), 1)
    tri_incl = jnp.where(jnp.where(fwd, trow - tcol, tcol - trow) >= 0, 1.0, 0.0).astype(BF16)
    ones_bd = jnp.where(bd_mask, 1.0, 0.0).astype(BF16)

    def split(x, n_parts):
        parts = []
        for _ in range(n_parts - 1):
            p = x.astype(BF16)
            parts.append(p)
            x = x - p.astype(F32)
        return parts + [x.astype(BF16)]

    k_k, k_a = kk_ref[...], ka_ref[...]
    grp = range(n_grp)

    n_sub = min(RWKV_CHUNKS_PER_ITER, n_chunk)

    def chunks(j, carry):
        rows_of = []
        for cc in range(n_sub):
            jj = j * n_sub + cc
            jj = jnp.where(fwd, jj, n_chunk - 1 - jj)
            rows_of.append(pl.ds(pl.multiple_of(jj * c_sz, c_sz), c_sz))
        units = [(rows_of[cc], slice(g * sw, (g + 1) * sw)) for cc in range(n_sub) for g in grp]
        un = range(len(units))
        v = [v_ref[rows, sl] for rows, sl in units]
        kkr = [k_ref[rows, sl] * k_k[:, sl] for rows, sl in units]
        sq = [split(x * x, 2) for x in kkr]
        ssum = [sum(jnp.dot(p, ones_bd, preferred_element_type=F32) for p in s2) for s2 in sq]
        kk = [x * lax.rsqrt(s + 1e-12) for x, s in zip(kkr, ssum)]
        a = [a_ref[rows, sl] for rows, sl in units]
        b = [x * y for x, y in zip(kk, a)]
        kdir = [k_ref[rows, sl] * (1.0 + (ai - 1.0) * k_a[:, sl]) for (rows, sl), ai in zip(units, a)]
        lw = [lw_ref[rows, sl] for rows, sl in units]
        cum = [sum(jnp.dot(tri_incl, p, preferred_element_type=F32) for p in split(x, 3)) for x in lw]
        total = [jnp.sum(x, axis=0, keepdims=True) for x in lw]
        half = [0.5 * t for t in total]
        cumx = [c - x for c, x in zip(cum, lw)]
        r = [r_ref[rows, sl] for rows, sl in units]
        lhs_g = [jnp.concatenate([kk[i] * jnp.exp(cumx[i] - half[i]), r[i] * jnp.exp(cum[i] - half[i])], axis=0)
                 for i in un]
        e_neg = [jnp.exp(half[i] - cum[i]) for i in un]
        g_k = [_bdot_nt(lhs_g[i], block_diag(kdir[i] * e_neg[i])) for i in un]
        g_b = [_bdot_nt(lhs_g[i], block_diag(b[i] * e_neg[i])) for i in un]
        l_k = [jnp.where(strict, x[:c_sz], 0.0) for x in g_k]
        a_rk = [jnp.where(incl, x[c_sz:], 0.0) for x in g_k]
        l_b = [jnp.where(strict, x[:c_sz], 0.0) for x in g_b]
        a_rb = [jnp.where(incl, x[c_sz:], 0.0) for x in g_b]
        x = [eye - jnp.where(levels[0], l, 0.0) for l in l_b]
        for lvl in levels[1:]:
            t = [_bdot(x[i], block_diag(jnp.where(lvl, l_b[i], 0.0))) for i in un]
            x = [x[i] - _bdot(t[i], block_diag(x[i])) for i in un]
        bd_v = [block_diag(x) for x in v]
        lkv = [_bdot(l_k[i], bd_v[i]) for i in un]
        wt = [_bdot(x[i], block_diag(kk[i] * jnp.exp(cumx[i]))) for i in un]
        vt = [_bdot(x[i], block_diag(lkv[i])) for i in un]
        r_abs = [r[i] * jnp.exp(cum[i]) for i in un]
        e_end = [jnp.exp(total[i] - cum[i]) for i in un]
        kb_end = [jnp.concatenate([kdir[i] * e_end[i], -(b[i] * e_end[i])], axis=0) for i in un]
        state = [st[g] for g in grp]
        for cc in range(n_sub):
            ids = [cc * n_grp + g for g in grp]
            su = [_bdot_nt(jnp.concatenate([wt[i], r_abs[i]], axis=0), state[g]) for g, i in zip(grp, ids)]
            u = [su[g][:c_sz] + vt[i] for g, i in zip(grp, ids)]
            y = [su[g][c_sz:] + _bdot(a_rk[i], bd_v[i]) - _bdot(a_rb[i], block_diag(u[g]))
                 for g, i in zip(grp, ids)]
            upd = [_bdot_tn(jnp.concatenate([v[i], u[g]], axis=0), kb_end[i]) for g, i in zip(grp, ids)]
            state = [state[g] * jnp.exp(total[i]) + jnp.where(bd_mask, upd[g], 0.0) for g, i in zip(grp, ids)]
            y_ref[rows_of[cc], :] = jnp.concatenate(y, axis=1)
        st[...] = jnp.stack(state)
        return carry

    lax.fori_loop(0, n_chunk // n_sub, chunks, 0)

    if emit_state:
        @pl.when(cb == n_blk - 1)
        def _():
            for g in grp:
                s = st[g]
                for h in range(hps):
                    so_ref[g * hps + h] = s[h * hd:(h + 1) * hd, h * hd:(h + 1) * hd]


def _rwkv_scan(rkv, lw, a, k_k, k_a, s0, *, row0, n_seq, seq_len, emit_state, heads_per_step=32):
    _, _, d_model = rkv.shape
    hd, sw = RWKV_HEAD_DIM, RWKV_SLAB
    n_heads = d_model // hd
    g = min(heads_per_step, n_heads)
    wg = g * hd
    n_grp = wg // sw
    n_hg = n_heads // g
    tb = min(256, seq_len)
    n_blk = seq_len // tb
    rb0 = row0 // tb

    def blk(c, d):
        return jnp.where(d == 0, c, n_blk - 1 - c)

    def rkv_map(part):
        return lambda s, d, hg, c: (part, rb0 + s * n_blk + blk(c, d), hg)

    dir_map = lambda s, d, hg, c: (d, rb0 + s * n_blk + blk(c, d), hg)
    in_specs = ([pl.BlockSpec((None, tb, wg), rkv_map(p)) for p in range(3)]
                + [pl.BlockSpec((None, tb, wg), dir_map)] * 2
                + [pl.BlockSpec((1, wg), lambda s, d, hg, c: (0, hg))] * 2)
    args = [rkv, rkv, rkv, lw, a, k_k.reshape(1, d_model), k_a.reshape(1, d_model)]
    if s0 is not None:
        in_specs.append(pl.BlockSpec((None, None, n_grp, sw, sw), lambda s, d, hg, c: (s, d, hg, 0, 0)))
        args.append(s0)
    out_specs = [pl.BlockSpec((None, tb, wg), lambda s, d, hg, c: (d, s * n_blk + blk(c, d), hg))]
    out_shape = [jax.ShapeDtypeStruct((2, n_seq * seq_len, d_model), F32)]
    if emit_state:
        out_specs.append(pl.BlockSpec((None, None, g, hd, hd), lambda s, d, hg, c: (s, d, hg, 0, 0)))
        out_shape.append(jax.ShapeDtypeStruct((n_seq, 2, n_heads, hd, hd), F32))
    body = functools.partial(_rwkv_scan_body, n_chunk=tb // RWKV_CHUNK, n_blk=n_blk,
                             has_s0=s0 is not None, emit_state=emit_state)
    return pl.pallas_call(
        body, grid=(n_seq, 2, n_hg, n_blk), in_specs=in_specs, out_specs=out_specs, out_shape=out_shape,
        scratch_shapes=[pltpu.VMEM((n_grp, sw, sw), F32)],
        compiler_params=_cparams(4), name="rwkv_scan")(*args)


def _group_sum(x, ones_bd):
    w = ones_bd.shape[0]
    out = []
    for c in range(x.shape[1] // w):
        xs = x[:, c * w:(c + 1) * w]
        hi = xs.astype(BF16)
        lo = (xs - hi.astype(F32)).astype(BF16)
        out.append(jnp.dot(hi, ones_bd, preferred_element_type=F32)
                   + jnp.dot(lo, ones_bd, preferred_element_type=F32))
    return jnp.concatenate(out, axis=1)


def _rwkv_finalize_body(yp_ref, ys_ref, rkv_ref, a_ref, g_ref, ka_ref, rk_ref, lnw_ref, lnb_ref, o_ref, *, pt):
    hd = RWKV_HEAD_DIM
    w = RWKV_SLAB
    ri = lax.broadcasted_iota(jnp.int32, (w, w), 0)
    ci = lax.broadcasted_iota(jnp.int32, (w, w), 1)
    ones_bd = jnp.where(ri // hd == ci // hd, 1.0, 0.0).astype(BF16)

    def finalize(y_ref):
        y = y_ref[0] + y_ref[1]
        mean = _group_sum(y, ones_bd) * (1.0 / hd)
        yc = y - mean
        var = _group_sum(yc * yc, ones_bd) * (1.0 / hd)
        yn = yc * lax.rsqrt(var + RWKV_LN_EPS) * lnw_ref[...] + lnb_ref[...]
        r, k, v = rkv_ref[0], rkv_ref[1], rkv_ref[2]
        a_sum = a_ref[0] + a_ref[1]
        k_sum = k * (2.0 + (a_sum - 2.0) * ka_ref[...])
        bonus = _group_sum(r * k_sum * rk_ref[...], ones_bd) * v
        o_ref[...] = ((yn + bonus) * g_ref[...]).astype(o_ref.dtype)

    _for_trunk_of_tile(pt, yp_ref, ys_ref, finalize)


def _rwkv_finalize(y_p, y_s, rkv, a, g, k_a, r_k, ln_w, ln_b, tm=256):
    n, d = y_p.shape[1] + y_s.shape[1], y_p.shape[2]
    specs, pt = _two_trunk_specs(y_p, y_s, tm)
    row = lambda i: (0, 0)
    return pl.pallas_call(
        functools.partial(_rwkv_finalize_body, pt=pt), grid=(n // tm,),
        in_specs=specs + [
                  pl.BlockSpec((3, tm, d), lambda i: (0, i, 0)),
                  pl.BlockSpec((2, tm, d), lambda i: (0, i, 0)),
                  pl.BlockSpec((tm, d), lambda i: (i, 0)),
                  pl.BlockSpec((1, d), row), pl.BlockSpec((1, d), row),
                  pl.BlockSpec((1, d), row), pl.BlockSpec((1, d), row)],
        out_specs=pl.BlockSpec((tm, d), lambda i: (i, 0)),
        out_shape=jax.ShapeDtypeStruct((n, d), BF16),
        compiler_params=_cparams(1), name="rwkv_finalize",
    )(y_p, y_s, rkv, a, g, k_a.reshape(1, d), r_k.reshape(1, d), ln_w.reshape(1, d), ln_b.reshape(1, d))


def _cast_expert_weights(be_ref, pairs):
    b = pl.program_id(1)

    @pl.when(jnp.logical_or(b == 0, be_ref[b] != be_ref[jnp.maximum(b - 1, 0)]))
    def _():
        for w_ref, w_bf in pairs:
            w_bf[...] = w_ref[...].astype(BF16)


def _moe_up_body(be_ref, na_ref, x_ref, wg_ref, wu_ref, o_ref, wg_bf, wu_bf):
    b = pl.program_id(1)
    _cast_expert_weights(be_ref, [(wg_ref, wg_bf), (wu_ref, wu_bf)])

    @pl.when(b < na_ref[0])
    def _():
        x = x_ref[...]
        gate = jnp.dot(x, wg_bf[...], preferred_element_type=F32)
        up = jnp.dot(x, wu_bf[...], preferred_element_type=F32)
        o_ref[...] = (_silu(gate) * up).astype(o_ref.dtype)

    @pl.when(b >= na_ref[0])
    def _():
        o_ref[...] = jnp.zeros_like(o_ref)


def _moe_down_body(be_ref, na_ref, *refs, n_parts):
    x_refs, w_refs = refs[:n_parts], refs[n_parts:2 * n_parts]
    o_ref = refs[2 * n_parts]
    w_bfs = refs[2 * n_parts + 1:]
    b = pl.program_id(1)
    _cast_expert_weights(be_ref, list(zip(w_refs, w_bfs)))

    @pl.when(b < na_ref[0])
    def _():
        o_ref[...] = sum(jnp.dot(x[...], w[...], preferred_element_type=F32) for x, w in zip(x_refs, w_bfs))

    @pl.when(b >= na_ref[0])
    def _():
        o_ref[...] = jnp.zeros_like(o_ref)


def _moe_up(xb, block_e, n_active, w_gate, w_up, col0, n_cols, tn):
    rows, d = xb.shape
    bm = MOE_BLOCK
    off = col0 // tn
    w_spec = pl.BlockSpec((None, d, tn), lambda j, b, be, na: (be[b], 0, off + j))
    return pl.pallas_call(
        _moe_up_body,
        grid_spec=pltpu.PrefetchScalarGridSpec(
            num_scalar_prefetch=2, grid=(n_cols // tn, rows // bm),
            in_specs=[pl.BlockSpec((bm, d), lambda j, b, be, na: (b, 0)), w_spec, w_spec],
            out_specs=pl.BlockSpec((bm, tn), lambda j, b, be, na: (b, j)),
            scratch_shapes=[pltpu.VMEM((d, tn), BF16)] * 2),
        out_shape=jax.ShapeDtypeStruct((rows, n_cols), BF16),
        compiler_params=_cparams(2), name="moe_up")(block_e, n_active, xb, w_gate, w_up)


def _moe_experts(xb, block_e, n_active, w_gate, w_up, w_down, tn_wide=1024, tn_narrow=512, tn_down=512):
    rows, d = xb.shape
    bm = MOE_BLOCK
    d_ff = w_gate.shape[2]
    wide = (d_ff // tn_wide) * tn_wide
    parts = []
    if wide:
        parts.append((_moe_up(xb, block_e, n_active, w_gate, w_up, 0, wide, tn_wide), 0))
    if d_ff > wide:
        tn = min(tn_narrow, d_ff - wide)
        assert (d_ff - wide) % tn == 0 and wide % (d_ff - wide) == 0, (d_ff, tn_wide, tn_narrow)
        parts.append((_moe_up(xb, block_e, n_active, w_gate, w_up, wide, d_ff - wide, tn), wide))
    tn_down = min(tn_down, d)
    x_specs = [pl.BlockSpec((bm, a.shape[1]), lambda j, b, be, na: (b, 0)) for a, _ in parts]
    w_specs = [pl.BlockSpec((None, a.shape[1], tn_down),
                            lambda j, b, be, na, blk=c0 // a.shape[1]: (be[b], blk, j)) for a, c0 in parts]
    return pl.pallas_call(
        functools.partial(_moe_down_body, n_parts=len(parts)),
        grid_spec=pltpu.PrefetchScalarGridSpec(
            num_scalar_prefetch=2, grid=(d // tn_down, rows // bm),
            in_specs=x_specs + w_specs,
            out_specs=pl.BlockSpec((bm, tn_down), lambda j, b, be, na: (b, j)),
            scratch_shapes=[pltpu.VMEM((a.shape[1], tn_down), BF16) for a, _ in parts]),
        out_shape=jax.ShapeDtypeStruct((rows, d), F32),
        compiler_params=_cparams(2), name="moe_down",
    )(block_e, n_active, *[a for a, _ in parts], *([w_down] * len(parts)))


def _row_copy(src_hbm, src_row, dst, dst_row, sem):
    return pltpu.make_async_copy(src_hbm.at[pl.ds(src_row, 1), :], dst.at[pl.ds(dst_row, 1), :], sem)


def _moe_gather_body(tok_ref, h_hbm, o_ref, rows, sem, *, bm):
    def start(r, carry):
        _row_copy(h_hbm, tok_ref[0, r], rows, r, sem).start()
        return carry

    def wait(r, carry):
        _row_copy(h_hbm, 0, rows, r, sem).wait()
        return carry

    lax.fori_loop(0, bm, start, 0, unroll=8)
    lax.fori_loop(0, bm, wait, 0, unroll=8)
    o_ref[...] = rows[...].astype(o_ref.dtype)


def _moe_gather(h, slot_tok):
    _, d = h.shape
    bm = MOE_BLOCK
    n_blk = slot_tok.shape[0] // bm
    return pl.pallas_call(
        functools.partial(_moe_gather_body, bm=bm), grid=(n_blk,),
        in_specs=[pl.BlockSpec((None, 1, bm), lambda b: (b, 0, 0), memory_space=pltpu.SMEM),
                  pl.BlockSpec(memory_space=pl.ANY)],
        out_specs=pl.BlockSpec((bm, d), lambda b: (b, 0)),
        out_shape=jax.ShapeDtypeStruct((n_blk * bm, d), BF16),
        scratch_shapes=[pltpu.VMEM((bm, d), h.dtype), pltpu.SemaphoreType.DMA(())],
        compiler_params=_cparams(1), name="moe_gather")(slot_tok.reshape(n_blk, 1, bm), h)


def _moe_combine_body(slot_ref, x_ref, g_ref, gt_ref, fw_ref, yb_hbm, op_ref, os_ref, ybuf, sem, *, tm, pt):
    def start(r, carry):
        for k in range(TOP_K):
            _row_copy(yb_hbm, slot_ref[0, TOP_K * r + k], ybuf.at[k], r, sem).start()
        return carry

    def wait(r, carry):
        for k in range(TOP_K):
            _row_copy(yb_hbm, 0, ybuf.at[k], r, sem).wait()
        return carry

    lax.fori_loop(0, tm, start, 0, unroll=4)
    lax.fori_loop(0, tm, wait, 0, unroll=4)
    g = g_ref[...]
    y = ybuf[0] * g[:, 0:1] + ybuf[1] * g[:, 1:2]
    x = x_ref[...] + gt_ref[...] * y
    out = x * lax.rsqrt(jnp.mean(x * x, axis=-1, keepdims=True) + NORM_EPS) * fw_ref[...]

    def store(o_ref):
        o_ref[...] = out

    _for_trunk_of_tile(pt, op_ref, os_ref, store)


def _moe_combine_final(x, yb, slot_of, gates, mod5, layer, final_w, lay, tm=256):
    n, d = x.shape
    tm = min(tm, lay.s_len, lay.p_rows)
    pt = lay.p_rows // tm
    row = pl.BlockSpec((tm, d), lambda i: (i, 0))
    return pl.pallas_call(
        functools.partial(_moe_combine_body, tm=tm, pt=pt), grid=(n // tm,),
        in_specs=[pl.BlockSpec((None, 1, TOP_K * tm), lambda i: (i, 0, 0), memory_space=pltpu.SMEM),
                  row, pl.BlockSpec((tm, TOP_K), lambda i: (i, 0)),
                  pl.BlockSpec((None, None, None, 1, d), _mod_spec(layer, 5, tm, lay)),
                  pl.BlockSpec((1, d), lambda i: (0, 0)),
                  pl.BlockSpec(memory_space=pl.ANY)],
        out_specs=[pl.BlockSpec((tm, d), lambda i: (jnp.minimum(i, pt - 1), 0)),
                   pl.BlockSpec((tm, d), lambda i: (jnp.maximum(i - pt, 0), 0))],
        out_shape=[jax.ShapeDtypeStruct((lay.p_rows, d), F32),
                   jax.ShapeDtypeStruct((n - lay.p_rows, d), F32)],
        scratch_shapes=[pltpu.VMEM((TOP_K, tm, d), F32), pltpu.SemaphoreType.DMA(())],
        compiler_params=_cparams(1), name="moe_combine_final",
    )(slot_of.reshape(n // tm, 1, TOP_K * tm), x, gates, mod5, final_w.reshape(1, d), yb)


def _moe(h, logits, w_gate, w_up, w_down):
    n_tok, d = h.shape
    bm = MOE_BLOCK
    top_val, top_idx = lax.top_k(logits, TOP_K)
    gates = jax.nn.softmax(top_val, axis=-1)
    n_assign = n_tok * TOP_K
    flat_e = top_idx.reshape(-1).astype(jnp.int32)
    order = jnp.argsort(flat_e).astype(jnp.int32)
    rank = jnp.argsort(order).astype(jnp.int32)
    counts = jnp.sum(flat_e[:, None] == jnp.arange(N_EXPERTS, dtype=jnp.int32)[None, :], axis=0,
                     dtype=jnp.int32)
    padded = ((counts + bm - 1) // bm) * bm
    pad_end = jnp.cumsum(padded)
    pad_start = pad_end - padded
    start = jnp.cumsum(counts) - counts
    n_blocks = -(-n_assign // bm) + N_EXPERTS
    block_e = jnp.minimum(jnp.searchsorted(pad_end, jnp.arange(n_blocks) * bm, side='right'),
                          N_EXPERTS - 1).astype(jnp.int32)
    n_active = (pad_end[-1] // bm).astype(jnp.int32).reshape(1)
    experts = jnp.arange(N_EXPERTS, dtype=jnp.int32)

    def lookup(table, e):
        return jnp.sum(jnp.where(e[..., None] == experts, table, 0), axis=-1, dtype=jnp.int32)

    slot = jnp.arange(n_blocks * bm, dtype=jnp.int32)
    slot_e = jnp.repeat(block_e, bm)
    pos = slot - lookup(pad_start, slot_e)
    src = jnp.clip(lookup(start, slot_e) + pos, 0, n_assign - 1)
    slot_tok = jnp.where(pos < lookup(counts, slot_e),
                         jnp.take(order, src, indices_are_sorted=True) // TOP_K, slot % n_tok)
    yb = _moe_experts(_moe_gather(h, slot_tok), block_e, n_active, w_gate, w_up, w_down)
    slot_of = (lookup(pad_start - start, flat_e) + rank).reshape(n_tok, TOP_K)
    return yb, slot_of, gates


def _rope_tables(seq_len, dim):
    t = jnp.arange(seq_len)
    row = (t // GRID_W).astype(F32)
    col = (t % GRID_W).astype(F32)
    half = dim // 2
    inv = ROPE_BASE ** (-(jnp.arange(0, half, 2, dtype=F32) / half))
    ang = jnp.concatenate([row[:, None] * inv, col[:, None] * inv], axis=-1)
    cos, sin = jnp.cos(ang), jnp.sin(ang)
    return jnp.repeat(cos, 2, axis=-1), jnp.stack([-sin, sin], axis=-1).reshape(seq_len, dim)


def _block_diag_states(s):
    b, two, n_heads, hd, _ = s.shape
    hps = RWKV_SLAB // hd
    s6 = s.reshape(b, two, n_heads // hps, hps, hd, hd)
    bd = jnp.einsum('bdghvk,hi->bdghvik', s6, jnp.eye(hps, dtype=s.dtype))
    return bd.reshape(b, two, n_heads // hps, RWKV_SLAB, RWKV_SLAB)


def _pad_cols(w, width):
    return jnp.pad(w, ((0, 0), (0, width - w.shape[1])))


def kernel(x_prompt, x_sample, state_l0_ret, state_l1_rwkv, c, c_ctx, ada_w, ada_b, norm_w, final_norm_w,
           l0_ret_w_in, l0_ret_w_out, l0_ret_decay, l0_ffn_w_gate, l0_ffn_w_up, l0_ffn_w_down,
           l1_rwkv_mu, l1_rwkv_w_rkv, l1_rwkv_w0, l1_rwkv_w1, l1_rwkv_w2, l1_rwkv_a0, l1_rwkv_a1, l1_rwkv_a2,
           l1_rwkv_g1, l1_rwkv_g2, l1_rwkv_k_k, l1_rwkv_k_a, l1_rwkv_r_k, l1_rwkv_ln_w, l1_rwkv_ln_b,
           l1_rwkv_w_out, l1_moe_router, l1_moe_w_gate, l1_moe_w_up, l1_moe_w_down):
    pb, p_len, d = x_prompt.shape
    sb, s_len, _ = x_sample.shape
    p_rows, s_rows = pb * p_len, sb * s_len
    lay = _Layout(p_rows, p_len, s_len, p_rows + s_rows)
    n_layers = ada_w.shape[0]

    x = jnp.concatenate([x_prompt.reshape(p_rows, d), x_sample.reshape(s_rows, d)], axis=0)
    cond8 = jnp.concatenate([c_ctx[None, :], c, jnp.zeros((8 - 1 - sb, d), F32)], axis=0)
    mod5 = _ada_mod(cond8, ada_w, ada_b)[:, :1 + sb].reshape(n_layers, 1 + sb, 6, 1, d)

    h = _norm_mod(x, norm_w[0, 0], mod5, 0, 1, 0, lay)
    qkvg = _linear(h, l0_ret_w_in, tn=1024)
    log_gamma = -jnp.exp(l0_ret_decay.astype(F32))
    dh = d // RET_HEADS
    o_p, new_state_l0_ret = _retention(qkvg, log_gamma, None, None, row0=0, n_seq=pb, seq_len=p_len,
                                       emit_state=True)
    (o_s,) = _retention(qkvg, log_gamma, state_l0_ret, _rope_tables(s_len, dh), row0=p_rows, n_seq=sb,
                        seq_len=s_len, emit_state=False)
    x = _linear_residual(_ret_finalize(o_p, o_s, qkvg), l0_ret_w_out, x, mod5, 0, 2, lay)
    h = _norm_mod(x, norm_w[0, 1], mod5, 0, 4, 3, lay)
    act = _swiglu_up(h, l0_ffn_w_gate, l0_ffn_w_up)
    x = _linear_residual(act, l0_ffn_w_down, x, mod5, 0, 5, lay, tm=512, tn=512)

    xs = _rwkv_mix(x, norm_w[1, 0], mod5, 1, l1_rwkv_mu, lay)
    rkv = _linear(xs, l1_rwkv_w_rkv, n_batch=3, tn=1024)
    lora_w = l1_rwkv_w1.shape[2]
    pad_w = -(-lora_w // 128) * 128
    w1p = jnp.concatenate([_pad_cols(l1_rwkv_w1[0], pad_w), _pad_cols(l1_rwkv_w1[1], pad_w)], axis=1)
    a1p = jnp.concatenate([_pad_cols(l1_rwkv_a1[0], pad_w), _pad_cols(l1_rwkv_a1[1], pad_w)], axis=1)
    t_w = _linear(xs, w1p, x_lead=3, act="tanh", out_dtype=BF16)
    t_a = _linear(xs, a1p, x_lead=4, out_dtype=BF16)
    t_g = _linear(xs, l1_rwkv_g1, x_lead=5, act="sigmoid", out_dtype=BF16)
    w2p = jnp.pad(l1_rwkv_w2, ((0, 0), (0, pad_w - lora_w), (0, 0)))
    a2p = jnp.pad(l1_rwkv_a2, ((0, 0), (0, pad_w - lora_w), (0, 0)))
    lw = _linear(t_w, w2p, n_batch=2, bias=l1_rwkv_w0, act="log_decay")
    a_lr = _linear(t_a, a2p, n_batch=2, bias=l1_rwkv_a0, act="sigmoid")
    g = _linear(t_g, l1_rwkv_g2)
    scan_args = (rkv, lw, a_lr, l1_rwkv_k_k, l1_rwkv_k_a)
    y_p, new_state_l1_rwkv = _rwkv_scan(*scan_args, None, row0=0, n_seq=pb, seq_len=p_len, emit_state=True)
    (y_s,) = _rwkv_scan(*scan_args, _block_diag_states(state_l1_rwkv), row0=p_rows, n_seq=sb, seq_len=s_len,
                        emit_state=False)
    yg = _rwkv_finalize(y_p, y_s, rkv, a_lr, g, l1_rwkv_k_a, l1_rwkv_r_k, l1_rwkv_ln_w, l1_rwkv_ln_b)
    x = _linear_residual(yg, l1_rwkv_w_out, x, mod5, 1, 2, lay)
    ne_pad = 128
    h, logits = _norm_mod(x, norm_w[1, 1], mod5, 1, 4, 3, lay, router_w=_pad_cols(l1_moe_router, ne_pad))
    yb, slot_of, gates = _moe(h, logits[:, :N_EXPERTS], l1_moe_w_gate, l1_moe_w_up, l1_moe_w_down)
    y_prompt, y_sample = _moe_combine_final(x, yb, slot_of, gates, mod5, 1, final_norm_w, lay)
    return (y_prompt.reshape(pb, p_len, d), y_sample.reshape(sb, s_len, d), new_state_l0_ret, new_state_l1_rwkv)
```

```python
import functools
from typing import NamedTuple

import jax
import jax.numpy as jnp
from jax import lax
from jax.experimental import pallas as pl
from jax.experimental.pallas import tpu as pltpu

F32 = jnp.float32
BF16 = jnp.bfloat16

GRID_W = 64
RET_HEADS = 8
RET_CHUNK = 128
RET_HEADS_PER_STEP = 4
ROPE_BASE = 10000.0
RWKV_HEAD_DIM = 64
RWKV_CHUNK = 64
RWKV_SLAB = 256
RWKV_CHUNKS_PER_ITER = 2
RWKV_LN_EPS = 64e-5
N_EXPERTS = 8
TOP_K = 2
MOE_BLOCK = 512
NORM_EPS = 1e-6

VMEM_LIMIT_BYTES = 60 * 1024 * 1024


class _Layout(NamedTuple):
    p_rows: int
    p_len: int
    s_len: int
    n_rows: int


def _cparams(n_axes):
    return pltpu.CompilerParams(dimension_semantics=("arbitrary",) * n_axes,
                                vmem_limit_bytes=VMEM_LIMIT_BYTES)


def _cond_of_tile(i, tm, lay):
    pt = lay.p_rows // tm
    st = lay.s_len // tm
    return jnp.where(i < pt, 0, 1 + (i - pt) // st)


def _bdot(a, b):
    return jnp.dot(a.astype(BF16), b.astype(BF16), preferred_element_type=F32)


def _bdot_nt(a, b):
    return lax.dot_general(a.astype(BF16), b.astype(BF16), (((1,), (1,)), ((), ())),
                           preferred_element_type=F32)


def _bdot_tn(a, b):
    return lax.dot_general(a.astype(BF16), b.astype(BF16), (((0,), (0,)), ((), ())),
                           preferred_element_type=F32)


def _silu(x):
    return x * jax.nn.sigmoid(x)


def _ada_body(c_ref, w_ref, b_ref, o_ref):
    o_ref[...] = _bdot(_silu(c_ref[...]), w_ref[...]) + b_ref[...]


def _ada_mod(cond8, ada_w, ada_b):
    n_layers, d, d6 = ada_w.shape
    tn = min(1024, d6)
    return pl.pallas_call(
        _ada_body,
        grid=(n_layers, d6 // tn),
        in_specs=[pl.BlockSpec((8, d), lambda l, j: (0, 0)),
                  pl.BlockSpec((None, d, tn), lambda l, j: (l, 0, j)),
                  pl.BlockSpec((None, 1, tn), lambda l, j: (l, 0, j))],
        out_specs=pl.BlockSpec((None, 8, tn), lambda l, j: (l, 0, j)),
        out_shape=jax.ShapeDtypeStruct((n_layers, 8, d6), F32),
        compiler_params=_cparams(2), name="ada_mod",
    )(cond8, ada_w, ada_b.reshape(n_layers, 1, d6))


def _mod_spec(layer, which, tm, lay, n_grid_axes=1):
    def imap(i, *_):
        return (layer, _cond_of_tile(i, tm, lay), which, 0, 0)
    return imap


def _rms_mod(x, nw, sc, sh):
    xn = x * lax.rsqrt(jnp.mean(x * x, axis=-1, keepdims=True) + NORM_EPS) * nw
    return xn * (1.0 + sc) + sh


def _norm_mod_body(x_ref, nw_ref, sc_ref, sh_ref, o_ref):
    o_ref[...] = _rms_mod(x_ref[...], nw_ref[...], sc_ref[...], sh_ref[...]).astype(o_ref.dtype)


def _norm_mod2_body(xp_ref, xs_ref, nw_ref, sc_ref, sh_ref, o_ref, *, pt):
    def norm(x_ref):
        o_ref[...] = _rms_mod(x_ref[...], nw_ref[...], sc_ref[...], sh_ref[...]).astype(o_ref.dtype)

    _for_trunk_of_tile(pt, xp_ref, xs_ref, norm)


def _norm_mod_router_body(x_ref, nw_ref, sc_ref, sh_ref, wr_ref, o_ref, lg_ref):
    h = _rms_mod(x_ref[...], nw_ref[...], sc_ref[...], sh_ref[...])
    o_ref[...] = h.astype(o_ref.dtype)
    lg_ref[...] = jnp.dot(h, wr_ref[...], preferred_element_type=F32, precision=lax.Precision.HIGHEST)


def _trunk_pair_specs(tm, tn, pt, n_grid_axes):
    if n_grid_axes == 1:
        return [pl.BlockSpec((tm, tn), lambda i: (jnp.minimum(i, pt - 1), 0)),
                pl.BlockSpec((tm, tn), lambda i: (jnp.maximum(i - pt, 0), 0))]
    return [pl.BlockSpec((tm, tn), lambda j, i: (jnp.minimum(i, pt - 1), j)),
            pl.BlockSpec((tm, tn), lambda j, i: (jnp.maximum(i - pt, 0), j))]


def _norm_mod(x, nw, mod5, layer, which_sc, which_sh, lay, router_w=None, tm=512):
    pair = isinstance(x, tuple)
    n, d = lay.n_rows, nw.shape[0]
    tm = min(tm, lay.s_len, lay.p_rows)
    mod_specs = [pl.BlockSpec((1, d), lambda i: (0, 0)),
                 pl.BlockSpec((None, None, None, 1, d), _mod_spec(layer, which_sc, tm, lay)),
                 pl.BlockSpec((None, None, None, 1, d), _mod_spec(layer, which_sh, tm, lay))]
    if pair:
        pt = lay.p_rows // tm
        return pl.pallas_call(
            functools.partial(_norm_mod2_body, pt=pt), grid=(n // tm,),
            in_specs=_trunk_pair_specs(tm, d, pt, 1) + mod_specs,
            out_specs=pl.BlockSpec((tm, d), lambda i: (i, 0)),
            out_shape=jax.ShapeDtypeStruct((n, d), BF16),
            compiler_params=_cparams(1), name="norm_mod")(*x, nw.reshape(1, d), mod5, mod5)
    in_specs = [pl.BlockSpec((tm, d), lambda i: (i, 0))] + mod_specs
    args = [x, nw.reshape(1, d), mod5, mod5]
    if router_w is None:
        return pl.pallas_call(
            _norm_mod_body, grid=(n // tm,), in_specs=in_specs,
            out_specs=pl.BlockSpec((tm, d), lambda i: (i, 0)),
            out_shape=jax.ShapeDtypeStruct((n, d), BF16),
            compiler_params=_cparams(1), name="norm_mod")(*args)
    ne = router_w.shape[1]
    return pl.pallas_call(
        _norm_mod_router_body, grid=(n // tm,),
        in_specs=in_specs + [pl.BlockSpec((d, ne), lambda i: (0, 0))],
        out_specs=[pl.BlockSpec((tm, d), lambda i: (i, 0)), pl.BlockSpec((tm, ne), lambda i: (i, 0))],
        out_shape=[jax.ShapeDtypeStruct((n, d), F32), jax.ShapeDtypeStruct((n, ne), F32)],
        compiler_params=_cparams(1), name="norm_mod_router")(*args, router_w)


def _cast_weights_once(row_axis, pairs):
    @pl.when(pl.program_id(row_axis) == 0)
    def _():
        for w_ref, w_bf in pairs:
            w_bf[...] = w_ref[...].astype(BF16)


def _softplus(x):
    return jnp.maximum(x, 0.0) + jnp.log1p(jnp.exp(-jnp.abs(x)))


def _linear_body(x_ref, w_ref, *refs, act, has_bias):
    refs = list(refs)
    bias_ref = refs.pop(0) if has_bias else None
    o_ref, w_bf = refs
    _cast_weights_once(2, [(w_ref, w_bf)])
    acc = jnp.dot(x_ref[...].astype(BF16), w_bf[...], preferred_element_type=F32)
    if has_bias:
        acc = acc + bias_ref[...]
    if act == "tanh":
        acc = jnp.tanh(acc)
    elif act == "sigmoid":
        acc = jax.nn.sigmoid(acc)
    elif act == "log_decay":
        acc = -jnp.exp(-_softplus(-acc) - 0.5)
    o_ref[...] = acc.astype(o_ref.dtype)


def _linear(x, w, *, n_batch=None, x_lead=0, bias=None, act=None, out_dtype=F32, tm=1024, tn=512):
    k, n = w.shape[-2:]
    m = x.shape[-2]
    tm, tn = min(tm, m), min(tn, n)
    if x.ndim == 3:
        x_spec = pl.BlockSpec((None, tm, k), lambda b, j, i: (x_lead + b, i, 0))
    else:
        x_spec = pl.BlockSpec((tm, k), lambda b, j, i: (i, b))
    if w.ndim == 3:
        w_spec = pl.BlockSpec((None, k, tn), lambda b, j, i: (b, 0, j))
    else:
        w_spec = pl.BlockSpec((k, tn), lambda b, j, i: (0, j))
    if n_batch is None:
        out_spec = pl.BlockSpec((tm, tn), lambda b, j, i: (i, j))
        out_shape = jax.ShapeDtypeStruct((m, n), out_dtype)
    else:
        out_spec = pl.BlockSpec((None, tm, tn), lambda b, j, i: (b, i, j))
        out_shape = jax.ShapeDtypeStruct((n_batch, m, n), out_dtype)
    in_specs, args = [x_spec, w_spec], [x, w]
    if bias is not None:
        in_specs.append(pl.BlockSpec((None, 1, tn), lambda b, j, i: (b, 0, j)))
        args.append(bias.reshape(bias.shape[0], 1, n))
    return pl.pallas_call(
        functools.partial(_linear_body, act=act, has_bias=bias is not None),
        grid=(n_batch or 1, n // tn, m // tm), in_specs=in_specs,
        out_specs=out_spec, out_shape=out_shape,
        scratch_shapes=[pltpu.VMEM((k, tn), BF16)],
        compiler_params=_cparams(3), name="linear")(*args)


def _linear_res_body(x_ref, w_ref, *refs, pt):
    res_refs, (gt_ref, o_ref, w_bf) = refs[:-3], refs[-3:]
    _cast_weights_once(1, [(w_ref, w_bf)])
    y = gt_ref[...] * jnp.dot(x_ref[...], w_bf[...], preferred_element_type=F32)

    def add(res_ref):
        o_ref[...] = res_ref[...] + y

    if len(res_refs) == 1:
        add(res_refs[0])
    else:
        _for_trunk_of_tile(pt, *res_refs, add, row_axis=1)


def _linear_residual(x, w, res, mod5, layer, which_gate, lay, tm=1024, tn=1024):
    m, k = x.shape
    n = w.shape[1]
    tm, tn = min(tm, lay.s_len, lay.p_rows), min(tn, n)
    pt = lay.p_rows // tm

    def gmap(j, i):
        return (layer, _cond_of_tile(i, tm, lay), which_gate, 0, j)

    if isinstance(res, tuple):
        res_specs, res_args = _trunk_pair_specs(tm, tn, pt, 2), list(res)
    else:
        res_specs, res_args = [pl.BlockSpec((tm, tn), lambda j, i: (i, j))], [res]
    return pl.pallas_call(
        functools.partial(_linear_res_body, pt=pt), grid=(n // tn, m // tm),
        in_specs=[pl.BlockSpec((tm, k), lambda j, i: (i, 0)),
                  pl.BlockSpec((k, tn), lambda j, i: (0, j))] + res_specs
                 + [pl.BlockSpec((None, None, None, 1, tn), gmap)],
        out_specs=pl.BlockSpec((tm, tn), lambda j, i: (i, j)),
        out_shape=jax.ShapeDtypeStruct((m, n), F32),
        scratch_shapes=[pltpu.VMEM((k, tn), BF16)],
        compiler_params=_cparams(2), name="linear_residual")(x, w, *res_args, mod5)


def _swiglu_up_body(x_ref, wg_ref, wu_ref, o_ref, wg_bf, wu_bf):
    _cast_weights_once(1, [(wg_ref, wg_bf), (wu_ref, wu_bf)])
    x = x_ref[...]
    gate = jnp.dot(x, wg_bf[...], preferred_element_type=F32)
    up = jnp.dot(x, wu_bf[...], preferred_element_type=F32)
    o_ref[...] = (_silu(gate) * up).astype(o_ref.dtype)


def _swiglu_up(x, wg, wu, tm=1024, tn=512):
    m, k = x.shape
    n = wg.shape[1]
    tm, tn = min(tm, m), min(tn, n)
    return pl.pallas_call(
        _swiglu_up_body, grid=(n // tn, m // tm),
        in_specs=[pl.BlockSpec((tm, k), lambda j, i: (i, 0)),
                  pl.BlockSpec((k, tn), lambda j, i: (0, j)),
                  pl.BlockSpec((k, tn), lambda j, i: (0, j))],
        out_specs=pl.BlockSpec((tm, tn), lambda j, i: (i, j)),
        out_shape=jax.ShapeDtypeStruct((m, n), BF16),
        scratch_shapes=[pltpu.VMEM((k, tn), BF16)] * 2,
        compiler_params=_cparams(2), name="swiglu_up")(x, wg, wu)


def _rope(x, c, s):
    w = x.shape[-1]
    lane = lax.broadcasted_iota(jnp.int32, x.shape, x.ndim - 1)
    nxt = pltpu.roll(x, w - 1, axis=x.ndim - 1)
    prv = pltpu.roll(x, 1, axis=x.ndim - 1)
    return x * c + jnp.where(lane % 2 == 0, nxt, prv) * s


def _retention_body(lg_ref, *refs, n_chunk, n_blk, has_rope, has_s0, emit_state, scale):
    refs = list(refs)
    q_ref, k_ref, v_ref = refs[:3]
    refs = refs[3:]
    if has_rope:
        cos_ref, sin_ref = refs[:2]
        refs = refs[2:]
    if has_s0:
        s0_ref = refs.pop(0)
    o_ref = refs.pop(0)
    if emit_state:
        so_ref = refs.pop(0)
    st = refs.pop(0)
    c_sz = RET_CHUNK
    hp, dh = st.shape[0], st.shape[1]
    hg, d, cb = pl.program_id(1), pl.program_id(2), pl.program_id(3)
    heads = range(hp)
    lanes = [slice(h * dh, (h + 1) * dh) for h in heads]

    @pl.when(cb == 0)
    def _():
        st[...] = s0_ref[...] if has_s0 else jnp.zeros_like(st)

    lgv = [lg_ref[d, hg * hp + h] for h in heads]
    fwd = d == 0
    row = lax.broadcasted_iota(jnp.int32, (c_sz, c_sz), 0)
    col = lax.broadcasted_iota(jnp.int32, (c_sz, c_sz), 1)
    diff = jnp.where(fwd, row - col, col - row).astype(F32)
    intra = [jnp.where(diff >= 0, jnp.exp(jnp.maximum(diff, 0.0) * lg), 0.0) for lg in lgv]
    pos = lax.broadcasted_iota(jnp.int32, (c_sz, dh), 0)
    npos = jnp.where(fwd, pos, c_sz - 1 - pos).astype(F32)
    q_decay = [jnp.exp((npos + 1.0) * lg) for lg in lgv]
    k_decay = [jnp.exp((c_sz - 1.0 - npos) * lg) for lg in lgv]
    chunk_decay = [jnp.exp(jnp.full((1, dh), c_sz, F32) * lg) for lg in lgv]

    for j in range(n_chunk):
        jj = jnp.where(fwd, j, n_chunk - 1 - j)
        rows = pl.ds(pl.multiple_of(jj * c_sz, c_sz), c_sz)
        q = [q_ref[rows, sl] for sl in lanes]
        k = [k_ref[rows, sl] * scale for sl in lanes]
        v = [v_ref[rows, sl] for sl in lanes]
        if has_rope:
            c, s = cos_ref[rows, :], sin_ref[rows, :]
            q = [_rope(x, c, s) for x in q]
            k = [_rope(x, c, s) for x in k]
        state = [st[h] for h in heads]
        scores = [_bdot_nt(q[h], k[h]) * intra[h] for h in heads]
        o = [_bdot(scores[h], v[h]) + _bdot(q[h], state[h]) * q_decay[h] for h in heads]
        new_state = [state[h] * chunk_decay[h] + _bdot_tn(k[h] * k_decay[h], v[h]) for h in heads]
        o_ref[rows, :] = jnp.concatenate(o, axis=1)
        st[...] = jnp.stack(new_state)

    if emit_state:
        @pl.when(cb == n_blk - 1)
        def _():
            so_ref[...] = st[...]


def _retention(qkvg, log_gamma, s0, rope, *, row0, n_seq, seq_len, emit_state):
    d_model = qkvg.shape[1] // 4
    n_heads = RET_HEADS
    dh = d_model // n_heads
    tb = min(512, seq_len)
    n_blk = seq_len // tb
    rb0 = row0 // tb

    def blk(c, d):
        return jnp.where(d == 0, c, n_blk - 1 - c)

    hp = RET_HEADS_PER_STEP
    n_hg = n_heads // hp

    def in_map(part):
        return lambda s, h, d, c, lg: (rb0 + s * n_blk + blk(c, d), part * n_hg + h)

    in_specs = [pl.BlockSpec((tb, hp * dh), in_map(p)) for p in range(3)]
    args = [qkvg, qkvg, qkvg]
    if rope is not None:
        in_specs += [pl.BlockSpec((tb, dh), lambda s, h, d, c, lg: (blk(c, d), 0))] * 2
        args += list(rope)
    if s0 is not None:
        in_specs.append(pl.BlockSpec((None, None, hp, dh, dh), lambda s, h, d, c, lg: (s, d, h, 0, 0)))
        args.append(s0)
    out_specs = [pl.BlockSpec((None, tb, hp * dh), lambda s, h, d, c, lg: (d, s * n_blk + blk(c, d), h))]
    out_shape = [jax.ShapeDtypeStruct((2, n_seq * seq_len, d_model), F32)]
    if emit_state:
        out_specs.append(pl.BlockSpec((None, None, hp, dh, dh), lambda s, h, d, c, lg: (s, d, h, 0, 0)))
        out_shape.append(jax.ShapeDtypeStruct((n_seq, 2, n_heads, dh, dh), F32))
    body = functools.partial(_retention_body, n_chunk=tb // RET_CHUNK, n_blk=n_blk, has_rope=rope is not None,
                             has_s0=s0 is not None, emit_state=emit_state, scale=dh ** -0.5)
    return pl.pallas_call(
        body,
        grid_spec=pltpu.PrefetchScalarGridSpec(
            num_scalar_prefetch=1, grid=(n_seq, n_hg, 2, n_blk), in_specs=in_specs, out_specs=out_specs,
            scratch_shapes=[pltpu.VMEM((hp, dh, dh), F32)]),
        out_shape=out_shape, compiler_params=_cparams(4), name="retention")(log_gamma, *args)


def _two_trunk_specs(a_p, a_s, tm):
    pt = a_p.shape[1] // tm
    d = a_p.shape[2]
    return [pl.BlockSpec((2, tm, d), lambda i: (0, jnp.minimum(i, pt - 1), 0)),
            pl.BlockSpec((2, tm, d), lambda i: (0, jnp.maximum(i - pt, 0), 0))], pt


def _for_trunk_of_tile(pt, p_ref, s_ref, fn, row_axis=0):
    i = pl.program_id(row_axis)

    @pl.when(i < pt)
    def _():
        fn(p_ref)

    @pl.when(i >= pt)
    def _():
        fn(s_ref)


def _ret_finalize_body(op_ref, os_ref, g_ref, out_ref, *, n_heads, pt):
    def finalize(o_ref):
        o = o_ref[0] + o_ref[1]
        g = g_ref[...]
        dh = o.shape[1] // n_heads
        for h in range(n_heads):
            sl = slice(h * dh, (h + 1) * dh)
            oh = o[:, sl]
            oh = oh * lax.rsqrt(jnp.mean(oh * oh, axis=-1, keepdims=True) + NORM_EPS)
            out_ref[:, sl] = (oh * _silu(g[:, sl])).astype(out_ref.dtype)

    _for_trunk_of_tile(pt, op_ref, os_ref, finalize)


def _ret_finalize(o_p, o_s, qkvg, tm=256):
    n, d = o_p.shape[1] + o_s.shape[1], o_p.shape[2]
    specs, pt = _two_trunk_specs(o_p, o_s, tm)
    return pl.pallas_call(
        functools.partial(_ret_finalize_body, n_heads=RET_HEADS, pt=pt), grid=(n // tm,),
        in_specs=specs + [pl.BlockSpec((tm, d), lambda i: (i, 3))],
        out_specs=pl.BlockSpec((tm, d), lambda i: (i, 0)),
        out_shape=jax.ShapeDtypeStruct((n, d), BF16),
        compiler_params=_cparams(1), name="ret_finalize")(o_p, o_s, qkvg)


def _rwkv_mix_body(x_ref, xp_ref, xn_ref, nw_ref, sc_ref, sh_ref, mu_ref, o_ref, hext, *, tm, halo, lay):
    i = pl.program_id(0)
    d = x_ref.shape[1]
    nw, sc, sh = nw_ref[...], sc_ref[...], sh_ref[...]
    hext[0:halo, :] = _rms_mod(xp_ref[...], nw, sc, sh)
    hext[halo:halo + tm, :] = _rms_mod(x_ref[...], nw, sc, sh)
    hext[halo + tm:halo + tm + halo, :] = _rms_mod(xn_ref[...], nw, sc, sh)
    h = hext[halo:halo + tm, :]
    g_row = i * tm + lax.broadcasted_iota(jnp.int32, (tm, 1), 0)

    def emit(h_shift):
        diff = h_shift - h
        for n in range(6):
            o_ref[n] = (h + diff * mu_ref[n:n + 1, :]).astype(o_ref.dtype)

    def shifted(off, lo, hi, keep):
        return jnp.where(keep, hext[halo + off:halo + off + tm, lo:hi], 0.0)

    @pl.when(i < lay.p_rows // tm)
    def _():
        t = g_row % lay.p_len
        hd = d // 2
        emit(jnp.concatenate([shifted(-1, 0, hd, t != 0),
                              shifted(1, hd, d, t != lay.p_len - 1)], axis=1))

    @pl.when(i >= lay.p_rows // tm)
    def _():
        t = (g_row - lay.p_rows) % lay.s_len
        colw = t % GRID_W
        qd = d // 4
        emit(jnp.concatenate([shifted(-1, 0, qd, colw != 0),
                              shifted(1, qd, 2 * qd, colw != GRID_W - 1),
                              shifted(-GRID_W, 2 * qd, 3 * qd, t >= GRID_W),
                              shifted(GRID_W, 3 * qd, d, t < lay.s_len - GRID_W)], axis=1))


def _rwkv_mix(x, nw, mod5, layer, mu, lay, tm=512):
    n, d = x.shape
    halo = GRID_W
    tm = min(tm, lay.s_len, lay.p_rows)
    r = tm // halo
    n_halo_blk = n // halo
    body = functools.partial(_rwkv_mix_body, tm=tm, halo=halo, lay=lay)
    return pl.pallas_call(
        body, grid=(n // tm,),
        in_specs=[pl.BlockSpec((tm, d), lambda i: (i, 0)),
                  pl.BlockSpec((halo, d), lambda i: (jnp.maximum(i * r - 1, 0), 0)),
                  pl.BlockSpec((halo, d), lambda i: (jnp.minimum((i + 1) * r, n_halo_blk - 1), 0)),
                  pl.BlockSpec((1, d), lambda i: (0, 0)),
                  pl.BlockSpec((None, None, None, 1, d), _mod_spec(layer, 1, tm, lay)),
                  pl.BlockSpec((None, None, None, 1, d), _mod_spec(layer, 0, tm, lay)),
                  pl.BlockSpec((6, d), lambda i: (0, 0))],
        out_specs=pl.BlockSpec((6, tm, d), lambda i: (0, i, 0)),
        out_shape=jax.ShapeDtypeStruct((6, n, d), BF16),
        scratch_shapes=[pltpu.VMEM((tm + 2 * halo, d), F32)],
        compiler_params=_cparams(1), name="rwkv_mix")(x, x, x, nw.reshape(1, d), mod5, mod5, mu)


def _rwkv_scan_body(*refs, n_chunk, n_blk, has_s0, emit_state):
    refs = list(refs)
    r_ref, k_ref, v_ref, lw_ref, a_ref, kk_ref, ka_ref = refs[:7]
    refs = refs[7:]
    if has_s0:
        s0_ref = refs.pop(0)
    y_ref = refs.pop(0)
    if emit_state:
        so_ref = refs.pop(0)
    st = refs.pop(0)
    c_sz, hd, sw = RWKV_CHUNK, RWKV_HEAD_DIM, RWKV_SLAB
    n_grp = st.shape[0]
    hps = sw // hd
    d, cb = pl.program_id(1), pl.program_id(3)
    fwd = d == 0

    ri = lax.broadcasted_iota(jnp.int32, (sw, sw), 0)
    ci = lax.broadcasted_iota(jnp.int32, (sw, sw), 1)
    bd_mask = (ri // hd) == (ci // hd)

    def block_diag(slab):
        return jnp.where(bd_mask, jnp.concatenate([slab] * hps, axis=0), 0.0).astype(BF16)

    @pl.when(cb == 0)
    def _():
        if has_s0:
            st[...] = s0_ref[...]
        else:
            st[...] = jnp.zeros_like(st)

    srow = lax.broadcasted_iota(jnp.int32, (c_sz, sw), 0)
    scol = lax.broadcasted_iota(jnp.int32, (c_sz, sw), 1) % hd
    diff = jnp.where(fwd, srow - scol, scol - srow)
    strict = diff > 0
    incl = diff >= 0
    eye = jnp.where(diff == 0, 1.0, 0.0).astype(F32)
    levels = []
    m = 1
    while m < c_sz:
        levels.append(jnp.logical_and(srow // (2 * m) == scol // (2 * m), srow // m != scol // m))
        m *= 2
    trow = lax.broadcasted_iota(jnp.int32, (c_sz, c_sz), 0)
    tcol = lax.broadcasted_iota(jnp.int32, (c_sz, c_sz), 1)
    tri_incl = jnp.where(jnp.where(fwd, trow - tcol, tcol - trow) >= 0, 1.0, 0.0).astype(BF16)
    ones_bd = jnp.where(bd_mask, 1.0, 0.0).astype(BF16)

    def split(x, n_parts):
        parts = []
        for _ in range(n_parts - 1):
            p = x.astype(BF16)
            parts.append(p)
            x = x - p.astype(F32)
        return parts + [x.astype(BF16)]

    k_k, k_a = kk_ref[...], ka_ref[...]
    grp = range(n_grp)

    n_sub = min(RWKV_CHUNKS_PER_ITER, n_chunk)

    def chunks(j, carry):
        rows_of = []
        for cc in range(n_sub):
            jj = j * n_sub + cc
            jj = jnp.where(fwd, jj, n_chunk - 1 - jj)
            rows_of.append(pl.ds(pl.multiple_of(jj * c_sz, c_sz), c_sz))
        units = [(rows_of[cc], slice(g * sw, (g + 1) * sw)) for cc in range(n_sub) for g in grp]
        un = range(len(units))
        v = [v_ref[rows, sl] for rows, sl in units]
        kkr = [k_ref[rows, sl] * k_k[:, sl] for rows, sl in units]
        sq = [split(x * x, 2) for x in kkr]
        ssum = [sum(jnp.dot(p, ones_bd, preferred_element_type=F32) for p in s2) for s2 in sq]
        kk = [x * lax.rsqrt(s + 1e-12) for x, s in zip(kkr, ssum)]
        a = [a_ref[rows, sl] for rows, sl in units]
        b = [x * y for x, y in zip(kk, a)]
        kdir = [k_ref[rows, sl] * (1.0 + (ai - 1.0) * k_a[:, sl]) for (rows, sl), ai in zip(units, a)]
        lw = [lw_ref[rows, sl] for rows, sl in units]
        cum = [sum(jnp.dot(tri_incl, p, preferred_element_type=F32) for p in split(x, 3)) for x in lw]
        total = [jnp.sum(x, axis=0, keepdims=True) for x in lw]
        half = [0.5 * t for t in total]
        cumx = [c - x for c, x in zip(cum, lw)]
        r = [r_ref[rows, sl] for rows, sl in units]
        lhs_g = [jnp.concatenate([kk[i] * jnp.exp(cumx[i] - half[i]), r[i] * jnp.exp(cum[i] - half[i])], axis=0)
                 for i in un]
        e_neg = [jnp.exp(half[i] - cum[i]) for i in un]
        g_k = [_bdot_nt(lhs_g[i], block_diag(kdir[i] * e_neg[i])) for i in un]
        g_b = [_bdot_nt(lhs_g[i], block_diag(b[i] * e_neg[i])) for i in un]
        l_k = [jnp.where(strict, x[:c_sz], 0.0) for x in g_k]
        a_rk = [jnp.where(incl, x[c_sz:], 0.0) for x in g_k]
        l_b = [jnp.where(strict, x[:c_sz], 0.0) for x in g_b]
        a_rb = [jnp.where(incl, x[c_sz:], 0.0) for x in g_b]
        x = [eye - jnp.where(levels[0], l, 0.0) for l in l_b]
        for lvl in levels[1:]:
            t = [_bdot(x[i], block_diag(jnp.where(lvl, l_b[i], 0.0))) for i in un]
            x = [x[i] - _bdot(t[i], block_diag(x[i])) for i in un]
        bd_v = [block_diag(x) for x in v]
        lkv = [_bdot(l_k[i], bd_v[i]) for i in un]
        wt = [_bdot(x[i], block_diag(kk[i] * jnp.exp(cumx[i]))) for i in un]
        vt = [_bdot(x[i], block_diag(lkv[i])) for i in un]
        r_abs = [r[i] * jnp.exp(cum[i]) for i in un]
        e_end = [jnp.exp(total[i] - cum[i]) for i in un]
        kb_end = [jnp.concatenate([kdir[i] * e_end[i], -(b[i] * e_end[i])], axis=0) for i in un]
        state = [st[g] for g in grp]
        for cc in range(n_sub):
            ids = [cc * n_grp + g for g in grp]
            su = [_bdot_nt(jnp.concatenate([wt[i], r_abs[i]], axis=0), state[g]) for g, i in zip(grp, ids)]
            u = [su[g][:c_sz] + vt[i] for g, i in zip(grp, ids)]
            y = [su[g][c_sz:] + _bdot(a_rk[i], bd_v[i]) - _bdot(a_rb[i], block_diag(u[g]))
                 for g, i in zip(grp, ids)]
            upd = [_bdot_tn(jnp.concatenate([v[i], u[g]], axis=0), kb_end[i]) for g, i in zip(grp, ids)]
            state = [state[g] * jnp.exp(total[i]) + jnp.where(bd_mask, upd[g], 0.0) for g, i in zip(grp, ids)]
            y_ref[rows_of[cc], :] = jnp.concatenate(y, axis=1)
        st[...] = jnp.stack(state)
        return carry

    lax.fori_loop(0, n_chunk // n_sub, chunks, 0)

    if emit_state:
        @pl.when(cb == n_blk - 1)
        def _():
            for g in grp:
                s = st[g]
                for h in range(hps):
                    so_ref[g * hps + h] = s[h * hd:(h + 1) * hd, h * hd:(h + 1) * hd]


def _rwkv_scan(rkv, lw, a, k_k, k_a, s0, *, row0, n_seq, seq_len, emit_state, heads_per_step=32):
    _, _, d_model = rkv.shape
    hd, sw = RWKV_HEAD_DIM, RWKV_SLAB
    n_heads = d_model // hd
    g = min(heads_per_step, n_heads)
    wg = g * hd
    n_grp = wg // sw
    n_hg = n_heads // g
    tb = min(256, seq_len)
    n_blk = seq_len // tb
    rb0 = row0 // tb

    def blk(c, d):
        return jnp.where(d == 0, c, n_blk - 1 - c)

    def rkv_map(part):
        return lambda s, d, hg, c: (part, rb0 + s * n_blk + blk(c, d), hg)

    dir_map = lambda s, d, hg, c: (d, rb0 + s * n_blk + blk(c, d), hg)
    in_specs = ([pl.BlockSpec((None, tb, wg), rkv_map(p)) for p in range(3)]
                + [pl.BlockSpec((None, tb, wg), dir_map)] * 2
                + [pl.BlockSpec((1, wg), lambda s, d, hg, c: (0, hg))] * 2)
    args = [rkv, rkv, rkv, lw, a, k_k.reshape(1, d_model), k_a.reshape(1, d_model)]
    if s0 is not None:
        in_specs.append(pl.BlockSpec((None, None, n_grp, sw, sw), lambda s, d, hg, c: (s, d, hg, 0, 0)))
        args.append(s0)
    out_specs = [pl.BlockSpec((None, tb, wg), lambda s, d, hg, c: (d, s * n_blk + blk(c, d), hg))]
    out_shape = [jax.ShapeDtypeStruct((2, n_seq * seq_len, d_model), F32)]
    if emit_state:
        out_specs.append(pl.BlockSpec((None, None, g, hd, hd), lambda s, d, hg, c: (s, d, hg, 0, 0)))
        out_shape.append(jax.ShapeDtypeStruct((n_seq, 2, n_heads, hd, hd), F32))
    body = functools.partial(_rwkv_scan_body, n_chunk=tb // RWKV_CHUNK, n_blk=n_blk,
                             has_s0=s0 is not None, emit_state=emit_state)
    return pl.pallas_call(
        body, grid=(n_seq, 2, n_hg, n_blk), in_specs=in_specs, out_specs=out_specs, out_shape=out_shape,
        scratch_shapes=[pltpu.VMEM((n_grp, sw, sw), F32)],
        compiler_params=_cparams(4), name="rwkv_scan")(*args)


def _group_sum(x, ones_bd):
    w = ones_bd.shape[0]
    out = []
    for c in range(x.shape[1] // w):
        xs = x[:, c * w:(c + 1) * w]
        hi = xs.astype(BF16)
        lo = (xs - hi.astype(F32)).astype(BF16)
        out.append(jnp.dot(hi, ones_bd, preferred_element_type=F32)
                   + jnp.dot(lo, ones_bd, preferred_element_type=F32))
    return jnp.concatenate(out, axis=1)


def _rwkv_finalize_body(yp_ref, ys_ref, rkv_ref, a_ref, g_ref, ka_ref, rk_ref, lnw_ref, lnb_ref, o_ref, *, pt):
    hd = RWKV_HEAD_DIM
    w = RWKV_SLAB
    ri = lax.broadcasted_iota(jnp.int32, (w, w), 0)
    ci = lax.broadcasted_iota(jnp.int32, (w, w), 1)
    ones_bd = jnp.where(ri // hd == ci // hd, 1.0, 0.0).astype(BF16)

    def finalize(y_ref):
        y = y_ref[0] + y_ref[1]
        mean = _group_sum(y, ones_bd) * (1.0 / hd)
        yc = y - mean
        var = _group_sum(yc * yc, ones_bd) * (1.0 / hd)
        yn = yc * lax.rsqrt(var + RWKV_LN_EPS) * lnw_ref[...] + lnb_ref[...]
        r, k, v = rkv_ref[0], rkv_ref[1], rkv_ref[2]
        a_sum = a_ref[0] + a_ref[1]
        k_sum = k * (2.0 + (a_sum - 2.0) * ka_ref[...])
        bonus = _group_sum(r * k_sum * rk_ref[...], ones_bd) * v
        o_ref[...] = ((yn + bonus) * g_ref[...]).astype(o_ref.dtype)

    _for_trunk_of_tile(pt, yp_ref, ys_ref, finalize)


def _rwkv_finalize(y_p, y_s, rkv, a, g, k_a, r_k, ln_w, ln_b, tm=256):
    n, d = y_p.shape[1] + y_s.shape[1], y_p.shape[2]
    specs, pt = _two_trunk_specs(y_p, y_s, tm)
    row = lambda i: (0, 0)
    return pl.pallas_call(
        functools.partial(_rwkv_finalize_body, pt=pt), grid=(n // tm,),
        in_specs=specs + [
                  pl.BlockSpec((3, tm, d), lambda i: (0, i, 0)),
                  pl.BlockSpec((2, tm, d), lambda i: (0, i, 0)),
                  pl.BlockSpec((tm, d), lambda i: (i, 0)),
                  pl.BlockSpec((1, d), row), pl.BlockSpec((1, d), row),
                  pl.BlockSpec((1, d), row), pl.BlockSpec((1, d), row)],
        out_specs=pl.BlockSpec((tm, d), lambda i: (i, 0)),
        out_shape=jax.ShapeDtypeStruct((n, d), BF16),
        compiler_params=_cparams(1), name="rwkv_finalize",
    )(y_p, y_s, rkv, a, g, k_a.reshape(1, d), r_k.reshape(1, d), ln_w.reshape(1, d), ln_b.reshape(1, d))


def _cast_expert_weights(be_ref, pairs):
    b = pl.program_id(1)

    @pl.when(jnp.logical_or(b == 0, be_ref[b] != be_ref[jnp.maximum(b - 1, 0)]))
    def _():
        for w_ref, w_bf in pairs:
            w_bf[...] = w_ref[...].astype(BF16)


def _moe_up_body(be_ref, na_ref, x_ref, wg_ref, wu_ref, o_ref, wg_bf, wu_bf):
    b = pl.program_id(1)
    _cast_expert_weights(be_ref, [(wg_ref, wg_bf), (wu_ref, wu_bf)])

    @pl.when(b < na_ref[0])
    def _():
        x = x_ref[...]
        gate = jnp.dot(x, wg_bf[...], preferred_element_type=F32)
        up = jnp.dot(x, wu_bf[...], preferred_element_type=F32)
        o_ref[...] = (_silu(gate) * up).astype(o_ref.dtype)

    @pl.when(b >= na_ref[0])
    def _():
        o_ref[...] = jnp.zeros_like(o_ref)


def _moe_down_body(be_ref, na_ref, *refs, n_parts):
    x_refs, w_refs = refs[:n_parts], refs[n_parts:2 * n_parts]
    o_ref = refs[2 * n_parts]
    w_bfs = refs[2 * n_parts + 1:]
    b = pl.program_id(1)
    _cast_expert_weights(be_ref, list(zip(w_refs, w_bfs)))

    @pl.when(b < na_ref[0])
    def _():
        o_ref[...] = sum(jnp.dot(x[...], w[...], preferred_element_type=F32) for x, w in zip(x_refs, w_bfs))

    @pl.when(b >= na_ref[0])
    def _():
        o_ref[...] = jnp.zeros_like(o_ref)


def _moe_up(xb, block_e, n_active, w_gate, w_up, col0, n_cols, tn):
    rows, d = xb.shape
    bm = MOE_BLOCK
    off = col0 // tn
    w_spec = pl.BlockSpec((None, d, tn), lambda j, b, be, na: (be[b], 0, off + j))
    return pl.pallas_call(
        _moe_up_body,
        grid_spec=pltpu.PrefetchScalarGridSpec(
            num_scalar_prefetch=2, grid=(n_cols // tn, rows // bm),
            in_specs=[pl.BlockSpec((bm, d), lambda j, b, be, na: (b, 0)), w_spec, w_spec],
            out_specs=pl.BlockSpec((bm, tn), lambda j, b, be, na: (b, j)),
            scratch_shapes=[pltpu.VMEM((d, tn), BF16)] * 2),
        out_shape=jax.ShapeDtypeStruct((rows, n_cols), BF16),
        compiler_params=_cparams(2), name="moe_up")(block_e, n_active, xb, w_gate, w_up)


def _moe_experts(xb, block_e, n_active, w_gate, w_up, w_down, tn_wide=1024, tn_narrow=512, tn_down=512):
    rows, d = xb.shape
    bm = MOE_BLOCK
    d_ff = w_gate.shape[2]
    wide = (d_ff // tn_wide) * tn_wide
    parts = []
    if wide:
        parts.append((_moe_up(xb, block_e, n_active, w_gate, w_up, 0, wide, tn_wide), 0))
    if d_ff > wide:
        tn = min(tn_narrow, d_ff - wide)
        assert (d_ff - wide) % tn == 0 and wide % (d_ff - wide) == 0, (d_ff, tn_wide, tn_narrow)
        parts.append((_moe_up(xb, block_e, n_active, w_gate, w_up, wide, d_ff - wide, tn), wide))
    tn_down = min(tn_down, d)
    x_specs = [pl.BlockSpec((bm, a.shape[1]), lambda j, b, be, na: (b, 0)) for a, _ in parts]
    w_specs = [pl.BlockSpec((None, a.shape[1], tn_down),
                            lambda j, b, be, na, blk=c0 // a.shape[1]: (be[b], blk, j)) for a, c0 in parts]
    return pl.pallas_call(
        functools.partial(_moe_down_body, n_parts=len(parts)),
        grid_spec=pltpu.PrefetchScalarGridSpec(
            num_scalar_prefetch=2, grid=(d // tn_down, rows // bm),
            in_specs=x_specs + w_specs,
            out_specs=pl.BlockSpec((bm, tn_down), lambda j, b, be, na: (b, j)),
            scratch_shapes=[pltpu.VMEM((a.shape[1], tn_down), BF16) for a, _ in parts]),
        out_shape=jax.ShapeDtypeStruct((rows, d), F32),
        compiler_params=_cparams(2), name="moe_down",
    )(block_e, n_active, *[a for a, _ in parts], *([w_down] * len(parts)))


def _row_copy(src_hbm, src_row, dst, dst_row, sem):
    return pltpu.make_async_copy(src_hbm.at[pl.ds(src_row, 1), :], dst.at[pl.ds(dst_row, 1), :], sem)


def _prefetched_rows(fetch, drain):
    i, n = pl.program_id(0), pl.num_programs(0)
    buf = i % 2

    @pl.when(i == 0)
    def _():
        fetch(0, False)

    @pl.when(i + 1 < n)
    def _():
        fetch(1 - buf, True)

    drain(buf)
    return buf


def _moe_gather_body(tok_ref, tok_next_ref, h_hbm, o_ref, rows, sems, *, bm):
    def fetch(buf, is_next):
        idx_ref = tok_next_ref if is_next else tok_ref

        def start(r, carry):
            _row_copy(h_hbm, idx_ref[0, r], rows.at[buf], r, sems.at[buf]).start()
            return carry

        lax.fori_loop(0, bm, start, 0, unroll=8)

    def drain(buf):
        def wait(r, carry):
            _row_copy(h_hbm, 0, rows.at[buf], r, sems.at[buf]).wait()
            return carry

        lax.fori_loop(0, bm, wait, 0, unroll=8)

    buf = _prefetched_rows(fetch, drain)
    o_ref[...] = rows[buf].astype(o_ref.dtype)


def _moe_gather(h, slot_tok):
    _, d = h.shape
    bm = MOE_BLOCK
    n_blk = slot_tok.shape[0] // bm
    tok = slot_tok.reshape(n_blk, 1, bm)
    return pl.pallas_call(
        functools.partial(_moe_gather_body, bm=bm), grid=(n_blk,),
        in_specs=[pl.BlockSpec((None, 1, bm), lambda b: (b, 0, 0), memory_space=pltpu.SMEM),
                  pl.BlockSpec((None, 1, bm), lambda b: (jnp.minimum(b + 1, n_blk - 1), 0, 0),
                               memory_space=pltpu.SMEM),
                  pl.BlockSpec(memory_space=pl.ANY)],
        out_specs=pl.BlockSpec((bm, d), lambda b: (b, 0)),
        out_shape=jax.ShapeDtypeStruct((n_blk * bm, d), BF16),
        scratch_shapes=[pltpu.VMEM((2, bm, d), h.dtype), pltpu.SemaphoreType.DMA((2,))],
        compiler_params=_cparams(1), name="moe_gather")(tok, tok, h)


def _moe_combine_body(slot_ref, slot_next_ref, x_ref, g_ref, gt_ref, fw_ref, yb_hbm, op_ref, os_ref, ybuf, sems,
                      *, tm, pt):
    def fetch(buf, is_next):
        idx_ref = slot_next_ref if is_next else slot_ref

        def start(r, carry):
            for k in range(TOP_K):
                _row_copy(yb_hbm, idx_ref[0, TOP_K * r + k], ybuf.at[buf, k], r, sems.at[buf]).start()
            return carry

        lax.fori_loop(0, tm, start, 0, unroll=4)

    def drain(buf):
        def wait(r, carry):
            for k in range(TOP_K):
                _row_copy(yb_hbm, 0, ybuf.at[buf, k], r, sems.at[buf]).wait()
            return carry

        lax.fori_loop(0, tm, wait, 0, unroll=4)

    buf = _prefetched_rows(fetch, drain)
    g = g_ref[...]
    y = ybuf[buf, 0] * g[:, 0:1] + ybuf[buf, 1] * g[:, 1:2]
    x = x_ref[...] + gt_ref[...] * y
    out = x * lax.rsqrt(jnp.mean(x * x, axis=-1, keepdims=True) + NORM_EPS) * fw_ref[...]

    def store(o_ref):
        o_ref[...] = out

    _for_trunk_of_tile(pt, op_ref, os_ref, store)


def _moe_combine_final(x, yb, slot_of, gates, mod5, layer, final_w, lay, tm=256):
    n, d = x.shape
    tm = min(tm, lay.s_len, lay.p_rows)
    pt = lay.p_rows // tm
    row = pl.BlockSpec((tm, d), lambda i: (i, 0))
    n_tiles = n // tm
    slots = slot_of.reshape(n_tiles, 1, TOP_K * tm)
    return pl.pallas_call(
        functools.partial(_moe_combine_body, tm=tm, pt=pt), grid=(n_tiles,),
        in_specs=[pl.BlockSpec((None, 1, TOP_K * tm), lambda i: (i, 0, 0), memory_space=pltpu.SMEM),
                  pl.BlockSpec((None, 1, TOP_K * tm), lambda i: (jnp.minimum(i + 1, n_tiles - 1), 0, 0),
                               memory_space=pltpu.SMEM),
                  row, pl.BlockSpec((tm, TOP_K), lambda i: (i, 0)),
                  pl.BlockSpec((None, None, None, 1, d), _mod_spec(layer, 5, tm, lay)),
                  pl.BlockSpec((1, d), lambda i: (0, 0)),
                  pl.BlockSpec(memory_space=pl.ANY)],
        out_specs=[pl.BlockSpec((tm, d), lambda i: (jnp.minimum(i, pt - 1), 0)),
                   pl.BlockSpec((tm, d), lambda i: (jnp.maximum(i - pt, 0), 0))],
        out_shape=[jax.ShapeDtypeStruct((lay.p_rows, d), F32),
                   jax.ShapeDtypeStruct((n - lay.p_rows, d), F32)],
        scratch_shapes=[pltpu.VMEM((2, TOP_K, tm, d), F32), pltpu.SemaphoreType.DMA((2,))],
        compiler_params=_cparams(1), name="moe_combine_final",
    )(slots, slots, x, gates, mod5, final_w.reshape(1, d), yb)


def _moe(h, logits, w_gate, w_up, w_down):
    n_tok, d = h.shape
    bm = MOE_BLOCK
    top_val, top_idx = lax.top_k(logits, TOP_K)
    gates = jax.nn.softmax(top_val, axis=-1)
    n_assign = n_tok * TOP_K
    flat_e = top_idx.reshape(-1).astype(jnp.int32)
    order = jnp.argsort(flat_e).astype(jnp.int32)
    rank = jnp.argsort(order).astype(jnp.int32)
    counts = jnp.sum(flat_e[:, None] == jnp.arange(N_EXPERTS, dtype=jnp.int32)[None, :], axis=0,
                     dtype=jnp.int32)
    padded = ((counts + bm - 1) // bm) * bm
    pad_end = jnp.cumsum(padded)
    pad_start = pad_end - padded
    start = jnp.cumsum(counts) - counts
    n_blocks = -(-n_assign // bm) + N_EXPERTS
    block_e = jnp.minimum(jnp.searchsorted(pad_end, jnp.arange(n_blocks) * bm, side='right'),
                          N_EXPERTS - 1).astype(jnp.int32)
    n_active = (pad_end[-1] // bm).astype(jnp.int32).reshape(1)
    experts = jnp.arange(N_EXPERTS, dtype=jnp.int32)

    def lookup(table, e):
        return jnp.sum(jnp.where(e[..., None] == experts, table, 0), axis=-1, dtype=jnp.int32)

    slot = jnp.arange(n_blocks * bm, dtype=jnp.int32)
    slot_e = jnp.repeat(block_e, bm)
    pos = slot - lookup(pad_start, slot_e)
    src = jnp.clip(lookup(start, slot_e) + pos, 0, n_assign - 1)
    slot_tok = jnp.where(pos < lookup(counts, slot_e),
                         jnp.take(order, src, indices_are_sorted=True) // TOP_K, slot % n_tok)
    yb = _moe_experts(_moe_gather(h, slot_tok), block_e, n_active, w_gate, w_up, w_down)
    slot_of = (lookup(pad_start - start, flat_e) + rank).reshape(n_tok, TOP_K)
    return yb, slot_of, gates


def _rope_tables(seq_len, dim):
    t = jnp.arange(seq_len)
    row = (t // GRID_W).astype(F32)
    col = (t % GRID_W).astype(F32)
    half = dim // 2
    inv = ROPE_BASE ** (-(jnp.arange(0, half, 2, dtype=F32) / half))
    ang = jnp.concatenate([row[:, None] * inv, col[:, None] * inv], axis=-1)
    cos, sin = jnp.cos(ang), jnp.sin(ang)
    return jnp.repeat(cos, 2, axis=-1), jnp.stack([-sin, sin], axis=-1).reshape(seq_len, dim)


def _block_diag_states(s):
    b, two, n_heads, hd, _ = s.shape
    hps = RWKV_SLAB // hd
    s6 = s.reshape(b, two, n_heads // hps, hps, hd, hd)
    bd = jnp.einsum('bdghvk,hi->bdghvik', s6, jnp.eye(hps, dtype=s.dtype))
    return bd.reshape(b, two, n_heads // hps, RWKV_SLAB, RWKV_SLAB)


def _pad_cols(w, width):
    return jnp.pad(w, ((0, 0), (0, width - w.shape[1])))


def kernel(x_prompt, x_sample, state_l0_ret, state_l1_rwkv, c, c_ctx, ada_w, ada_b, norm_w, final_norm_w,
           l0_ret_w_in, l0_ret_w_out, l0_ret_decay, l0_ffn_w_gate, l0_ffn_w_up, l0_ffn_w_down,
           l1_rwkv_mu, l1_rwkv_w_rkv, l1_rwkv_w0, l1_rwkv_w1, l1_rwkv_w2, l1_rwkv_a0, l1_rwkv_a1, l1_rwkv_a2,
           l1_rwkv_g1, l1_rwkv_g2, l1_rwkv_k_k, l1_rwkv_k_a, l1_rwkv_r_k, l1_rwkv_ln_w, l1_rwkv_ln_b,
           l1_rwkv_w_out, l1_moe_router, l1_moe_w_gate, l1_moe_w_up, l1_moe_w_down):
    pb, p_len, d = x_prompt.shape
    sb, s_len, _ = x_sample.shape
    p_rows, s_rows = pb * p_len, sb * s_len
    lay = _Layout(p_rows, p_len, s_len, p_rows + s_rows)
    n_layers = ada_w.shape[0]

    x = (x_prompt.reshape(p_rows, d), x_sample.reshape(s_rows, d))
    cond8 = jnp.concatenate([c_ctx[None, :], c, jnp.zeros((8 - 1 - sb, d), F32)], axis=0)
    mod5 = _ada_mod(cond8, ada_w, ada_b)[:, :1 + sb].reshape(n_layers, 1 + sb, 6, 1, d)

    h = _norm_mod(x, norm_w[0, 0], mod5, 0, 1, 0, lay)
    qkvg = _linear(h, l0_ret_w_in, tn=1024)
    log_gamma = -jnp.exp(l0_ret_decay.astype(F32))
    dh = d // RET_HEADS
    o_p, new_state_l0_ret = _retention(qkvg, log_gamma, None, None, row0=0, n_seq=pb, seq_len=p_len,
                                       emit_state=True)
    (o_s,) = _retention(qkvg, log_gamma, state_l0_ret, _rope_tables(s_len, dh), row0=p_rows, n_seq=sb,
                        seq_len=s_len, emit_state=False)
    x = _linear_residual(_ret_finalize(o_p, o_s, qkvg), l0_ret_w_out, x, mod5, 0, 2, lay)
    h = _norm_mod(x, norm_w[0, 1], mod5, 0, 4, 3, lay)
    act = _swiglu_up(h, l0_ffn_w_gate, l0_ffn_w_up)
    x = _linear_residual(act, l0_ffn_w_down, x, mod5, 0, 5, lay, tm=512, tn=512)

    xs = _rwkv_mix(x, norm_w[1, 0], mod5, 1, l1_rwkv_mu, lay)
    rkv = _linear(xs, l1_rwkv_w_rkv, n_batch=3, tn=1024)
    lora_w = l1_rwkv_w1.shape[2]
    pad_w = -(-lora_w // 128) * 128
    w1p = jnp.concatenate([_pad_cols(l1_rwkv_w1[0], pad_w), _pad_cols(l1_rwkv_w1[1], pad_w)], axis=1)
    a1p = jnp.concatenate([_pad_cols(l1_rwkv_a1[0], pad_w), _pad_cols(l1_rwkv_a1[1], pad_w)], axis=1)
    t_w = _linear(xs, w1p, x_lead=3, act="tanh", out_dtype=BF16)
    t_a = _linear(xs, a1p, x_lead=4, out_dtype=BF16)
    t_g = _linear(xs, l1_rwkv_g1, x_lead=5, act="sigmoid", out_dtype=BF16)
    w2p = jnp.pad(l1_rwkv_w2, ((0, 0), (0, pad_w - lora_w), (0, 0)))
    a2p = jnp.pad(l1_rwkv_a2, ((0, 0), (0, pad_w - lora_w), (0, 0)))
    lw = _linear(t_w, w2p, n_batch=2, bias=l1_rwkv_w0, act="log_decay")
    a_lr = _linear(t_a, a2p, n_batch=2, bias=l1_rwkv_a0, act="sigmoid")
    g = _linear(t_g, l1_rwkv_g2)
    scan_args = (rkv, lw, a_lr, l1_rwkv_k_k, l1_rwkv_k_a)
    y_p, new_state_l1_rwkv = _rwkv_scan(*scan_args, None, row0=0, n_seq=pb, seq_len=p_len, emit_state=True)
    (y_s,) = _rwkv_scan(*scan_args, _block_diag_states(state_l1_rwkv), row0=p_rows, n_seq=sb, seq_len=s_len,
                        emit_state=False)
    yg = _rwkv_finalize(y_p, y_s, rkv, a_lr, g, l1_rwkv_k_a, l1_rwkv_r_k, l1_rwkv_ln_w, l1_rwkv_ln_b)
    x = _linear_residual(yg, l1_rwkv_w_out, x, mod5, 1, 2, lay)
    ne_pad = 128
    h, logits = _norm_mod(x, norm_w[1, 1], mod5, 1, 4, 3, lay, router_w=_pad_cols(l1_moe_router, ne_pad))
    yb, slot_of, gates = _moe(h, logits[:, :N_EXPERTS], l1_moe_w_gate, l1_moe_w_up, l1_moe_w_down)
    y_prompt, y_sample = _moe_combine_final(x, yb, slot_of, gates, mod5, 1, final_norm_w, lay)
    return (y_prompt.reshape(pb, p_len, d), y_sample.reshape(sb, s_len, d), new_state_l0_ret, new_state_l1_rwkv)
```

```python
import functools
from typing import NamedTuple

import jax
import jax.numpy as jnp
from jax import lax
from jax.experimental import pallas as pl
from jax.experimental.pallas import tpu as pltpu

F32 = jnp.float32
BF16 = jnp.bfloat16

GRID_W = 64
RET_HEADS = 8
RET_CHUNK = 128
RET_HEADS_PER_STEP = 4
ROPE_BASE = 10000.0
RWKV_HEAD_DIM = 64
RWKV_CHUNK = 64
RWKV_SLAB = 256
RWKV_CHUNKS_PER_ITER = 4
RWKV_LN_EPS = 64e-5
N_EXPERTS = 8
TOP_K = 2
MOE_BLOCK = 512
NORM_EPS = 1e-6

VMEM_LIMIT_BYTES = 60 * 1024 * 1024


class _Layout(NamedTuple):
    p_rows: int
    p_len: int
    s_len: int
    n_rows: int


def _cparams(n_axes):
    return pltpu.CompilerParams(dimension_semantics=("arbitrary",) * n_axes,
                                vmem_limit_bytes=VMEM_LIMIT_BYTES)


def _cond_of_tile(i, tm, lay):
    pt = lay.p_rows // tm
    st = lay.s_len // tm
    return jnp.where(i < pt, 0, 1 + (i - pt) // st)


def _bdot(a, b):
    return jnp.dot(a.astype(BF16), b.astype(BF16), preferred_element_type=F32)


def _bdot_nt(a, b):
    return lax.dot_general(a.astype(BF16), b.astype(BF16), (((1,), (1,)), ((), ())),
                           preferred_element_type=F32)


def _bdot_tn(a, b):
    return lax.dot_general(a.astype(BF16), b.astype(BF16), (((0,), (0,)), ((), ())),
                           preferred_element_type=F32)


def _silu(x):
    return x * jax.nn.sigmoid(x)


def _ada_body(c_ref, w_ref, b_ref, o_ref):
    o_ref[...] = _bdot(_silu(c_ref[...]), w_ref[...]) + b_ref[...]


def _ada_mod(cond8, ada_w, ada_b):
    n_layers, d, d6 = ada_w.shape
    tn = min(1024, d6)
    return pl.pallas_call(
        _ada_body,
        grid=(n_layers, d6 // tn),
        in_specs=[pl.BlockSpec((8, d), lambda l, j: (0, 0)),
                  pl.BlockSpec((None, d, tn), lambda l, j: (l, 0, j)),
                  pl.BlockSpec((None, 1, tn), lambda l, j: (l, 0, j))],
        out_specs=pl.BlockSpec((None, 8, tn), lambda l, j: (l, 0, j)),
        out_shape=jax.ShapeDtypeStruct((n_layers, 8, d6), F32),
        compiler_params=_cparams(2), name="ada_mod",
    )(cond8, ada_w, ada_b.reshape(n_layers, 1, d6))


def _mod_spec(layer, which, tm, lay, n_grid_axes=1):
    def imap(i, *_):
        return (layer, _cond_of_tile(i, tm, lay), which, 0, 0)
    return imap


def _rms_mod(x, nw, sc, sh):
    xn = x * lax.rsqrt(jnp.mean(x * x, axis=-1, keepdims=True) + NORM_EPS) * nw
    return xn * (1.0 + sc) + sh


def _norm_mod_body(x_ref, nw_ref, sc_ref, sh_ref, o_ref):
    o_ref[...] = _rms_mod(x_ref[...], nw_ref[...], sc_ref[...], sh_ref[...]).astype(o_ref.dtype)


def _norm_mod2_body(xp_ref, xs_ref, nw_ref, sc_ref, sh_ref, o_ref, *, pt):
    def norm(x_ref):
        o_ref[...] = _rms_mod(x_ref[...], nw_ref[...], sc_ref[...], sh_ref[...]).astype(o_ref.dtype)

    _for_trunk_of_tile(pt, xp_ref, xs_ref, norm)


def _norm_mod_router_body(x_ref, nw_ref, sc_ref, sh_ref, wr_ref, o_ref, lg_ref):
    h = _rms_mod(x_ref[...], nw_ref[...], sc_ref[...], sh_ref[...])
    o_ref[...] = h.astype(o_ref.dtype)
    lg_ref[...] = jnp.dot(h, wr_ref[...], preferred_element_type=F32, precision=lax.Precision.HIGHEST)


def _trunk_pair_specs(tm, tn, pt, n_grid_axes):
    if n_grid_axes == 1:
        return [pl.BlockSpec((tm, tn), lambda i: (jnp.minimum(i, pt - 1), 0)),
                pl.BlockSpec((tm, tn), lambda i: (jnp.maximum(i - pt, 0), 0))]
    return [pl.BlockSpec((tm, tn), lambda j, i: (jnp.minimum(i, pt - 1), j)),
            pl.BlockSpec((tm, tn), lambda j, i: (jnp.maximum(i - pt, 0), j))]


def _norm_mod(x, nw, mod5, layer, which_sc, which_sh, lay, router_w=None, tm=512):
    pair = isinstance(x, tuple)
    n, d = lay.n_rows, nw.shape[0]
    tm = min(tm, lay.s_len, lay.p_rows)
    mod_specs = [pl.BlockSpec((1, d), lambda i: (0, 0)),
                 pl.BlockSpec((None, None, None, 1, d), _mod_spec(layer, which_sc, tm, lay)),
                 pl.BlockSpec((None, None, None, 1, d), _mod_spec(layer, which_sh, tm, lay))]
    if pair:
        pt = lay.p_rows // tm
        return pl.pallas_call(
            functools.partial(_norm_mod2_body, pt=pt), grid=(n // tm,),
            in_specs=_trunk_pair_specs(tm, d, pt, 1) + mod_specs,
            out_specs=pl.BlockSpec((tm, d), lambda i: (i, 0)),
            out_shape=jax.ShapeDtypeStruct((n, d), BF16),
            compiler_params=_cparams(1), name="norm_mod")(*x, nw.reshape(1, d), mod5, mod5)
    in_specs = [pl.BlockSpec((tm, d), lambda i: (i, 0))] + mod_specs
    args = [x, nw.reshape(1, d), mod5, mod5]
    if router_w is None:
        return pl.pallas_call(
            _norm_mod_body, grid=(n // tm,), in_specs=in_specs,
            out_specs=pl.BlockSpec((tm, d), lambda i: (i, 0)),
            out_shape=jax.ShapeDtypeStruct((n, d), BF16),
            compiler_params=_cparams(1), name="norm_mod")(*args)
    ne = router_w.shape[1]
    return pl.pallas_call(
        _norm_mod_router_body, grid=(n // tm,),
        in_specs=in_specs + [pl.BlockSpec((d, ne), lambda i: (0, 0))],
        out_specs=[pl.BlockSpec((tm, d), lambda i: (i, 0)), pl.BlockSpec((tm, ne), lambda i: (i, 0))],
        out_shape=[jax.ShapeDtypeStruct((n, d), F32), jax.ShapeDtypeStruct((n, ne), F32)],
        compiler_params=_cparams(1), name="norm_mod_router")(*args, router_w)


def _cast_weights_once(row_axis, pairs):
    @pl.when(pl.program_id(row_axis) == 0)
    def _():
        for w_ref, w_bf in pairs:
            w_bf[...] = w_ref[...].astype(BF16)


def _softplus(x):
    return jnp.maximum(x, 0.0) + jnp.log1p(jnp.exp(-jnp.abs(x)))


def _linear_body(x_ref, w_ref, *refs, act, has_bias):
    refs = list(refs)
    bias_ref = refs.pop(0) if has_bias else None
    o_ref, w_bf = refs
    _cast_weights_once(2, [(w_ref, w_bf)])
    acc = jnp.dot(x_ref[...].astype(BF16), w_bf[...], preferred_element_type=F32)
    if has_bias:
        acc = acc + bias_ref[...]
    if act == "tanh":
        acc = jnp.tanh(acc)
    elif act == "sigmoid":
        acc = jax.nn.sigmoid(acc)
    elif act == "log_decay":
        acc = -jnp.exp(-_softplus(-acc) - 0.5)
    o_ref[...] = acc.astype(o_ref.dtype)


def _linear(x, w, *, n_batch=None, x_lead=0, bias=None, act=None, out_dtype=F32, tm=1024, tn=512):
    k, n = w.shape[-2:]
    m = x.shape[-2]
    tm, tn = min(tm, m), min(tn, n)
    if x.ndim == 3:
        x_spec = pl.BlockSpec((None, tm, k), lambda b, j, i: (x_lead + b, i, 0))
    else:
        x_spec = pl.BlockSpec((tm, k), lambda b, j, i: (i, b))
    if w.ndim == 3:
        w_spec = pl.BlockSpec((None, k, tn), lambda b, j, i: (b, 0, j))
    else:
        w_spec = pl.BlockSpec((k, tn), lambda b, j, i: (0, j))
    if n_batch is None:
        out_spec = pl.BlockSpec((tm, tn), lambda b, j, i: (i, j))
        out_shape = jax.ShapeDtypeStruct((m, n), out_dtype)
    else:
        out_spec = pl.BlockSpec((None, tm, tn), lambda b, j, i: (b, i, j))
        out_shape = jax.ShapeDtypeStruct((n_batch, m, n), out_dtype)
    in_specs, args = [x_spec, w_spec], [x, w]
    if bias is not None:
        in_specs.append(pl.BlockSpec((None, 1, tn), lambda b, j, i: (b, 0, j)))
        args.append(bias.reshape(bias.shape[0], 1, n))
    return pl.pallas_call(
        functools.partial(_linear_body, act=act, has_bias=bias is not None),
        grid=(n_batch or 1, n // tn, m // tm), in_specs=in_specs,
        out_specs=out_spec, out_shape=out_shape,
        scratch_shapes=[pltpu.VMEM((k, tn), BF16)],
        compiler_params=_cparams(3), name="linear")(*args)


def _linear_res_body(x_ref, w_ref, *refs, pt):
    res_refs, (gt_ref, o_ref, w_bf) = refs[:-3], refs[-3:]
    _cast_weights_once(1, [(w_ref, w_bf)])
    y = gt_ref[...] * jnp.dot(x_ref[...], w_bf[...], preferred_element_type=F32)

    def add(res_ref):
        o_ref[...] = res_ref[...] + y

    if len(res_refs) == 1:
        add(res_refs[0])
    else:
        _for_trunk_of_tile(pt, *res_refs, add, row_axis=1)


def _linear_residual(x, w, res, mod5, layer, which_gate, lay, tm=1024, tn=1024):
    m, k = x.shape
    n = w.shape[1]
    tm, tn = min(tm, lay.s_len, lay.p_rows), min(tn, n)
    pt = lay.p_rows // tm

    def gmap(j, i):
        return (layer, _cond_of_tile(i, tm, lay), which_gate, 0, j)

    if isinstance(res, tuple):
        res_specs, res_args = _trunk_pair_specs(tm, tn, pt, 2), list(res)
    else:
        res_specs, res_args = [pl.BlockSpec((tm, tn), lambda j, i: (i, j))], [res]
    return pl.pallas_call(
        functools.partial(_linear_res_body, pt=pt), grid=(n // tn, m // tm),
        in_specs=[pl.BlockSpec((tm, k), lambda j, i: (i, 0)),
                  pl.BlockSpec((k, tn), lambda j, i: (0, j))] + res_specs
                 + [pl.BlockSpec((None, None, None, 1, tn), gmap)],
        out_specs=pl.BlockSpec((tm, tn), lambda j, i: (i, j)),
        out_shape=jax.ShapeDtypeStruct((m, n), F32),
        scratch_shapes=[pltpu.VMEM((k, tn), BF16)],
        compiler_params=_cparams(2), name="linear_residual")(x, w, *res_args, mod5)


def _swiglu_up_body(x_ref, wg_ref, wu_ref, o_ref, wg_bf, wu_bf):
    _cast_weights_once(1, [(wg_ref, wg_bf), (wu_ref, wu_bf)])
    x = x_ref[...]
    gate = jnp.dot(x, wg_bf[...], preferred_element_type=F32)
    up = jnp.dot(x, wu_bf[...], preferred_element_type=F32)
    o_ref[...] = (_silu(gate) * up).astype(o_ref.dtype)


def _swiglu_up(x, wg, wu, tm=1024, tn=512):
    m, k = x.shape
    n = wg.shape[1]
    tm, tn = min(tm, m), min(tn, n)
    return pl.pallas_call(
        _swiglu_up_body, grid=(n // tn, m // tm),
        in_specs=[pl.BlockSpec((tm, k), lambda j, i: (i, 0)),
                  pl.BlockSpec((k, tn), lambda j, i: (0, j)),
                  pl.BlockSpec((k, tn), lambda j, i: (0, j))],
        out_specs=pl.BlockSpec((tm, tn), lambda j, i: (i, j)),
        out_shape=jax.ShapeDtypeStruct((m, n), BF16),
        scratch_shapes=[pltpu.VMEM((k, tn), BF16)] * 2,
        compiler_params=_cparams(2), name="swiglu_up")(x, wg, wu)


def _rope(x, c, s):
    w = x.shape[-1]
    lane = lax.broadcasted_iota(jnp.int32, x.shape, x.ndim - 1)
    nxt = pltpu.roll(x, w - 1, axis=x.ndim - 1)
    prv = pltpu.roll(x, 1, axis=x.ndim - 1)
    return x * c + jnp.where(lane % 2 == 0, nxt, prv) * s


def _retention_body(lg_ref, *refs, n_chunk, n_blk, has_rope, has_s0, emit_state, scale):
    refs = list(refs)
    q_ref, k_ref, v_ref = refs[:3]
    refs = refs[3:]
    if has_rope:
        cos_ref, sin_ref = refs[:2]
        refs = refs[2:]
    if has_s0:
        s0_ref = refs.pop(0)
    o_ref = refs.pop(0)
    if emit_state:
        so_ref = refs.pop(0)
    st = refs.pop(0)
    c_sz = RET_CHUNK
    hp, dh = st.shape[0], st.shape[1]
    hg, d, cb = pl.program_id(1), pl.program_id(2), pl.program_id(3)
    heads = range(hp)
    lanes = [slice(h * dh, (h + 1) * dh) for h in heads]

    @pl.when(cb == 0)
    def _():
        st[...] = s0_ref[...] if has_s0 else jnp.zeros_like(st)

    lgv = [lg_ref[d, hg * hp + h] for h in heads]
    fwd = d == 0
    row = lax.broadcasted_iota(jnp.int32, (c_sz, c_sz), 0)
    col = lax.broadcasted_iota(jnp.int32, (c_sz, c_sz), 1)
    diff = jnp.where(fwd, row - col, col - row).astype(F32)
    intra = [jnp.where(diff >= 0, jnp.exp(jnp.maximum(diff, 0.0) * lg), 0.0) for lg in lgv]
    pos = lax.broadcasted_iota(jnp.int32, (c_sz, dh), 0)
    npos = jnp.where(fwd, pos, c_sz - 1 - pos).astype(F32)
    q_decay = [jnp.exp((npos + 1.0) * lg) for lg in lgv]
    k_decay = [jnp.exp((c_sz - 1.0 - npos) * lg) for lg in lgv]
    chunk_decay = [jnp.exp(jnp.full((1, dh), c_sz, F32) * lg) for lg in lgv]

    for j in range(n_chunk):
        jj = jnp.where(fwd, j, n_chunk - 1 - j)
        rows = pl.ds(pl.multiple_of(jj * c_sz, c_sz), c_sz)
        q = [q_ref[rows, sl] for sl in lanes]
        k = [k_ref[rows, sl] * scale for sl in lanes]
        v = [v_ref[rows, sl] for sl in lanes]
        if has_rope:
            c, s = cos_ref[rows, :], sin_ref[rows, :]
            q = [_rope(x, c, s) for x in q]
            k = [_rope(x, c, s) for x in k]
        state = [st[h] for h in heads]
        scores = [_bdot_nt(q[h], k[h]) * intra[h] for h in heads]
        o = [_bdot(scores[h], v[h]) + _bdot(q[h], state[h]) * q_decay[h] for h in heads]
        new_state = [state[h] * chunk_decay[h] + _bdot_tn(k[h] * k_decay[h], v[h]) for h in heads]
        o_ref[rows, :] = jnp.concatenate(o, axis=1)
        st[...] = jnp.stack(new_state)

    if emit_state:
        @pl.when(cb == n_blk - 1)
        def _():
            so_ref[...] = st[...]


def _retention(qkvg, log_gamma, s0, rope, *, row0, n_seq, seq_len, emit_state):
    d_model = qkvg.shape[1] // 4
    n_heads = RET_HEADS
    dh = d_model // n_heads
    tb = min(512, seq_len)
    n_blk = seq_len // tb
    rb0 = row0 // tb

    def blk(c, d):
        return jnp.where(d == 0, c, n_blk - 1 - c)

    hp = RET_HEADS_PER_STEP
    n_hg = n_heads // hp

    def in_map(part):
        return lambda s, h, d, c, lg: (rb0 + s * n_blk + blk(c, d), part * n_hg + h)

    in_specs = [pl.BlockSpec((tb, hp * dh), in_map(p)) for p in range(3)]
    args = [qkvg, qkvg, qkvg]
    if rope is not None:
        in_specs += [pl.BlockSpec((tb, dh), lambda s, h, d, c, lg: (blk(c, d), 0))] * 2
        args += list(rope)
    if s0 is not None:
        in_specs.append(pl.BlockSpec((None, None, hp, dh, dh), lambda s, h, d, c, lg: (s, d, h, 0, 0)))
        args.append(s0)
    out_specs = [pl.BlockSpec((None, tb, hp * dh), lambda s, h, d, c, lg: (d, s * n_blk + blk(c, d), h))]
    out_shape = [jax.ShapeDtypeStruct((2, n_seq * seq_len, d_model), F32)]
    if emit_state:
        out_specs.append(pl.BlockSpec((None, None, hp, dh, dh), lambda s, h, d, c, lg: (s, d, h, 0, 0)))
        out_shape.append(jax.ShapeDtypeStruct((n_seq, 2, n_heads, dh, dh), F32))
    body = functools.partial(_retention_body, n_chunk=tb // RET_CHUNK, n_blk=n_blk, has_rope=rope is not None,
                             has_s0=s0 is not None, emit_state=emit_state, scale=dh ** -0.5)
    return pl.pallas_call(
        body,
        grid_spec=pltpu.PrefetchScalarGridSpec(
            num_scalar_prefetch=1, grid=(n_seq, n_hg, 2, n_blk), in_specs=in_specs, out_specs=out_specs,
            scratch_shapes=[pltpu.VMEM((hp, dh, dh), F32)]),
        out_shape=out_shape, compiler_params=_cparams(4), name="retention")(log_gamma, *args)


def _two_trunk_specs(a_p, a_s, tm):
    pt = a_p.shape[1] // tm
    d = a_p.shape[2]
    return [pl.BlockSpec((2, tm, d), lambda i: (0, jnp.minimum(i, pt - 1), 0)),
            pl.BlockSpec((2, tm, d), lambda i: (0, jnp.maximum(i - pt, 0), 0))], pt


def _for_trunk_of_tile(pt, p_ref, s_ref, fn, row_axis=0):
    i = pl.program_id(row_axis)

    @pl.when(i < pt)
    def _():
        fn(p_ref)

    @pl.when(i >= pt)
    def _():
        fn(s_ref)


def _ret_finalize_body(op_ref, os_ref, g_ref, out_ref, *, n_heads, pt):
    def finalize(o_ref):
        o = o_ref[0] + o_ref[1]
        g = g_ref[...]
        dh = o.shape[1] // n_heads
        for h in range(n_heads):
            sl = slice(h * dh, (h + 1) * dh)
            oh = o[:, sl]
            oh = oh * lax.rsqrt(jnp.mean(oh * oh, axis=-1, keepdims=True) + NORM_EPS)
            out_ref[:, sl] = (oh * _silu(g[:, sl])).astype(out_ref.dtype)

    _for_trunk_of_tile(pt, op_ref, os_ref, finalize)


def _ret_finalize(o_p, o_s, qkvg, tm=256):
    n, d = o_p.shape[1] + o_s.shape[1], o_p.shape[2]
    specs, pt = _two_trunk_specs(o_p, o_s, tm)
    return pl.pallas_call(
        functools.partial(_ret_finalize_body, n_heads=RET_HEADS, pt=pt), grid=(n // tm,),
        in_specs=specs + [pl.BlockSpec((tm, d), lambda i: (i, 3))],
        out_specs=pl.BlockSpec((tm, d), lambda i: (i, 0)),
        out_shape=jax.ShapeDtypeStruct((n, d), BF16),
        compiler_params=_cparams(1), name="ret_finalize")(o_p, o_s, qkvg)


def _rwkv_mix_body(x_ref, xp_ref, xn_ref, nw_ref, sc_ref, sh_ref, mu_ref, o_ref, hext, *, tm, halo, lay):
    i = pl.program_id(0)
    d = x_ref.shape[1]
    nw, sc, sh = nw_ref[...], sc_ref[...], sh_ref[...]
    hext[0:halo, :] = _rms_mod(xp_ref[...], nw, sc, sh)
    hext[halo:halo + tm, :] = _rms_mod(x_ref[...], nw, sc, sh)
    hext[halo + tm:halo + tm + halo, :] = _rms_mod(xn_ref[...], nw, sc, sh)
    h = hext[halo:halo + tm, :]
    g_row = i * tm + lax.broadcasted_iota(jnp.int32, (tm, 1), 0)

    def emit(h_shift):
        diff = h_shift - h
        for n in range(6):
            o_ref[n] = (h + diff * mu_ref[n:n + 1, :]).astype(o_ref.dtype)

    def shifted(off, lo, hi, keep):
        return jnp.where(keep, hext[halo + off:halo + off + tm, lo:hi], 0.0)

    @pl.when(i < lay.p_rows // tm)
    def _():
        t = g_row % lay.p_len
        hd = d // 2
        emit(jnp.concatenate([shifted(-1, 0, hd, t != 0),
                              shifted(1, hd, d, t != lay.p_len - 1)], axis=1))

    @pl.when(i >= lay.p_rows // tm)
    def _():
        t = (g_row - lay.p_rows) % lay.s_len
        colw = t % GRID_W
        qd = d // 4
        emit(jnp.concatenate([shifted(-1, 0, qd, colw != 0),
                              shifted(1, qd, 2 * qd, colw != GRID_W - 1),
                              shifted(-GRID_W, 2 * qd, 3 * qd, t >= GRID_W),
                              shifted(GRID_W, 3 * qd, d, t < lay.s_len - GRID_W)], axis=1))


def _rwkv_mix(x, nw, mod5, layer, mu, lay, tm=512):
    n, d = x.shape
    halo = GRID_W
    tm = min(tm, lay.s_len, lay.p_rows)
    r = tm // halo
    n_halo_blk = n // halo
    body = functools.partial(_rwkv_mix_body, tm=tm, halo=halo, lay=lay)
    return pl.pallas_call(
        body, grid=(n // tm,),
        in_specs=[pl.BlockSpec((tm, d), lambda i: (i, 0)),
                  pl.BlockSpec((halo, d), lambda i: (jnp.maximum(i * r - 1, 0), 0)),
                  pl.BlockSpec((halo, d), lambda i: (jnp.minimum((i + 1) * r, n_halo_blk - 1), 0)),
                  pl.BlockSpec((1, d), lambda i: (0, 0)),
                  pl.BlockSpec((None, None, None, 1, d), _mod_spec(layer, 1, tm, lay)),
                  pl.BlockSpec((None, None, None, 1, d), _mod_spec(layer, 0, tm, lay)),
                  pl.BlockSpec((6, d), lambda i: (0, 0))],
        out_specs=pl.BlockSpec((6, tm, d), lambda i: (0, i, 0)),
        out_shape=jax.ShapeDtypeStruct((6, n, d), BF16),
        scratch_shapes=[pltpu.VMEM((tm + 2 * halo, d), F32)],
        compiler_params=_cparams(1), name="rwkv_mix")(x, x, x, nw.reshape(1, d), mod5, mod5, mu)


def _rwkv_scan_body(*refs, n_chunk, n_blk, has_s0, emit_state):
    refs = list(refs)
    r_ref, k_ref, v_ref, lw_ref, a_ref, kk_ref, ka_ref = refs[:7]
    refs = refs[7:]
    if has_s0:
        s0_ref = refs.pop(0)
    y_ref = refs.pop(0)
    if emit_state:
        so_ref = refs.pop(0)
    st = refs.pop(0)
    c_sz, hd, sw = RWKV_CHUNK, RWKV_HEAD_DIM, RWKV_SLAB
    n_grp = st.shape[0]
    hps = sw // hd
    d, cb = pl.program_id(1), pl.program_id(3)
    fwd = d == 0

    ri = lax.broadcasted_iota(jnp.int32, (sw, sw), 0)
    ci = lax.broadcasted_iota(jnp.int32, (sw, sw), 1)
    bd_mask = (ri // hd) == (ci // hd)

    def block_diag(slab):
        return jnp.where(bd_mask, jnp.concatenate([slab] * hps, axis=0), 0.0).astype(BF16)

    @pl.when(cb == 0)
    def _():
        if has_s0:
            st[...] = s0_ref[...]
        else:
            st[...] = jnp.zeros_like(st)

    srow = lax.broadcasted_iota(jnp.int32, (c_sz, sw), 0)
    scol = lax.broadcasted_iota(jnp.int32, (c_sz, sw), 1) % hd
    diff = jnp.where(fwd, srow - scol, scol - srow)
    strict = diff > 0
    incl = diff >= 0
    eye = jnp.where(diff == 0, 1.0, 0.0).astype(F32)
    levels = []
    m = 1
    while m < c_sz:
        levels.append(jnp.logical_and(srow // (2 * m) == scol // (2 * m), srow // m != scol // m))
        m *= 2
    trow = lax.broadcasted_iota(jnp.int32, (c_sz, c_sz), 0)
    tcol = lax.broadcasted_iota(jnp.int32, (c_sz, c_sz), 1)
    tri_incl = jnp.where(jnp.where(fwd, trow - tcol, tcol - trow) >= 0, 1.0, 0.0).astype(BF16)
    ones_bd = jnp.where(bd_mask, 1.0, 0.0).astype(BF16)

    def split(x, n_parts):
        parts = []
        for _ in range(n_parts - 1):
            p = x.astype(BF16)
            parts.append(p)
            x = x - p.astype(F32)
        return parts + [x.astype(BF16)]

    k_k, k_a = kk_ref[...], ka_ref[...]
    grp = range(n_grp)

    n_sub = min(RWKV_CHUNKS_PER_ITER, n_chunk)

    def chunks(j, carry):
        rows_of = []
        for cc in range(n_sub):
            jj = j * n_sub + cc
            jj = jnp.where(fwd, jj, n_chunk - 1 - jj)
            rows_of.append(pl.ds(pl.multiple_of(jj * c_sz, c_sz), c_sz))
        units = [(rows_of[cc], slice(g * sw, (g + 1) * sw)) for cc in range(n_sub) for g in grp]
        un = range(len(units))
        v = [v_ref[rows, sl] for rows, sl in units]
        kkr = [k_ref[rows, sl] * k_k[:, sl] for rows, sl in units]
        sq = jnp.concatenate([p for x in kkr for p in split(x * x, 2)], axis=0)
        sq = jnp.dot(sq, ones_bd, preferred_element_type=F32)
        ssum = [sq[2 * i * c_sz:(2 * i + 1) * c_sz] + sq[(2 * i + 1) * c_sz:(2 * i + 2) * c_sz] for i in un]
        kk = [x * lax.rsqrt(s + 1e-12) for x, s in zip(kkr, ssum)]
        a = [a_ref[rows, sl] for rows, sl in units]
        b = [x * y for x, y in zip(kk, a)]
        kdir = [k_ref[rows, sl] * (1.0 + (ai - 1.0) * k_a[:, sl]) for (rows, sl), ai in zip(units, a)]
        lw = [lw_ref[rows, sl] for rows, sl in units]
        cum = [sum(jnp.dot(tri_incl, p, preferred_element_type=F32) for p in split(x, 3)) for x in lw]
        total = [jnp.sum(x, axis=0, keepdims=True) for x in lw]
        half = [0.5 * t for t in total]
        cumx = [c - x for c, x in zip(cum, lw)]
        r = [r_ref[rows, sl] for rows, sl in units]
        lhs_g = [jnp.concatenate([kk[i] * jnp.exp(cumx[i] - half[i]), r[i] * jnp.exp(cum[i] - half[i])], axis=0)
                 for i in un]
        e_neg = [jnp.exp(half[i] - cum[i]) for i in un]
        g_k = [_bdot_nt(lhs_g[i], block_diag(kdir[i] * e_neg[i])) for i in un]
        g_b = [_bdot_nt(lhs_g[i], block_diag(b[i] * e_neg[i])) for i in un]
        l_k = [jnp.where(strict, x[:c_sz], 0.0) for x in g_k]
        a_rk = [jnp.where(incl, x[c_sz:], 0.0) for x in g_k]
        l_b = [jnp.where(strict, x[:c_sz], 0.0) for x in g_b]
        a_rb = [jnp.where(incl, x[c_sz:], 0.0) for x in g_b]
        x = [eye - jnp.where(levels[0], l, 0.0) for l in l_b]
        for lvl in levels[1:]:
            t = [_bdot(x[i], block_diag(jnp.where(lvl, l_b[i], 0.0))) for i in un]
            x = [x[i] - _bdot(t[i], block_diag(x[i])) for i in un]
        bd_v = [block_diag(x) for x in v]
        lav = [_bdot(jnp.concatenate([l_k[i], a_rk[i]], axis=0), bd_v[i]) for i in un]
        lkv = [m[:c_sz] for m in lav]
        wt = [_bdot(x[i], block_diag(kk[i] * jnp.exp(cumx[i]))) for i in un]
        vt = [_bdot(x[i], block_diag(lkv[i])) for i in un]
        r_abs = [r[i] * jnp.exp(cum[i]) for i in un]
        e_end = [jnp.exp(total[i] - cum[i]) for i in un]
        kb_end = [jnp.concatenate([kdir[i] * e_end[i], -(b[i] * e_end[i])], axis=0) for i in un]
        state = [st[g] for g in grp]
        for cc in range(n_sub):
            ids = [cc * n_grp + g for g in grp]
            su = [_bdot_nt(jnp.concatenate([wt[i], r_abs[i]], axis=0), state[g]) for g, i in zip(grp, ids)]
            u = [su[g][:c_sz] + vt[i] for g, i in zip(grp, ids)]
            y = [su[g][c_sz:] + lav[i][c_sz:] - _bdot(a_rb[i], block_diag(u[g]))
                 for g, i in zip(grp, ids)]
            upd = [_bdot_tn(jnp.concatenate([v[i], u[g]], axis=0), kb_end[i]) for g, i in zip(grp, ids)]
            state = [state[g] * jnp.exp(total[i]) + jnp.where(bd_mask, upd[g], 0.0) for g, i in zip(grp, ids)]
            y_ref[rows_of[cc], :] = jnp.concatenate(y, axis=1)
        st[...] = jnp.stack(state)
        return carry

    lax.fori_loop(0, n_chunk // n_sub, chunks, 0)

    if emit_state:
        @pl.when(cb == n_blk - 1)
        def _():
            for g in grp:
                s = st[g]
                for h in range(hps):
                    so_ref[g * hps + h] = s[h * hd:(h + 1) * hd, h * hd:(h + 1) * hd]


def _rwkv_scan(rkv, lw, a, k_k, k_a, s0, *, row0, n_seq, seq_len, emit_state, heads_per_step=32):
    _, _, d_model = rkv.shape
    hd, sw = RWKV_HEAD_DIM, RWKV_SLAB
    n_heads = d_model // hd
    g = min(heads_per_step, n_heads)
    wg = g * hd
    n_grp = wg // sw
    n_hg = n_heads // g
    tb = min(256, seq_len)
    n_blk = seq_len // tb
    rb0 = row0 // tb

    def blk(c, d):
        return jnp.where(d == 0, c, n_blk - 1 - c)

    def rkv_map(part):
        return lambda s, d, hg, c: (part, rb0 + s * n_blk + blk(c, d), hg)

    dir_map = lambda s, d, hg, c: (d, rb0 + s * n_blk + blk(c, d), hg)
    in_specs = ([pl.BlockSpec((None, tb, wg), rkv_map(p)) for p in range(3)]
                + [pl.BlockSpec((None, tb, wg), dir_map)] * 2
                + [pl.BlockSpec((1, wg), lambda s, d, hg, c: (0, hg))] * 2)
    args = [rkv, rkv, rkv, lw, a, k_k.reshape(1, d_model), k_a.reshape(1, d_model)]
    if s0 is not None:
        in_specs.append(pl.BlockSpec((None, None, n_grp, sw, sw), lambda s, d, hg, c: (s, d, hg, 0, 0)))
        args.append(s0)
    out_specs = [pl.BlockSpec((None, tb, wg), lambda s, d, hg, c: (d, s * n_blk + blk(c, d), hg))]
    out_shape = [jax.ShapeDtypeStruct((2, n_seq * seq_len, d_model), F32)]
    if emit_state:
        out_specs.append(pl.BlockSpec((None, None, g, hd, hd), lambda s, d, hg, c: (s, d, hg, 0, 0)))
        out_shape.append(jax.ShapeDtypeStruct((n_seq, 2, n_heads, hd, hd), F32))
    body = functools.partial(_rwkv_scan_body, n_chunk=tb // RWKV_CHUNK, n_blk=n_blk,
                             has_s0=s0 is not None, emit_state=emit_state)
    return pl.pallas_call(
        body, grid=(n_seq, 2, n_hg, n_blk), in_specs=in_specs, out_specs=out_specs, out_shape=out_shape,
        scratch_shapes=[pltpu.VMEM((n_grp, sw, sw), F32)],
        compiler_params=_cparams(4), name="rwkv_scan")(*args)


def _group_sum(x, ones_bd):
    w = ones_bd.shape[0]
    out = []
    for c in range(x.shape[1] // w):
        xs = x[:, c * w:(c + 1) * w]
        hi = xs.astype(BF16)
        lo = (xs - hi.astype(F32)).astype(BF16)
        out.append(jnp.dot(hi, ones_bd, preferred_element_type=F32)
                   + jnp.dot(lo, ones_bd, preferred_element_type=F32))
    return jnp.concatenate(out, axis=1)


def _rwkv_finalize_body(yp_ref, ys_ref, rkv_ref, a_ref, g_ref, ka_ref, rk_ref, lnw_ref, lnb_ref, o_ref, *, pt):
    hd = RWKV_HEAD_DIM
    w = RWKV_SLAB
    ri = lax.broadcasted_iota(jnp.int32, (w, w), 0)
    ci = lax.broadcasted_iota(jnp.int32, (w, w), 1)
    ones_bd = jnp.where(ri // hd == ci // hd, 1.0, 0.0).astype(BF16)

    def finalize(y_ref):
        y = y_ref[0] + y_ref[1]
        mean = _group_sum(y, ones_bd) * (1.0 / hd)
        yc = y - mean
        var = _group_sum(yc * yc, ones_bd) * (1.0 / hd)
        yn = yc * lax.rsqrt(var + RWKV_LN_EPS) * lnw_ref[...] + lnb_ref[...]
        r, k, v = rkv_ref[0], rkv_ref[1], rkv_ref[2]
        a_sum = a_ref[0] + a_ref[1]
        k_sum = k * (2.0 + (a_sum - 2.0) * ka_ref[...])
        bonus = _group_sum(r * k_sum * rk_ref[...], ones_bd) * v
        o_ref[...] = ((yn + bonus) * g_ref[...]).astype(o_ref.dtype)

    _for_trunk_of_tile(pt, yp_ref, ys_ref, finalize)


def _rwkv_finalize(y_p, y_s, rkv, a, g, k_a, r_k, ln_w, ln_b, tm=256):
    n, d = y_p.shape[1] + y_s.shape[1], y_p.shape[2]
    specs, pt = _two_trunk_specs(y_p, y_s, tm)
    row = lambda i: (0, 0)
    return pl.pallas_call(
        functools.partial(_rwkv_finalize_body, pt=pt), grid=(n // tm,),
        in_specs=specs + [
                  pl.BlockSpec((3, tm, d), lambda i: (0, i, 0)),
                  pl.BlockSpec((2, tm, d), lambda i: (0, i, 0)),
                  pl.BlockSpec((tm, d), lambda i: (i, 0)),
                  pl.BlockSpec((1, d), row), pl.BlockSpec((1, d), row),
                  pl.BlockSpec((1, d), row), pl.BlockSpec((1, d), row)],
        out_specs=pl.BlockSpec((tm, d), lambda i: (i, 0)),
        out_shape=jax.ShapeDtypeStruct((n, d), BF16),
        compiler_params=_cparams(1), name="rwkv_finalize",
    )(y_p, y_s, rkv, a, g, k_a.reshape(1, d), r_k.reshape(1, d), ln_w.reshape(1, d), ln_b.reshape(1, d))


def _cast_expert_weights(be_ref, pairs):
    b = pl.program_id(1)

    @pl.when(jnp.logical_or(b == 0, be_ref[b] != be_ref[jnp.maximum(b - 1, 0)]))
    def _():
        for w_ref, w_bf in pairs:
            w_bf[...] = w_ref[...].astype(BF16)


def _moe_up_body(be_ref, na_ref, x_ref, wg_ref, wu_ref, o_ref, wg_bf, wu_bf):
    b = pl.program_id(1)
    _cast_expert_weights(be_ref, [(wg_ref, wg_bf), (wu_ref, wu_bf)])

    @pl.when(b < na_ref[0])
    def _():
        x = x_ref[...]
        gate = jnp.dot(x, wg_bf[...], preferred_element_type=F32)
        up = jnp.dot(x, wu_bf[...], preferred_element_type=F32)
        o_ref[...] = (_silu(gate) * up).astype(o_ref.dtype)

    @pl.when(b >= na_ref[0])
    def _():
        o_ref[...] = jnp.zeros_like(o_ref)


def _moe_down_body(be_ref, na_ref, *refs, n_parts):
    x_refs, w_refs = refs[:n_parts], refs[n_parts:2 * n_parts]
    o_ref = refs[2 * n_parts]
    w_bfs = refs[2 * n_parts + 1:]
    b = pl.program_id(1)
    _cast_expert_weights(be_ref, list(zip(w_refs, w_bfs)))

    @pl.when(b < na_ref[0])
    def _():
        o_ref[...] = sum(jnp.dot(x[...], w[...], preferred_element_type=F32) for x, w in zip(x_refs, w_bfs))

    @pl.when(b >= na_ref[0])
    def _():
        o_ref[...] = jnp.zeros_like(o_ref)


def _moe_up(xb, block_e, n_active, w_gate, w_up, col0, n_cols, tn):
    rows, d = xb.shape
    bm = MOE_BLOCK
    off = col0 // tn
    w_spec = pl.BlockSpec((None, d, tn), lambda j, b, be, na: (be[b], 0, off + j))
    return pl.pallas_call(
        _moe_up_body,
        grid_spec=pltpu.PrefetchScalarGridSpec(
            num_scalar_prefetch=2, grid=(n_cols // tn, rows // bm),
            in_specs=[pl.BlockSpec((bm, d), lambda j, b, be, na: (b, 0)), w_spec, w_spec],
            out_specs=pl.BlockSpec((bm, tn), lambda j, b, be, na: (b, j)),
            scratch_shapes=[pltpu.VMEM((d, tn), BF16)] * 2),
        out_shape=jax.ShapeDtypeStruct((rows, n_cols), BF16),
        compiler_params=_cparams(2), name="moe_up")(block_e, n_active, xb, w_gate, w_up)


def _moe_experts(xb, block_e, n_active, w_gate, w_up, w_down, tn_wide=1024, tn_narrow=512, tn_down=512):
    rows, d = xb.shape
    bm = MOE_BLOCK
    d_ff = w_gate.shape[2]
    wide = (d_ff // tn_wide) * tn_wide
    parts = []
    if wide:
        parts.append((_moe_up(xb, block_e, n_active, w_gate, w_up, 0, wide, tn_wide), 0))
    if d_ff > wide:
        tn = min(tn_narrow, d_ff - wide)
        assert (d_ff - wide) % tn == 0 and wide % (d_ff - wide) == 0, (d_ff, tn_wide, tn_narrow)
        parts.append((_moe_up(xb, block_e, n_active, w_gate, w_up, wide, d_ff - wide, tn), wide))
    tn_down = min(tn_down, d)
    x_specs = [pl.BlockSpec((bm, a.shape[1]), lambda j, b, be, na: (b, 0)) for a, _ in parts]
    w_specs = [pl.BlockSpec((None, a.shape[1], tn_down),
                            lambda j, b, be, na, blk=c0 // a.shape[1]: (be[b], blk, j)) for a, c0 in parts]
    return pl.pallas_call(
        functools.partial(_moe_down_body, n_parts=len(parts)),
        grid_spec=pltpu.PrefetchScalarGridSpec(
            num_scalar_prefetch=2, grid=(d // tn_down, rows // bm),
            in_specs=x_specs + w_specs,
            out_specs=pl.BlockSpec((bm, tn_down), lambda j, b, be, na: (b, j)),
            scratch_shapes=[pltpu.VMEM((a.shape[1], tn_down), BF16) for a, _ in parts]),
        out_shape=jax.ShapeDtypeStruct((rows, d), F32),
        compiler_params=_cparams(2), name="moe_down",
    )(block_e, n_active, *[a for a, _ in parts], *([w_down] * len(parts)))


def _row_copy(src_hbm, src_row, dst, dst_row, sem):
    return pltpu.make_async_copy(src_hbm.at[pl.ds(src_row, 1), :], dst.at[pl.ds(dst_row, 1), :], sem)


def _prefetched_rows(fetch, drain):
    i, n = pl.program_id(0), pl.num_programs(0)
    buf = i % 2

    @pl.when(i == 0)
    def _():
        fetch(0, False)

    @pl.when(i + 1 < n)
    def _():
        fetch(1 - buf, True)

    drain(buf)
    return buf


def _moe_gather_body(tok_ref, tok_next_ref, h_hbm, o_ref, rows, sems, *, bm):
    def fetch(buf, is_next):
        idx_ref = tok_next_ref if is_next else tok_ref

        def start(r, carry):
            _row_copy(h_hbm, idx_ref[0, r], rows.at[buf], r, sems.at[buf]).start()
            return carry

        lax.fori_loop(0, bm, start, 0, unroll=8)

    def drain(buf):
        def wait(r, carry):
            _row_copy(h_hbm, 0, rows.at[buf], r, sems.at[buf]).wait()
            return carry

        lax.fori_loop(0, bm, wait, 0, unroll=8)

    buf = _prefetched_rows(fetch, drain)
    o_ref[...] = rows[buf].astype(o_ref.dtype)


def _moe_gather(h, slot_tok):
    _, d = h.shape
    bm = MOE_BLOCK
    n_blk = slot_tok.shape[0] // bm
    tok = slot_tok.reshape(n_blk, 1, bm)
    return pl.pallas_call(
        functools.partial(_moe_gather_body, bm=bm), grid=(n_blk,),
        in_specs=[pl.BlockSpec((None, 1, bm), lambda b: (b, 0, 0), memory_space=pltpu.SMEM),
                  pl.BlockSpec((None, 1, bm), lambda b: (jnp.minimum(b + 1, n_blk - 1), 0, 0),
                               memory_space=pltpu.SMEM),
                  pl.BlockSpec(memory_space=pl.ANY)],
        out_specs=pl.BlockSpec((bm, d), lambda b: (b, 0)),
        out_shape=jax.ShapeDtypeStruct((n_blk * bm, d), BF16),
        scratch_shapes=[pltpu.VMEM((2, bm, d), h.dtype), pltpu.SemaphoreType.DMA((2,))],
        compiler_params=_cparams(1), name="moe_gather")(tok, tok, h)


def _moe_combine_body(slot_ref, slot_next_ref, x_ref, g_ref, gt_ref, fw_ref, yb_hbm, op_ref, os_ref, ybuf, sems,
                      *, tm, pt):
    def fetch(buf, is_next):
        idx_ref = slot_next_ref if is_next else slot_ref

        def start(r, carry):
            for k in range(TOP_K):
                _row_copy(yb_hbm, idx_ref[0, TOP_K * r + k], ybuf.at[buf, k], r, sems.at[buf]).start()
            return carry

        lax.fori_loop(0, tm, start, 0, unroll=4)

    def drain(buf):
        def wait(r, carry):
            for k in range(TOP_K):
                _row_copy(yb_hbm, 0, ybuf.at[buf, k], r, sems.at[buf]).wait()
            return carry

        lax.fori_loop(0, tm, wait, 0, unroll=4)

    buf = _prefetched_rows(fetch, drain)
    g = g_ref[...]
    y = ybuf[buf, 0] * g[:, 0:1] + ybuf[buf, 1] * g[:, 1:2]
    x = x_ref[...] + gt_ref[...] * y
    out = x * lax.rsqrt(jnp.mean(x * x, axis=-1, keepdims=True) + NORM_EPS) * fw_ref[...]

    def store(o_ref):
        o_ref[...] = out

    _for_trunk_of_tile(pt, op_ref, os_ref, store)


def _moe_combine_final(x, yb, slot_of, gates, mod5, layer, final_w, lay, tm=256):
    n, d = x.shape
    tm = min(tm, lay.s_len, lay.p_rows)
    pt = lay.p_rows // tm
    row = pl.BlockSpec((tm, d), lambda i: (i, 0))
    n_tiles = n // tm
    slots = slot_of.reshape(n_tiles, 1, TOP_K * tm)
    return pl.pallas_call(
        functools.partial(_moe_combine_body, tm=tm, pt=pt), grid=(n_tiles,),
        in_specs=[pl.BlockSpec((None, 1, TOP_K * tm), lambda i: (i, 0, 0), memory_space=pltpu.SMEM),
                  pl.BlockSpec((None, 1, TOP_K * tm), lambda i: (jnp.minimum(i + 1, n_tiles - 1), 0, 0),
                               memory_space=pltpu.SMEM),
                  row, pl.BlockSpec((tm, TOP_K), lambda i: (i, 0)),
                  pl.BlockSpec((None, None, None, 1, d), _mod_spec(layer, 5, tm, lay)),
                  pl.BlockSpec((1, d), lambda i: (0, 0)),
                  pl.BlockSpec(memory_space=pl.ANY)],
        out_specs=[pl.BlockSpec((tm, d), lambda i: (jnp.minimum(i, pt - 1), 0)),
                   pl.BlockSpec((tm, d), lambda i: (jnp.maximum(i - pt, 0), 0))],
        out_shape=[jax.ShapeDtypeStruct((lay.p_rows, d), F32),
                   jax.ShapeDtypeStruct((n - lay.p_rows, d), F32)],
        scratch_shapes=[pltpu.VMEM((2, TOP_K, tm, d), F32), pltpu.SemaphoreType.DMA((2,))],
        compiler_params=_cparams(1), name="moe_combine_final",
    )(slots, slots, x, gates, mod5, final_w.reshape(1, d), yb)


def _moe(h, logits, w_gate, w_up, w_down):
    n_tok, d = h.shape
    bm = MOE_BLOCK
    top_val, top_idx = lax.top_k(logits, TOP_K)
    gates = jax.nn.softmax(top_val, axis=-1)
    n_assign = n_tok * TOP_K
    flat_e = top_idx.reshape(-1).astype(jnp.int32)
    order = jnp.argsort(flat_e).astype(jnp.int32)
    rank = jnp.argsort(order).astype(jnp.int32)
    counts = jnp.sum(flat_e[:, None] == jnp.arange(N_EXPERTS, dtype=jnp.int32)[None, :], axis=0,
                     dtype=jnp.int32)
    padded = ((counts + bm - 1) // bm) * bm
    pad_end = jnp.cumsum(padded)
    pad_start = pad_end - padded
    start = jnp.cumsum(counts) - counts
    n_blocks = -(-n_assign // bm) + N_EXPERTS
    block_e = jnp.minimum(jnp.searchsorted(pad_end, jnp.arange(n_blocks) * bm, side='right'),
                          N_EXPERTS - 1).astype(jnp.int32)
    n_active = (pad_end[-1] // bm).astype(jnp.int32).reshape(1)
    experts = jnp.arange(N_EXPERTS, dtype=jnp.int32)

    def lookup(table, e):
        return jnp.sum(jnp.where(e[..., None] == experts, table, 0), axis=-1, dtype=jnp.int32)

    slot = jnp.arange(n_blocks * bm, dtype=jnp.int32)
    slot_e = jnp.repeat(block_e, bm)
    pos = slot - lookup(pad_start, slot_e)
    src = jnp.clip(lookup(start, slot_e) + pos, 0, n_assign - 1)
    slot_tok = jnp.where(pos < lookup(counts, slot_e),
                         jnp.take(order, src, indices_are_sorted=True) // TOP_K, slot % n_tok)
    yb = _moe_experts(_moe_gather(h, slot_tok), block_e, n_active, w_gate, w_up, w_down)
    slot_of = (lookup(pad_start - start, flat_e) + rank).reshape(n_tok, TOP_K)
    return yb, slot_of, gates


def _rope_tables(seq_len, dim):
    t = jnp.arange(seq_len)
    row = (t // GRID_W).astype(F32)
    col = (t % GRID_W).astype(F32)
    half = dim // 2
    inv = ROPE_BASE ** (-(jnp.arange(0, half, 2, dtype=F32) / half))
    ang = jnp.concatenate([row[:, None] * inv, col[:, None] * inv], axis=-1)
    cos, sin = jnp.cos(ang), jnp.sin(ang)
    return jnp.repeat(cos, 2, axis=-1), jnp.stack([-sin, sin], axis=-1).reshape(seq_len, dim)


def _block_diag_states(s):
    b, two, n_heads, hd, _ = s.shape
    hps = RWKV_SLAB // hd
    s6 = s.reshape(b, two, n_heads // hps, hps, hd, hd)
    bd = jnp.einsum('bdghvk,hi->bdghvik', s6, jnp.eye(hps, dtype=s.dtype))
    return bd.reshape(b, two, n_heads // hps, RWKV_SLAB, RWKV_SLAB)


def _pad_cols(w, width):
    return jnp.pad(w, ((0, 0), (0, width - w.shape[1])))


def kernel(x_prompt, x_sample, state_l0_ret, state_l1_rwkv, c, c_ctx, ada_w, ada_b, norm_w, final_norm_w,
           l0_ret_w_in, l0_ret_w_out, l0_ret_decay, l0_ffn_w_gate, l0_ffn_w_up, l0_ffn_w_down,
           l1_rwkv_mu, l1_rwkv_w_rkv, l1_rwkv_w0, l1_rwkv_w1, l1_rwkv_w2, l1_rwkv_a0, l1_rwkv_a1, l1_rwkv_a2,
           l1_rwkv_g1, l1_rwkv_g2, l1_rwkv_k_k, l1_rwkv_k_a, l1_rwkv_r_k, l1_rwkv_ln_w, l1_rwkv_ln_b,
           l1_rwkv_w_out, l1_moe_router, l1_moe_w_gate, l1_moe_w_up, l1_moe_w_down):
    pb, p_len, d = x_prompt.shape
    sb, s_len, _ = x_sample.shape
    p_rows, s_rows = pb * p_len, sb * s_len
    lay = _Layout(p_rows, p_len, s_len, p_rows + s_rows)
    n_layers = ada_w.shape[0]

    x = (x_prompt.reshape(p_rows, d), x_sample.reshape(s_rows, d))
    cond8 = jnp.concatenate([c_ctx[None, :], c, jnp.zeros((8 - 1 - sb, d), F32)], axis=0)
    mod5 = _ada_mod(cond8, ada_w, ada_b)[:, :1 + sb].reshape(n_layers, 1 + sb, 6, 1, d)

    h = _norm_mod(x, norm_w[0, 0], mod5, 0, 1, 0, lay)
    qkvg = _linear(h, l0_ret_w_in, tn=1024)
    log_gamma = -jnp.exp(l0_ret_decay.astype(F32))
    dh = d // RET_HEADS
    o_p, new_state_l0_ret = _retention(qkvg, log_gamma, None, None, row0=0, n_seq=pb, seq_len=p_len,
                                       emit_state=True)
    (o_s,) = _retention(qkvg, log_gamma, state_l0_ret, _rope_tables(s_len, dh), row0=p_rows, n_seq=sb,
                        seq_len=s_len, emit_state=False)
    x = _linear_residual(_ret_finalize(o_p, o_s, qkvg), l0_ret_w_out, x, mod5, 0, 2, lay)
    h = _norm_mod(x, norm_w[0, 1], mod5, 0, 4, 3, lay)
    act = _swiglu_up(h, l0_ffn_w_gate, l0_ffn_w_up)
    x = _linear_residual(act, l0_ffn_w_down, x, mod5, 0, 5, lay, tm=512, tn=512)

    xs = _rwkv_mix(x, norm_w[1, 0], mod5, 1, l1_rwkv_mu, lay)
    rkv = _linear(xs, l1_rwkv_w_rkv, n_batch=3, tn=1024)
    lora_w = l1_rwkv_w1.shape[2]
    pad_w = -(-lora_w // 128) * 128
    w1p = jnp.concatenate([_pad_cols(l1_rwkv_w1[0], pad_w), _pad_cols(l1_rwkv_w1[1], pad_w)], axis=1)
    a1p = jnp.concatenate([_pad_cols(l1_rwkv_a1[0], pad_w), _pad_cols(l1_rwkv_a1[1], pad_w)], axis=1)
    t_w = _linear(xs, w1p, x_lead=3, act="tanh", out_dtype=BF16)
    t_a = _linear(xs, a1p, x_lead=4, out_dtype=BF16)
    t_g = _linear(xs, l1_rwkv_g1, x_lead=5, act="sigmoid", out_dtype=BF16)
    w2p = jnp.pad(l1_rwkv_w2, ((0, 0), (0, pad_w - lora_w), (0, 0)))
    a2p = jnp.pad(l1_rwkv_a2, ((0, 0), (0, pad_w - lora_w), (0, 0)))
    lw = _linear(t_w, w2p, n_batch=2, bias=l1_rwkv_w0, act="log_decay")
    a_lr = _linear(t_a, a2p, n_batch=2, bias=l1_rwkv_a0, act="sigmoid")
    g = _linear(t_g, l1_rwkv_g2)
    scan_args = (rkv, lw, a_lr, l1_rwkv_k_k, l1_rwkv_k_a)
    y_p, new_state_l1_rwkv = _rwkv_scan(*scan_args, None, row0=0, n_seq=pb, seq_len=p_len, emit_state=True)
    (y_s,) = _rwkv_scan(*scan_args, _block_diag_states(state_l1_rwkv), row0=p_rows, n_seq=sb, seq_len=s_len,
                        emit_state=False)
    yg = _rwkv_finalize(y_p, y_s, rkv, a_lr, g, l1_rwkv_k_a, l1_rwkv_r_k, l1_rwkv_ln_w, l1_rwkv_ln_b)
    x = _linear_residual(yg, l1_rwkv_w_out, x, mod5, 1, 2, lay)
    ne_pad = 128
    h, logits = _norm_mod(x, norm_w[1, 1], mod5, 1, 4, 3, lay, router_w=_pad_cols(l1_moe_router, ne_pad))
    yb, slot_of, gates = _moe(h, logits[:, :N_EXPERTS], l1_moe_w_gate, l1_moe_w_up, l1_moe_w_down)
    y_prompt, y_sample = _moe_combine_final(x, yb, slot_of, gates, mod5, 1, final_norm_w, lay)
    return (y_prompt.reshape(pb, p_len, d), y_sample.reshape(sb, s_len, d), new_state_l0_ret, new_state_l1_rwkv)
```

```python
import functools
from typing import NamedTuple

import jax
import jax.numpy as jnp
from jax import lax
from jax.experimental import pallas as pl
from jax.experimental.pallas import tpu as pltpu

F32 = jnp.float32
BF16 = jnp.bfloat16

GRID_W = 64
RET_HEADS = 8
RET_CHUNK = 128
RET_HEADS_PER_STEP = 8
ROPE_BASE = 10000.0
RWKV_HEAD_DIM = 64
RWKV_CHUNK = 64
RWKV_SLAB = 256
RWKV_CHUNKS_PER_ITER = 4
RWKV_LN_EPS = 64e-5
LOG_DECAY_SCALE = 0.6065306597126334
N_EXPERTS = 8
TOP_K = 2
MOE_BLOCK = 512
NORM_EPS = 1e-6

VMEM_LIMIT_BYTES = 60 * 1024 * 1024


class _Layout(NamedTuple):
    p_rows: int
    p_len: int
    s_len: int
    n_rows: int


def _cparams(n_axes):
    return pltpu.CompilerParams(dimension_semantics=("arbitrary",) * n_axes,
                                vmem_limit_bytes=VMEM_LIMIT_BYTES)


def _cond_of_tile(i, tm, lay):
    pt = lay.p_rows // tm
    st = lay.s_len // tm
    return jnp.where(i < pt, 0, 1 + (i - pt) // st)


def _bdot(a, b):
    return jnp.dot(a.astype(BF16), b.astype(BF16), preferred_element_type=F32)


def _bdot_nt(a, b):
    return lax.dot_general(a.astype(BF16), b.astype(BF16), (((1,), (1,)), ((), ())),
                           preferred_element_type=F32)


def _bdot_tn(a, b):
    return lax.dot_general(a.astype(BF16), b.astype(BF16), (((0,), (0,)), ((), ())),
                           preferred_element_type=F32)


def _silu(x):
    return x * jax.nn.sigmoid(x)


def _ada_body(c_ref, w_ref, b_ref, o_ref):
    o_ref[...] = _bdot(_silu(c_ref[...]), w_ref[...]) + b_ref[...]


def _ada_mod(cond8, ada_w, ada_b):
    n_layers, d, d6 = ada_w.shape
    tn = min(1024, d6)
    return pl.pallas_call(
        _ada_body,
        grid=(n_layers, d6 // tn),
        in_specs=[pl.BlockSpec((8, d), lambda l, j: (0, 0)),
                  pl.BlockSpec((None, d, tn), lambda l, j: (l, 0, j)),
                  pl.BlockSpec((None, 1, tn), lambda l, j: (l, 0, j))],
        out_specs=pl.BlockSpec((None, 8, tn), lambda l, j: (l, 0, j)),
        out_shape=jax.ShapeDtypeStruct((n_layers, 8, d6), F32),
        compiler_params=_cparams(2), name="ada_mod",
    )(cond8, ada_w, ada_b.reshape(n_layers, 1, d6))


def _mod_spec(layer, which, tm, lay, n_grid_axes=1):
    def imap(i, *_):
        return (layer, _cond_of_tile(i, tm, lay), which, 0, 0)
    return imap


def _rms_mod(x, nw, sc, sh):
    xn = x * lax.rsqrt(jnp.mean(x * x, axis=-1, keepdims=True) + NORM_EPS) * nw
    return xn * (1.0 + sc) + sh


def _norm_mod_body(x_ref, nw_ref, sc_ref, sh_ref, o_ref):
    o_ref[...] = _rms_mod(x_ref[...], nw_ref[...], sc_ref[...], sh_ref[...]).astype(o_ref.dtype)


def _norm_mod2_body(xp_ref, xs_ref, nw_ref, sc_ref, sh_ref, o_ref, *, pt):
    def norm(x_ref):
        o_ref[...] = _rms_mod(x_ref[...], nw_ref[...], sc_ref[...], sh_ref[...]).astype(o_ref.dtype)

    _for_trunk_of_tile(pt, xp_ref, xs_ref, norm)


def _norm_mod_router_body(x_ref, nw_ref, sc_ref, sh_ref, wr_ref, o_ref, lg_ref):
    h = _rms_mod(x_ref[...], nw_ref[...], sc_ref[...], sh_ref[...])
    o_ref[...] = h.astype(o_ref.dtype)
    lg_ref[...] = jnp.dot(h, wr_ref[...], preferred_element_type=F32, precision=lax.Precision.HIGHEST)


def _trunk_pair_specs(tm, tn, pt, n_grid_axes):
    if n_grid_axes == 1:
        return [pl.BlockSpec((tm, tn), lambda i: (jnp.minimum(i, pt - 1), 0)),
                pl.BlockSpec((tm, tn), lambda i: (jnp.maximum(i - pt, 0), 0))]
    return [pl.BlockSpec((tm, tn), lambda j, i: (jnp.minimum(i, pt - 1), j)),
            pl.BlockSpec((tm, tn), lambda j, i: (jnp.maximum(i - pt, 0), j))]


def _norm_mod(x, nw, mod5, layer, which_sc, which_sh, lay, router_w=None, tm=512):
    pair = isinstance(x, tuple)
    n, d = lay.n_rows, nw.shape[0]
    tm = min(tm, lay.s_len, lay.p_rows)
    mod_specs = [pl.BlockSpec((1, d), lambda i: (0, 0)),
                 pl.BlockSpec((None, None, None, 1, d), _mod_spec(layer, which_sc, tm, lay)),
                 pl.BlockSpec((None, None, None, 1, d), _mod_spec(layer, which_sh, tm, lay))]
    if pair:
        pt = lay.p_rows // tm
        return pl.pallas_call(
            functools.partial(_norm_mod2_body, pt=pt), grid=(n // tm,),
            in_specs=_trunk_pair_specs(tm, d, pt, 1) + mod_specs,
            out_specs=pl.BlockSpec((tm, d), lambda i: (i, 0)),
            out_shape=jax.ShapeDtypeStruct((n, d), BF16),
            compiler_params=_cparams(1), name="norm_mod")(*x, nw.reshape(1, d), mod5, mod5)
    in_specs = [pl.BlockSpec((tm, d), lambda i: (i, 0))] + mod_specs
    args = [x, nw.reshape(1, d), mod5, mod5]
    if router_w is None:
        return pl.pallas_call(
            _norm_mod_body, grid=(n // tm,), in_specs=in_specs,
            out_specs=pl.BlockSpec((tm, d), lambda i: (i, 0)),
            out_shape=jax.ShapeDtypeStruct((n, d), BF16),
            compiler_params=_cparams(1), name="norm_mod")(*args)
    ne = router_w.shape[1]
    return pl.pallas_call(
        _norm_mod_router_body, grid=(n // tm,),
        in_specs=in_specs + [pl.BlockSpec((d, ne), lambda i: (0, 0))],
        out_specs=[pl.BlockSpec((tm, d), lambda i: (i, 0)), pl.BlockSpec((tm, ne), lambda i: (i, 0))],
        out_shape=[jax.ShapeDtypeStruct((n, d), F32), jax.ShapeDtypeStruct((n, ne), F32)],
        compiler_params=_cparams(1), name="norm_mod_router")(*args, router_w)


def _cast_weights_once(row_axis, pairs):
    @pl.when(pl.program_id(row_axis) == 0)
    def _():
        for w_ref, w_bf in pairs:
            w_bf[...] = w_ref[...].astype(BF16)


def _linear_body(x_ref, w_ref, *refs, act, has_bias):
    refs = list(refs)
    bias_ref = refs.pop(0) if has_bias else None
    o_ref, w_bf = refs
    _cast_weights_once(2, [(w_ref, w_bf)])
    acc = jnp.dot(x_ref[...].astype(BF16), w_bf[...], preferred_element_type=F32)
    if has_bias:
        acc = acc + bias_ref[...]
    if act == "tanh":
        acc = jnp.tanh(acc)
    elif act == "sigmoid":
        acc = jax.nn.sigmoid(acc)
    elif act == "log_decay":
        acc = -LOG_DECAY_SCALE * jax.nn.sigmoid(acc)
    o_ref[...] = acc.astype(o_ref.dtype)


def _linear(x, w, *, n_batch=None, x_lead=0, bias=None, act=None, out_dtype=F32, tm=1024, tn=512):
    k, n = w.shape[-2:]
    m = x.shape[-2]
    tm, tn = min(tm, m), min(tn, n)
    if x.ndim == 3:
        x_spec = pl.BlockSpec((None, tm, k), lambda b, j, i: (x_lead + b, i, 0))
    else:
        x_spec = pl.BlockSpec((tm, k), lambda b, j, i: (i, b))
    if w.ndim == 3:
        w_spec = pl.BlockSpec((None, k, tn), lambda b, j, i: (b, 0, j))
    else:
        w_spec = pl.BlockSpec((k, tn), lambda b, j, i: (0, j))
    if n_batch is None:
        out_spec = pl.BlockSpec((tm, tn), lambda b, j, i: (i, j))
        out_shape = jax.ShapeDtypeStruct((m, n), out_dtype)
    else:
        out_spec = pl.BlockSpec((None, tm, tn), lambda b, j, i: (b, i, j))
        out_shape = jax.ShapeDtypeStruct((n_batch, m, n), out_dtype)
    in_specs, args = [x_spec, w_spec], [x, w]
    if bias is not None:
        in_specs.append(pl.BlockSpec((None, 1, tn), lambda b, j, i: (b, 0, j)))
        args.append(bias.reshape(bias.shape[0], 1, n))
    return pl.pallas_call(
        functools.partial(_linear_body, act=act, has_bias=bias is not None),
        grid=(n_batch or 1, n // tn, m // tm), in_specs=in_specs,
        out_specs=out_spec, out_shape=out_shape,
        scratch_shapes=[pltpu.VMEM((k, tn), BF16)],
        compiler_params=_cparams(3), name="linear")(*args)


def _linear_res_body(x_ref, w_ref, *refs, pt):
    res_refs, (gt_ref, o_ref, w_bf) = refs[:-3], refs[-3:]
    _cast_weights_once(1, [(w_ref, w_bf)])
    y = gt_ref[...] * jnp.dot(x_ref[...], w_bf[...], preferred_element_type=F32)

    def add(res_ref):
        o_ref[...] = res_ref[...] + y

    if len(res_refs) == 1:
        add(res_refs[0])
    else:
        _for_trunk_of_tile(pt, *res_refs, add, row_axis=1)


def _linear_residual(x, w, res, mod5, layer, which_gate, lay, tm=1024, tn=1024):
    m, k = x.shape
    n = w.shape[1]
    tm, tn = min(tm, lay.s_len, lay.p_rows), min(tn, n)
    pt = lay.p_rows // tm

    def gmap(j, i):
        return (layer, _cond_of_tile(i, tm, lay), which_gate, 0, j)

    if isinstance(res, tuple):
        res_specs, res_args = _trunk_pair_specs(tm, tn, pt, 2), list(res)
    else:
        res_specs, res_args = [pl.BlockSpec((tm, tn), lambda j, i: (i, j))], [res]
    return pl.pallas_call(
        functools.partial(_linear_res_body, pt=pt), grid=(n // tn, m // tm),
        in_specs=[pl.BlockSpec((tm, k), lambda j, i: (i, 0)),
                  pl.BlockSpec((k, tn), lambda j, i: (0, j))] + res_specs
                 + [pl.BlockSpec((None, None, None, 1, tn), gmap)],
        out_specs=pl.BlockSpec((tm, tn), lambda j, i: (i, j)),
        out_shape=jax.ShapeDtypeStruct((m, n), F32),
        scratch_shapes=[pltpu.VMEM((k, tn), BF16)],
        compiler_params=_cparams(2), name="linear_residual")(x, w, *res_args, mod5)


def _swiglu_up_body(x_ref, wg_ref, wu_ref, o_ref, wg_bf, wu_bf):
    _cast_weights_once(1, [(wg_ref, wg_bf), (wu_ref, wu_bf)])
    x = x_ref[...]
    gate = jnp.dot(x, wg_bf[...], preferred_element_type=F32)
    up = jnp.dot(x, wu_bf[...], preferred_element_type=F32)
    o_ref[...] = (_silu(gate) * up).astype(o_ref.dtype)


def _swiglu_up(x, wg, wu, tm=1024, tn=512):
    m, k = x.shape
    n = wg.shape[1]
    tm, tn = min(tm, m), min(tn, n)
    return pl.pallas_call(
        _swiglu_up_body, grid=(n // tn, m // tm),
        in_specs=[pl.BlockSpec((tm, k), lambda j, i: (i, 0)),
                  pl.BlockSpec((k, tn), lambda j, i: (0, j)),
                  pl.BlockSpec((k, tn), lambda j, i: (0, j))],
        out_specs=pl.BlockSpec((tm, tn), lambda j, i: (i, j)),
        out_shape=jax.ShapeDtypeStruct((m, n), BF16),
        scratch_shapes=[pltpu.VMEM((k, tn), BF16)] * 2,
        compiler_params=_cparams(2), name="swiglu_up")(x, wg, wu)


def _rope(x, c, s):
    w = x.shape[-1]
    lane = lax.broadcasted_iota(jnp.int32, x.shape, x.ndim - 1)
    nxt = pltpu.roll(x, w - 1, axis=x.ndim - 1)
    prv = pltpu.roll(x, 1, axis=x.ndim - 1)
    return x * c + jnp.where(lane % 2 == 0, nxt, prv) * s


def _retention_body(lg_ref, *refs, n_chunk, n_blk, has_rope, has_s0, emit_state, scale):
    refs = list(refs)
    q_ref, k_ref, v_ref = refs[:3]
    refs = refs[3:]
    if has_rope:
        cos_ref, sin_ref = refs[:2]
        refs = refs[2:]
    if has_s0:
        s0_ref = refs.pop(0)
    o_ref = refs.pop(0)
    if emit_state:
        so_ref = refs.pop(0)
    st = refs.pop(0)
    c_sz = RET_CHUNK
    hp, dh = st.shape[0], st.shape[1]
    hg, d, cb = pl.program_id(1), pl.program_id(2), pl.program_id(3)
    heads = range(hp)
    lanes = [slice(h * dh, (h + 1) * dh) for h in heads]

    @pl.when(cb == 0)
    def _():
        st[...] = s0_ref[...] if has_s0 else jnp.zeros_like(st)

    lgv = [lg_ref[d, hg * hp + h] for h in heads]
    fwd = d == 0
    row = lax.broadcasted_iota(jnp.int32, (c_sz, c_sz), 0)
    col = lax.broadcasted_iota(jnp.int32, (c_sz, c_sz), 1)
    diff = jnp.where(fwd, row - col, col - row).astype(F32)
    intra = [jnp.where(diff >= 0, jnp.exp(jnp.maximum(diff, 0.0) * lg), 0.0) for lg in lgv]
    pos = lax.broadcasted_iota(jnp.int32, (c_sz, dh), 0)
    npos = jnp.where(fwd, pos, c_sz - 1 - pos).astype(F32)
    q_decay = [jnp.exp((npos + 1.0) * lg) for lg in lgv]
    k_decay = [jnp.exp((c_sz - 1.0 - npos) * lg) for lg in lgv]
    chunk_decay = [jnp.exp(jnp.full((1, dh), c_sz, F32) * lg) for lg in lgv]

    for j in range(n_chunk):
        jj = jnp.where(fwd, j, n_chunk - 1 - j)
        rows = pl.ds(pl.multiple_of(jj * c_sz, c_sz), c_sz)
        q = [q_ref[rows, sl] for sl in lanes]
        k = [k_ref[rows, sl] * scale for sl in lanes]
        v = [v_ref[rows, sl] for sl in lanes]
        if has_rope:
            c, s = cos_ref[rows, :], sin_ref[rows, :]
            q = [_rope(x, c, s) for x in q]
            k = [_rope(x, c, s) for x in k]
        state = [st[h] for h in heads]
        scores = [_bdot_nt(q[h], k[h]) * intra[h] for h in heads]
        o = [_bdot(scores[h], v[h]) + _bdot(q[h], state[h]) * q_decay[h] for h in heads]
        new_state = [state[h] * chunk_decay[h] + _bdot_tn(k[h] * k_decay[h], v[h]) for h in heads]
        o_ref[rows, :] = jnp.concatenate(o, axis=1)
        st[...] = jnp.stack(new_state)

    if emit_state:
        @pl.when(cb == n_blk - 1)
        def _():
            so_ref[...] = st[...]


def _retention(qkvg, log_gamma, s0, rope, *, row0, n_seq, seq_len, emit_state):
    d_model = qkvg.shape[1] // 4
    n_heads = RET_HEADS
    dh = d_model // n_heads
    tb = min(512, seq_len)
    n_blk = seq_len // tb
    rb0 = row0 // tb

    def blk(c, d):
        return jnp.where(d == 0, c, n_blk - 1 - c)

    hp = RET_HEADS_PER_STEP
    n_hg = n_heads // hp

    def in_map(part):
        return lambda s, h, d, c, lg: (rb0 + s * n_blk + blk(c, d), part * n_hg + h)

    in_specs = [pl.BlockSpec((tb, hp * dh), in_map(p)) for p in range(3)]
    args = [qkvg, qkvg, qkvg]
    if rope is not None:
        in_specs += [pl.BlockSpec((tb, dh), lambda s, h, d, c, lg: (blk(c, d), 0))] * 2
        args += list(rope)
    if s0 is not None:
        in_specs.append(pl.BlockSpec((None, None, hp, dh, dh), lambda s, h, d, c, lg: (s, d, h, 0, 0)))
        args.append(s0)
    out_specs = [pl.BlockSpec((None, tb, hp * dh), lambda s, h, d, c, lg: (d, s * n_blk + blk(c, d), h))]
    out_shape = [jax.ShapeDtypeStruct((2, n_seq * seq_len, d_model), F32)]
    if emit_state:
        out_specs.append(pl.BlockSpec((None, None, hp, dh, dh), lambda s, h, d, c, lg: (s, d, h, 0, 0)))
        out_shape.append(jax.ShapeDtypeStruct((n_seq, 2, n_heads, dh, dh), F32))
    body = functools.partial(_retention_body, n_chunk=tb // RET_CHUNK, n_blk=n_blk, has_rope=rope is not None,
                             has_s0=s0 is not None, emit_state=emit_state, scale=dh ** -0.5)
    return pl.pallas_call(
        body,
        grid_spec=pltpu.PrefetchScalarGridSpec(
            num_scalar_prefetch=1, grid=(n_seq, n_hg, 2, n_blk), in_specs=in_specs, out_specs=out_specs,
            scratch_shapes=[pltpu.VMEM((hp, dh, dh), F32)]),
        out_shape=out_shape, compiler_params=_cparams(4), name="retention")(log_gamma, *args)


def _two_trunk_specs(a_p, a_s, tm):
    pt = a_p.shape[1] // tm
    d = a_p.shape[2]
    return [pl.BlockSpec((2, tm, d), lambda i: (0, jnp.minimum(i, pt - 1), 0)),
            pl.BlockSpec((2, tm, d), lambda i: (0, jnp.maximum(i - pt, 0), 0))], pt


def _for_trunk_of_tile(pt, p_ref, s_ref, fn, row_axis=0):
    i = pl.program_id(row_axis)

    @pl.when(i < pt)
    def _():
        fn(p_ref)

    @pl.when(i >= pt)
    def _():
        fn(s_ref)


def _ret_finalize_body(op_ref, os_ref, g_ref, out_ref, *, n_heads, pt):
    def finalize(o_ref):
        o = o_ref[0] + o_ref[1]
        g = g_ref[...]
        dh = o.shape[1] // n_heads
        for h in range(n_heads):
            sl = slice(h * dh, (h + 1) * dh)
            oh = o[:, sl]
            oh = oh * lax.rsqrt(jnp.mean(oh * oh, axis=-1, keepdims=True) + NORM_EPS)
            out_ref[:, sl] = (oh * _silu(g[:, sl])).astype(out_ref.dtype)

    _for_trunk_of_tile(pt, op_ref, os_ref, finalize)


def _ret_finalize(o_p, o_s, qkvg, tm=256):
    n, d = o_p.shape[1] + o_s.shape[1], o_p.shape[2]
    specs, pt = _two_trunk_specs(o_p, o_s, tm)
    return pl.pallas_call(
        functools.partial(_ret_finalize_body, n_heads=RET_HEADS, pt=pt), grid=(n // tm,),
        in_specs=specs + [pl.BlockSpec((tm, d), lambda i: (i, 3))],
        out_specs=pl.BlockSpec((tm, d), lambda i: (i, 0)),
        out_shape=jax.ShapeDtypeStruct((n, d), BF16),
        compiler_params=_cparams(1), name="ret_finalize")(o_p, o_s, qkvg)


def _rwkv_mix_body(x_ref, xp_ref, xn_ref, nw_ref, sc_ref, sh_ref, mu_ref, o_ref, hext, *, tm, halo, lay):
    i = pl.program_id(0)
    d = x_ref.shape[1]
    nw, sc, sh = nw_ref[...], sc_ref[...], sh_ref[...]
    hext[0:halo, :] = _rms_mod(xp_ref[...], nw, sc, sh)
    hext[halo:halo + tm, :] = _rms_mod(x_ref[...], nw, sc, sh)
    hext[halo + tm:halo + tm + halo, :] = _rms_mod(xn_ref[...], nw, sc, sh)
    h = hext[halo:halo + tm, :]
    g_row = i * tm + lax.broadcasted_iota(jnp.int32, (tm, 1), 0)

    def emit(h_shift):
        diff = h_shift - h
        for n in range(6):
            o_ref[n] = (h + diff * mu_ref[n:n + 1, :]).astype(o_ref.dtype)

    def shifted(off, lo, hi, keep):
        return jnp.where(keep, hext[halo + off:halo + off + tm, lo:hi], 0.0)

    @pl.when(i < lay.p_rows // tm)
    def _():
        t = g_row % lay.p_len
        hd = d // 2
        emit(jnp.concatenate([shifted(-1, 0, hd, t != 0),
                              shifted(1, hd, d, t != lay.p_len - 1)], axis=1))

    @pl.when(i >= lay.p_rows // tm)
    def _():
        t = (g_row - lay.p_rows) % lay.s_len
        colw = t % GRID_W
        qd = d // 4
        emit(jnp.concatenate([shifted(-1, 0, qd, colw != 0),
                              shifted(1, qd, 2 * qd, colw != GRID_W - 1),
                              shifted(-GRID_W, 2 * qd, 3 * qd, t >= GRID_W),
                              shifted(GRID_W, 3 * qd, d, t < lay.s_len - GRID_W)], axis=1))


def _rwkv_mix(x, nw, mod5, layer, mu, lay, tm=512):
    n, d = x.shape
    halo = GRID_W
    tm = min(tm, lay.s_len, lay.p_rows)
    r = tm // halo
    n_halo_blk = n // halo
    body = functools.partial(_rwkv_mix_body, tm=tm, halo=halo, lay=lay)
    return pl.pallas_call(
        body, grid=(n // tm,),
        in_specs=[pl.BlockSpec((tm, d), lambda i: (i, 0)),
                  pl.BlockSpec((halo, d), lambda i: (jnp.maximum(i * r - 1, 0), 0)),
                  pl.BlockSpec((halo, d), lambda i: (jnp.minimum((i + 1) * r, n_halo_blk - 1), 0)),
                  pl.BlockSpec((1, d), lambda i: (0, 0)),
                  pl.BlockSpec((None, None, None, 1, d), _mod_spec(layer, 1, tm, lay)),
                  pl.BlockSpec((None, None, None, 1, d), _mod_spec(layer, 0, tm, lay)),
                  pl.BlockSpec((6, d), lambda i: (0, 0))],
        out_specs=pl.BlockSpec((6, tm, d), lambda i: (0, i, 0)),
        out_shape=jax.ShapeDtypeStruct((6, n, d), BF16),
        scratch_shapes=[pltpu.VMEM((tm + 2 * halo, d), F32)],
        compiler_params=_cparams(1), name="rwkv_mix")(x, x, x, nw.reshape(1, d), mod5, mod5, mu)


def _rwkv_scan_body(*refs, n_chunk, n_blk, has_s0, emit_state):
    refs = list(refs)
    r_ref, k_ref, v_ref, lw_ref, a_ref, kk_ref, ka_ref = refs[:7]
    refs = refs[7:]
    if has_s0:
        s0_ref = refs.pop(0)
    y_ref = refs.pop(0)
    if emit_state:
        so_ref = refs.pop(0)
    st = refs.pop(0)
    c_sz, hd, sw = RWKV_CHUNK, RWKV_HEAD_DIM, RWKV_SLAB
    n_grp = st.shape[0]
    hps = sw // hd
    d, cb = pl.program_id(1), pl.program_id(3)
    fwd = d == 0

    ri = lax.broadcasted_iota(jnp.int32, (sw, sw), 0)
    ci = lax.broadcasted_iota(jnp.int32, (sw, sw), 1)
    bd_mask = (ri // hd) == (ci // hd)

    def block_diag(slab):
        return jnp.where(bd_mask, jnp.concatenate([slab] * hps, axis=0), 0.0).astype(BF16)

    @pl.when(cb == 0)
    def _():
        if has_s0:
            st[...] = s0_ref[...]
        else:
            st[...] = jnp.zeros_like(st)

    srow = lax.broadcasted_iota(jnp.int32, (c_sz, sw), 0)
    scol = lax.broadcasted_iota(jnp.int32, (c_sz, sw), 1) % hd
    diff = jnp.where(fwd, srow - scol, scol - srow)
    strict = diff > 0
    incl = diff >= 0
    eye = jnp.where(diff == 0, 1.0, 0.0).astype(F32)
    levels = []
    m = 1
    while m < c_sz:
        levels.append(jnp.logical_and(srow // (2 * m) == scol // (2 * m), srow // m != scol // m))
        m *= 2
    trow = lax.broadcasted_iota(jnp.int32, (c_sz, c_sz), 0)
    tcol = lax.broadcasted_iota(jnp.int32, (c_sz, c_sz), 1)
    tri_incl = jnp.where(jnp.where(fwd, trow - tcol, tcol - trow) >= 0, 1.0, 0.0).astype(BF16)
    ones_bd = jnp.where(bd_mask, 1.0, 0.0).astype(BF16)

    def split(x, n_parts):
        parts = []
        for _ in range(n_parts - 1):
            p = x.astype(BF16)
            parts.append(p)
            x = x - p.astype(F32)
        return parts + [x.astype(BF16)]

    k_k, k_a = kk_ref[...], ka_ref[...]
    grp = range(n_grp)

    n_sub = min(RWKV_CHUNKS_PER_ITER, n_chunk)

    def chunks(j, carry):
        rows_of = []
        for cc in range(n_sub):
            jj = j * n_sub + cc
            jj = jnp.where(fwd, jj, n_chunk - 1 - jj)
            rows_of.append(pl.ds(pl.multiple_of(jj * c_sz, c_sz), c_sz))
        units = [(rows_of[cc], slice(g * sw, (g + 1) * sw)) for cc in range(n_sub) for g in grp]
        un = range(len(units))
        v = [v_ref[rows, sl] for rows, sl in units]
        kkr = [k_ref[rows, sl] * k_k[:, sl] for rows, sl in units]
        sq = jnp.concatenate([p for x in kkr for p in split(x * x, 2)], axis=0)
        sq = jnp.dot(sq, ones_bd, preferred_element_type=F32)
        ssum = [sq[2 * i * c_sz:(2 * i + 1) * c_sz] + sq[(2 * i + 1) * c_sz:(2 * i + 2) * c_sz] for i in un]
        kk = [x * lax.rsqrt(s + 1e-12) for x, s in zip(kkr, ssum)]
        a = [a_ref[rows, sl] for rows, sl in units]
        b = [x * y for x, y in zip(kk, a)]
        kdir = [k_ref[rows, sl] * (1.0 + (ai - 1.0) * k_a[:, sl]) for (rows, sl), ai in zip(units, a)]
        lw = [lw_ref[rows, sl] for rows, sl in units]
        cum = [sum(jnp.dot(tri_incl, p, preferred_element_type=F32) for p in split(x, 3)) for x in lw]
        total = [jnp.sum(x, axis=0, keepdims=True) for x in lw]
        half = [0.5 * t for t in total]
        cumx = [c - x for c, x in zip(cum, lw)]
        r = [r_ref[rows, sl] for rows, sl in units]
        lhs_g = [jnp.concatenate([kk[i] * jnp.exp(cumx[i] - half[i]), r[i] * jnp.exp(cum[i] - half[i])], axis=0)
                 for i in un]
        e_neg = [jnp.exp(half[i] - cum[i]) for i in un]
        g_k = [_bdot_nt(lhs_g[i], block_diag(kdir[i] * e_neg[i])) for i in un]
        g_b = [_bdot_nt(lhs_g[i], block_diag(b[i] * e_neg[i])) for i in un]
        l_k = [jnp.where(strict, x[:c_sz], 0.0) for x in g_k]
        a_rk = [jnp.where(incl, x[c_sz:], 0.0) for x in g_k]
        l_b = [jnp.where(strict, x[:c_sz], 0.0) for x in g_b]
        a_rb = [jnp.where(incl, x[c_sz:], 0.0) for x in g_b]
        x = [eye - jnp.where(levels[0], l, 0.0) for l in l_b]
        for lvl in levels[1:]:
            t = [_bdot(x[i], block_diag(jnp.where(lvl, l_b[i], 0.0))) for i in un]
            x = [x[i] - _bdot(t[i], block_diag(x[i])) for i in un]
        bd_v = [block_diag(x) for x in v]
        lav = [_bdot(jnp.concatenate([l_k[i], a_rk[i]], axis=0), bd_v[i]) for i in un]
        lkv = [m[:c_sz] for m in lav]
        wt = [_bdot(x[i], block_diag(kk[i] * jnp.exp(cumx[i]))) for i in un]
        vt = [_bdot(x[i], block_diag(lkv[i])) for i in un]
        r_abs = [r[i] * jnp.exp(cum[i]) for i in un]
        e_end = [jnp.exp(total[i] - cum[i]) for i in un]
        kb_end = [jnp.concatenate([kdir[i] * e_end[i], -(b[i] * e_end[i])], axis=0) for i in un]
        state = [st[g] for g in grp]
        for cc in range(n_sub):
            ids = [cc * n_grp + g for g in grp]
            su = [_bdot_nt(jnp.concatenate([wt[i], r_abs[i]], axis=0), state[g]) for g, i in zip(grp, ids)]
            u = [su[g][:c_sz] + vt[i] for g, i in zip(grp, ids)]
            y = [su[g][c_sz:] + lav[i][c_sz:] - _bdot(a_rb[i], block_diag(u[g]))
                 for g, i in zip(grp, ids)]
            upd = [_bdot_tn(jnp.concatenate([v[i], u[g]], axis=0), kb_end[i]) for g, i in zip(grp, ids)]
            state = [state[g] * jnp.exp(total[i]) + jnp.where(bd_mask, upd[g], 0.0) for g, i in zip(grp, ids)]
            y_ref[rows_of[cc], :] = jnp.concatenate(y, axis=1)
        st[...] = jnp.stack(state)
        return carry

    lax.fori_loop(0, n_chunk // n_sub, chunks, 0)

    if emit_state:
        @pl.when(cb == n_blk - 1)
        def _():
            for g in grp:
                s = st[g]
                for h in range(hps):
                    so_ref[g * hps + h] = s[h * hd:(h + 1) * hd, h * hd:(h + 1) * hd]


def _rwkv_scan(rkv, lw, a, k_k, k_a, s0, *, row0, n_seq, seq_len, emit_state, heads_per_step=32):
    _, _, d_model = rkv.shape
    hd, sw = RWKV_HEAD_DIM, RWKV_SLAB
    n_heads = d_model // hd
    g = min(heads_per_step, n_heads)
    wg = g * hd
    n_grp = wg // sw
    n_hg = n_heads // g
    tb = min(256, seq_len)
    n_blk = seq_len // tb
    rb0 = row0 // tb

    def blk(c, d):
        return jnp.where(d == 0, c, n_blk - 1 - c)

    def rkv_map(part):
        return lambda s, d, hg, c: (part, rb0 + s * n_blk + blk(c, d), hg)

    dir_map = lambda s, d, hg, c: (d, rb0 + s * n_blk + blk(c, d), hg)
    in_specs = ([pl.BlockSpec((None, tb, wg), rkv_map(p)) for p in range(3)]
                + [pl.BlockSpec((None, tb, wg), dir_map)] * 2
                + [pl.BlockSpec((1, wg), lambda s, d, hg, c: (0, hg))] * 2)
    args = [rkv, rkv, rkv, lw, a, k_k.reshape(1, d_model), k_a.reshape(1, d_model)]
    if s0 is not None:
        in_specs.append(pl.BlockSpec((None, None, n_grp, sw, sw), lambda s, d, hg, c: (s, d, hg, 0, 0)))
        args.append(s0)
    out_specs = [pl.BlockSpec((None, tb, wg), lambda s, d, hg, c: (d, s * n_blk + blk(c, d), hg))]
    out_shape = [jax.ShapeDtypeStruct((2, n_seq * seq_len, d_model), F32)]
    if emit_state:
        out_specs.append(pl.BlockSpec((None, None, g, hd, hd), lambda s, d, hg, c: (s, d, hg, 0, 0)))
        out_shape.append(jax.ShapeDtypeStruct((n_seq, 2, n_heads, hd, hd), F32))
    body = functools.partial(_rwkv_scan_body, n_chunk=tb // RWKV_CHUNK, n_blk=n_blk,
                             has_s0=s0 is not None, emit_state=emit_state)
    return pl.pallas_call(
        body, grid=(n_seq, 2, n_hg, n_blk), in_specs=in_specs, out_specs=out_specs, out_shape=out_shape,
        scratch_shapes=[pltpu.VMEM((n_grp, sw, sw), F32)],
        compiler_params=_cparams(4), name="rwkv_scan")(*args)


def _group_sum(x, ones_bd):
    w = ones_bd.shape[0]
    out = []
    for c in range(x.shape[1] // w):
        xs = x[:, c * w:(c + 1) * w]
        hi = xs.astype(BF16)
        lo = (xs - hi.astype(F32)).astype(BF16)
        out.append(jnp.dot(hi, ones_bd, preferred_element_type=F32)
                   + jnp.dot(lo, ones_bd, preferred_element_type=F32))
    return jnp.concatenate(out, axis=1)


def _rwkv_finalize_body(yp_ref, ys_ref, rkv_ref, a_ref, g_ref, ka_ref, rk_ref, lnw_ref, lnb_ref, o_ref, *, pt):
    hd = RWKV_HEAD_DIM
    w = RWKV_SLAB
    ri = lax.broadcasted_iota(jnp.int32, (w, w), 0)
    ci = lax.broadcasted_iota(jnp.int32, (w, w), 1)
    ones_bd = jnp.where(ri // hd == ci // hd, 1.0, 0.0).astype(BF16)

    def finalize(y_ref):
        y = y_ref[0] + y_ref[1]
        mean = _group_sum(y, ones_bd) * (1.0 / hd)
        yc = y - mean
        var = _group_sum(yc * yc, ones_bd) * (1.0 / hd)
        yn = yc * lax.rsqrt(var + RWKV_LN_EPS) * lnw_ref[...] + lnb_ref[...]
        r, k, v = rkv_ref[0], rkv_ref[1], rkv_ref[2]
        a_sum = a_ref[0] + a_ref[1]
        k_sum = k * (2.0 + (a_sum - 2.0) * ka_ref[...])
        bonus = _group_sum(r * k_sum * rk_ref[...], ones_bd) * v
        o_ref[...] = ((yn + bonus) * g_ref[...]).astype(o_ref.dtype)

    _for_trunk_of_tile(pt, yp_ref, ys_ref, finalize)


def _rwkv_finalize(y_p, y_s, rkv, a, g, k_a, r_k, ln_w, ln_b, tm=256):
    n, d = y_p.shape[1] + y_s.shape[1], y_p.shape[2]
    specs, pt = _two_trunk_specs(y_p, y_s, tm)
    row = lambda i: (0, 0)
    return pl.pallas_call(
        functools.partial(_rwkv_finalize_body, pt=pt), grid=(n // tm,),
        in_specs=specs + [
                  pl.BlockSpec((3, tm, d), lambda i: (0, i, 0)),
                  pl.BlockSpec((2, tm, d), lambda i: (0, i, 0)),
                  pl.BlockSpec((tm, d), lambda i: (i, 0)),
                  pl.BlockSpec((1, d), row), pl.BlockSpec((1, d), row),
                  pl.BlockSpec((1, d), row), pl.BlockSpec((1, d), row)],
        out_specs=pl.BlockSpec((tm, d), lambda i: (i, 0)),
        out_shape=jax.ShapeDtypeStruct((n, d), BF16),
        compiler_params=_cparams(1), name="rwkv_finalize",
    )(y_p, y_s, rkv, a, g, k_a.reshape(1, d), r_k.reshape(1, d), ln_w.reshape(1, d), ln_b.reshape(1, d))


def _cast_expert_weights(be_ref, pairs):
    b = pl.program_id(1)

    @pl.when(jnp.logical_or(b == 0, be_ref[b] != be_ref[jnp.maximum(b - 1, 0)]))
    def _():
        for w_ref, w_bf in pairs:
            w_bf[...] = w_ref[...].astype(BF16)


def _moe_up_body(be_ref, na_ref, x_ref, wg_ref, wu_ref, o_ref, wg_bf, wu_bf):
    b = pl.program_id(1)
    _cast_expert_weights(be_ref, [(wg_ref, wg_bf), (wu_ref, wu_bf)])

    @pl.when(b < na_ref[0])
    def _():
        x = x_ref[...]
        gate = jnp.dot(x, wg_bf[...], preferred_element_type=F32)
        up = jnp.dot(x, wu_bf[...], preferred_element_type=F32)
        o_ref[...] = (_silu(gate) * up).astype(o_ref.dtype)

    @pl.when(b >= na_ref[0])
    def _():
        o_ref[...] = jnp.zeros_like(o_ref)


def _moe_down_body(be_ref, na_ref, *refs, n_parts):
    x_refs, w_refs = refs[:n_parts], refs[n_parts:2 * n_parts]
    o_ref = refs[2 * n_parts]
    w_bfs = refs[2 * n_parts + 1:]
    b = pl.program_id(1)
    _cast_expert_weights(be_ref, list(zip(w_refs, w_bfs)))

    @pl.when(b < na_ref[0])
    def _():
        o_ref[...] = sum(jnp.dot(x[...], w[...], preferred_element_type=F32) for x, w in zip(x_refs, w_bfs))

    @pl.when(b >= na_ref[0])
    def _():
        o_ref[...] = jnp.zeros_like(o_ref)


def _moe_up(xb, block_e, n_active, w_gate, w_up, col0, n_cols, tn):
    rows, d = xb.shape
    bm = MOE_BLOCK
    off = col0 // tn
    w_spec = pl.BlockSpec((None, d, tn), lambda j, b, be, na: (be[b], 0, off + j))
    return pl.pallas_call(
        _moe_up_body,
        grid_spec=pltpu.PrefetchScalarGridSpec(
            num_scalar_prefetch=2, grid=(n_cols // tn, rows // bm),
            in_specs=[pl.BlockSpec((bm, d), lambda j, b, be, na: (b, 0)), w_spec, w_spec],
            out_specs=pl.BlockSpec((bm, tn), lambda j, b, be, na: (b, j)),
            scratch_shapes=[pltpu.VMEM((d, tn), BF16)] * 2),
        out_shape=jax.ShapeDtypeStruct((rows, n_cols), BF16),
        compiler_params=_cparams(2), name="moe_up")(block_e, n_active, xb, w_gate, w_up)


def _moe_experts(xb, block_e, n_active, w_gate, w_up, w_down, tn_wide=1024, tn_narrow=512, tn_down=512):
    rows, d = xb.shape
    bm = MOE_BLOCK
    d_ff = w_gate.shape[2]
    wide = (d_ff // tn_wide) * tn_wide
    parts = []
    if wide:
        parts.append((_moe_up(xb, block_e, n_active, w_gate, w_up, 0, wide, tn_wide), 0))
    if d_ff > wide:
        tn = min(tn_narrow, d_ff - wide)
        assert (d_ff - wide) % tn == 0 and wide % (d_ff - wide) == 0, (d_ff, tn_wide, tn_narrow)
        parts.append((_moe_up(xb, block_e, n_active, w_gate, w_up, wide, d_ff - wide, tn), wide))
    tn_down = min(tn_down, d)
    x_specs = [pl.BlockSpec((bm, a.shape[1]), lambda j, b, be, na: (b, 0)) for a, _ in parts]
    w_specs = [pl.BlockSpec((None, a.shape[1], tn_down),
                            lambda j, b, be, na, blk=c0 // a.shape[1]: (be[b], blk, j)) for a, c0 in parts]
    return pl.pallas_call(
        functools.partial(_moe_down_body, n_parts=len(parts)),
        grid_spec=pltpu.PrefetchScalarGridSpec(
            num_scalar_prefetch=2, grid=(d // tn_down, rows // bm),
            in_specs=x_specs + w_specs,
            out_specs=pl.BlockSpec((bm, tn_down), lambda j, b, be, na: (b, j)),
            scratch_shapes=[pltpu.VMEM((a.shape[1], tn_down), BF16) for a, _ in parts]),
        out_shape=jax.ShapeDtypeStruct((rows, d), F32),
        compiler_params=_cparams(2), name="moe_down",
    )(block_e, n_active, *[a for a, _ in parts], *([w_down] * len(parts)))


def _row_copy(src_hbm, src_row, dst, dst_row, sem):
    return pltpu.make_async_copy(src_hbm.at[pl.ds(src_row, 1), :], dst.at[pl.ds(dst_row, 1), :], sem)


def _prefetched_rows(fetch, drain):
    i, n = pl.program_id(0), pl.num_programs(0)
    buf = i % 2

    @pl.when(i == 0)
    def _():
        fetch(0, False)

    @pl.when(i + 1 < n)
    def _():
        fetch(1 - buf, True)

    drain(buf)
    return buf


def _moe_gather_body(tok_ref, tok_next_ref, h_hbm, o_ref, rows, sems, *, bm):
    def fetch(buf, is_next):
        idx_ref = tok_next_ref if is_next else tok_ref

        def start(r, carry):
            _row_copy(h_hbm, idx_ref[0, r], rows.at[buf], r, sems.at[buf]).start()
            return carry

        lax.fori_loop(0, bm, start, 0, unroll=8)

    def drain(buf):
        def wait(r, carry):
            _row_copy(h_hbm, 0, rows.at[buf], r, sems.at[buf]).wait()
            return carry

        lax.fori_loop(0, bm, wait, 0, unroll=8)

    buf = _prefetched_rows(fetch, drain)
    o_ref[...] = rows[buf].astype(o_ref.dtype)


def _moe_gather(h, slot_tok):
    _, d = h.shape
    bm = MOE_BLOCK
    n_blk = slot_tok.shape[0] // bm
    tok = slot_tok.reshape(n_blk, 1, bm)
    return pl.pallas_call(
        functools.partial(_moe_gather_body, bm=bm), grid=(n_blk,),
        in_specs=[pl.BlockSpec((None, 1, bm), lambda b: (b, 0, 0), memory_space=pltpu.SMEM),
                  pl.BlockSpec((None, 1, bm), lambda b: (jnp.minimum(b + 1, n_blk - 1), 0, 0),
                               memory_space=pltpu.SMEM),
                  pl.BlockSpec(memory_space=pl.ANY)],
        out_specs=pl.BlockSpec((bm, d), lambda b: (b, 0)),
        out_shape=jax.ShapeDtypeStruct((n_blk * bm, d), BF16),
        scratch_shapes=[pltpu.VMEM((2, bm, d), h.dtype), pltpu.SemaphoreType.DMA((2,))],
        compiler_params=_cparams(1), name="moe_gather")(tok, tok, h)


def _moe_combine_body(slot_ref, slot_next_ref, x_ref, g_ref, gt_ref, fw_ref, yb_hbm, op_ref, os_ref, ybuf, sems,
                      *, tm, pt):
    def fetch(buf, is_next):
        idx_ref = slot_next_ref if is_next else slot_ref

        def start(r, carry):
            for k in range(TOP_K):
                _row_copy(yb_hbm, idx_ref[0, TOP_K * r + k], ybuf.at[buf, k], r, sems.at[buf]).start()
            return carry

        lax.fori_loop(0, tm, start, 0, unroll=4)

    def drain(buf):
        def wait(r, carry):
            for k in range(TOP_K):
                _row_copy(yb_hbm, 0, ybuf.at[buf, k], r, sems.at[buf]).wait()
            return carry

        lax.fori_loop(0, tm, wait, 0, unroll=4)

    buf = _prefetched_rows(fetch, drain)
    g = g_ref[...]
    y = ybuf[buf, 0] * g[:, 0:1] + ybuf[buf, 1] * g[:, 1:2]
    x = x_ref[...] + gt_ref[...] * y
    out = x * lax.rsqrt(jnp.mean(x * x, axis=-1, keepdims=True) + NORM_EPS) * fw_ref[...]

    def store(o_ref):
        o_ref[...] = out

    _for_trunk_of_tile(pt, op_ref, os_ref, store)


def _moe_combine_final(x, yb, slot_of, gates, mod5, layer, final_w, lay, tm=256):
    n, d = x.shape
    tm = min(tm, lay.s_len, lay.p_rows)
    pt = lay.p_rows // tm
    row = pl.BlockSpec((tm, d), lambda i: (i, 0))
    n_tiles = n // tm
    slots = slot_of.reshape(n_tiles, 1, TOP_K * tm)
    return pl.pallas_call(
        functools.partial(_moe_combine_body, tm=tm, pt=pt), grid=(n_tiles,),
        in_specs=[pl.BlockSpec((None, 1, TOP_K * tm), lambda i: (i, 0, 0), memory_space=pltpu.SMEM),
                  pl.BlockSpec((None, 1, TOP_K * tm), lambda i: (jnp.minimum(i + 1, n_tiles - 1), 0, 0),
                               memory_space=pltpu.SMEM),
                  row, pl.BlockSpec((tm, TOP_K), lambda i: (i, 0)),
                  pl.BlockSpec((None, None, None, 1, d), _mod_spec(layer, 5, tm, lay)),
                  pl.BlockSpec((1, d), lambda i: (0, 0)),
                  pl.BlockSpec(memory_space=pl.ANY)],
        out_specs=[pl.BlockSpec((tm, d), lambda i: (jnp.minimum(i, pt - 1), 0)),
                   pl.BlockSpec((tm, d), lambda i: (jnp.maximum(i - pt, 0), 0))],
        out_shape=[jax.ShapeDtypeStruct((lay.p_rows, d), F32),
                   jax.ShapeDtypeStruct((n - lay.p_rows, d), F32)],
        scratch_shapes=[pltpu.VMEM((2, TOP_K, tm, d), F32), pltpu.SemaphoreType.DMA((2,))],
        compiler_params=_cparams(1), name="moe_combine_final",
    )(slots, slots, x, gates, mod5, final_w.reshape(1, d), yb)


def _moe(h, logits, w_gate, w_up, w_down):
    n_tok, d = h.shape
    bm = MOE_BLOCK
    top_val, top_idx = lax.top_k(logits, TOP_K)
    gates = jax.nn.softmax(top_val, axis=-1)
    n_assign = n_tok * TOP_K
    flat_e = top_idx.reshape(-1).astype(jnp.int32)
    order = jnp.argsort(flat_e).astype(jnp.int32)
    rank = jnp.argsort(order).astype(jnp.int32)
    counts = jnp.sum(flat_e[:, None] == jnp.arange(N_EXPERTS, dtype=jnp.int32)[None, :], axis=0,
                     dtype=jnp.int32)
    padded = ((counts + bm - 1) // bm) * bm
    pad_end = jnp.cumsum(padded)
    pad_start = pad_end - padded
    start = jnp.cumsum(counts) - counts
    n_blocks = -(-n_assign // bm) + N_EXPERTS
    block_e = jnp.minimum(jnp.searchsorted(pad_end, jnp.arange(n_blocks) * bm, side='right'),
                          N_EXPERTS - 1).astype(jnp.int32)
    n_active = (pad_end[-1] // bm).astype(jnp.int32).reshape(1)
    experts = jnp.arange(N_EXPERTS, dtype=jnp.int32)

    def lookup(table, e):
        return jnp.sum(jnp.where(e[..., None] == experts, table, 0), axis=-1, dtype=jnp.int32)

    slot = jnp.arange(n_blocks * bm, dtype=jnp.int32)
    slot_e = jnp.repeat(block_e, bm)
    pos = slot - lookup(pad_start, slot_e)
    src = jnp.clip(lookup(start, slot_e) + pos, 0, n_assign - 1)
    slot_tok = jnp.where(pos < lookup(counts, slot_e),
                         jnp.take(order, src, indices_are_sorted=True) // TOP_K, slot % n_tok)
    yb = _moe_experts(_moe_gather(h, slot_tok), block_e, n_active, w_gate, w_up, w_down)
    slot_of = (lookup(pad_start - start, flat_e) + rank).reshape(n_tok, TOP_K)
    return yb, slot_of, gates


def _rope_tables(seq_len, dim):
    t = jnp.arange(seq_len)
    row = (t // GRID_W).astype(F32)
    col = (t % GRID_W).astype(F32)
    half = dim // 2
    inv = ROPE_BASE ** (-(jnp.arange(0, half, 2, dtype=F32) / half))
    ang = jnp.concatenate([row[:, None] * inv, col[:, None] * inv], axis=-1)
    cos, sin = jnp.cos(ang), jnp.sin(ang)
    return jnp.repeat(cos, 2, axis=-1), jnp.stack([-sin, sin], axis=-1).reshape(seq_len, dim)


def _block_diag_states(s):
    b, two, n_heads, hd, _ = s.shape
    hps = RWKV_SLAB // hd
    s6 = s.reshape(b, two, n_heads // hps, hps, hd, hd)
    bd = jnp.einsum('bdghvk,hi->bdghvik', s6, jnp.eye(hps, dtype=s.dtype))
    return bd.reshape(b, two, n_heads // hps, RWKV_SLAB, RWKV_SLAB)


def _pad_cols(w, width):
    return jnp.pad(w, ((0, 0), (0, width - w.shape[1])))


def kernel(x_prompt, x_sample, state_l0_ret, state_l1_rwkv, c, c_ctx, ada_w, ada_b, norm_w, final_norm_w,
           l0_ret_w_in, l0_ret_w_out, l0_ret_decay, l0_ffn_w_gate, l0_ffn_w_up, l0_ffn_w_down,
           l1_rwkv_mu, l1_rwkv_w_rkv, l1_rwkv_w0, l1_rwkv_w1, l1_rwkv_w2, l1_rwkv_a0, l1_rwkv_a1, l1_rwkv_a2,
           l1_rwkv_g1, l1_rwkv_g2, l1_rwkv_k_k, l1_rwkv_k_a, l1_rwkv_r_k, l1_rwkv_ln_w, l1_rwkv_ln_b,
           l1_rwkv_w_out, l1_moe_router, l1_moe_w_gate, l1_moe_w_up, l1_moe_w_down):
    pb, p_len, d = x_prompt.shape
    sb, s_len, _ = x_sample.shape
    p_rows, s_rows = pb * p_len, sb * s_len
    lay = _Layout(p_rows, p_len, s_len, p_rows + s_rows)
    n_layers = ada_w.shape[0]

    x = (x_prompt.reshape(p_rows, d), x_sample.reshape(s_rows, d))
    cond8 = jnp.concatenate([c_ctx[None, :], c, jnp.zeros((8 - 1 - sb, d), F32)], axis=0)
    mod5 = _ada_mod(cond8, ada_w, ada_b)[:, :1 + sb].reshape(n_layers, 1 + sb, 6, 1, d)

    h = _norm_mod(x, norm_w[0, 0], mod5, 0, 1, 0, lay)
    qkvg = _linear(h, l0_ret_w_in, tn=1024)
    log_gamma = -jnp.exp(l0_ret_decay.astype(F32))
    dh = d // RET_HEADS
    o_p, new_state_l0_ret = _retention(qkvg, log_gamma, None, None, row0=0, n_seq=pb, seq_len=p_len,
                                       emit_state=True)
    (o_s,) = _retention(qkvg, log_gamma, state_l0_ret, _rope_tables(s_len, dh), row0=p_rows, n_seq=sb,
                        seq_len=s_len, emit_state=False)
    x = _linear_residual(_ret_finalize(o_p, o_s, qkvg), l0_ret_w_out, x, mod5, 0, 2, lay)
    h = _norm_mod(x, norm_w[0, 1], mod5, 0, 4, 3, lay)
    act = _swiglu_up(h, l0_ffn_w_gate, l0_ffn_w_up)
    x = _linear_residual(act, l0_ffn_w_down, x, mod5, 0, 5, lay, tm=512, tn=512)

    xs = _rwkv_mix(x, norm_w[1, 0], mod5, 1, l1_rwkv_mu, lay)
    rkv = _linear(xs, l1_rwkv_w_rkv, n_batch=3, tn=1024)
    lora_w = l1_rwkv_w1.shape[2]
    pad_w = -(-lora_w // 128) * 128
    w1p = jnp.concatenate([_pad_cols(l1_rwkv_w1[0], pad_w), _pad_cols(l1_rwkv_w1[1], pad_w)], axis=1)
    a1p = jnp.concatenate([_pad_cols(l1_rwkv_a1[0], pad_w), _pad_cols(l1_rwkv_a1[1], pad_w)], axis=1)
    t_w = _linear(xs, w1p, x_lead=3, act="tanh", out_dtype=BF16)
    t_a = _linear(xs, a1p, x_lead=4, out_dtype=BF16)
    t_g = _linear(xs, l1_rwkv_g1, x_lead=5, act="sigmoid", out_dtype=BF16)
    w2p = jnp.pad(l1_rwkv_w2, ((0, 0), (0, pad_w - lora_w), (0, 0)))
    a2p = jnp.pad(l1_rwkv_a2, ((0, 0), (0, pad_w - lora_w), (0, 0)))
    lw = _linear(t_w, w2p, n_batch=2, bias=l1_rwkv_w0, act="log_decay")
    a_lr = _linear(t_a, a2p, n_batch=2, bias=l1_rwkv_a0, act="sigmoid")
    g = _linear(t_g, l1_rwkv_g2)
    scan_args = (rkv, lw, a_lr, l1_rwkv_k_k, l1_rwkv_k_a)
    y_p, new_state_l1_rwkv = _rwkv_scan(*scan_args, None, row0=0, n_seq=pb, seq_len=p_len, emit_state=True)
    (y_s,) = _rwkv_scan(*scan_args, _block_diag_states(state_l1_rwkv), row0=p_rows, n_seq=sb, seq_len=s_len,
                        emit_state=False)
    yg = _rwkv_finalize(y_p, y_s, rkv, a_lr, g, l1_rwkv_k_a, l1_rwkv_r_k, l1_rwkv_ln_w, l1_rwkv_ln_b)
    x = _linear_residual(yg, l1_rwkv_w_out, x, mod5, 1, 2, lay)
    ne_pad = 128
    h, logits = _norm_mod(x, norm_w[1, 1], mod5, 1, 4, 3, lay, router_w=_pad_cols(l1_moe_router, ne_pad))
    yb, slot_of, gates = _moe(h, logits[:, :N_EXPERTS], l1_moe_w_gate, l1_moe_w_up, l1_moe_w_down)
    y_prompt, y_sample = _moe_combine_final(x, yb, slot_of, gates, mod5, 1, final_norm_w, lay)
    return (y_prompt.reshape(pb, p_len, d), y_sample.reshape(sb, s_len, d), new_state_l0_ret, new_state_l1_rwkv)
```

```python
import functools
from typing import NamedTuple

import jax
import jax.numpy as jnp
from jax import lax
from jax.experimental import pallas as pl
from jax.experimental.pallas import tpu as pltpu

F32 = jnp.float32
BF16 = jnp.bfloat16

GRID_W = 64
RET_HEADS = 8
RET_CHUNK = 128
RET_HEADS_PER_STEP = 8
ROPE_BASE = 10000.0
RWKV_HEAD_DIM = 64
RWKV_CHUNK = 64
RWKV_SLAB = 256
RWKV_CHUNKS_PER_ITER = 4
RWKV_LN_EPS = 64e-5
LOG_DECAY_SCALE = 0.6065306597126334
N_EXPERTS = 8
TOP_K = 2
MOE_BLOCK = 512
NORM_EPS = 1e-6

VMEM_LIMIT_BYTES = 60 * 1024 * 1024


class _Layout(NamedTuple):
    p_rows: int
    p_len: int
    s_len: int
    n_rows: int


def _cparams(n_axes):
    return pltpu.CompilerParams(dimension_semantics=("arbitrary",) * n_axes,
                                vmem_limit_bytes=VMEM_LIMIT_BYTES)


def _cond_of_tile(i, tm, lay):
    pt = lay.p_rows // tm
    st = lay.s_len // tm
    return jnp.where(i < pt, 0, 1 + (i - pt) // st)


def _bdot(a, b):
    return jnp.dot(a.astype(BF16), b.astype(BF16), preferred_element_type=F32)


def _bdot_nt(a, b):
    return lax.dot_general(a.astype(BF16), b.astype(BF16), (((1,), (1,)), ((), ())),
                           preferred_element_type=F32)


def _bdot_tn(a, b):
    return lax.dot_general(a.astype(BF16), b.astype(BF16), (((0,), (0,)), ((), ())),
                           preferred_element_type=F32)


def _silu(x):
    return x * jax.nn.sigmoid(x)


def _ada_body(c_ref, w_ref, b_ref, o_ref):
    o_ref[...] = _bdot(_silu(c_ref[...]), w_ref[...]) + b_ref[...]


def _ada_mod(cond8, ada_w, ada_b):
    n_layers, d, d6 = ada_w.shape
    tn = min(1024, d6)
    return pl.pallas_call(
        _ada_body,
        grid=(n_layers, d6 // tn),
        in_specs=[pl.BlockSpec((8, d), lambda l, j: (0, 0)),
                  pl.BlockSpec((None, d, tn), lambda l, j: (l, 0, j)),
                  pl.BlockSpec((None, 1, tn), lambda l, j: (l, 0, j))],
        out_specs=pl.BlockSpec((None, 8, tn), lambda l, j: (l, 0, j)),
        out_shape=jax.ShapeDtypeStruct((n_layers, 8, d6), F32),
        compiler_params=_cparams(2), name="ada_mod",
    )(cond8, ada_w, ada_b.reshape(n_layers, 1, d6))


def _mod_spec(layer, which, tm, lay, n_grid_axes=1):
    def imap(i, *_):
        return (layer, _cond_of_tile(i, tm, lay), which, 0, 0)
    return imap


def _rms_mod(x, nw, sc, sh):
    xn = x * lax.rsqrt(jnp.mean(x * x, axis=-1, keepdims=True) + NORM_EPS) * nw
    return xn * (1.0 + sc) + sh


def _norm_mod_body(x_ref, nw_ref, sc_ref, sh_ref, o_ref):
    o_ref[...] = _rms_mod(x_ref[...], nw_ref[...], sc_ref[...], sh_ref[...]).astype(o_ref.dtype)


def _norm_mod2_body(xp_ref, xs_ref, nw_ref, sc_ref, sh_ref, o_ref, *, pt):
    def norm(x_ref):
        o_ref[...] = _rms_mod(x_ref[...], nw_ref[...], sc_ref[...], sh_ref[...]).astype(o_ref.dtype)

    _for_trunk_of_tile(pt, xp_ref, xs_ref, norm)


def _norm_mod_router_body(x_ref, nw_ref, sc_ref, sh_ref, wr_ref, o_ref, lg_ref):
    h = _rms_mod(x_ref[...], nw_ref[...], sc_ref[...], sh_ref[...])
    o_ref[...] = h.astype(o_ref.dtype)
    lg_ref[...] = jnp.dot(h, wr_ref[...], preferred_element_type=F32, precision=lax.Precision.HIGHEST)


def _trunk_pair_specs(tm, tn, pt, n_grid_axes):
    if n_grid_axes == 1:
        return [pl.BlockSpec((tm, tn), lambda i: (jnp.minimum(i, pt - 1), 0)),
                pl.BlockSpec((tm, tn), lambda i: (jnp.maximum(i - pt, 0), 0))]
    return [pl.BlockSpec((tm, tn), lambda j, i: (jnp.minimum(i, pt - 1), j)),
            pl.BlockSpec((tm, tn), lambda j, i: (jnp.maximum(i - pt, 0), j))]


def _norm_mod(x, nw, mod5, layer, which_sc, which_sh, lay, router_w=None, tm=512):
    pair = isinstance(x, tuple)
    n, d = lay.n_rows, nw.shape[0]
    tm = min(tm, lay.s_len, lay.p_rows)
    mod_specs = [pl.BlockSpec((1, d), lambda i: (0, 0)),
                 pl.BlockSpec((None, None, None, 1, d), _mod_spec(layer, which_sc, tm, lay)),
                 pl.BlockSpec((None, None, None, 1, d), _mod_spec(layer, which_sh, tm, lay))]
    if pair:
        pt = lay.p_rows // tm
        return pl.pallas_call(
            functools.partial(_norm_mod2_body, pt=pt), grid=(n // tm,),
            in_specs=_trunk_pair_specs(tm, d, pt, 1) + mod_specs,
            out_specs=pl.BlockSpec((tm, d), lambda i: (i, 0)),
            out_shape=jax.ShapeDtypeStruct((n, d), BF16),
            compiler_params=_cparams(1), name="norm_mod")(*x, nw.reshape(1, d), mod5, mod5)
    in_specs = [pl.BlockSpec((tm, d), lambda i: (i, 0))] + mod_specs
    args = [x, nw.reshape(1, d), mod5, mod5]
    if router_w is None:
        return pl.pallas_call(
            _norm_mod_body, grid=(n // tm,), in_specs=in_specs,
            out_specs=pl.BlockSpec((tm, d), lambda i: (i, 0)),
            out_shape=jax.ShapeDtypeStruct((n, d), BF16),
            compiler_params=_cparams(1), name="norm_mod")(*args)
    ne = router_w.shape[1]
    return pl.pallas_call(
        _norm_mod_router_body, grid=(n // tm,),
        in_specs=in_specs + [pl.BlockSpec((d, ne), lambda i: (0, 0))],
        out_specs=[pl.BlockSpec((tm, d), lambda i: (i, 0)), pl.BlockSpec((tm, ne), lambda i: (i, 0))],
        out_shape=[jax.ShapeDtypeStruct((n, d), F32), jax.ShapeDtypeStruct((n, ne), F32)],
        compiler_params=_cparams(1), name="norm_mod_router")(*args, router_w)


def _cast_weights_once(row_axis, pairs):
    @pl.when(pl.program_id(row_axis) == 0)
    def _():
        for w_ref, w_bf in pairs:
            w_bf[...] = w_ref[...].astype(BF16)


def _linear_body(x_ref, w_ref, *refs, act, has_bias):
    refs = list(refs)
    bias_ref = refs.pop(0) if has_bias else None
    o_ref, w_bf = refs
    _cast_weights_once(2, [(w_ref, w_bf)])
    acc = jnp.dot(x_ref[...].astype(BF16), w_bf[...], preferred_element_type=F32)
    if has_bias:
        acc = acc + bias_ref[...]
    if act == "tanh":
        acc = jnp.tanh(acc)
    elif act == "sigmoid":
        acc = jax.nn.sigmoid(acc)
    elif act == "log_decay":
        acc = -LOG_DECAY_SCALE * jax.nn.sigmoid(acc)
    o_ref[...] = acc.astype(o_ref.dtype)


def _linear(x, w, *, n_batch=None, x_lead=0, bias=None, act=None, out_dtype=F32, tm=1024, tn=512):
    k, n = w.shape[-2:]
    m = x.shape[-2]
    tm, tn = min(tm, m), min(tn, n)
    if x.ndim == 3:
        x_spec = pl.BlockSpec((None, tm, k), lambda b, j, i: (x_lead + b, i, 0))
    else:
        x_spec = pl.BlockSpec((tm, k), lambda b, j, i: (i, b))
    if w.ndim == 3:
        w_spec = pl.BlockSpec((None, k, tn), lambda b, j, i: (b, 0, j))
    else:
        w_spec = pl.BlockSpec((k, tn), lambda b, j, i: (0, j))
    if n_batch is None:
        out_spec = pl.BlockSpec((tm, tn), lambda b, j, i: (i, j))
        out_shape = jax.ShapeDtypeStruct((m, n), out_dtype)
    else:
        out_spec = pl.BlockSpec((None, tm, tn), lambda b, j, i: (b, i, j))
        out_shape = jax.ShapeDtypeStruct((n_batch, m, n), out_dtype)
    in_specs, args = [x_spec, w_spec], [x, w]
    if bias is not None:
        in_specs.append(pl.BlockSpec((None, 1, tn), lambda b, j, i: (b, 0, j)))
        args.append(bias.reshape(bias.shape[0], 1, n))
    return pl.pallas_call(
        functools.partial(_linear_body, act=act, has_bias=bias is not None),
        grid=(n_batch or 1, n // tn, m // tm), in_specs=in_specs,
        out_specs=out_spec, out_shape=out_shape,
        scratch_shapes=[pltpu.VMEM((k, tn), BF16)],
        compiler_params=_cparams(3), name="linear")(*args)


def _linear_res_body(x_ref, w_ref, *refs, pt):
    res_refs, (gt_ref, o_ref, w_bf) = refs[:-3], refs[-3:]
    _cast_weights_once(1, [(w_ref, w_bf)])
    y = gt_ref[...] * jnp.dot(x_ref[...], w_bf[...], preferred_element_type=F32)

    def add(res_ref):
        o_ref[...] = res_ref[...] + y

    if len(res_refs) == 1:
        add(res_refs[0])
    else:
        _for_trunk_of_tile(pt, *res_refs, add, row_axis=1)


def _linear_residual(x, w, res, mod5, layer, which_gate, lay, tm=1024, tn=1024):
    m, k = x.shape
    n = w.shape[1]
    tm, tn = min(tm, lay.s_len, lay.p_rows), min(tn, n)
    pt = lay.p_rows // tm

    def gmap(j, i):
        return (layer, _cond_of_tile(i, tm, lay), which_gate, 0, j)

    if isinstance(res, tuple):
        res_specs, res_args = _trunk_pair_specs(tm, tn, pt, 2), list(res)
    else:
        res_specs, res_args = [pl.BlockSpec((tm, tn), lambda j, i: (i, j))], [res]
    return pl.pallas_call(
        functools.partial(_linear_res_body, pt=pt), grid=(n // tn, m // tm),
        in_specs=[pl.BlockSpec((tm, k), lambda j, i: (i, 0)),
                  pl.BlockSpec((k, tn), lambda j, i: (0, j))] + res_specs
                 + [pl.BlockSpec((None, None, None, 1, tn), gmap)],
        out_specs=pl.BlockSpec((tm, tn), lambda j, i: (i, j)),
        out_shape=jax.ShapeDtypeStruct((m, n), F32),
        scratch_shapes=[pltpu.VMEM((k, tn), BF16)],
        compiler_params=_cparams(2), name="linear_residual")(x, w, *res_args, mod5)


def _swiglu_up_body(x_ref, wg_ref, wu_ref, o_ref, wg_bf, wu_bf):
    _cast_weights_once(1, [(wg_ref, wg_bf), (wu_ref, wu_bf)])
    x = x_ref[...]
    gate = jnp.dot(x, wg_bf[...], preferred_element_type=F32)
    up = jnp.dot(x, wu_bf[...], preferred_element_type=F32)
    o_ref[...] = (_silu(gate) * up).astype(o_ref.dtype)


def _swiglu_up(x, wg, wu, tm=1024, tn=512):
    m, k = x.shape
    n = wg.shape[1]
    tm, tn = min(tm, m), min(tn, n)
    return pl.pallas_call(
        _swiglu_up_body, grid=(n // tn, m // tm),
        in_specs=[pl.BlockSpec((tm, k), lambda j, i: (i, 0)),
                  pl.BlockSpec((k, tn), lambda j, i: (0, j)),
                  pl.BlockSpec((k, tn), lambda j, i: (0, j))],
        out_specs=pl.BlockSpec((tm, tn), lambda j, i: (i, j)),
        out_shape=jax.ShapeDtypeStruct((m, n), BF16),
        scratch_shapes=[pltpu.VMEM((k, tn), BF16)] * 2,
        compiler_params=_cparams(2), name="swiglu_up")(x, wg, wu)


def _rope(x, c, s):
    w = x.shape[-1]
    lane = lax.broadcasted_iota(jnp.int32, x.shape, x.ndim - 1)
    nxt = pltpu.roll(x, w - 1, axis=x.ndim - 1)
    prv = pltpu.roll(x, 1, axis=x.ndim - 1)
    return x * c + jnp.where(lane % 2 == 0, nxt, prv) * s


def _retention_body(lg_ref, *refs, n_chunk, n_blk, has_rope, has_s0, emit_state, scale):
    refs = list(refs)
    q_ref, k_ref, v_ref = refs[:3]
    refs = refs[3:]
    if has_rope:
        cos_ref, sin_ref = refs[:2]
        refs = refs[2:]
    if has_s0:
        s0_ref = refs.pop(0)
    o_ref = refs.pop(0)
    if emit_state:
        so_ref = refs.pop(0)
    st = refs.pop(0)
    c_sz = RET_CHUNK
    hp, dh = st.shape[0], st.shape[1]
    hg, d, cb = pl.program_id(1), pl.program_id(2), pl.program_id(3)
    heads = range(hp)
    lanes = [slice(h * dh, (h + 1) * dh) for h in heads]

    @pl.when(cb == 0)
    def _():
        st[...] = s0_ref[...] if has_s0 else jnp.zeros_like(st)

    lgv = [lg_ref[d, hg * hp + h] for h in heads]
    fwd = d == 0
    row = lax.broadcasted_iota(jnp.int32, (c_sz, c_sz), 0)
    col = lax.broadcasted_iota(jnp.int32, (c_sz, c_sz), 1)
    diff = jnp.where(fwd, row - col, col - row).astype(F32)
    intra = [jnp.where(diff >= 0, jnp.exp(jnp.maximum(diff, 0.0) * lg), 0.0) for lg in lgv]
    pos = lax.broadcasted_iota(jnp.int32, (c_sz, dh), 0)
    npos = jnp.where(fwd, pos, c_sz - 1 - pos).astype(F32)
    q_decay = [jnp.exp((npos + 1.0) * lg) for lg in lgv]
    k_decay = [jnp.exp((c_sz - 1.0 - npos) * lg) for lg in lgv]
    chunk_decay = [jnp.exp(jnp.full((1, dh), c_sz, F32) * lg) for lg in lgv]

    for j in range(n_chunk):
        jj = jnp.where(fwd, j, n_chunk - 1 - j)
        rows = pl.ds(pl.multiple_of(jj * c_sz, c_sz), c_sz)
        q = [q_ref[rows, sl] for sl in lanes]
        k = [k_ref[rows, sl] * scale for sl in lanes]
        v = [v_ref[rows, sl] for sl in lanes]
        if has_rope:
            c, s = cos_ref[rows, :], sin_ref[rows, :]
            q = [_rope(x, c, s) for x in q]
            k = [_rope(x, c, s) for x in k]
        state = [st[h] for h in heads]
        scores = [_bdot_nt(q[h], k[h]) * intra[h] for h in heads]
        o = [_bdot(scores[h], v[h]) + _bdot(q[h], state[h]) * q_decay[h] for h in heads]
        new_state = [state[h] * chunk_decay[h] + _bdot_tn(k[h] * k_decay[h], v[h]) for h in heads]
        o_ref[rows, :] = jnp.concatenate(o, axis=1)
        st[...] = jnp.stack(new_state)

    if emit_state:
        @pl.when(cb == n_blk - 1)
        def _():
            so_ref[...] = st[...]


def _retention(qkvg, log_gamma, s0, rope, *, row0, n_seq, seq_len, emit_state):
    d_model = qkvg.shape[1] // 4
    n_heads = RET_HEADS
    dh = d_model // n_heads
    tb = min(512, seq_len)
    n_blk = seq_len // tb
    rb0 = row0 // tb

    def blk(c, d):
        return jnp.where(d == 0, c, n_blk - 1 - c)

    hp = RET_HEADS_PER_STEP
    n_hg = n_heads // hp

    def in_map(part):
        return lambda s, h, d, c, lg: (rb0 + s * n_blk + blk(c, d), part * n_hg + h)

    in_specs = [pl.BlockSpec((tb, hp * dh), in_map(p)) for p in range(3)]
    args = [qkvg, qkvg, qkvg]
    if rope is not None:
        in_specs += [pl.BlockSpec((tb, dh), lambda s, h, d, c, lg: (blk(c, d), 0))] * 2
        args += list(rope)
    if s0 is not None:
        in_specs.append(pl.BlockSpec((None, None, hp, dh, dh), lambda s, h, d, c, lg: (s, d, h, 0, 0)))
        args.append(s0)
    out_specs = [pl.BlockSpec((None, tb, hp * dh), lambda s, h, d, c, lg: (d, s * n_blk + blk(c, d), h))]
    out_shape = [jax.ShapeDtypeStruct((2, n_seq * seq_len, d_model), F32)]
    if emit_state:
        out_specs.append(pl.BlockSpec((None, None, hp, dh, dh), lambda s, h, d, c, lg: (s, d, h, 0, 0)))
        out_shape.append(jax.ShapeDtypeStruct((n_seq, 2, n_heads, dh, dh), F32))
    body = functools.partial(_retention_body, n_chunk=tb // RET_CHUNK, n_blk=n_blk, has_rope=rope is not None,
                             has_s0=s0 is not None, emit_state=emit_state, scale=dh ** -0.5)
    return pl.pallas_call(
        body,
        grid_spec=pltpu.PrefetchScalarGridSpec(
            num_scalar_prefetch=1, grid=(n_seq, n_hg, 2, n_blk), in_specs=in_specs, out_specs=out_specs,
            scratch_shapes=[pltpu.VMEM((hp, dh, dh), F32)]),
        out_shape=out_shape, compiler_params=_cparams(4), name="retention")(log_gamma, *args)


def _two_trunk_specs(a_p, a_s, tm):
    pt = a_p.shape[1] // tm
    d = a_p.shape[2]
    return [pl.BlockSpec((2, tm, d), lambda i: (0, jnp.minimum(i, pt - 1), 0)),
            pl.BlockSpec((2, tm, d), lambda i: (0, jnp.maximum(i - pt, 0), 0))], pt


def _for_trunk_of_tile(pt, p_ref, s_ref, fn, row_axis=0):
    i = pl.program_id(row_axis)

    @pl.when(i < pt)
    def _():
        fn(p_ref)

    @pl.when(i >= pt)
    def _():
        fn(s_ref)


def _ret_finalize_body(op_ref, os_ref, g_ref, out_ref, *, n_heads, pt):
    def finalize(o_ref):
        o = o_ref[0] + o_ref[1]
        g = g_ref[...]
        dh = o.shape[1] // n_heads
        for h in range(n_heads):
            sl = slice(h * dh, (h + 1) * dh)
            oh = o[:, sl]
            oh = oh * lax.rsqrt(jnp.mean(oh * oh, axis=-1, keepdims=True) + NORM_EPS)
            out_ref[:, sl] = (oh * _silu(g[:, sl])).astype(out_ref.dtype)

    _for_trunk_of_tile(pt, op_ref, os_ref, finalize)


def _ret_finalize(o_p, o_s, qkvg, tm=256):
    n, d = o_p.shape[1] + o_s.shape[1], o_p.shape[2]
    specs, pt = _two_trunk_specs(o_p, o_s, tm)
    return pl.pallas_call(
        functools.partial(_ret_finalize_body, n_heads=RET_HEADS, pt=pt), grid=(n // tm,),
        in_specs=specs + [pl.BlockSpec((tm, d), lambda i: (i, 3))],
        out_specs=pl.BlockSpec((tm, d), lambda i: (i, 0)),
        out_shape=jax.ShapeDtypeStruct((n, d), BF16),
        compiler_params=_cparams(1), name="ret_finalize")(o_p, o_s, qkvg)


def _rwkv_mix_body(x_ref, xp_ref, xn_ref, nw_ref, sc_ref, sh_ref, mu_ref, o_ref, hext, *, tm, halo, lay):
    i = pl.program_id(0)
    d = x_ref.shape[1]
    nw, sc, sh = nw_ref[...], sc_ref[...], sh_ref[...]
    hext[0:halo, :] = _rms_mod(xp_ref[...], nw, sc, sh)
    hext[halo:halo + tm, :] = _rms_mod(x_ref[...], nw, sc, sh)
    hext[halo + tm:halo + tm + halo, :] = _rms_mod(xn_ref[...], nw, sc, sh)
    h = hext[halo:halo + tm, :]
    g_row = i * tm + lax.broadcasted_iota(jnp.int32, (tm, 1), 0)

    def emit(h_shift):
        diff = h_shift - h
        for n in range(6):
            o_ref[n] = (h + diff * mu_ref[n:n + 1, :]).astype(o_ref.dtype)

    def shifted(off, lo, hi, keep):
        return jnp.where(keep, hext[halo + off:halo + off + tm, lo:hi], 0.0)

    @pl.when(i < lay.p_rows // tm)
    def _():
        t = g_row % lay.p_len
        hd = d // 2
        emit(jnp.concatenate([shifted(-1, 0, hd, t != 0),
                              shifted(1, hd, d, t != lay.p_len - 1)], axis=1))

    @pl.when(i >= lay.p_rows // tm)
    def _():
        t = (g_row - lay.p_rows) % lay.s_len
        colw = t % GRID_W
        qd = d // 4
        emit(jnp.concatenate([shifted(-1, 0, qd, colw != 0),
                              shifted(1, qd, 2 * qd, colw != GRID_W - 1),
                              shifted(-GRID_W, 2 * qd, 3 * qd, t >= GRID_W),
                              shifted(GRID_W, 3 * qd, d, t < lay.s_len - GRID_W)], axis=1))


def _rwkv_mix(x, nw, mod5, layer, mu, lay, tm=512):
    n, d = x.shape
    halo = GRID_W
    tm = min(tm, lay.s_len, lay.p_rows)
    r = tm // halo
    n_halo_blk = n // halo
    body = functools.partial(_rwkv_mix_body, tm=tm, halo=halo, lay=lay)
    return pl.pallas_call(
        body, grid=(n // tm,),
        in_specs=[pl.BlockSpec((tm, d), lambda i: (i, 0)),
                  pl.BlockSpec((halo, d), lambda i: (jnp.maximum(i * r - 1, 0), 0)),
                  pl.BlockSpec((halo, d), lambda i: (jnp.minimum((i + 1) * r, n_halo_blk - 1), 0)),
                  pl.BlockSpec((1, d), lambda i: (0, 0)),
                  pl.BlockSpec((None, None, None, 1, d), _mod_spec(layer, 1, tm, lay)),
                  pl.BlockSpec((None, None, None, 1, d), _mod_spec(layer, 0, tm, lay)),
                  pl.BlockSpec((6, d), lambda i: (0, 0))],
        out_specs=pl.BlockSpec((6, tm, d), lambda i: (0, i, 0)),
        out_shape=jax.ShapeDtypeStruct((6, n, d), BF16),
        scratch_shapes=[pltpu.VMEM((tm + 2 * halo, d), F32)],
        compiler_params=_cparams(1), name="rwkv_mix")(x, x, x, nw.reshape(1, d), mod5, mod5, mu)


def _rwkv_scan_body(*refs, n_chunk, n_blk, has_s0, emit_state):
    refs = list(refs)
    r_ref, k_ref, v_ref, lw_ref, a_ref, kk_ref, ka_ref = refs[:7]
    refs = refs[7:]
    if has_s0:
        s0_ref = refs.pop(0)
    y_ref = refs.pop(0)
    if emit_state:
        so_ref = refs.pop(0)
    st = refs.pop(0)
    c_sz, hd, sw = RWKV_CHUNK, RWKV_HEAD_DIM, RWKV_SLAB
    n_grp = st.shape[0]
    hps = sw // hd
    d, cb = pl.program_id(1), pl.program_id(3)
    fwd = d == 0

    ri = lax.broadcasted_iota(jnp.int32, (sw, sw), 0)
    ci = lax.broadcasted_iota(jnp.int32, (sw, sw), 1)
    bd_mask = (ri // hd) == (ci // hd)

    def block_diag(slab):
        return jnp.where(bd_mask, jnp.concatenate([slab] * hps, axis=0), 0.0).astype(BF16)

    @pl.when(cb == 0)
    def _():
        if has_s0:
            st[...] = s0_ref[...]
        else:
            st[...] = jnp.zeros_like(st)

    srow = lax.broadcasted_iota(jnp.int32, (c_sz, sw), 0)
    scol = lax.broadcasted_iota(jnp.int32, (c_sz, sw), 1) % hd
    diff = jnp.where(fwd, srow - scol, scol - srow)
    strict = diff > 0
    incl = diff >= 0
    eye = jnp.where(diff == 0, 1.0, 0.0).astype(F32)
    levels = []
    m = 1
    while m < c_sz:
        levels.append(jnp.logical_and(srow // (2 * m) == scol // (2 * m), srow // m != scol // m))
        m *= 2
    trow = lax.broadcasted_iota(jnp.int32, (c_sz, c_sz), 0)
    tcol = lax.broadcasted_iota(jnp.int32, (c_sz, c_sz), 1)
    tri_incl = jnp.where(jnp.where(fwd, trow - tcol, tcol - trow) >= 0, 1.0, 0.0).astype(BF16)
    ones_bd = jnp.where(bd_mask, 1.0, 0.0).astype(BF16)

    def split(x, n_parts):
        parts = []
        for _ in range(n_parts - 1):
            p = x.astype(BF16)
            parts.append(p)
            x = x - p.astype(F32)
        return parts + [x.astype(BF16)]

    k_k, k_a = kk_ref[...], ka_ref[...]
    grp = range(n_grp)

    n_sub = min(RWKV_CHUNKS_PER_ITER, n_chunk)

    def chunks(j, carry):
        rows_of = []
        for cc in range(n_sub):
            jj = j * n_sub + cc
            jj = jnp.where(fwd, jj, n_chunk - 1 - jj)
            rows_of.append(pl.ds(pl.multiple_of(jj * c_sz, c_sz), c_sz))
        units = [(rows_of[cc], slice(g * sw, (g + 1) * sw)) for cc in range(n_sub) for g in grp]
        un = range(len(units))
        v = [v_ref[rows, sl] for rows, sl in units]
        kkr = [k_ref[rows, sl] * k_k[:, sl] for rows, sl in units]
        sq = jnp.concatenate([p for x in kkr for p in split(x * x, 2)], axis=0)
        sq = jnp.dot(sq, ones_bd, preferred_element_type=F32)
        ssum = [sq[2 * i * c_sz:(2 * i + 1) * c_sz] + sq[(2 * i + 1) * c_sz:(2 * i + 2) * c_sz] for i in un]
        kk = [x * lax.rsqrt(s + 1e-12) for x, s in zip(kkr, ssum)]
        a = [a_ref[rows, sl] for rows, sl in units]
        b = [x * y for x, y in zip(kk, a)]
        kdir = [k_ref[rows, sl] * (1.0 + (ai - 1.0) * k_a[:, sl]) for (rows, sl), ai in zip(units, a)]
        lw = [lw_ref[rows, sl] for rows, sl in units]
        cum = [sum(jnp.dot(tri_incl, p, preferred_element_type=F32) for p in split(x, 3)) for x in lw]
        total = [jnp.sum(x, axis=0, keepdims=True) for x in lw]
        half = [0.5 * t for t in total]
        cumx = [c - x for c, x in zip(cum, lw)]
        r = [r_ref[rows, sl] for rows, sl in units]
        lhs_g = [jnp.concatenate([kk[i] * jnp.exp(cumx[i] - half[i]), r[i] * jnp.exp(cum[i] - half[i])], axis=0)
                 for i in un]
        e_neg = [jnp.exp(half[i] - cum[i]) for i in un]
        g_k = [_bdot_nt(lhs_g[i], block_diag(kdir[i] * e_neg[i])) for i in un]
        g_b = [_bdot_nt(lhs_g[i], block_diag(b[i] * e_neg[i])) for i in un]
        l_k = [jnp.where(strict, x[:c_sz], 0.0) for x in g_k]
        a_rk = [jnp.where(incl, x[c_sz:], 0.0) for x in g_k]
        l_b = [jnp.where(strict, x[:c_sz], 0.0) for x in g_b]
        a_rb = [jnp.where(incl, x[c_sz:], 0.0) for x in g_b]
        x = [eye - jnp.where(levels[0], l, 0.0) for l in l_b]
        for lvl in levels[1:]:
            t = [_bdot(x[i], block_diag(jnp.where(lvl, l_b[i], 0.0))) for i in un]
            x = [x[i] - _bdot(t[i], block_diag(x[i])) for i in un]
        bd_v = [block_diag(x) for x in v]
        lav = [_bdot(jnp.concatenate([l_k[i], a_rk[i]], axis=0), bd_v[i]) for i in un]
        lkv = [m[:c_sz] for m in lav]
        wt = [_bdot(x[i], block_diag(kk[i] * jnp.exp(cumx[i]))) for i in un]
        vt = [_bdot(x[i], block_diag(lkv[i])) for i in un]
        r_abs = [r[i] * jnp.exp(cum[i]) for i in un]
        e_end = [jnp.exp(total[i] - cum[i]) for i in un]
        kb_end = [jnp.concatenate([kdir[i] * e_end[i], -(b[i] * e_end[i])], axis=0) for i in un]
        state = [st[g] for g in grp]
        for cc in range(n_sub):
            ids = [cc * n_grp + g for g in grp]
            su = [_bdot_nt(jnp.concatenate([wt[i], r_abs[i]], axis=0), state[g]) for g, i in zip(grp, ids)]
            u = [su[g][:c_sz] + vt[i] for g, i in zip(grp, ids)]
            y = [su[g][c_sz:] + lav[i][c_sz:] - _bdot(a_rb[i], block_diag(u[g]))
                 for g, i in zip(grp, ids)]
            upd = [_bdot_tn(jnp.concatenate([v[i], u[g]], axis=0), kb_end[i]) for g, i in zip(grp, ids)]
            state = [state[g] * jnp.exp(total[i]) + jnp.where(bd_mask, upd[g], 0.0) for g, i in zip(grp, ids)]
            y_ref[rows_of[cc], :] = jnp.concatenate(y, axis=1)
        st[...] = jnp.stack(state)
        return carry

    lax.fori_loop(0, n_chunk // n_sub, chunks, 0)

    if emit_state:
        @pl.when(cb == n_blk - 1)
        def _():
            for g in grp:
                s = st[g]
                for h in range(hps):
                    so_ref[g * hps + h] = s[h * hd:(h + 1) * hd, h * hd:(h + 1) * hd]


def _rwkv_scan(rkv, lw, a, k_k, k_a, s0, *, row0, n_seq, seq_len, emit_state, heads_per_step=32):
    _, _, d_model = rkv.shape
    hd, sw = RWKV_HEAD_DIM, RWKV_SLAB
    n_heads = d_model // hd
    g = min(heads_per_step, n_heads)
    wg = g * hd
    n_grp = wg // sw
    n_hg = n_heads // g
    tb = min(256, seq_len)
    n_blk = seq_len // tb
    rb0 = row0 // tb

    def blk(c, d):
        return jnp.where(d == 0, c, n_blk - 1 - c)

    def rkv_map(part):
        return lambda s, d, hg, c: (part, rb0 + s * n_blk + blk(c, d), hg)

    dir_map = lambda s, d, hg, c: (d, rb0 + s * n_blk + blk(c, d), hg)
    in_specs = ([pl.BlockSpec((None, tb, wg), rkv_map(p)) for p in range(3)]
                + [pl.BlockSpec((None, tb, wg), dir_map)] * 2
                + [pl.BlockSpec((1, wg), lambda s, d, hg, c: (0, hg))] * 2)
    args = [rkv, rkv, rkv, lw, a, k_k.reshape(1, d_model), k_a.reshape(1, d_model)]
    if s0 is not None:
        in_specs.append(pl.BlockSpec((None, None, n_grp, sw, sw), lambda s, d, hg, c: (s, d, hg, 0, 0)))
        args.append(s0)
    out_specs = [pl.BlockSpec((None, tb, wg), lambda s, d, hg, c: (d, s * n_blk + blk(c, d), hg))]
    out_shape = [jax.ShapeDtypeStruct((2, n_seq * seq_len, d_model), F32)]
    if emit_state:
        out_specs.append(pl.BlockSpec((None, None, g, hd, hd), lambda s, d, hg, c: (s, d, hg, 0, 0)))
        out_shape.append(jax.ShapeDtypeStruct((n_seq, 2, n_heads, hd, hd), F32))
    body = functools.partial(_rwkv_scan_body, n_chunk=tb // RWKV_CHUNK, n_blk=n_blk,
                             has_s0=s0 is not None, emit_state=emit_state)
    return pl.pallas_call(
        body, grid=(n_seq, 2, n_hg, n_blk), in_specs=in_specs, out_specs=out_specs, out_shape=out_shape,
        scratch_shapes=[pltpu.VMEM((n_grp, sw, sw), F32)],
        compiler_params=_cparams(4), name="rwkv_scan")(*args)


def _group_sum(x, ones_bd):
    w = ones_bd.shape[0]
    out = []
    for c in range(x.shape[1] // w):
        xs = x[:, c * w:(c + 1) * w]
        hi = xs.astype(BF16)
        lo = (xs - hi.astype(F32)).astype(BF16)
        out.append(jnp.dot(hi, ones_bd, preferred_element_type=F32)
                   + jnp.dot(lo, ones_bd, preferred_element_type=F32))
    return jnp.concatenate(out, axis=1)


def _rwkv_finalize_body(yp_ref, ys_ref, rkv_ref, a_ref, g_ref, ka_ref, rk_ref, lnw_ref, lnb_ref, o_ref, *, pt):
    hd = RWKV_HEAD_DIM
    w = RWKV_SLAB
    ri = lax.broadcasted_iota(jnp.int32, (w, w), 0)
    ci = lax.broadcasted_iota(jnp.int32, (w, w), 1)
    ones_bd = jnp.where(ri // hd == ci // hd, 1.0, 0.0).astype(BF16)

    def finalize(y_ref):
        y = y_ref[0] + y_ref[1]
        mean = _group_sum(y, ones_bd) * (1.0 / hd)
        yc = y - mean
        var = _group_sum(yc * yc, ones_bd) * (1.0 / hd)
        yn = yc * lax.rsqrt(var + RWKV_LN_EPS) * lnw_ref[...] + lnb_ref[...]
        r, k, v = rkv_ref[0], rkv_ref[1], rkv_ref[2]
        a_sum = a_ref[0] + a_ref[1]
        k_sum = k * (2.0 + (a_sum - 2.0) * ka_ref[...])
        bonus = _group_sum(r * k_sum * rk_ref[...], ones_bd) * v
        o_ref[...] = ((yn + bonus) * g_ref[...]).astype(o_ref.dtype)

    _for_trunk_of_tile(pt, yp_ref, ys_ref, finalize)


def _rwkv_finalize(y_p, y_s, rkv, a, g, k_a, r_k, ln_w, ln_b, tm=256):
    n, d = y_p.shape[1] + y_s.shape[1], y_p.shape[2]
    specs, pt = _two_trunk_specs(y_p, y_s, tm)
    row = lambda i: (0, 0)
    return pl.pallas_call(
        functools.partial(_rwkv_finalize_body, pt=pt), grid=(n // tm,),
        in_specs=specs + [
                  pl.BlockSpec((3, tm, d), lambda i: (0, i, 0)),
                  pl.BlockSpec((2, tm, d), lambda i: (0, i, 0)),
                  pl.BlockSpec((tm, d), lambda i: (i, 0)),
                  pl.BlockSpec((1, d), row), pl.BlockSpec((1, d), row),
                  pl.BlockSpec((1, d), row), pl.BlockSpec((1, d), row)],
        out_specs=pl.BlockSpec((tm, d), lambda i: (i, 0)),
        out_shape=jax.ShapeDtypeStruct((n, d), BF16),
        compiler_params=_cparams(1), name="rwkv_finalize",
    )(y_p, y_s, rkv, a, g, k_a.reshape(1, d), r_k.reshape(1, d), ln_w.reshape(1, d), ln_b.reshape(1, d))


def _cast_expert_weights(be_ref, pairs):
    b = pl.program_id(1)

    @pl.when(jnp.logical_or(b == 0, be_ref[b] != be_ref[jnp.maximum(b - 1, 0)]))
    def _():
        for w_ref, w_bf in pairs:
            w_bf[...] = w_ref[...].astype(BF16)


def _moe_up_body(be_ref, na_ref, x_ref, wg_ref, wu_ref, o_ref, wg_bf, wu_bf):
    b = pl.program_id(1)
    _cast_expert_weights(be_ref, [(wg_ref, wg_bf), (wu_ref, wu_bf)])

    @pl.when(b < na_ref[0])
    def _():
        x = x_ref[...]
        gate = jnp.dot(x, wg_bf[...], preferred_element_type=F32)
        up = jnp.dot(x, wu_bf[...], preferred_element_type=F32)
        o_ref[...] = (_silu(gate) * up).astype(o_ref.dtype)

    @pl.when(b >= na_ref[0])
    def _():
        o_ref[...] = jnp.zeros_like(o_ref)


def _moe_down_body(be_ref, na_ref, *refs, n_parts):
    x_refs, w_refs = refs[:n_parts], refs[n_parts:2 * n_parts]
    o_ref = refs[2 * n_parts]
    w_bfs = refs[2 * n_parts + 1:]
    b = pl.program_id(1)
    _cast_expert_weights(be_ref, list(zip(w_refs, w_bfs)))

    @pl.when(b < na_ref[0])
    def _():
        o_ref[...] = sum(jnp.dot(x[...], w[...], preferred_element_type=F32) for x, w in zip(x_refs, w_bfs))

    @pl.when(b >= na_ref[0])
    def _():
        o_ref[...] = jnp.zeros_like(o_ref)


def _moe_up(xb, block_e, n_active, w_gate, w_up, col0, n_cols, tn):
    rows, d = xb.shape
    bm = MOE_BLOCK
    off = col0 // tn
    w_spec = pl.BlockSpec((None, d, tn), lambda j, b, be, na: (be[b], 0, off + j))
    return pl.pallas_call(
        _moe_up_body,
        grid_spec=pltpu.PrefetchScalarGridSpec(
            num_scalar_prefetch=2, grid=(n_cols // tn, rows // bm),
            in_specs=[pl.BlockSpec((bm, d), lambda j, b, be, na: (b, 0)), w_spec, w_spec],
            out_specs=pl.BlockSpec((bm, tn), lambda j, b, be, na: (b, j)),
            scratch_shapes=[pltpu.VMEM((d, tn), BF16)] * 2),
        out_shape=jax.ShapeDtypeStruct((rows, n_cols), BF16),
        compiler_params=_cparams(2), name="moe_up")(block_e, n_active, xb, w_gate, w_up)


def _moe_experts(xb, block_e, n_active, w_gate, w_up, w_down, tn_wide=1024, tn_narrow=512, tn_down=512):
    rows, d = xb.shape
    bm = MOE_BLOCK
    d_ff = w_gate.shape[2]
    wide = (d_ff // tn_wide) * tn_wide
    parts = []
    if wide:
        parts.append((_moe_up(xb, block_e, n_active, w_gate, w_up, 0, wide, tn_wide), 0))
    if d_ff > wide:
        tn = min(tn_narrow, d_ff - wide)
        assert (d_ff - wide) % tn == 0 and wide % (d_ff - wide) == 0, (d_ff, tn_wide, tn_narrow)
        parts.append((_moe_up(xb, block_e, n_active, w_gate, w_up, wide, d_ff - wide, tn), wide))
    tn_down = min(tn_down, d)
    x_specs = [pl.BlockSpec((bm, a.shape[1]), lambda j, b, be, na: (b, 0)) for a, _ in parts]
    w_specs = [pl.BlockSpec((None, a.shape[1], tn_down),
                            lambda j, b, be, na, blk=c0 // a.shape[1]: (be[b], blk, j)) for a, c0 in parts]
    return pl.pallas_call(
        functools.partial(_moe_down_body, n_parts=len(parts)),
        grid_spec=pltpu.PrefetchScalarGridSpec(
            num_scalar_prefetch=2, grid=(d // tn_down, rows // bm),
            in_specs=x_specs + w_specs,
            out_specs=pl.BlockSpec((bm, tn_down), lambda j, b, be, na: (b, j)),
            scratch_shapes=[pltpu.VMEM((a.shape[1], tn_down), BF16) for a, _ in parts]),
        out_shape=jax.ShapeDtypeStruct((rows, d), F32),
        compiler_params=_cparams(2), name="moe_down",
    )(block_e, n_active, *[a for a, _ in parts], *([w_down] * len(parts)))


def _row_copy(src_hbm, src_row, dst, dst_row, sem):
    return pltpu.make_async_copy(src_hbm.at[pl.ds(src_row, 1), :], dst.at[pl.ds(dst_row, 1), :], sem)


def _prefetched_rows(fetch, drain):
    i, n = pl.program_id(0), pl.num_programs(0)
    buf = i % 2

    @pl.when(i == 0)
    def _():
        fetch(0, False)

    @pl.when(i + 1 < n)
    def _():
        fetch(1 - buf, True)

    drain(buf)
    return buf


def _moe_gather_body(tok_ref, tok_next_ref, h_hbm, o_ref, rows, sems, *, bm):
    def fetch(buf, is_next):
        idx_ref = tok_next_ref if is_next else tok_ref

        def start(r, carry):
            _row_copy(h_hbm, idx_ref[0, r], rows.at[buf], r, sems.at[buf]).start()
            return carry

        lax.fori_loop(0, bm, start, 0, unroll=8)

    def drain(buf):
        def wait(r, carry):
            _row_copy(h_hbm, 0, rows.at[buf], r, sems.at[buf]).wait()
            return carry

        lax.fori_loop(0, bm, wait, 0, unroll=8)

    buf = _prefetched_rows(fetch, drain)
    o_ref[...] = rows[buf].astype(o_ref.dtype)


def _moe_gather(h, slot_tok):
    _, d = h.shape
    bm = MOE_BLOCK
    n_blk = slot_tok.shape[0] // bm
    tok = slot_tok.reshape(n_blk, 1, bm)
    return pl.pallas_call(
        functools.partial(_moe_gather_body, bm=bm), grid=(n_blk,),
        in_specs=[pl.BlockSpec((None, 1, bm), lambda b: (b, 0, 0), memory_space=pltpu.SMEM),
                  pl.BlockSpec((None, 1, bm), lambda b: (jnp.minimum(b + 1, n_blk - 1), 0, 0),
                               memory_space=pltpu.SMEM),
                  pl.BlockSpec(memory_space=pl.ANY)],
        out_specs=pl.BlockSpec((bm, d), lambda b: (b, 0)),
        out_shape=jax.ShapeDtypeStruct((n_blk * bm, d), BF16),
        scratch_shapes=[pltpu.VMEM((2, bm, d), h.dtype), pltpu.SemaphoreType.DMA((2,))],
        compiler_params=_cparams(1), name="moe_gather")(tok, tok, h)


def _moe_combine_body(slot_ref, slot_next_ref, x_ref, g_ref, gt_ref, fw_ref, yb_hbm, op_ref, os_ref, ybuf, sems,
                      *, tm, pt):
    def fetch(buf, is_next):
        idx_ref = slot_next_ref if is_next else slot_ref

        def start(r, carry):
            for k in range(TOP_K):
                _row_copy(yb_hbm, idx_ref[0, TOP_K * r + k], ybuf.at[buf, k], r, sems.at[buf]).start()
            return carry

        lax.fori_loop(0, tm, start, 0, unroll=4)

    def drain(buf):
        def wait(r, carry):
            for k in range(TOP_K):
                _row_copy(yb_hbm, 0, ybuf.at[buf, k], r, sems.at[buf]).wait()
            return carry

        lax.fori_loop(0, tm, wait, 0, unroll=4)

    buf = _prefetched_rows(fetch, drain)
    g = g_ref[...]
    y = ybuf[buf, 0] * g[:, 0:1] + ybuf[buf, 1] * g[:, 1:2]
    x = x_ref[...] + gt_ref[...] * y
    out = x * lax.rsqrt(jnp.mean(x * x, axis=-1, keepdims=True) + NORM_EPS) * fw_ref[...]

    def store(o_ref):
        o_ref[...] = out

    _for_trunk_of_tile(pt, op_ref, os_ref, store)


def _moe_combine_final(x, yb, slot_of, gates, mod5, layer, final_w, lay, tm=256):
    n, d = x.shape
    tm = min(tm, lay.s_len, lay.p_rows)
    pt = lay.p_rows // tm
    row = pl.BlockSpec((tm, d), lambda i: (i, 0))
    n_tiles = n // tm
    slots = slot_of.reshape(n_tiles, 1, TOP_K * tm)
    return pl.pallas_call(
        functools.partial(_moe_combine_body, tm=tm, pt=pt), grid=(n_tiles,),
        in_specs=[pl.BlockSpec((None, 1, TOP_K * tm), lambda i: (i, 0, 0), memory_space=pltpu.SMEM),
                  pl.BlockSpec((None, 1, TOP_K * tm), lambda i: (jnp.minimum(i + 1, n_tiles - 1), 0, 0),
                               memory_space=pltpu.SMEM),
                  row, pl.BlockSpec((tm, TOP_K), lambda i: (i, 0)),
                  pl.BlockSpec((None, None, None, 1, d), _mod_spec(layer, 5, tm, lay)),
                  pl.BlockSpec((1, d), lambda i: (0, 0)),
                  pl.BlockSpec(memory_space=pl.ANY)],
        out_specs=[pl.BlockSpec((tm, d), lambda i: (jnp.minimum(i, pt - 1), 0)),
                   pl.BlockSpec((tm, d), lambda i: (jnp.maximum(i - pt, 0), 0))],
        out_shape=[jax.ShapeDtypeStruct((lay.p_rows, d), F32),
                   jax.ShapeDtypeStruct((n - lay.p_rows, d), F32)],
        scratch_shapes=[pltpu.VMEM((2, TOP_K, tm, d), F32), pltpu.SemaphoreType.DMA((2,))],
        compiler_params=_cparams(1), name="moe_combine_final",
    )(slots, slots, x, gates, mod5, final_w.reshape(1, d), yb)


def _moe(h, logits, w_gate, w_up, w_down):
    n_tok, d = h.shape
    bm = MOE_BLOCK
    top_val, top_idx = lax.top_k(logits, TOP_K)
    gates = jax.nn.softmax(top_val, axis=-1)
    n_assign = n_tok * TOP_K
    flat_e = top_idx.reshape(-1).astype(jnp.int32)
    order = jnp.argsort(flat_e).astype(jnp.int32)
    rank = jnp.argsort(order).astype(jnp.int32)
    counts = jnp.sum(flat_e[:, None] == jnp.arange(N_EXPERTS, dtype=jnp.int32)[None, :], axis=0,
                     dtype=jnp.int32)
    padded = ((counts + bm - 1) // bm) * bm
    pad_end = jnp.cumsum(padded)
    pad_start = pad_end - padded
    start = jnp.cumsum(counts) - counts
    n_blocks = -(-n_assign // bm) + N_EXPERTS
    block_e = jnp.minimum(jnp.searchsorted(pad_end, jnp.arange(n_blocks) * bm, side='right'),
                          N_EXPERTS - 1).astype(jnp.int32)
    n_active = (pad_end[-1] // bm).astype(jnp.int32).reshape(1)
    experts = jnp.arange(N_EXPERTS, dtype=jnp.int32)

    def lookup(table, e):
        return jnp.sum(jnp.where(e[..., None] == experts, table, 0), axis=-1, dtype=jnp.int32)

    slot = jnp.arange(n_blocks * bm, dtype=jnp.int32)
    slot_e = jnp.repeat(block_e, bm)
    pos = slot - lookup(pad_start, slot_e)
    src = jnp.clip(lookup(start, slot_e) + pos, 0, n_assign - 1)
    slot_tok = jnp.where(pos < lookup(counts, slot_e),
                         jnp.take(order, src, indices_are_sorted=True) // TOP_K, slot % n_tok)
    yb = _moe_experts(_moe_gather(h, slot_tok), block_e, n_active, w_gate, w_up, w_down)
    slot_of = (lookup(pad_start - start, flat_e) + rank).reshape(n_tok, TOP_K)
    return yb, slot_of, gates


def _rope_tables(seq_len, dim):
    t = jnp.arange(seq_len)
    row = (t // GRID_W).astype(F32)
    col = (t % GRID_W).astype(F32)
    half = dim // 2
    inv = ROPE_BASE ** (-(jnp.arange(0, half, 2, dtype=F32) / half))
    ang = jnp.concatenate([row[:, None] * inv, col[:, None] * inv], axis=-1)
    cos, sin = jnp.cos(ang), jnp.sin(ang)
    return jnp.repeat(cos, 2, axis=-1), jnp.stack([-sin, sin], axis=-1).reshape(seq_len, dim)


def _block_diag_states(s):
    b, two, n_heads, hd, _ = s.shape
    hps = RWKV_SLAB // hd
    s6 = s.reshape(b, two, n_heads // hps, hps, hd, hd)
    bd = jnp.einsum('bdghvk,hi->bdghvik', s6, jnp.eye(hps, dtype=s.dtype))
    return bd.reshape(b, two, n_heads // hps, RWKV_SLAB, RWKV_SLAB)


def _pad_cols(w, width):
    return jnp.pad(w, ((0, 0), (0, width - w.shape[1])))


def kernel(x_prompt, x_sample, state_l0_ret, state_l1_rwkv, c, c_ctx, ada_w, ada_b, norm_w, final_norm_w,
           l0_ret_w_in, l0_ret_w_out, l0_ret_decay, l0_ffn_w_gate, l0_ffn_w_up, l0_ffn_w_down,
           l1_rwkv_mu, l1_rwkv_w_rkv, l1_rwkv_w0, l1_rwkv_w1, l1_rwkv_w2, l1_rwkv_a0, l1_rwkv_a1, l1_rwkv_a2,
           l1_rwkv_g1, l1_rwkv_g2, l1_rwkv_k_k, l1_rwkv_k_a, l1_rwkv_r_k, l1_rwkv_ln_w, l1_rwkv_ln_b,
           l1_rwkv_w_out, l1_moe_router, l1_moe_w_gate, l1_moe_w_up, l1_moe_w_down):
    pb, p_len, d = x_prompt.shape
    sb, s_len, _ = x_sample.shape
    p_rows, s_rows = pb * p_len, sb * s_len
    lay = _Layout(p_rows, p_len, s_len, p_rows + s_rows)
    n_layers = ada_w.shape[0]

    x = (x_prompt.reshape(p_rows, d), x_sample.reshape(s_rows, d))
    cond8 = jnp.concatenate([c_ctx[None, :], c, jnp.zeros((8 - 1 - sb, d), F32)], axis=0)
    mod5 = _ada_mod(cond8, ada_w, ada_b)[:, :1 + sb].reshape(n_layers, 1 + sb, 6, 1, d)

    h = _norm_mod(x, norm_w[0, 0], mod5, 0, 1, 0, lay)
    qkvg = _linear(h, l0_ret_w_in, tn=1024)
    log_gamma = -jnp.exp(l0_ret_decay.astype(F32))
    dh = d // RET_HEADS
    o_p, new_state_l0_ret = _retention(qkvg, log_gamma, None, None, row0=0, n_seq=pb, seq_len=p_len,
                                       emit_state=True)
    (o_s,) = _retention(qkvg, log_gamma, state_l0_ret, _rope_tables(s_len, dh), row0=p_rows, n_seq=sb,
                        seq_len=s_len, emit_state=False)
    x = _linear_residual(_ret_finalize(o_p, o_s, qkvg), l0_ret_w_out, x, mod5, 0, 2, lay)
    h = _norm_mod(x, norm_w[0, 1], mod5, 0, 4, 3, lay)
    act = _swiglu_up(h, l0_ffn_w_gate, l0_ffn_w_up, tm=2048)
    x = _linear_residual(act, l0_ffn_w_down, x, mod5, 0, 5, lay, tm=512, tn=512)

    xs = _rwkv_mix(x, norm_w[1, 0], mod5, 1, l1_rwkv_mu, lay)
    rkv = _linear(xs, l1_rwkv_w_rkv, n_batch=3, tn=1024)
    lora_w = l1_rwkv_w1.shape[2]
    pad_w = -(-lora_w // 128) * 128
    w1p = jnp.concatenate([_pad_cols(l1_rwkv_w1[0], pad_w), _pad_cols(l1_rwkv_w1[1], pad_w)], axis=1)
    a1p = jnp.concatenate([_pad_cols(l1_rwkv_a1[0], pad_w), _pad_cols(l1_rwkv_a1[1], pad_w)], axis=1)
    t_w = _linear(xs, w1p, x_lead=3, act="tanh", out_dtype=BF16, tm=2048)
    t_a = _linear(xs, a1p, x_lead=4, out_dtype=BF16, tm=2048)
    t_g = _linear(xs, l1_rwkv_g1, x_lead=5, act="sigmoid", out_dtype=BF16, tm=2048)
    w2p = jnp.pad(l1_rwkv_w2, ((0, 0), (0, pad_w - lora_w), (0, 0)))
    a2p = jnp.pad(l1_rwkv_a2, ((0, 0), (0, pad_w - lora_w), (0, 0)))
    lw = _linear(t_w, w2p, n_batch=2, bias=l1_rwkv_w0, act="log_decay", tm=2048)
    a_lr = _linear(t_a, a2p, n_batch=2, bias=l1_rwkv_a0, act="sigmoid", tm=2048)
    g = _linear(t_g, l1_rwkv_g2, tm=2048)
    scan_args = (rkv, lw, a_lr, l1_rwkv_k_k, l1_rwkv_k_a)
    y_p, new_state_l1_rwkv = _rwkv_scan(*scan_args, None, row0=0, n_seq=pb, seq_len=p_len, emit_state=True)
    (y_s,) = _rwkv_scan(*scan_args, _block_diag_states(state_l1_rwkv), row0=p_rows, n_seq=sb, seq_len=s_len,
                        emit_state=False)
    yg = _rwkv_finalize(y_p, y_s, rkv, a_lr, g, l1_rwkv_k_a, l1_rwkv_r_k, l1_rwkv_ln_w, l1_rwkv_ln_b)
    x = _linear_residual(yg, l1_rwkv_w_out, x, mod5, 1, 2, lay)
    ne_pad = 128
    h, logits = _norm_mod(x, norm_w[1, 1], mod5, 1, 4, 3, lay, router_w=_pad_cols(l1_moe_router, ne_pad))
    yb, slot_of, gates = _moe(h, logits[:, :N_EXPERTS], l1_moe_w_gate, l1_moe_w_up, l1_moe_w_down)
    y_prompt, y_sample = _moe_combine_final(x, yb, slot_of, gates, mod5, 1, final_norm_w, lay)
    return (y_prompt.reshape(pb, p_len, d), y_sample.reshape(sb, s_len, d), new_state_l0_ret, new_state_l1_rwkv)
```
